```python
import jax, jax.numpy as jnp
from jax import lax
import numpy as np

D_MODEL = 1024
BATCH = 8
SEQ = 8192
DEPTH = 2

GRID_W = 64
ROPE_THETA = 10000.0
ATTN_HEAD_DIM = 64
ATTN_Q_HEADS = 8
ATTN_KV_HEADS = 2
ATTN_GROUP = ATTN_Q_HEADS // ATTN_KV_HEADS
Q_BLOCK = 128
RET_HEADS = 4
RET_HEAD_DIM = 128
RET_CHUNK = 128
ATTN_W = ATTN_Q_HEADS * ATTN_HEAD_DIM
KV_W = ATTN_KV_HEADS * ATTN_HEAD_DIM
RET_W = RET_HEADS * RET_HEAD_DIM
MIX_W = ATTN_W + RET_W
IN_W = ATTN_W + 2 * KV_W + 4 * RET_W
SPLIT_POINTS = (ATTN_W, ATTN_W + KV_W, ATTN_W + 2 * KV_W, ATTN_W + 2 * KV_W + RET_W, ATTN_W + 2 * KV_W + 2 * RET_W, ATTN_W + 2 * KV_W + 3 * RET_W)
N_EXPERTS = 32
TOP_K = 4
D_FF_EXPERT = D_MODEL
SWIGLU_LIMIT = 7.0
SWIGLU_ALPHA = 1.702
MOE_BLOCK = 256
NORM_EPS = 1e-6

kernel_name = 'hybrid_gqa_retention_moe_encoder'


def rms_norm(x, g):
    xf = x.astype(jnp.float32)
    y = xf * lax.rsqrt(jnp.mean(xf * xf, axis=-1, keepdims=True) + NORM_EPS)
    return (y * g.astype(jnp.float32)).astype(x.dtype)


def axial_rope_tables(row_ids, col_ids, head_dim):
    quarter = head_dim // 4
    inv_freq = ROPE_THETA ** (-jnp.arange(quarter, dtype=jnp.float32) / quarter)
    ang_r = row_ids[:, None] * inv_freq[None, :]
    ang_c = col_ids[:, None] * inv_freq[None, :]
    return (jnp.cos(ang_r), jnp.sin(ang_r), jnp.cos(ang_c), jnp.sin(ang_c))


def apply_axial_rope(x, tables):
    cr, sr, cc, sc = [t[None, :, None, :].astype(x.dtype) for t in tables]
    a1, a2, b1, b2 = jnp.split(x, 4, axis=-1)
    return jnp.concatenate([a1 * cr - a2 * sr, a2 * cr + a1 * sr, b1 * cc - b2 * sc, b2 * cc + b1 * sc], axis=-1)


def block_attention(q, k, v):
    b, s, _, dh = q.shape
    nb = s // Q_BLOCK
    qb = jnp.moveaxis(q.reshape(b, nb, Q_BLOCK, ATTN_KV_HEADS, ATTN_GROUP, dh), 1, 0)
    scale = dh ** -0.5

    def one_block(qblk):
        sc = jnp.einsum('bqkgd,bskd->bkgqs', qblk, k).astype(jnp.float32) * scale
        p = jax.nn.softmax(sc, axis=-1).astype(v.dtype)
        return jnp.einsum('bkgqs,bskd->bqkgd', p, v)

    o = lax.map(one_block, qb)
    return jnp.moveaxis(o, 0, 1).reshape(b, s, ATTN_W)


def retention_one_direction(q, k, v, log_gamma):
    b, s, h, dk = q.shape
    dv = v.shape[-1]
    nc = s // RET_CHUNK
    idx = jnp.arange(RET_CHUNK, dtype=jnp.float32)
    diff = idx[:, None] - idx[None, :]
    inner_decay = jnp.where(diff >= 0, jnp.exp(log_gamma[:, None, None] * jnp.maximum(diff, 0.0)), 0.0)
    q_decay = jnp.exp(log_gamma[:, None] * (idx + 1.0))[None, :, :, None]
    k_decay = jnp.exp(log_gamma[:, None] * (RET_CHUNK - 1.0 - idx))[None, :, :, None]
    chunk_decay = jnp.exp(log_gamma * RET_CHUNK)[None, :, None, None]

    def to_chunks(t):
        return t.reshape(b, nc, RET_CHUNK, h, t.shape[-1]).transpose(1, 0, 3, 2, 4)

    def step(state, inp):
        qi, ki, vi = inp
        scores = jnp.einsum('bhqd,bhkd->bhqk', qi, ki) * inner_decay
        inner = jnp.einsum('bhqk,bhkv->bhqv', scores, vi)
        cross = jnp.einsum('bhqd,bhdv->bhqv', qi, state) * q_decay
        new_state = state * chunk_decay + jnp.einsum('bhkd,bhkv->bhdv', ki * k_decay, vi)
        return new_state, inner + cross

    state0 = jnp.zeros((b, h, dk, dv), jnp.float32)
    _, out = lax.scan(step, state0, (to_chunks(q), to_chunks(k), to_chunks(v)))
    return out.transpose(1, 0, 3, 2, 4).reshape(b, s, h, dv)


def hybrid_mixer(h, w_in, q_norm_g, k_norm_g, decay_fwd, decay_bwd, gn_g, gn_b, w_out, rope_a, rope_r):
    b, s, _ = h.shape
    proj = h @ w_in
    q_a, k_a, v_a, q_r, k_r, v_r, g_r = jnp.split(proj, SPLIT_POINTS, axis=-1)

    q_a = apply_axial_rope(rms_norm(q_a.reshape(b, s, ATTN_Q_HEADS, ATTN_HEAD_DIM), q_norm_g), rope_a)
    k_a = apply_axial_rope(rms_norm(k_a.reshape(b, s, ATTN_KV_HEADS, ATTN_HEAD_DIM), k_norm_g), rope_a)
    v_a = v_a.reshape(b, s, ATTN_KV_HEADS, ATTN_HEAD_DIM)
    o_attn = block_attention(q_a, k_a, v_a)

    q_r = apply_axial_rope(q_r.reshape(b, s, RET_HEADS, RET_HEAD_DIM), rope_r).astype(jnp.float32)
    k_r = (apply_axial_rope(k_r.reshape(b, s, RET_HEADS, RET_HEAD_DIM), rope_r).astype(jnp.float32) * (RET_HEAD_DIM ** -0.5))
    v_r = v_r.reshape(b, s, RET_HEADS, RET_HEAD_DIM).astype(jnp.float32)
    fwd = retention_one_direction(q_r, k_r, v_r, decay_fwd.astype(jnp.float32))
    bwd = jnp.flip(retention_one_direction(jnp.flip(q_r, 1), jnp.flip(k_r, 1), jnp.flip(v_r, 1), decay_bwd.astype(jnp.float32)), 1)
    ret = fwd + bwd
    mu = jnp.mean(ret, axis=-1, keepdims=True)
    var = jnp.mean(jnp.square(ret - mu), axis=-1, keepdims=True)
    ret = ((ret - mu) * lax.rsqrt(var + NORM_EPS)).reshape(b, s, RET_W)
    ret = ret * gn_g.astype(jnp.float32) + gn_b.astype(jnp.float32)
    o_ret = (jax.nn.silu(g_r.astype(jnp.float32)) * ret).astype(h.dtype)

    return jnp.concatenate([o_attn, o_ret], axis=-1) @ w_out


def routed_experts(h, w_router, b_router, w_up, b_up, w_down, b_down):
    n_tok = h.shape[0]
    logits = (h @ w_router + b_router).astype(jnp.float32)
    top_val, top_idx = lax.top_k(logits, TOP_K)
    top_w = jax.nn.softmax(top_val, axis=-1)
    n_asg = n_tok * TOP_K
    flat_e = top_idx.reshape(n_asg)
    order = jnp.argsort(flat_e)
    e_sorted = flat_e[order]
    tok_sorted = order // TOP_K
    w_sorted = top_w.reshape(n_asg)[order]
    n_blk = n_asg // MOE_BLOCK
    e_blk = e_sorted.reshape(n_blk, MOE_BLOCK)
    first = e_blk[:, 0]
    span = e_blk[:, -1] - first + 1
    ends = jnp.cumsum(span)
    starts = ends - span
    n_items = n_blk + N_EXPERTS - 1
    item = jnp.arange(n_items)
    item_blk = jnp.minimum(jnp.searchsorted(ends, item, side='right'), n_blk - 1)
    item_valid = item < ends[-1]
    item_exp = jnp.where(item_valid, first[item_blk] + item - starts[item_blk], 0)

    def step(acc, it):
        blk, e, valid = it
        start = blk * MOE_BLOCK
        rows = lax.dynamic_slice_in_dim(tok_sorted, start, MOE_BLOCK)
        es = lax.dynamic_slice_in_dim(e_sorted, start, MOE_BLOCK)
        ws = lax.dynamic_slice_in_dim(w_sorted, start, MOE_BLOCK)
        xb = jnp.take(h, rows, axis=0)
        up = xb @ w_up[e] + b_up[e]
        glu, lin = jnp.split(up, 2, axis=-1)
        glu = jnp.minimum(glu, SWIGLU_LIMIT)
        lin = jnp.clip(lin, -SWIGLU_LIMIT, SWIGLU_LIMIT)
        act = glu * jax.nn.sigmoid(SWIGLU_ALPHA * glu) * (lin + 1.0)
        y = act @ w_down[e] + b_down[e]
        coef = jnp.where((es == e) & valid, ws, 0.0).astype(h.dtype)
        return acc.at[rows].add(y * coef[:, None]), None

    out, _ = lax.scan(step, jnp.zeros_like(h), (item_blk, item_exp, item_valid))
    return out


def setup_inputs(seed: int = 0) -> dict:
    key = jax.random.key(seed)
    ks = jax.random.split(key, 22)
    f32 = jnp.float32
    L, D = DEPTH, D_MODEL

    def nrm(k, shape, scale):
        return jax.random.normal(k, shape, f32) * scale

    base_decay = jnp.log1p(-jnp.exp2(-5.0 - jnp.arange(RET_HEADS, dtype=f32)))
    return {
        'x': nrm(ks[0], (BATCH, SEQ, D), 1.0),
        'c': nrm(ks[1], (BATCH, D), 1.0),
        'ada_w': nrm(ks[2], (L, D, 6 * D), 0.5 * D ** -0.5),
        'ada_b': nrm(ks[3], (L, 6 * D), 0.02),
        'pre_mix_g': 1.0 + nrm(ks[4], (L, D), 0.05),
        'post_mix_g': 1.0 + nrm(ks[5], (L, D), 0.05),
        'w_in': nrm(ks[6], (L, D, IN_W), D ** -0.5),
        'q_norm_g': 1.0 + nrm(ks[7], (L, ATTN_HEAD_DIM), 0.05),
        'k_norm_g': 1.0 + nrm(ks[8], (L, ATTN_HEAD_DIM), 0.05),
        'ret_decay_fwd': base_decay * jnp.exp(nrm(ks[9], (L, RET_HEADS), 0.1)),
        'ret_decay_bwd': base_decay * jnp.exp(nrm(ks[10], (L, RET_HEADS), 0.1)),
        'ret_gn_g': 1.0 + nrm(ks[11], (L, RET_W), 0.05),
        'ret_gn_b': nrm(ks[12], (L, RET_W), 0.02),
        'w_out': nrm(ks[13], (L, MIX_W, D), MIX_W ** -0.5),
        'pre_ffn_g': 1.0 + nrm(ks[14], (L, D), 0.05),
        'post_ffn_g': 1.0 + nrm(ks[15], (L, D), 0.05),
        'router_w': nrm(ks[16], (L, D, N_EXPERTS), D ** -0.5),
        'router_b': nrm(ks[17], (L, N_EXPERTS), 0.01),
        'exp_w_up': nrm(ks[18], (L, N_EXPERTS, D, 2 * D_FF_EXPERT), D ** -0.5),
        'exp_b_up': nrm(ks[19], (L, N_EXPERTS, 2 * D_FF_EXPERT), 0.02),
        'exp_w_down': nrm(ks[20], (L, N_EXPERTS, D_FF_EXPERT, D), D_FF_EXPERT ** -0.5),
        'exp_b_down': nrm(ks[21], (L, N_EXPERTS, D), 0.02),
    }


def reference(x, c, ada_w, ada_b, pre_mix_g, post_mix_g, w_in, q_norm_g, k_norm_g, ret_decay_fwd, ret_decay_bwd, ret_gn_g, ret_gn_b, w_out, pre_ffn_g, post_ffn_g, router_w, router_b, exp_w_up, exp_b_up, exp_w_down, exp_b_down):
    b, s, d = x.shape
    rows = s // GRID_W
    row_ids = jnp.repeat(jnp.arange(rows, dtype=jnp.float32), GRID_W)
    col_ids = jnp.tile(jnp.arange(GRID_W, dtype=jnp.float32), rows)
    rope_a = axial_rope_tables(row_ids, col_ids, ATTN_HEAD_DIM)
    rope_r = axial_rope_tables(row_ids, col_ids, RET_HEAD_DIM)
    cond = jax.nn.silu(c)
    for l in range(DEPTH):
        mod = (cond @ ada_w[l] + ada_b[l])[:, None, :]
        sh_m, sc_m, g_m, sh_f, sc_f, g_f = jnp.split(mod, 6, axis=-1)
        hm = rms_norm(x, pre_mix_g[l]) * (1.0 + sc_m) + sh_m
        ym = hybrid_mixer(hm, w_in[l], q_norm_g[l], k_norm_g[l], ret_decay_fwd[l], ret_decay_bwd[l], ret_gn_g[l], ret_gn_b[l], w_out[l], rope_a, rope_r)
        x = x + g_m * rms_norm(ym, post_mix_g[l])
        hf = rms_norm(x, pre_ffn_g[l]) * (1.0 + sc_f) + sh_f
        yf = routed_experts(hf.reshape(b * s, d), router_w[l], router_b[l], exp_w_up[l], exp_b_up[l], exp_w_down[l], exp_b_down[l]).reshape(b, s, d)
        x = x + g_f * rms_norm(yf, post_ffn_g[l])
    return x
```

```python
import functools

import numpy as np
import jax
import jax.numpy as jnp
from jax import lax
from jax.experimental import pallas as pl
from jax.experimental.pallas import tpu as pltpu
from jax.experimental.pallas import tpu_sc as plsc

F32 = jnp.float32
BF16 = jnp.bfloat16
U32 = jnp.uint32
I32 = jnp.int32

D_MODEL = 1024
GRID_W = 64
ROPE_THETA = 10000.0
HEAD_DIM = 64
Q_HEADS = 8
KV_HEADS = 2
RET_HEADS = 4
RET_DIM = 128
RET_CHUNK = 128
ATTN_W = Q_HEADS * HEAD_DIM
KV_W = KV_HEADS * HEAD_DIM
RET_W = RET_HEADS * RET_DIM
IN_W = ATTN_W + 2 * KV_W + 4 * RET_W
N_EXPERTS = 32
TOP_K = 4
D_FF = D_MODEL
SWIGLU_LIMIT = 7.0
SWIGLU_ALPHA = 1.702
NORM_EPS = 1e-6
HALF = D_MODEL // 2
LANES = 128

ROW_TILE = 512
Q_TILE = 128
KV_TILE = 512
MOE_TILE = 512
VMEM_LIMIT = 48 * 1024 * 1024

SC_CORES = 2
SC_SUBCORES = 16
SC_WORKERS = SC_CORES * SC_SUBCORES
SC_CHUNK = 64


def _params(sem):
    return pltpu.CompilerParams(dimension_semantics=sem, vmem_limit_bytes=VMEM_LIMIT)


def _rms(x, g):
    return x * lax.rsqrt(jnp.mean(x * x, axis=-1, keepdims=True) + NORM_EPS) * g


def _pack_bf16_pairs(y):
    u = lax.bitcast_convert_type(y.astype(BF16).astype(F32), U32)
    return (u[:, :HALF] >> 16) | u[:, HALF:]


def _unpack_lo(w):
    return lax.bitcast_convert_type(w << 16, F32)


def _unpack_hi(w):
    return lax.bitcast_convert_type(w & jnp.uint32(0xFFFF0000), F32)


def _mod_kernel(c_ref, w_ref, b_ref, o_ref):
    c = c_ref[...]
    cond = c * jax.nn.sigmoid(c)
    o_ref[0] = jnp.dot(cond, w_ref[0], preferred_element_type=F32, precision=lax.Precision.HIGHEST) + b_ref[0]


def _modulation(c, ada_w, ada_b):
    depth, d, w6 = ada_w.shape
    b = c.shape[0]
    tn = 1536
    return pl.pallas_call(
        _mod_kernel,
        grid=(depth, w6 // tn),
        in_specs=[
            pl.BlockSpec((b, d), lambda l, j: (0, 0)),
            pl.BlockSpec((1, d, tn), lambda l, j: (l, 0, j)),
            pl.BlockSpec((1, 1, tn), lambda l, j: (l, 0, j)),
        ],
        out_specs=pl.BlockSpec((1, b, tn), lambda l, j: (l, 0, j)),
        out_shape=jax.ShapeDtypeStruct((depth, b, w6), F32),
        compiler_params=_params(("arbitrary", "arbitrary")),
        name="modulation",
    )(c, ada_w, ada_b.reshape(depth, 1, w6))


def _rope_tables(s, head_dim):
    quarter = head_dim // 4
    t = jnp.arange(s, dtype=F32)
    row = jnp.floor(t / GRID_W)
    col = t - row * GRID_W
    inv_freq = ROPE_THETA ** (-jnp.arange(quarter, dtype=F32) / quarter)
    lane = np.arange(LANES) % head_dim
    use_col = lane >= head_dim // 2
    second = (lane % (head_dim // 2)) >= quarter
    freq = inv_freq[lane % quarter]
    pos = jnp.where(use_col[None, :], col[:, None], row[:, None])
    ang = pos * freq[None, :]
    sign = jnp.where(second, 1.0, -1.0).astype(F32)
    return jnp.cos(ang), jnp.sin(ang) * sign[None, :]


def _rope(z, cos, sin_signed, quarter):
    lane = lax.broadcasted_iota(I32, z.shape, 1)
    first = (lane % (2 * quarter)) < quarter
    partner = jnp.where(first, pltpu.roll(z, LANES - quarter, 1), pltpu.roll(z, quarter, 1))
    return z * cos + partner * sin_signed


def _inproj_kernel(x_ref, mod_ref, g_ref, w_ref, ca_ref, sa_ref, cr_ref, sr_ref, qg_ref, kg_ref,
                   qa_ref, kdt_ref, vd_ref, qr_ref, kr_ref, vr_ref, gr_ref):
    x = x_ref[...]
    m = mod_ref[0]
    h = _rms(x, g_ref[...]) * (1.0 + m[1:2, :]) + m[0:1, :]
    hb = h.astype(BF16)
    tm = x.shape[0]

    def proj(c0):
        z = jnp.dot(hb, w_ref[:, c0:c0 + 2 * LANES], preferred_element_type=F32)
        return z[:, :LANES], z[:, LANES:]

    ri = lax.broadcasted_iota(I32, (LANES, LANES), 0) // HEAD_DIM
    ci = lax.broadcasted_iota(I32, (LANES, LANES), 1) // HEAD_DIM
    head_ones = (ri == ci).astype(BF16)
    ca, sa = ca_ref[...], sa_ref[...]
    cr, sr = cr_ref[...], sr_ref[...]
    lane = lax.broadcasted_iota(I32, (tm, LANES), 1)
    low = lane < HEAD_DIM

    def head_norm_rope(z, g):
        ss = jnp.dot((z * z).astype(BF16), head_ones, preferred_element_type=F32)
        zn = z * lax.rsqrt(ss * (1.0 / HEAD_DIM) + NORM_EPS) * g
        return _rope(zn, ca, sa, HEAD_DIM // 4)

    def store_pair(ref, j, fn, c0):
        for i, z in enumerate(proj(c0 + 2 * j * LANES)):
            ref[:, (2 * j + i) * LANES:(2 * j + i + 1) * LANES] = fn(z).astype(BF16)

    for j in range(ATTN_W // (2 * LANES)):
        store_pair(qa_ref, j, lambda z: head_norm_rope(z, qg_ref[...]) * (HEAD_DIM ** -0.5), 0)

    k, v = proj(ATTN_W)
    k = head_norm_rope(k, kg_ref[...])
    k_sw = pltpu.roll(k, HEAD_DIM, 1)
    kdt_ref[0, 0] = jnp.where(low, k, k_sw).T.astype(BF16)
    kdt_ref[0, 1] = jnp.where(low, k_sw, k).T.astype(BF16)
    v_sw = pltpu.roll(v, HEAD_DIM, 1)
    vd_ref[0, 0] = jnp.where(low, v, v_sw).astype(BF16)
    vd_ref[0, 1] = jnp.where(low, v_sw, v).astype(BF16)

    base = ATTN_W + 2 * KV_W
    for j in range(RET_W // (2 * LANES)):
        store_pair(qr_ref, j, lambda z: _rope(z, cr, sr, RET_DIM // 4), base)
        store_pair(kr_ref, j, lambda z: _rope(z, cr, sr, RET_DIM // 4) * (RET_DIM ** -0.5), base + RET_W)
        store_pair(vr_ref, j, lambda z: z, base + 2 * RET_W)
        store_pair(gr_ref, j, lambda z: z * jax.nn.sigmoid(z), base + 3 * RET_W)


def _inproj(x2, mod_l, pre_g, w_in_b, ropes, qg, kg, b, s):
    n = x2.shape[0]
    tm = min(ROW_TILE, s)
    nb = s // tm
    ca, sa, cr, sr = ropes
    row = lambda i: (i, 0)
    full = lambda i: (0, 0)
    rope_spec = pl.BlockSpec((tm, LANES), lambda i: (i % nb, 0))
    wide = jax.ShapeDtypeStruct((n, RET_W), BF16)
    return pl.pallas_call(
        _inproj_kernel,
        grid=(n // tm,),
        in_specs=[
            pl.BlockSpec((tm, D_MODEL), row),
            pl.BlockSpec((1, 6, D_MODEL), lambda i: (i // nb, 0, 0)),
            pl.BlockSpec((1, D_MODEL), full),
            pl.BlockSpec((D_MODEL, IN_W), full),
            rope_spec, rope_spec, rope_spec, rope_spec,
            pl.BlockSpec((1, LANES), full),
            pl.BlockSpec((1, LANES), full),
        ],
        out_specs=[
            pl.BlockSpec((tm, ATTN_W), row),
            pl.BlockSpec((1, KV_HEADS, LANES, tm), lambda i: (i // nb, 0, 0, i % nb)),
            pl.BlockSpec((1, KV_HEADS, tm, LANES), lambda i: (i // nb, 0, i % nb, 0)),
            pl.BlockSpec((tm, RET_W), row),
            pl.BlockSpec((tm, RET_W), row),
            pl.BlockSpec((tm, RET_W), row),
            pl.BlockSpec((tm, RET_W), row),
        ],
        out_shape=[
            jax.ShapeDtypeStruct((n, ATTN_W), BF16),
            jax.ShapeDtypeStruct((b, KV_HEADS, LANES, s), BF16),
            jax.ShapeDtypeStruct((b, KV_HEADS, s, LANES), BF16),
            wide, wide, wide, wide,
        ],
        compiler_params=_params(("parallel",)),
        name="inproj",
    )(x2, mod_l, pre_g, w_in_b, ca, sa, cr, sr, qg, kg)


def _attn_kernel(q_ref, kt_ref, v_ref, o_ref, m_ref, l_ref, acc_ref, *, tk):
    tq = q_ref.shape[0]
    s = v_ref.shape[2]
    group = Q_HEADS // KV_HEADS
    lane = lax.broadcasted_iota(I32, (tq, LANES), 1)
    low = lane < HEAD_DIM
    zero = jnp.zeros((tq, LANES), BF16)
    parts = []
    for p in range(group // 2):
        pair = q_ref[:, p * LANES:(p + 1) * LANES]
        parts += [jnp.where(low, pair, zero), jnp.where(low, zero, pair)]
    q4 = jnp.concatenate(parts, axis=0)

    m_ref[...] = jnp.full(m_ref.shape, -jnp.inf, F32)
    l_ref[...] = jnp.zeros(l_ref.shape, F32)
    acc_ref[...] = jnp.zeros(acc_ref.shape, F32)

    def body(j, carry):
        off = pl.multiple_of(j * tk, tk)
        kt = kt_ref[0, 0, :, pl.ds(off, tk)]
        v = v_ref[0, 0, pl.ds(off, tk), :]
        sc = jnp.dot(q4, kt, preferred_element_type=F32)
        m_old = m_ref[...]
        m_new = jnp.maximum(m_old, jnp.max(sc, axis=-1, keepdims=True))
        p = jnp.exp(sc - m_new)
        alpha = jnp.exp(m_old - m_new)
        l_ref[...] = alpha * l_ref[...] + jnp.sum(p, axis=-1, keepdims=True)
        acc_ref[...] = alpha * acc_ref[...] + jnp.dot(p.astype(BF16), v, preferred_element_type=F32)
        m_ref[...] = m_new
        return carry

    lax.fori_loop(0, s // tk, body, 0)
    o = acc_ref[...] / l_ref[...]
    for p in range(group // 2):
        even = o[(2 * p) * tq:(2 * p + 1) * tq]
        odd = o[(2 * p + 1) * tq:(2 * p + 2) * tq]
        o_ref[:, p * LANES:(p + 1) * LANES] = jnp.where(low, even, odd).astype(BF16)


def _attention(qa, kdt, vd, b, s):
    n = qa.shape[0]
    tq = min(Q_TILE, s)
    tk = min(KV_TILE, s)
    nq = s // tq
    gw = ATTN_W // KV_HEADS
    rows = (Q_HEADS // KV_HEADS) * tq
    return pl.pallas_call(
        functools.partial(_attn_kernel, tk=tk),
        grid=(b, KV_HEADS, nq),
        in_specs=[
            pl.BlockSpec((tq, gw), lambda bi, h, qi: (bi * nq + qi, h)),
            pl.BlockSpec((1, 1, LANES, s), lambda bi, h, qi: (bi, h, 0, 0)),
            pl.BlockSpec((1, 1, s, LANES), lambda bi, h, qi: (bi, h, 0, 0)),
        ],
        out_specs=pl.BlockSpec((tq, gw), lambda bi, h, qi: (bi * nq + qi, h)),
        out_shape=jax.ShapeDtypeStruct((n, ATTN_W), BF16),
        scratch_shapes=[
            pltpu.VMEM((rows, 1), F32),
            pltpu.VMEM((rows, 1), F32),
            pltpu.VMEM((rows, LANES), F32),
        ],
        compiler_params=_params(("parallel", "parallel", "arbitrary")),
        name="attention",
    )(qa, kdt, vd)


def _retention_kernel(dec_ref, q_ref, k_ref, v_ref, g_ref, gng_ref, gnb_ref, o_ref,
                      fbuf, bbuf, sf_ref, sb_ref):
    h = pl.program_id(1)
    c = RET_CHUNK
    s = q_ref.shape[1]
    nc = s // c
    lgf = jnp.full((1, 1), dec_ref[0, h], F32)
    lgb = jnp.full((1, 1), dec_ref[1, h], F32)
    ii = lax.broadcasted_iota(I32, (c, c), 0)
    jj = lax.broadcasted_iota(I32, (c, c), 1)
    diff = (ii - jj).astype(F32)
    d_f = jnp.where(diff >= 0, jnp.exp(lgf * jnp.maximum(diff, 0.0)), 0.0)
    d_b = jnp.where(diff <= 0, jnp.exp(lgb * jnp.maximum(-diff, 0.0)), 0.0)
    idx = lax.broadcasted_iota(I32, (c, 1), 0).astype(F32)
    xi_f = jnp.exp(lgf * (idx + 1.0))
    zeta_f = jnp.exp(lgf * (c - 1.0 - idx))
    xi_b = jnp.exp(lgb * (c - idx))
    zeta_b = jnp.exp(lgb * idx)
    cd_f = jnp.exp(lgf * c)
    cd_b = jnp.exp(lgb * c)
    sf_ref[...] = jnp.zeros(sf_ref.shape, F32)
    sb_ref[...] = jnp.zeros(sb_ref.shape, F32)
    nt = (((1,), (1,)), ((), ()))

    def one(ci, decay, xi, zeta, cd, st_ref, out_ref):
        off = pl.multiple_of(ci * c, c)
        q = q_ref[0, pl.ds(off, c), :]
        k = k_ref[0, pl.ds(off, c), :]
        v = v_ref[0, pl.ds(off, c), :]
        a = lax.dot_general(q, k, nt, preferred_element_type=F32) * decay
        inner = jnp.dot(a.astype(BF16), v, preferred_element_type=F32)
        st = st_ref[...]
        cross = jnp.dot(q, st.astype(BF16), preferred_element_type=F32) * xi
        out_ref[pl.ds(off, c), :] = inner + cross
        kzt = (k.astype(F32) * zeta).T.astype(BF16)
        st_ref[...] = st * cd + jnp.dot(kzt, v, preferred_element_type=F32)

    def body(ci, carry):
        one(ci, d_f, xi_f, zeta_f, cd_f, sf_ref, fbuf)
        one(nc - 1 - ci, d_b, xi_b, zeta_b, cd_b, sb_ref, bbuf)
        return carry

    lax.fori_loop(0, nc, body, 0)

    rows = min(512, s)

    def fin(t, carry):
        off = pl.multiple_of(t * rows, rows)
        r = fbuf[pl.ds(off, rows), :] + bbuf[pl.ds(off, rows), :]
        mu = jnp.mean(r, axis=-1, keepdims=True)
        d = r - mu
        var = jnp.mean(d * d, axis=-1, keepdims=True)
        y = d * lax.rsqrt(var + NORM_EPS) * gng_ref[...] + gnb_ref[...]
        o_ref[0, pl.ds(off, rows), :] = (g_ref[0, pl.ds(off, rows), :].astype(F32) * y).astype(BF16)
        return carry

    lax.fori_loop(0, s // rows, fin, 0)


def _retention(qr, kr, vr, gr, dec, gn_g, gn_b, b, s):
    shp = (b, s, RET_W)
    head = pl.BlockSpec((1, s, RET_DIM), lambda bi, h: (bi, 0, h))
    vec = pl.BlockSpec((1, RET_DIM), lambda bi, h: (0, h))
    out = pl.pallas_call(
        _retention_kernel,
        grid=(b, RET_HEADS),
        in_specs=[pl.BlockSpec(memory_space=pltpu.SMEM), head, head, head, head, vec, vec],
        out_specs=head,
        out_shape=jax.ShapeDtypeStruct(shp, BF16),
        scratch_shapes=[
            pltpu.VMEM((s, RET_DIM), F32),
            pltpu.VMEM((s, RET_DIM), F32),
            pltpu.VMEM((RET_DIM, RET_DIM), F32),
            pltpu.VMEM((RET_DIM, RET_DIM), F32),
        ],
        compiler_params=_params(("parallel", "parallel")),
        name="retention",
    )(dec, qr.reshape(shp), kr.reshape(shp), vr.reshape(shp), gr.reshape(shp), gn_g, gn_b)
    return out.reshape(b * s, RET_W)


def _outproj_kernel(oa_ref, or_ref, w_ref, x_ref, mod_ref, pg_ref, fg_ref, rw_ref, rb_ref,
                    xn_ref, hp_ref, ti_ref, tw_ref, rk_ref, cnt_ref, base_ref):
    @pl.when(pl.program_id(0) == 0)
    def _():
        base_ref[...] = jnp.zeros(base_ref.shape, F32)

    m = mod_ref[0]
    y = jnp.dot(oa_ref[...], w_ref[:ATTN_W, :], preferred_element_type=F32)
    y = y + jnp.dot(or_ref[...], w_ref[ATTN_W:, :], preferred_element_type=F32)
    xn = x_ref[...] + m[2:3, :] * _rms(y, pg_ref[...])
    xn_ref[...] = xn
    h = _rms(xn, fg_ref[...]) * (1.0 + m[4:5, :]) + m[3:4, :]
    hb = h.astype(BF16)
    u = lax.bitcast_convert_type(hb.astype(F32), U32)
    hp_ref[...] = (u[:, :HALF] >> 16) | u[:, HALF:]

    tm = hb.shape[0]
    nt = (((1,), (1,)), ((), ()))
    logits = lax.dot_general(rw_ref[...], hb, nt, preferred_element_type=F32) + rb_ref[...]
    e_iota = lax.broadcasted_iota(I32, (N_EXPERTS, tm), 0).astype(F32)
    vals, idxs = [], []
    cur = logits
    for _ in range(TOP_K):
        mx = jnp.max(cur, axis=0, keepdims=True)
        ik = jnp.min(jnp.where(cur == mx, e_iota, float(N_EXPERTS)), axis=0, keepdims=True)
        vals.append(mx)
        idxs.append(ik)
        cur = jnp.where(e_iota == ik, -jnp.inf, cur)
    v = jnp.concatenate(vals, axis=0)
    w = jnp.exp(v - v[0:1, :])
    tw_ref[...] = w / jnp.sum(w, axis=0, keepdims=True)
    ti_ref[...] = jnp.concatenate(idxs, axis=0).astype(I32)

    onehot = jnp.zeros((N_EXPERTS, tm), F32)
    for ik in idxs:
        onehot = onehot + (e_iota == ik).astype(F32)
    before = (lax.broadcasted_iota(I32, (tm, tm), 0) < lax.broadcasted_iota(I32, (tm, tm), 1)).astype(BF16)
    seen = jnp.dot(onehot.astype(BF16), before, preferred_element_type=F32) + base_ref[...]
    ranks = [jnp.sum(jnp.where(e_iota == ik, seen, 0.0), axis=0, keepdims=True) for ik in idxs]
    rk_ref[...] = jnp.concatenate(ranks, axis=0).astype(I32)
    total = base_ref[...] + jnp.sum(onehot, axis=1, keepdims=True)
    base_ref[...] = total
    cnt_ref[...] = jnp.broadcast_to(total, cnt_ref.shape).astype(I32)


def _outproj(oa, orr, w_out_b, x2, mod_l, post_g, ffn_g, rw_t, rb, b, s):
    n = x2.shape[0]
    tm = min(ROW_TILE, s)
    nb = s // tm
    row = lambda i: (i, 0)
    full = lambda i: (0, 0)
    col = lambda i: (0, i)
    return pl.pallas_call(
        _outproj_kernel,
        grid=(n // tm,),
        in_specs=[
            pl.BlockSpec((tm, ATTN_W), row),
            pl.BlockSpec((tm, RET_W), row),
            pl.BlockSpec((D_MODEL, D_MODEL), full),
            pl.BlockSpec((tm, D_MODEL), row),
            pl.BlockSpec((1, 6, D_MODEL), lambda i: (i // nb, 0, 0)),
            pl.BlockSpec((1, D_MODEL), full),
            pl.BlockSpec((1, D_MODEL), full),
            pl.BlockSpec((N_EXPERTS, D_MODEL), full),
            pl.BlockSpec((N_EXPERTS, 1), full),
        ],
        out_specs=[
            pl.BlockSpec((tm, D_MODEL), row),
            pl.BlockSpec((tm, HALF), row),
            pl.BlockSpec((TOP_K, tm), col),
            pl.BlockSpec((TOP_K, tm), col),
            pl.BlockSpec((TOP_K, tm), col),
            pl.BlockSpec((N_EXPERTS, LANES), full),
        ],
        out_shape=[
            jax.ShapeDtypeStruct((n, D_MODEL), F32),
            jax.ShapeDtypeStruct((n, HALF), U32),
            jax.ShapeDtypeStruct((TOP_K, n), I32),
            jax.ShapeDtypeStruct((TOP_K, n), F32),
            jax.ShapeDtypeStruct((TOP_K, n), I32),
            jax.ShapeDtypeStruct((N_EXPERTS, LANES), I32),
        ],
        scratch_shapes=[pltpu.VMEM((N_EXPERTS, 1), F32)],
        compiler_params=_params(("arbitrary",)),
        name="outproj_router",
    )(oa, orr, w_out_b, x2, mod_l, post_g, ffn_g, rw_t, rb)


def _sc_scatter_rows(rows, pos3, n_out):
    n, w = rows.shape
    nchunk, kk, c = pos3.shape
    per_w = nchunk // SC_WORKERS
    mesh = plsc.VectorSubcoreMesh(core_axis_name="c", subcore_axis_name="s")

    @functools.partial(
        pl.kernel, mesh=mesh,
        out_type=jax.ShapeDtypeStruct((n_out, w), rows.dtype),
        scratch_types=[pltpu.VMEM((kk, c), I32), pltpu.VMEM((c, w), rows.dtype)],
    )
    def k(rows_hbm, pos_hbm, out_hbm, idx_v, rows_v):
        wid = lax.axis_index("s") * SC_CORES + lax.axis_index("c")

        @pl.loop(0, per_w)
        def _(i):
            ch = wid * per_w + i
            pltpu.sync_copy(pos_hbm.at[ch], idx_v)
            pltpu.sync_copy(rows_hbm.at[pl.ds(ch * c, c)], rows_v)
            for j in range(kk):
                pltpu.sync_copy(rows_v, out_hbm.at[idx_v.at[j]])

    return k(rows, pos3)


def _sc_gather_rows(table, idx):
    w = table.shape[1]
    b = idx.shape[0]
    c = SC_CHUNK
    per_w = b // (SC_WORKERS * c)
    mesh = plsc.VectorSubcoreMesh(core_axis_name="c", subcore_axis_name="s")

    @functools.partial(
        pl.kernel, mesh=mesh,
        out_type=jax.ShapeDtypeStruct((b, w), table.dtype),
        scratch_types=[pltpu.VMEM((c,), I32), pltpu.VMEM((c, w), table.dtype), pltpu.SemaphoreType.DMA],
    )
    def k(table_hbm, idx_hbm, out_hbm, idx_v, rows_v, sem):
        wid = lax.axis_index("s") * SC_CORES + lax.axis_index("c")

        @pl.loop(0, per_w)
        def _(i):
            base = (wid * per_w + i) * c
            pltpu.sync_copy(idx_hbm.at[pl.ds(base, c)], idx_v)
            pltpu.async_copy(table_hbm.at[idx_v], rows_v, sem).wait()
            pltpu.sync_copy(rows_v, out_hbm.at[pl.ds(base, c)])

    return k(table, idx)


def _experts_kernel(te_ref, nv_ref, xs_ref, wu_ref, bu_ref, wd_ref, bd_ref, ys_ref):
    @pl.when(pl.program_id(0) < nv_ref[0])
    def _():
        w = xs_ref[...]
        lo = _unpack_lo(w).astype(BF16)
        hi = _unpack_hi(w).astype(BF16)
        up = jnp.dot(lo, wu_ref[0, :HALF, :], preferred_element_type=F32)
        up = up + jnp.dot(hi, wu_ref[0, HALF:, :], preferred_element_type=F32) + bu_ref[0]
        glu = jnp.minimum(up[:, :D_FF], SWIGLU_LIMIT)
        lin = jnp.clip(up[:, D_FF:], -SWIGLU_LIMIT, SWIGLU_LIMIT)
        act = glu * jax.nn.sigmoid(SWIGLU_ALPHA * glu) * (lin + 1.0)
        y = jnp.dot(act.astype(BF16), wd_ref[0], preferred_element_type=F32) + bd_ref[0]
        ys_ref[...] = _pack_bf16_pairs(y)


def _experts(xs, tile_expert, n_valid, wu, bu, wd, bd):
    n_pad = xs.shape[0]
    t = MOE_TILE
    grid_spec = pltpu.PrefetchScalarGridSpec(
        num_scalar_prefetch=2,
        grid=(n_pad // t,),
        in_specs=[
            pl.BlockSpec((t, HALF), lambda i, te, nv: (i, 0)),
            pl.BlockSpec((1, D_MODEL, 2 * D_FF), lambda i, te, nv: (te[i], 0, 0)),
            pl.BlockSpec((1, 1, 2 * D_FF), lambda i, te, nv: (te[i], 0, 0)),
            pl.BlockSpec((1, D_FF, D_MODEL), lambda i, te, nv: (te[i], 0, 0)),
            pl.BlockSpec((1, 1, D_MODEL), lambda i, te, nv: (te[i], 0, 0)),
        ],
        out_specs=pl.BlockSpec((t, HALF), lambda i, te, nv: (i, 0)),
    )
    return pl.pallas_call(
        _experts_kernel,
        grid_spec=grid_spec,
        out_shape=jax.ShapeDtypeStruct((n_pad, HALF), U32),
        compiler_params=_params(("arbitrary",)),
        name="experts",
    )(tile_expert, n_valid, xs, wu, bu, wd, bd)


def _combine_kernel(y0_ref, y1_ref, y2_ref, y3_ref, w_ref, x_ref, mod_ref, g_ref, o_ref):
    w = w_ref[...]
    lo = None
    hi = None
    for k, y_ref in enumerate((y0_ref, y1_ref, y2_ref, y3_ref)):
        word = y_ref[...]
        wk = w[:, k:k + 1]
        lo_k = wk * _unpack_lo(word)
        hi_k = wk * _unpack_hi(word)
        lo = lo_k if lo is None else lo + lo_k
        hi = hi_k if hi is None else hi + hi_k
    yf = jnp.concatenate([lo, hi], axis=1)
    m = mod_ref[0]
    o_ref[...] = x_ref[...] + m[5:6, :] * _rms(yf, g_ref[...])


def _combine(yg, tw_t, xn, mod_l, post_g, b, s):
    n = xn.shape[0]
    tm = min(ROW_TILE, s)
    nb = s // tm
    nblk = n // tm
    row = lambda i: (i, 0)
    full = lambda i: (0, 0)
    y_specs = [pl.BlockSpec((tm, HALF), functools.partial(lambda i, k: (k * nblk + i, 0), k=k)) for k in range(TOP_K)]
    return pl.pallas_call(
        _combine_kernel,
        grid=(nblk,),
        in_specs=y_specs + [
            pl.BlockSpec((tm, TOP_K), row),
            pl.BlockSpec((tm, D_MODEL), row),
            pl.BlockSpec((1, 6, D_MODEL), lambda i: (i // nb, 0, 0)),
            pl.BlockSpec((1, D_MODEL), full),
        ],
        out_specs=pl.BlockSpec((tm, D_MODEL), row),
        out_shape=jax.ShapeDtypeStruct((n, D_MODEL), F32),
        compiler_params=_params(("parallel",)),
        name="combine",
    )(yg, yg, yg, yg, tw_t, xn, mod_l, post_g)


def _route(counts, tidx, rank, n_tiles):
    t = MOE_TILE
    cnt = counts[:, 0]
    padded = ((cnt + t - 1) // t) * t
    ends = jnp.cumsum(padded)
    starts = ends - padded
    pos = starts[tidx] + rank
    tile_ends = ends // t
    n_valid = tile_ends[-1]
    tile = jnp.minimum(jnp.arange(n_tiles, dtype=I32), n_valid - 1)
    tile_expert = jnp.minimum(jnp.searchsorted(tile_ends, tile, side="right"), N_EXPERTS - 1).astype(I32)
    return pos.astype(I32), tile_expert, n_valid.reshape(1).astype(I32)


def kernel(x, c, ada_w, ada_b, pre_mix_g, post_mix_g, w_in, q_norm_g, k_norm_g, ret_decay_fwd, ret_decay_bwd, ret_gn_g, ret_gn_b, w_out, pre_ffn_g, post_ffn_g, router_w, router_b, exp_w_up, exp_b_up, exp_w_down, exp_b_down):
    b, s, d = x.shape
    depth = ada_w.shape[0]
    n = b * s
    n_asg = n * TOP_K
    n_pad = n_asg + N_EXPERTS * MOE_TILE
    n_tiles = n_pad // MOE_TILE

    mod = _modulation(c, ada_w, ada_b)
    ropes = _rope_tables(s, HEAD_DIM) + _rope_tables(s, RET_DIM)
    x2 = x.reshape(n, d)
    for l in range(depth):
        mod_l = mod[l].reshape(b, 6, d)
        qg = jnp.tile(q_norm_g[l], LANES // HEAD_DIM).reshape(1, LANES)
        kg = jnp.tile(k_norm_g[l], LANES // HEAD_DIM).reshape(1, LANES)
        qa, kdt, vd, qr, kr, vr, gr = _inproj(x2, mod_l, pre_mix_g[l].reshape(1, d), w_in[l].astype(BF16),
                                              ropes, qg, kg, b, s)
        oa = _attention(qa, kdt, vd, b, s)
        dec = jnp.stack([ret_decay_fwd[l], ret_decay_bwd[l]]).astype(F32)
        orr = _retention(qr, kr, vr, gr, dec, ret_gn_g[l].reshape(1, RET_W), ret_gn_b[l].reshape(1, RET_W), b, s)
        xn, hp, tidx, tw, rank, counts = _outproj(
            oa, orr, w_out[l].astype(BF16), x2, mod_l, post_mix_g[l].reshape(1, d), pre_ffn_g[l].reshape(1, d),
            router_w[l].T.astype(BF16), router_b[l].reshape(N_EXPERTS, 1), b, s)
        pos, tile_expert, n_valid = _route(counts, tidx, rank, n_tiles)
        pos3 = pos.reshape(TOP_K, n // SC_CHUNK, SC_CHUNK).transpose(1, 0, 2)
        xs = _sc_scatter_rows(hp, pos3, n_pad)
        ys = _experts(xs, tile_expert, n_valid, exp_w_up[l].astype(BF16), exp_b_up[l].reshape(N_EXPERTS, 1, 2 * D_FF),
                      exp_w_down[l].astype(BF16), exp_b_down[l].reshape(N_EXPERTS, 1, d))
        yg = _sc_gather_rows(ys, pos.reshape(n_asg))
        x2 = _combine(yg, tw.T, xn, mod_l, post_ffn_g[l].reshape(1, d), b, s)
    return x2.reshape(b, s, d)
```

```python
import functools

import numpy as np
import jax
import jax.numpy as jnp
from jax import lax
from jax.experimental import pallas as pl
from jax.experimental.pallas import tpu as pltpu
from jax.experimental.pallas import tpu_sc as plsc

F32 = jnp.float32
BF16 = jnp.bfloat16
U32 = jnp.uint32
I32 = jnp.int32

D_MODEL = 1024
GRID_W = 64
ROPE_THETA = 10000.0
HEAD_DIM = 64
Q_HEADS = 8
KV_HEADS = 2
RET_HEADS = 4
RET_DIM = 128
RET_CHUNK = 128
ATTN_W = Q_HEADS * HEAD_DIM
KV_W = KV_HEADS * HEAD_DIM
RET_W = RET_HEADS * RET_DIM
IN_W = ATTN_W + 2 * KV_W + 4 * RET_W
N_EXPERTS = 32
TOP_K = 4
D_FF = D_MODEL
SWIGLU_LIMIT = 7.0
SWIGLU_ALPHA = 1.702
NORM_EPS = 1e-6
Q_SCALE = HEAD_DIM ** -0.5 * float(np.log2(np.e))
HALF = D_MODEL // 2
LANES = 128

ROW_TILE = 512
Q_TILE = 128
KV_TILE = 1024
MOE_TILE = 512
VMEM_LIMIT = 48 * 1024 * 1024

SC_CORES = 2
SC_SUBCORES = 16
SC_WORKERS = SC_CORES * SC_SUBCORES
SC_CHUNK = 64


def _params(sem):
    return pltpu.CompilerParams(dimension_semantics=sem, vmem_limit_bytes=VMEM_LIMIT)


def _rms(x, g):
    return x * lax.rsqrt(jnp.mean(x * x, axis=-1, keepdims=True) + NORM_EPS) * g


def _pack_bf16_pairs(y):
    u = lax.bitcast_convert_type(y.astype(BF16).astype(F32), U32)
    return (u[:, :HALF] >> 16) | u[:, HALF:]


def _unpack_lo(w):
    return lax.bitcast_convert_type(w << 16, F32)


def _unpack_hi(w):
    return lax.bitcast_convert_type(w & jnp.uint32(0xFFFF0000), F32)


def _mod_kernel(c_ref, w_ref, b_ref, o_ref):
    c = c_ref[...]
    cond = c * jax.nn.sigmoid(c)
    o_ref[0] = jnp.dot(cond, w_ref[0], preferred_element_type=F32, precision=lax.Precision.HIGHEST) + b_ref[0]


def _modulation(c, ada_w, ada_b):
    depth, d, w6 = ada_w.shape
    b = c.shape[0]
    tn = 1536
    return pl.pallas_call(
        _mod_kernel,
        grid=(depth, w6 // tn),
        in_specs=[
            pl.BlockSpec((b, d), lambda l, j: (0, 0)),
            pl.BlockSpec((1, d, tn), lambda l, j: (l, 0, j)),
            pl.BlockSpec((1, 1, tn), lambda l, j: (l, 0, j)),
        ],
        out_specs=pl.BlockSpec((1, b, tn), lambda l, j: (l, 0, j)),
        out_shape=jax.ShapeDtypeStruct((depth, b, w6), F32),
        compiler_params=_params(("arbitrary", "arbitrary")),
        name="modulation",
    )(c, ada_w, ada_b.reshape(depth, 1, w6))


def _rope_tables(s, head_dim):
    quarter = head_dim // 4
    t = jnp.arange(s, dtype=F32)
    row = jnp.floor(t / GRID_W)
    col = t - row * GRID_W
    inv_freq = ROPE_THETA ** (-jnp.arange(quarter, dtype=F32) / quarter)
    lane = np.arange(LANES) % head_dim
    use_col = lane >= head_dim // 2
    second = (lane % (head_dim // 2)) >= quarter
    freq = inv_freq[lane % quarter]
    pos = jnp.where(use_col[None, :], col[:, None], row[:, None])
    ang = pos * freq[None, :]
    sign = jnp.where(second, 1.0, -1.0).astype(F32)
    return jnp.cos(ang), jnp.sin(ang) * sign[None, :]


def _rope(z, cos, sin_signed, quarter):
    lane = lax.broadcasted_iota(I32, z.shape, 1)
    first = (lane % (2 * quarter)) < quarter
    partner = jnp.where(first, pltpu.roll(z, LANES - quarter, 1), pltpu.roll(z, quarter, 1))
    return z * cos + partner * sin_signed


def _inproj_kernel(x_ref, mod_ref, g_ref, w_ref, ca_ref, sa_ref, cr_ref, sr_ref, qg_ref, kg_ref,
                   qa_ref, kdt_ref, vd_ref, qr_ref, kr_ref, vr_ref, gr_ref):
    x = x_ref[...]
    m = mod_ref[0]
    h = _rms(x, g_ref[...]) * (1.0 + m[1:2, :]) + m[0:1, :]
    hb = h.astype(BF16)
    tm = x.shape[0]

    def proj(c0):
        z = jnp.dot(hb, w_ref[:, c0:c0 + 2 * LANES], preferred_element_type=F32)
        return z[:, :LANES], z[:, LANES:]

    ri = lax.broadcasted_iota(I32, (LANES, LANES), 0) // HEAD_DIM
    ci = lax.broadcasted_iota(I32, (LANES, LANES), 1) // HEAD_DIM
    head_ones = (ri == ci).astype(BF16)
    ca, sa = ca_ref[...], sa_ref[...]
    cr, sr = cr_ref[...], sr_ref[...]
    lane = lax.broadcasted_iota(I32, (tm, LANES), 1)
    low = lane < HEAD_DIM

    def head_norm_rope(z, g):
        ss = jnp.dot((z * z).astype(BF16), head_ones, preferred_element_type=F32)
        zn = z * lax.rsqrt(ss * (1.0 / HEAD_DIM) + NORM_EPS) * g
        return _rope(zn, ca, sa, HEAD_DIM // 4)

    def store_pair(ref, j, fn, c0):
        for i, z in enumerate(proj(c0 + 2 * j * LANES)):
            ref[:, (2 * j + i) * LANES:(2 * j + i + 1) * LANES] = fn(z).astype(BF16)

    for j in range(ATTN_W // (2 * LANES)):
        store_pair(qa_ref, j, lambda z: head_norm_rope(z, qg_ref[...]) * Q_SCALE, 0)

    k, v = proj(ATTN_W)
    k = head_norm_rope(k, kg_ref[...])
    k_sw = pltpu.roll(k, HEAD_DIM, 1)
    kdt_ref[0, 0] = jnp.where(low, k, k_sw).T.astype(BF16)
    kdt_ref[0, 1] = jnp.where(low, k_sw, k).T.astype(BF16)
    vd_ref[0, 0] = jnp.where(low, v, 1.0).astype(BF16)
    vd_ref[0, 1] = jnp.where(low, pltpu.roll(v, HEAD_DIM, 1), 1.0).astype(BF16)

    base = ATTN_W + 2 * KV_W
    for j in range(RET_W // (2 * LANES)):
        store_pair(qr_ref, j, lambda z: _rope(z, cr, sr, RET_DIM // 4), base)
        store_pair(kr_ref, j, lambda z: _rope(z, cr, sr, RET_DIM // 4) * (RET_DIM ** -0.5), base + RET_W)
        store_pair(vr_ref, j, lambda z: z, base + 2 * RET_W)
        store_pair(gr_ref, j, lambda z: z * jax.nn.sigmoid(z), base + 3 * RET_W)


def _inproj(x2, mod_l, pre_g, w_in_b, ropes, qg, kg, b, s):
    n = x2.shape[0]
    tm = min(ROW_TILE, s)
    nb = s // tm
    ca, sa, cr, sr = ropes
    row = lambda i: (i, 0)
    full = lambda i: (0, 0)
    rope_spec = pl.BlockSpec((tm, LANES), lambda i: (i % nb, 0))
    wide = jax.ShapeDtypeStruct((n, RET_W), BF16)
    return pl.pallas_call(
        _inproj_kernel,
        grid=(n // tm,),
        in_specs=[
            pl.BlockSpec((tm, D_MODEL), row),
            pl.BlockSpec((1, 6, D_MODEL), lambda i: (i // nb, 0, 0)),
            pl.BlockSpec((1, D_MODEL), full),
            pl.BlockSpec((D_MODEL, IN_W), full),
            rope_spec, rope_spec, rope_spec, rope_spec,
            pl.BlockSpec((1, LANES), full),
            pl.BlockSpec((1, LANES), full),
        ],
        out_specs=[
            pl.BlockSpec((tm, ATTN_W), row),
            pl.BlockSpec((1, KV_HEADS, LANES, tm), lambda i: (i // nb, 0, 0, i % nb)),
            pl.BlockSpec((1, KV_HEADS, tm, LANES), lambda i: (i // nb, 0, i % nb, 0)),
            pl.BlockSpec((tm, RET_W), row),
            pl.BlockSpec((tm, RET_W), row),
            pl.BlockSpec((tm, RET_W), row),
            pl.BlockSpec((tm, RET_W), row),
        ],
        out_shape=[
            jax.ShapeDtypeStruct((n, ATTN_W), BF16),
            jax.ShapeDtypeStruct((b, KV_HEADS, LANES, s), BF16),
            jax.ShapeDtypeStruct((b, KV_HEADS, s, LANES), BF16),
            wide, wide, wide, wide,
        ],
        compiler_params=_params(("parallel",)),
        name="inproj",
    )(x2, mod_l, pre_g, w_in_b, ca, sa, cr, sr, qg, kg)


def _attn_kernel(q_ref, kt_ref, v_ref, o_ref, q_s, m_ref, acc_ref, sa_ref, sb_ref, *, tk):
    tq = q_ref.shape[0]
    s = v_ref.shape[2]
    nk = s // tk
    group = Q_HEADS // KV_HEADS
    lane = lax.broadcasted_iota(I32, (tq, LANES), 1)
    low = lane < HEAD_DIM
    zero = jnp.zeros((tq, LANES), BF16)
    for p in range(group // 2):
        pair = q_ref[:, p * LANES:(p + 1) * LANES]
        q_s[2 * p] = jnp.where(low, pair, zero)
        q_s[2 * p + 1] = jnp.where(low, zero, pair)

    m_ref[...] = jnp.full(m_ref.shape, -jnp.inf, F32)
    acc_ref[...] = jnp.zeros(acc_ref.shape, F32)

    def scores(g, kt):
        return jnp.dot(q_s[g], kt, preferred_element_type=F32)

    def key_tile(j):
        return kt_ref[0, 0, :, pl.ds(pl.multiple_of(j * tk, tk), tk)]

    def step(j, cur_ref, nxt_ref, j_next):
        v = v_ref[0, 0, pl.ds(pl.multiple_of(j * tk, tk), tk), :]
        kt = None if j_next is None else key_tile(j_next)
        if kt is not None:
            nxt_ref[0] = scores(0, kt)
        for g in range(group):
            sc = cur_ref[g]
            m_old = m_ref[g]
            m_new = jnp.maximum(m_old, jnp.max(sc, axis=-1, keepdims=True))
            p = jnp.exp2(sc - m_new)
            alpha = jnp.exp2(m_old - m_new)
            if kt is not None and g + 1 < group:
                nxt_ref[g + 1] = scores(g + 1, kt)
            acc_ref[g] = alpha * acc_ref[g] + jnp.dot(p.astype(BF16), v, preferred_element_type=F32)
            m_ref[g] = m_new

    kt0 = key_tile(0)
    for g in range(group):
        sa_ref[g] = scores(g, kt0)

    def body(i, carry):
        step(2 * i, sa_ref, sb_ref, 2 * i + 1)
        step(2 * i + 1, sb_ref, sa_ref, 2 * i + 2)
        return carry

    if nk % 2 == 0:
        lax.fori_loop(0, nk // 2 - 1, body, 0)
        step(nk - 2, sa_ref, sb_ref, nk - 1)
        step(nk - 1, sb_ref, None, None)
    else:
        lax.fori_loop(0, nk // 2, body, 0)
        step(nk - 1, sa_ref, None, None)
    outs = []
    for g in range(group):
        acc = acc_ref[g]
        outs.append(acc / pltpu.roll(acc, HEAD_DIM, 1))
    for p in range(group // 2):
        pair = jnp.where(low, outs[2 * p], pltpu.roll(outs[2 * p + 1], HEAD_DIM, 1))
        o_ref[:, p * LANES:(p + 1) * LANES] = pair.astype(BF16)


def _attention(qa, kdt, vd, b, s):
    n = qa.shape[0]
    tq = min(Q_TILE, s)
    tk = min(KV_TILE, s)
    nq = s // tq
    gw = ATTN_W // KV_HEADS
    group = Q_HEADS // KV_HEADS
    return pl.pallas_call(
        functools.partial(_attn_kernel, tk=tk),
        grid=(b, KV_HEADS, nq),
        in_specs=[
            pl.BlockSpec((tq, gw), lambda bi, h, qi: (bi * nq + qi, h)),
            pl.BlockSpec((1, 1, LANES, s), lambda bi, h, qi: (bi, h, 0, 0)),
            pl.BlockSpec((1, 1, s, LANES), lambda bi, h, qi: (bi, h, 0, 0)),
        ],
        out_specs=pl.BlockSpec((tq, gw), lambda bi, h, qi: (bi * nq + qi, h)),
        out_shape=jax.ShapeDtypeStruct((n, ATTN_W), BF16),
        scratch_shapes=[
            pltpu.VMEM((group, tq, LANES), BF16),
            pltpu.VMEM((group, tq, 1), F32),
            pltpu.VMEM((group, tq, LANES), F32),
            pltpu.VMEM((group, tq, tk), F32),
            pltpu.VMEM((group, tq, tk), F32),
        ],
        compiler_params=_params(("parallel", "parallel", "arbitrary")),
        name="attention",
    )(qa, kdt, vd)


def _retention_kernel(dec_ref, q_ref, k_ref, v_ref, g_ref, gng_ref, gnb_ref, o_ref,
                      fbuf, bbuf, sf_ref, sb_ref):
    h = pl.program_id(1)
    c = RET_CHUNK
    s = q_ref.shape[1]
    nc = s // c
    lgf = jnp.full((1, 1), dec_ref[0, h], F32)
    lgb = jnp.full((1, 1), dec_ref[1, h], F32)
    ii = lax.broadcasted_iota(I32, (c, c), 0)
    jj = lax.broadcasted_iota(I32, (c, c), 1)
    diff = (ii - jj).astype(F32)
    d_f = jnp.where(diff >= 0, jnp.exp(lgf * jnp.maximum(diff, 0.0)), 0.0)
    d_b = jnp.where(diff <= 0, jnp.exp(lgb * jnp.maximum(-diff, 0.0)), 0.0)
    idx = lax.broadcasted_iota(I32, (c, 1), 0).astype(F32)
    xi_f = jnp.exp(lgf * (idx + 1.0))
    zeta_f = jnp.exp(lgf * (c - 1.0 - idx))
    xi_b = jnp.exp(lgb * (c - idx))
    zeta_b = jnp.exp(lgb * idx)
    cd_f = jnp.exp(lgf * c)
    cd_b = jnp.exp(lgb * c)
    sf_ref[...] = jnp.zeros(sf_ref.shape, F32)
    sb_ref[...] = jnp.zeros(sb_ref.shape, F32)
    nt = (((1,), (1,)), ((), ()))

    def one(ci, decay, xi, zeta, cd, st_ref, out_ref):
        off = pl.multiple_of(ci * c, c)
        q = q_ref[0, pl.ds(off, c), :]
        k = k_ref[0, pl.ds(off, c), :]
        v = v_ref[0, pl.ds(off, c), :]
        a = lax.dot_general(q, k, nt, preferred_element_type=F32) * decay
        inner = jnp.dot(a.astype(BF16), v, preferred_element_type=F32)
        st = st_ref[...]
        cross = jnp.dot(q, st.astype(BF16), preferred_element_type=F32) * xi
        out_ref[pl.ds(off, c), :] = inner + cross
        kzt = (k.astype(F32) * zeta).T.astype(BF16)
        st_ref[...] = st * cd + jnp.dot(kzt, v, preferred_element_type=F32)

    def body(ci, carry):
        one(ci, d_f, xi_f, zeta_f, cd_f, sf_ref, fbuf)
        one(nc - 1 - ci, d_b, xi_b, zeta_b, cd_b, sb_ref, bbuf)
        return carry

    lax.fori_loop(0, nc, body, 0)

    rows = min(512, s)

    def fin(t, carry):
        off = pl.multiple_of(t * rows, rows)
        r = fbuf[pl.ds(off, rows), :] + bbuf[pl.ds(off, rows), :]
        mu = jnp.mean(r, axis=-1, keepdims=True)
        d = r - mu
        var = jnp.mean(d * d, axis=-1, keepdims=True)
        y = d * lax.rsqrt(var + NORM_EPS) * gng_ref[...] + gnb_ref[...]
        o_ref[0, pl.ds(off, rows), :] = (g_ref[0, pl.ds(off, rows), :].astype(F32) * y).astype(BF16)
        return carry

    lax.fori_loop(0, s // rows, fin, 0)


def _retention(qr, kr, vr, gr, dec, gn_g, gn_b, b, s):
    shp = (b, s, RET_W)
    head = pl.BlockSpec((1, s, RET_DIM), lambda bi, h: (bi, 0, h))
    vec = pl.BlockSpec((1, RET_DIM), lambda bi, h: (0, h))
    out = pl.pallas_call(
        _retention_kernel,
        grid=(b, RET_HEADS),
        in_specs=[pl.BlockSpec(memory_space=pltpu.SMEM), head, head, head, head, vec, vec],
        out_specs=head,
        out_shape=jax.ShapeDtypeStruct(shp, BF16),
        scratch_shapes=[
            pltpu.VMEM((s, RET_DIM), F32),
            pltpu.VMEM((s, RET_DIM), F32),
            pltpu.VMEM((RET_DIM, RET_DIM), F32),
            pltpu.VMEM((RET_DIM, RET_DIM), F32),
        ],
        compiler_params=_params(("parallel", "parallel")),
        name="retention",
    )(dec, qr.reshape(shp), kr.reshape(shp), vr.reshape(shp), gr.reshape(shp), gn_g, gn_b)
    return out.reshape(b * s, RET_W)


def _outproj_kernel(oa_ref, or_ref, w_ref, x_ref, mod_ref, pg_ref, fg_ref, rw_ref, rb_ref,
                    xn_ref, hp_ref, ti_ref, tw_ref, rk_ref, cnt_ref, base_ref):
    @pl.when(pl.program_id(0) == 0)
    def _():
        base_ref[...] = jnp.zeros(base_ref.shape, F32)

    m = mod_ref[0]
    y = jnp.dot(oa_ref[...], w_ref[:ATTN_W, :], preferred_element_type=F32)
    y = y + jnp.dot(or_ref[...], w_ref[ATTN_W:, :], preferred_element_type=F32)
    xn = x_ref[...] + m[2:3, :] * _rms(y, pg_ref[...])
    xn_ref[...] = xn
    h = _rms(xn, fg_ref[...]) * (1.0 + m[4:5, :]) + m[3:4, :]
    hb = h.astype(BF16)
    u = lax.bitcast_convert_type(hb.astype(F32), U32)
    hp_ref[...] = (u[:, :HALF] >> 16) | u[:, HALF:]

    tm = hb.shape[0]
    nt = (((1,), (1,)), ((), ()))
    logits = lax.dot_general(rw_ref[...], hb, nt, preferred_element_type=F32) + rb_ref[...]
    e_iota = lax.broadcasted_iota(I32, (N_EXPERTS, tm), 0).astype(F32)
    vals, idxs = [], []
    cur = logits
    for _ in range(TOP_K):
        mx = jnp.max(cur, axis=0, keepdims=True)
        ik = jnp.min(jnp.where(cur == mx, e_iota, float(N_EXPERTS)), axis=0, keepdims=True)
        vals.append(mx)
        idxs.append(ik)
        cur = jnp.where(e_iota == ik, -jnp.inf, cur)
    v = jnp.concatenate(vals, axis=0)
    w = jnp.exp(v - v[0:1, :])
    tw_ref[...] = w / jnp.sum(w, axis=0, keepdims=True)
    ti_ref[...] = jnp.concatenate(idxs, axis=0).astype(I32)

    onehot = jnp.zeros((N_EXPERTS, tm), F32)
    for ik in idxs:
        onehot = onehot + (e_iota == ik).astype(F32)
    before = (lax.broadcasted_iota(I32, (tm, tm), 0) < lax.broadcasted_iota(I32, (tm, tm), 1)).astype(BF16)
    seen = jnp.dot(onehot.astype(BF16), before, preferred_element_type=F32) + base_ref[...]
    ranks = [jnp.sum(jnp.where(e_iota == ik, seen, 0.0), axis=0, keepdims=True) for ik in idxs]
    rk_ref[...] = jnp.concatenate(ranks, axis=0).astype(I32)
    total = base_ref[...] + jnp.sum(onehot, axis=1, keepdims=True)
    base_ref[...] = total
    cnt_ref[...] = jnp.broadcast_to(total, cnt_ref.shape).astype(I32)


def _outproj(oa, orr, w_out_b, x2, mod_l, post_g, ffn_g, rw_t, rb, b, s):
    n = x2.shape[0]
    tm = min(ROW_TILE, s)
    nb = s // tm
    row = lambda i: (i, 0)
    full = lambda i: (0, 0)
    col = lambda i: (0, i)
    return pl.pallas_call(
        _outproj_kernel,
        grid=(n // tm,),
        in_specs=[
            pl.BlockSpec((tm, ATTN_W), row),
            pl.BlockSpec((tm, RET_W), row),
            pl.BlockSpec((D_MODEL, D_MODEL), full),
            pl.BlockSpec((tm, D_MODEL), row),
            pl.BlockSpec((1, 6, D_MODEL), lambda i: (i // nb, 0, 0)),
            pl.BlockSpec((1, D_MODEL), full),
            pl.BlockSpec((1, D_MODEL), full),
            pl.BlockSpec((N_EXPERTS, D_MODEL), full),
            pl.BlockSpec((N_EXPERTS, 1), full),
        ],
        out_specs=[
            pl.BlockSpec((tm, D_MODEL), row),
            pl.BlockSpec((tm, HALF), row),
            pl.BlockSpec((TOP_K, tm), col),
            pl.BlockSpec((TOP_K, tm), col),
            pl.BlockSpec((TOP_K, tm), col),
            pl.BlockSpec((N_EXPERTS, LANES), full),
        ],
        out_shape=[
            jax.ShapeDtypeStruct((n, D_MODEL), F32),
            jax.ShapeDtypeStruct((n, HALF), U32),
            jax.ShapeDtypeStruct((TOP_K, n), I32),
            jax.ShapeDtypeStruct((TOP_K, n), F32),
            jax.ShapeDtypeStruct((TOP_K, n), I32),
            jax.ShapeDtypeStruct((N_EXPERTS, LANES), I32),
        ],
        scratch_shapes=[pltpu.VMEM((N_EXPERTS, 1), F32)],
        compiler_params=_params(("arbitrary",)),
        name="outproj_router",
    )(oa, orr, w_out_b, x2, mod_l, post_g, ffn_g, rw_t, rb)


def _sc_scatter_rows(rows, pos3, n_out):
    n, w = rows.shape
    nchunk, kk, c = pos3.shape
    per_w = nchunk // SC_WORKERS
    mesh = plsc.VectorSubcoreMesh(core_axis_name="c", subcore_axis_name="s")

    @functools.partial(
        pl.kernel, mesh=mesh,
        out_type=jax.ShapeDtypeStruct((n_out, w), rows.dtype),
        scratch_types=[pltpu.VMEM((kk, c), I32), pltpu.VMEM((c, w), rows.dtype)],
    )
    def k(rows_hbm, pos_hbm, out_hbm, idx_v, rows_v):
        wid = lax.axis_index("s") * SC_CORES + lax.axis_index("c")

        @pl.loop(0, per_w)
        def _(i):
            ch = wid * per_w + i
            pltpu.sync_copy(pos_hbm.at[ch], idx_v)
            pltpu.sync_copy(rows_hbm.at[pl.ds(ch * c, c)], rows_v)
            for j in range(kk):
                pltpu.sync_copy(rows_v, out_hbm.at[idx_v.at[j]])

    return k(rows, pos3)


def _sc_gather_rows(table, idx):
    w = table.shape[1]
    b = idx.shape[0]
    c = SC_CHUNK
    per_w = b // (SC_WORKERS * c)
    mesh = plsc.VectorSubcoreMesh(core_axis_name="c", subcore_axis_name="s")

    @functools.partial(
        pl.kernel, mesh=mesh,
        out_type=jax.ShapeDtypeStruct((b, w), table.dtype),
        scratch_types=[pltpu.VMEM((c,), I32), pltpu.VMEM((c, w), table.dtype), pltpu.SemaphoreType.DMA],
    )
    def k(table_hbm, idx_hbm, out_hbm, idx_v, rows_v, sem):
        wid = lax.axis_index("s") * SC_CORES + lax.axis_index("c")

        @pl.loop(0, per_w)
        def _(i):
            base = (wid * per_w + i) * c
            pltpu.sync_copy(idx_hbm.at[pl.ds(base, c)], idx_v)
            pltpu.async_copy(table_hbm.at[idx_v], rows_v, sem).wait()
            pltpu.sync_copy(rows_v, out_hbm.at[pl.ds(base, c)])

    return k(table, idx)


def _experts_kernel(te_ref, nv_ref, xs_ref, wu_ref, bu_ref, wd_ref, bd_ref, ys_ref):
    @pl.when(pl.program_id(0) < nv_ref[0])
    def _():
        w = xs_ref[...]
        lo = _unpack_lo(w).astype(BF16)
        hi = _unpack_hi(w).astype(BF16)
        up = jnp.dot(lo, wu_ref[0, :HALF, :], preferred_element_type=F32)
        up = up + jnp.dot(hi, wu_ref[0, HALF:, :], preferred_element_type=F32) + bu_ref[0]
        glu = jnp.minimum(up[:, :D_FF], SWIGLU_LIMIT)
        lin = jnp.clip(up[:, D_FF:], -SWIGLU_LIMIT, SWIGLU_LIMIT)
        act = glu * jax.nn.sigmoid(SWIGLU_ALPHA * glu) * (lin + 1.0)
        y = jnp.dot(act.astype(BF16), wd_ref[0], preferred_element_type=F32) + bd_ref[0]
        ys_ref[...] = _pack_bf16_pairs(y)


def _experts(xs, tile_expert, n_valid, wu, bu, wd, bd):
    n_pad = xs.shape[0]
    t = MOE_TILE
    grid_spec = pltpu.PrefetchScalarGridSpec(
        num_scalar_prefetch=2,
        grid=(n_pad // t,),
        in_specs=[
            pl.BlockSpec((t, HALF), lambda i, te, nv: (i, 0)),
            pl.BlockSpec((1, D_MODEL, 2 * D_FF), lambda i, te, nv: (te[i], 0, 0)),
            pl.BlockSpec((1, 1, 2 * D_FF), lambda i, te, nv: (te[i], 0, 0)),
            pl.BlockSpec((1, D_FF, D_MODEL), lambda i, te, nv: (te[i], 0, 0)),
            pl.BlockSpec((1, 1, D_MODEL), lambda i, te, nv: (te[i], 0, 0)),
        ],
        out_specs=pl.BlockSpec((t, HALF), lambda i, te, nv: (i, 0)),
    )
    return pl.pallas_call(
        _experts_kernel,
        grid_spec=grid_spec,
        out_shape=jax.ShapeDtypeStruct((n_pad, HALF), U32),
        compiler_params=_params(("arbitrary",)),
        name="experts",
    )(tile_expert, n_valid, xs, wu, bu, wd, bd)


def _combine_kernel(y0_ref, y1_ref, y2_ref, y3_ref, w_ref, x_ref, mod_ref, g_ref, o_ref):
    w = w_ref[...]
    lo = None
    hi = None
    for k, y_ref in enumerate((y0_ref, y1_ref, y2_ref, y3_ref)):
        word = y_ref[...]
        wk = w[:, k:k + 1]
        lo_k = wk * _unpack_lo(word)
        hi_k = wk * _unpack_hi(word)
        lo = lo_k if lo is None else lo + lo_k
        hi = hi_k if hi is None else hi + hi_k
    yf = jnp.concatenate([lo, hi], axis=1)
    m = mod_ref[0]
    o_ref[...] = x_ref[...] + m[5:6, :] * _rms(yf, g_ref[...])


def _combine(yg, tw_t, xn, mod_l, post_g, b, s):
    n = xn.shape[0]
    tm = min(ROW_TILE, s)
    nb = s // tm
    nblk = n // tm
    row = lambda i: (i, 0)
    full = lambda i: (0, 0)
    y_specs = [pl.BlockSpec((tm, HALF), functools.partial(lambda i, k: (k * nblk + i, 0), k=k)) for k in range(TOP_K)]
    return pl.pallas_call(
        _combine_kernel,
        grid=(nblk,),
        in_specs=y_specs + [
            pl.BlockSpec((tm, TOP_K), row),
            pl.BlockSpec((tm, D_MODEL), row),
            pl.BlockSpec((1, 6, D_MODEL), lambda i: (i // nb, 0, 0)),
            pl.BlockSpec((1, D_MODEL), full),
        ],
        out_specs=pl.BlockSpec((tm, D_MODEL), row),
        out_shape=jax.ShapeDtypeStruct((n, D_MODEL), F32),
        compiler_params=_params(("parallel",)),
        name="combine",
    )(yg, yg, yg, yg, tw_t, xn, mod_l, post_g)


def _route(counts, tidx, rank, n_tiles):
    t = MOE_TILE
    cnt = counts[:, 0]
    padded = ((cnt + t - 1) // t) * t
    ends = jnp.cumsum(padded)
    starts = ends - padded
    experts = jnp.arange(N_EXPERTS, dtype=I32)[:, None, None]
    start_of = jnp.sum(jnp.where(tidx[None] == experts, starts[:, None, None], 0), axis=0)
    pos = start_of + rank
    tile_ends = ends // t
    n_valid = tile_ends[-1]
    tile = jnp.minimum(jnp.arange(n_tiles, dtype=I32), n_valid - 1)
    tile_expert = jnp.minimum(jnp.searchsorted(tile_ends, tile, side="right"), N_EXPERTS - 1).astype(I32)
    return pos.astype(I32), tile_expert, n_valid.reshape(1).astype(I32)


def kernel(x, c, ada_w, ada_b, pre_mix_g, post_mix_g, w_in, q_norm_g, k_norm_g, ret_decay_fwd, ret_decay_bwd, ret_gn_g, ret_gn_b, w_out, pre_ffn_g, post_ffn_g, router_w, router_b, exp_w_up, exp_b_up, exp_w_down, exp_b_down):
    b, s, d = x.shape
    depth = ada_w.shape[0]
    n = b * s
    n_asg = n * TOP_K
    n_pad = n_asg + N_EXPERTS * MOE_TILE
    n_tiles = n_pad // MOE_TILE

    mod = _modulation(c, ada_w, ada_b)
    ropes = _rope_tables(s, HEAD_DIM) + _rope_tables(s, RET_DIM)
    x2 = x.reshape(n, d)
    for l in range(depth):
        mod_l = mod[l].reshape(b, 6, d)
        qg = jnp.tile(q_norm_g[l], LANES // HEAD_DIM).reshape(1, LANES)
        kg = jnp.tile(k_norm_g[l], LANES // HEAD_DIM).reshape(1, LANES)
        qa, kdt, vd, qr, kr, vr, gr = _inproj(x2, mod_l, pre_mix_g[l].reshape(1, d), w_in[l].astype(BF16),
                                              ropes, qg, kg, b, s)
        oa = _attention(qa, kdt, vd, b, s)
        dec = jnp.stack([ret_decay_fwd[l], ret_decay_bwd[l]]).astype(F32)
        orr = _retention(qr, kr, vr, gr, dec, ret_gn_g[l].reshape(1, RET_W), ret_gn_b[l].reshape(1, RET_W), b, s)
        xn, hp, tidx, tw, rank, counts = _outproj(
            oa, orr, w_out[l].astype(BF16), x2, mod_l, post_mix_g[l].reshape(1, d), pre_ffn_g[l].reshape(1, d),
            router_w[l].T.astype(BF16), router_b[l].reshape(N_EXPERTS, 1), b, s)
        pos, tile_expert, n_valid = _route(counts, tidx, rank, n_tiles)
        pos3 = pos.reshape(TOP_K, n // SC_CHUNK, SC_CHUNK).transpose(1, 0, 2)
        xs = _sc_scatter_rows(hp, pos3, n_pad)
        ys = _experts(xs, tile_expert, n_valid, exp_w_up[l].astype(BF16), exp_b_up[l].reshape(N_EXPERTS, 1, 2 * D_FF),
                      exp_w_down[l].astype(BF16), exp_b_down[l].reshape(N_EXPERTS, 1, d))
        yg = _sc_gather_rows(ys, pos.reshape(n_asg))
        x2 = _combine(yg, tw.T, xn, mod_l, post_ffn_g[l].reshape(1, d), b, s)
    return x2.reshape(b, s, d)
```

```python
import functools

import numpy as np
import jax
import jax.numpy as jnp
from jax import lax
from jax.experimental import pallas as pl
from jax.experimental.pallas import tpu as pltpu
from jax.experimental.pallas import tpu_sc as plsc

F32 = jnp.float32
BF16 = jnp.bfloat16
U32 = jnp.uint32
I32 = jnp.int32

D_MODEL = 1024
GRID_W = 64
ROPE_THETA = 10000.0
HEAD_DIM = 64
Q_HEADS = 8
KV_HEADS = 2
RET_HEADS = 4
RET_DIM = 128
RET_CHUNK = 128
ATTN_W = Q_HEADS * HEAD_DIM
KV_W = KV_HEADS * HEAD_DIM
RET_W = RET_HEADS * RET_DIM
IN_W = ATTN_W + 2 * KV_W + 4 * RET_W
N_EXPERTS = 32
TOP_K = 4
D_FF = D_MODEL
SWIGLU_LIMIT = 7.0
SWIGLU_ALPHA = 1.702
NORM_EPS = 1e-6
Q_SCALE = HEAD_DIM ** -0.5 * float(np.log2(np.e))
HALF = D_MODEL // 2
LANES = 128

ROW_TILE = 512
Q_TILE = 128
KV_TILE = 1024
MOE_TILE = 512
VMEM_LIMIT = 48 * 1024 * 1024

SC_CORES = 2
SC_SUBCORES = 16
SC_WORKERS = SC_CORES * SC_SUBCORES
SC_CHUNK = 64


def _params(sem):
    return pltpu.CompilerParams(dimension_semantics=sem, vmem_limit_bytes=VMEM_LIMIT)


def _rms(x, g):
    return x * lax.rsqrt(jnp.mean(x * x, axis=-1, keepdims=True) + NORM_EPS) * g


def _pack_bf16_pairs(y):
    u = lax.bitcast_convert_type(y.astype(BF16).astype(F32), U32)
    return (u[:, :HALF] >> 16) | u[:, HALF:]


def _unpack_lo(w):
    return lax.bitcast_convert_type(w << 16, F32)


def _unpack_hi(w):
    return lax.bitcast_convert_type(w & jnp.uint32(0xFFFF0000), F32)


def _mod_kernel(c_ref, w_ref, b_ref, o_ref):
    c = c_ref[...]
    cond = c * jax.nn.sigmoid(c)
    o_ref[0] = jnp.dot(cond, w_ref[0], preferred_element_type=F32, precision=lax.Precision.HIGHEST) + b_ref[0]


def _modulation(c, ada_w, ada_b):
    depth, d, w6 = ada_w.shape
    b = c.shape[0]
    tn = 1536
    return pl.pallas_call(
        _mod_kernel,
        grid=(depth, w6 // tn),
        in_specs=[
            pl.BlockSpec((b, d), lambda l, j: (0, 0)),
            pl.BlockSpec((1, d, tn), lambda l, j: (l, 0, j)),
            pl.BlockSpec((1, 1, tn), lambda l, j: (l, 0, j)),
        ],
        out_specs=pl.BlockSpec((1, b, tn), lambda l, j: (l, 0, j)),
        out_shape=jax.ShapeDtypeStruct((depth, b, w6), F32),
        compiler_params=_params(("arbitrary", "arbitrary")),
        name="modulation",
    )(c, ada_w, ada_b.reshape(depth, 1, w6))


def _rope_tables(s, head_dim):
    quarter = head_dim // 4
    t = jnp.arange(s, dtype=F32)
    row = jnp.floor(t / GRID_W)
    col = t - row * GRID_W
    inv_freq = ROPE_THETA ** (-jnp.arange(quarter, dtype=F32) / quarter)
    lane = np.arange(LANES) % head_dim
    use_col = lane >= head_dim // 2
    second = (lane % (head_dim // 2)) >= quarter
    freq = inv_freq[lane % quarter]
    pos = jnp.where(use_col[None, :], col[:, None], row[:, None])
    ang = pos * freq[None, :]
    sign = jnp.where(second, 1.0, -1.0).astype(F32)
    return jnp.cos(ang), jnp.sin(ang) * sign[None, :]


def _rope(z, cos, sin_signed, quarter):
    lane = lax.broadcasted_iota(I32, z.shape, 1)
    first = (lane % (2 * quarter)) < quarter
    partner = jnp.where(first, pltpu.roll(z, LANES - quarter, 1), pltpu.roll(z, quarter, 1))
    return z * cos + partner * sin_signed


def _inproj_kernel(x_ref, mod_ref, g_ref, w_ref, ca_ref, sa_ref, cr_ref, sr_ref, qg_ref, kg_ref,
                   qa_ref, kdt_ref, vd_ref, qr_ref, kr_ref, vr_ref, gr_ref):
    x = x_ref[...]
    m = mod_ref[0]
    h = _rms(x, g_ref[...]) * (1.0 + m[1:2, :]) + m[0:1, :]
    hb = h.astype(BF16)
    tm = x.shape[0]

    def proj(c0):
        z = jnp.dot(hb, w_ref[:, c0:c0 + 2 * LANES], preferred_element_type=F32)
        return z[:, :LANES], z[:, LANES:]

    ri = lax.broadcasted_iota(I32, (LANES, LANES), 0) // HEAD_DIM
    ci = lax.broadcasted_iota(I32, (LANES, LANES), 1) // HEAD_DIM
    head_ones = (ri == ci).astype(BF16)
    ca, sa = ca_ref[...], sa_ref[...]
    cr, sr = cr_ref[...], sr_ref[...]
    lane = lax.broadcasted_iota(I32, (tm, LANES), 1)
    low = lane < HEAD_DIM

    def head_norm_rope(z, g):
        ss = jnp.dot((z * z).astype(BF16), head_ones, preferred_element_type=F32)
        zn = z * lax.rsqrt(ss * (1.0 / HEAD_DIM) + NORM_EPS) * g
        return _rope(zn, ca, sa, HEAD_DIM // 4)

    def store_pair(ref, j, fn, c0):
        for i, z in enumerate(proj(c0 + 2 * j * LANES)):
            ref[:, (2 * j + i) * LANES:(2 * j + i + 1) * LANES] = fn(z).astype(BF16)

    for j in range(ATTN_W // (2 * LANES)):
        store_pair(qa_ref, j, lambda z: head_norm_rope(z, qg_ref[...]) * Q_SCALE, 0)

    k, v = proj(ATTN_W)
    k = head_norm_rope(k, kg_ref[...])
    k_sw = pltpu.roll(k, HEAD_DIM, 1)
    kdt_ref[0, 0] = jnp.where(low, k, k_sw).T.astype(BF16)
    kdt_ref[0, 1] = jnp.where(low, k_sw, k).T.astype(BF16)
    vd_ref[0, 0] = jnp.where(low, v, 1.0).astype(BF16)
    vd_ref[0, 1] = jnp.where(low, pltpu.roll(v, HEAD_DIM, 1), 1.0).astype(BF16)

    base = ATTN_W + 2 * KV_W
    for j in range(RET_W // (2 * LANES)):
        store_pair(qr_ref, j, lambda z: _rope(z, cr, sr, RET_DIM // 4), base)
        store_pair(kr_ref, j, lambda z: _rope(z, cr, sr, RET_DIM // 4) * (RET_DIM ** -0.5), base + RET_W)
        store_pair(vr_ref, j, lambda z: z, base + 2 * RET_W)
        store_pair(gr_ref, j, lambda z: z * jax.nn.sigmoid(z), base + 3 * RET_W)


def _inproj(x2, mod_l, pre_g, w_in_b, ropes, qg, kg, b, s):
    n = x2.shape[0]
    tm = min(ROW_TILE, s)
    nb = s // tm
    ca, sa, cr, sr = ropes
    row = lambda i: (i, 0)
    full = lambda i: (0, 0)
    rope_spec = pl.BlockSpec((tm, LANES), lambda i: (i % nb, 0))
    wide = jax.ShapeDtypeStruct((n, RET_W), BF16)
    return pl.pallas_call(
        _inproj_kernel,
        grid=(n // tm,),
        in_specs=[
            pl.BlockSpec((tm, D_MODEL), row),
            pl.BlockSpec((1, 6, D_MODEL), lambda i: (i // nb, 0, 0)),
            pl.BlockSpec((1, D_MODEL), full),
            pl.BlockSpec((D_MODEL, IN_W), full),
            rope_spec, rope_spec, rope_spec, rope_spec,
            pl.BlockSpec((1, LANES), full),
            pl.BlockSpec((1, LANES), full),
        ],
        out_specs=[
            pl.BlockSpec((tm, ATTN_W), row),
            pl.BlockSpec((1, KV_HEADS, LANES, tm), lambda i: (i // nb, 0, 0, i % nb)),
            pl.BlockSpec((1, KV_HEADS, tm, LANES), lambda i: (i // nb, 0, i % nb, 0)),
            pl.BlockSpec((tm, RET_W), row),
            pl.BlockSpec((tm, RET_W), row),
            pl.BlockSpec((tm, RET_W), row),
            pl.BlockSpec((tm, RET_W), row),
        ],
        out_shape=[
            jax.ShapeDtypeStruct((n, ATTN_W), BF16),
            jax.ShapeDtypeStruct((b, KV_HEADS, LANES, s), BF16),
            jax.ShapeDtypeStruct((b, KV_HEADS, s, LANES), BF16),
            wide, wide, wide, wide,
        ],
        compiler_params=_params(("parallel",)),
        name="inproj",
    )(x2, mod_l, pre_g, w_in_b, ca, sa, cr, sr, qg, kg)


def _attn_kernel(q_ref, kt_ref, v_ref, o_ref, q_s, qn_s, m_ref, acc_ref, sa_ref, sb_ref, *, tq, tk):
    s = v_ref.shape[2]
    nk = s // tk
    nq = s // tq
    group = Q_HEADS // KV_HEADS
    lane = lax.broadcasted_iota(I32, (tq, LANES), 1)
    low = lane < HEAD_DIM
    zero = jnp.zeros((tq, LANES), BF16)

    def load_q(qi, dst):
        rows = pl.ds(pl.multiple_of(qi * tq, tq), tq)
        for p in range(group // 2):
            pair = q_ref[rows, p * LANES:(p + 1) * LANES]
            dst[2 * p] = jnp.where(low, pair, zero)
            dst[2 * p + 1] = jnp.where(low, zero, pair)

    def key_tile(j):
        return kt_ref[0, 0, :, j * tk:(j + 1) * tk]

    def step(j, cur_ref, nxt_ref, q_next, j_next):
        v = v_ref[0, 0, j * tk:(j + 1) * tk, :]
        kt = key_tile(j_next)
        nxt_ref[0] = jnp.dot(q_next[0], kt, preferred_element_type=F32)
        for g in range(group):
            sc = cur_ref[g]
            m_old = m_ref[g]
            m_new = jnp.maximum(m_old, jnp.max(sc, axis=-1, keepdims=True))
            p = jnp.exp2(sc - m_new)
            alpha = jnp.exp2(m_old - m_new)
            if g + 1 < group:
                nxt_ref[g + 1] = jnp.dot(q_next[g + 1], kt, preferred_element_type=F32)
            acc_ref[g] = alpha * acc_ref[g] + jnp.dot(p.astype(BF16), v, preferred_element_type=F32)
            m_ref[g] = m_new

    load_q(0, q_s)
    for g in range(group):
        sa_ref[g] = jnp.dot(q_s[g], key_tile(0), preferred_element_type=F32)

    def body(qi, carry):
        load_q(qi, q_s)
        load_q(jnp.minimum(qi + 1, nq - 1), qn_s)
        m_ref[...] = jnp.full(m_ref.shape, -jnp.inf, F32)
        acc_ref[...] = jnp.zeros(acc_ref.shape, F32)
        bufs = (sa_ref, sb_ref)
        for j in range(nk):
            last = j + 1 == nk
            step(j, bufs[j % 2], bufs[(j + 1) % 2], qn_s if last else q_s, 0 if last else j + 1)
        outs = []
        for g in range(group):
            acc = acc_ref[g]
            outs.append(acc / pltpu.roll(acc, HEAD_DIM, 1))
        rows = pl.ds(pl.multiple_of(qi * tq, tq), tq)
        for p in range(group // 2):
            pair = jnp.where(low, outs[2 * p], pltpu.roll(outs[2 * p + 1], HEAD_DIM, 1))
            o_ref[rows, p * LANES:(p + 1) * LANES] = pair.astype(BF16)
        return carry

    lax.fori_loop(0, nq, body, 0)


def _attention(qa, kdt, vd, b, s):
    n = qa.shape[0]
    tq = min(Q_TILE, s)
    tk = min(KV_TILE, s // 2)
    assert s % tq == 0 and s % (2 * tk) == 0 and tk % LANES == 0, (s, tq, tk)
    gw = ATTN_W // KV_HEADS
    group = Q_HEADS // KV_HEADS
    return pl.pallas_call(
        functools.partial(_attn_kernel, tq=tq, tk=tk),
        grid=(b, KV_HEADS),
        in_specs=[
            pl.BlockSpec((s, gw), lambda bi, h: (bi, h)),
            pl.BlockSpec((1, 1, LANES, s), lambda bi, h: (bi, h, 0, 0)),
            pl.BlockSpec((1, 1, s, LANES), lambda bi, h: (bi, h, 0, 0)),
        ],
        out_specs=pl.BlockSpec((s, gw), lambda bi, h: (bi, h)),
        out_shape=jax.ShapeDtypeStruct((n, ATTN_W), BF16),
        scratch_shapes=[
            pltpu.VMEM((group, tq, LANES), BF16),
            pltpu.VMEM((group, tq, LANES), BF16),
            pltpu.VMEM((group, tq, 1), F32),
            pltpu.VMEM((group, tq, LANES), F32),
            pltpu.VMEM((group, tq, tk), F32),
            pltpu.VMEM((group, tq, tk), F32),
        ],
        compiler_params=_params(("parallel", "parallel")),
        name="attention",
    )(qa, kdt, vd)


def _retention_kernel(dec_ref, q_ref, k_ref, v_ref, g_ref, gng_ref, gnb_ref, o_ref,
                      fbuf, bbuf, sf_ref, sb_ref):
    h = pl.program_id(1)
    c = RET_CHUNK
    s = q_ref.shape[1]
    nc = s // c
    lgf = jnp.full((1, 1), dec_ref[0, h], F32)
    lgb = jnp.full((1, 1), dec_ref[1, h], F32)
    ii = lax.broadcasted_iota(I32, (c, c), 0)
    jj = lax.broadcasted_iota(I32, (c, c), 1)
    diff = (ii - jj).astype(F32)
    d_f = jnp.where(diff >= 0, jnp.exp(lgf * jnp.maximum(diff, 0.0)), 0.0)
    d_b = jnp.where(diff <= 0, jnp.exp(lgb * jnp.maximum(-diff, 0.0)), 0.0)
    idx = lax.broadcasted_iota(I32, (c, 1), 0).astype(F32)
    xi_f = jnp.exp(lgf * (idx + 1.0))
    zeta_f = jnp.exp(lgf * (c - 1.0 - idx))
    xi_b = jnp.exp(lgb * (c - idx))
    zeta_b = jnp.exp(lgb * idx)
    cd_f = jnp.exp(lgf * c)
    cd_b = jnp.exp(lgb * c)
    sf_ref[...] = jnp.zeros(sf_ref.shape, F32)
    sb_ref[...] = jnp.zeros(sb_ref.shape, F32)
    nt = (((1,), (1,)), ((), ()))

    def one(ci, decay, xi, zeta, cd, st_ref, out_ref):
        off = pl.multiple_of(ci * c, c)
        q = q_ref[0, pl.ds(off, c), :]
        k = k_ref[0, pl.ds(off, c), :]
        v = v_ref[0, pl.ds(off, c), :]
        a = lax.dot_general(q, k, nt, preferred_element_type=F32) * decay
        inner = jnp.dot(a.astype(BF16), v, preferred_element_type=F32)
        st = st_ref[...]
        cross = jnp.dot(q, st.astype(BF16), preferred_element_type=F32) * xi
        out_ref[pl.ds(off, c), :] = inner + cross
        kzt = (k.astype(F32) * zeta).T.astype(BF16)
        st_ref[...] = st * cd + jnp.dot(kzt, v, preferred_element_type=F32)

    def body(ci, carry):
        one(ci, d_f, xi_f, zeta_f, cd_f, sf_ref, fbuf)
        one(nc - 1 - ci, d_b, xi_b, zeta_b, cd_b, sb_ref, bbuf)
        return carry

    lax.fori_loop(0, nc, body, 0)

    rows = min(512, s)

    def fin(t, carry):
        off = pl.multiple_of(t * rows, rows)
        r = fbuf[pl.ds(off, rows), :] + bbuf[pl.ds(off, rows), :]
        mu = jnp.mean(r, axis=-1, keepdims=True)
        d = r - mu
        var = jnp.mean(d * d, axis=-1, keepdims=True)
        y = d * lax.rsqrt(var + NORM_EPS) * gng_ref[...] + gnb_ref[...]
        o_ref[0, pl.ds(off, rows), :] = (g_ref[0, pl.ds(off, rows), :].astype(F32) * y).astype(BF16)
        return carry

    lax.fori_loop(0, s // rows, fin, 0)


def _retention(qr, kr, vr, gr, dec, gn_g, gn_b, b, s):
    shp = (b, s, RET_W)
    head = pl.BlockSpec((1, s, RET_DIM), lambda bi, h: (bi, 0, h))
    vec = pl.BlockSpec((1, RET_DIM), lambda bi, h: (0, h))
    out = pl.pallas_call(
        _retention_kernel,
        grid=(b, RET_HEADS),
        in_specs=[pl.BlockSpec(memory_space=pltpu.SMEM), head, head, head, head, vec, vec],
        out_specs=head,
        out_shape=jax.ShapeDtypeStruct(shp, BF16),
        scratch_shapes=[
            pltpu.VMEM((s, RET_DIM), F32),
            pltpu.VMEM((s, RET_DIM), F32),
            pltpu.VMEM((RET_DIM, RET_DIM), F32),
            pltpu.VMEM((RET_DIM, RET_DIM), F32),
        ],
        compiler_params=_params(("parallel", "parallel")),
        name="retention",
    )(dec, qr.reshape(shp), kr.reshape(shp), vr.reshape(shp), gr.reshape(shp), gn_g, gn_b)
    return out.reshape(b * s, RET_W)


def _outproj_kernel(oa_ref, or_ref, w_ref, x_ref, mod_ref, pg_ref, fg_ref, rw_ref, rb_ref,
                    xn_ref, hp_ref, ti_ref, tw_ref, rk_ref, cnt_ref, base_ref):
    @pl.when(pl.program_id(0) == 0)
    def _():
        base_ref[...] = jnp.zeros(base_ref.shape, F32)

    m = mod_ref[0]
    y = jnp.dot(oa_ref[...], w_ref[:ATTN_W, :], preferred_element_type=F32)
    y = y + jnp.dot(or_ref[...], w_ref[ATTN_W:, :], preferred_element_type=F32)
    xn = x_ref[...] + m[2:3, :] * _rms(y, pg_ref[...])
    xn_ref[...] = xn
    h = _rms(xn, fg_ref[...]) * (1.0 + m[4:5, :]) + m[3:4, :]
    hb = h.astype(BF16)
    u = lax.bitcast_convert_type(hb.astype(F32), U32)
    hp_ref[...] = (u[:, :HALF] >> 16) | u[:, HALF:]

    tm = hb.shape[0]
    nt = (((1,), (1,)), ((), ()))
    logits = lax.dot_general(rw_ref[...], hb, nt, preferred_element_type=F32) + rb_ref[...]
    e_iota = lax.broadcasted_iota(I32, (N_EXPERTS, tm), 0).astype(F32)
    vals, idxs = [], []
    cur = logits
    for _ in range(TOP_K):
        mx = jnp.max(cur, axis=0, keepdims=True)
        ik = jnp.min(jnp.where(cur == mx, e_iota, float(N_EXPERTS)), axis=0, keepdims=True)
        vals.append(mx)
        idxs.append(ik)
        cur = jnp.where(e_iota == ik, -jnp.inf, cur)
    v = jnp.concatenate(vals, axis=0)
    w = jnp.exp(v - v[0:1, :])
    tw_ref[...] = w / jnp.sum(w, axis=0, keepdims=True)
    ti_ref[...] = jnp.concatenate(idxs, axis=0).astype(I32)

    onehot = jnp.zeros((N_EXPERTS, tm), F32)
    for ik in idxs:
        onehot = onehot + (e_iota == ik).astype(F32)
    before = (lax.broadcasted_iota(I32, (tm, tm), 0) < lax.broadcasted_iota(I32, (tm, tm), 1)).astype(BF16)
    seen = jnp.dot(onehot.astype(BF16), before, preferred_element_type=F32) + base_ref[...]
    ranks = [jnp.sum(jnp.where(e_iota == ik, seen, 0.0), axis=0, keepdims=True) for ik in idxs]
    rk_ref[...] = jnp.concatenate(ranks, axis=0).astype(I32)
    total = base_ref[...] + jnp.sum(onehot, axis=1, keepdims=True)
    base_ref[...] = total
    cnt_ref[...] = jnp.broadcast_to(total, cnt_ref.shape).astype(I32)


def _outproj(oa, orr, w_out_b, x2, mod_l, post_g, ffn_g, rw_t, rb, b, s):
    n = x2.shape[0]
    tm = min(ROW_TILE, s)
    nb = s // tm
    row = lambda i: (i, 0)
    full = lambda i: (0, 0)
    col = lambda i: (0, i)
    return pl.pallas_call(
        _outproj_kernel,
        grid=(n // tm,),
        in_specs=[
            pl.BlockSpec((tm, ATTN_W), row),
            pl.BlockSpec((tm, RET_W), row),
            pl.BlockSpec((D_MODEL, D_MODEL), full),
            pl.BlockSpec((tm, D_MODEL), row),
            pl.BlockSpec((1, 6, D_MODEL), lambda i: (i // nb, 0, 0)),
            pl.BlockSpec((1, D_MODEL), full),
            pl.BlockSpec((1, D_MODEL), full),
            pl.BlockSpec((N_EXPERTS, D_MODEL), full),
            pl.BlockSpec((N_EXPERTS, 1), full),
        ],
        out_specs=[
            pl.BlockSpec((tm, D_MODEL), row),
            pl.BlockSpec((tm, HALF), row),
            pl.BlockSpec((TOP_K, tm), col),
            pl.BlockSpec((TOP_K, tm), col),
            pl.BlockSpec((TOP_K, tm), col),
            pl.BlockSpec((N_EXPERTS, LANES), full),
        ],
        out_shape=[
            jax.ShapeDtypeStruct((n, D_MODEL), F32),
            jax.ShapeDtypeStruct((n, HALF), U32),
            jax.ShapeDtypeStruct((TOP_K, n), I32),
            jax.ShapeDtypeStruct((TOP_K, n), F32),
            jax.ShapeDtypeStruct((TOP_K, n), I32),
            jax.ShapeDtypeStruct((N_EXPERTS, LANES), I32),
        ],
        scratch_shapes=[pltpu.VMEM((N_EXPERTS, 1), F32)],
        compiler_params=_params(("arbitrary",)),
        name="outproj_router",
    )(oa, orr, w_out_b, x2, mod_l, post_g, ffn_g, rw_t, rb)


def _sc_scatter_rows(rows, pos3, n_out):
    n, w = rows.shape
    nchunk, kk, c = pos3.shape
    per_w = nchunk // SC_WORKERS
    mesh = plsc.VectorSubcoreMesh(core_axis_name="c", subcore_axis_name="s")

    @functools.partial(
        pl.kernel, mesh=mesh,
        out_type=jax.ShapeDtypeStruct((n_out, w), rows.dtype),
        scratch_types=[pltpu.VMEM((kk, c), I32), pltpu.VMEM((c, w), rows.dtype)],
    )
    def k(rows_hbm, pos_hbm, out_hbm, idx_v, rows_v):
        wid = lax.axis_index("s") * SC_CORES + lax.axis_index("c")

        @pl.loop(0, per_w)
        def _(i):
            ch = wid * per_w + i
            pltpu.sync_copy(pos_hbm.at[ch], idx_v)
            pltpu.sync_copy(rows_hbm.at[pl.ds(ch * c, c)], rows_v)
            for j in range(kk):
                pltpu.sync_copy(rows_v, out_hbm.at[idx_v.at[j]])

    return k(rows, pos3)


def _sc_gather_rows(table, idx):
    w = table.shape[1]
    b = idx.shape[0]
    c = SC_CHUNK
    per_w = b // (SC_WORKERS * c)
    mesh = plsc.VectorSubcoreMesh(core_axis_name="c", subcore_axis_name="s")

    @functools.partial(
        pl.kernel, mesh=mesh,
        out_type=jax.ShapeDtypeStruct((b, w), table.dtype),
        scratch_types=[pltpu.VMEM((c,), I32), pltpu.VMEM((c, w), table.dtype), pltpu.SemaphoreType.DMA],
    )
    def k(table_hbm, idx_hbm, out_hbm, idx_v, rows_v, sem):
        wid = lax.axis_index("s") * SC_CORES + lax.axis_index("c")

        @pl.loop(0, per_w)
        def _(i):
            base = (wid * per_w + i) * c
            pltpu.sync_copy(idx_hbm.at[pl.ds(base, c)], idx_v)
            pltpu.async_copy(table_hbm.at[idx_v], rows_v, sem).wait()
            pltpu.sync_copy(rows_v, out_hbm.at[pl.ds(base, c)])

    return k(table, idx)


def _experts_kernel(te_ref, nv_ref, xs_ref, wu_ref, bu_ref, wd_ref, bd_ref, ys_ref):
    @pl.when(pl.program_id(0) < nv_ref[0])
    def _():
        w = xs_ref[...]
        lo = _unpack_lo(w).astype(BF16)
        hi = _unpack_hi(w).astype(BF16)
        up = jnp.dot(lo, wu_ref[0, :HALF, :], preferred_element_type=F32)
        up = up + jnp.dot(hi, wu_ref[0, HALF:, :], preferred_element_type=F32) + bu_ref[0]
        glu = jnp.minimum(up[:, :D_FF], SWIGLU_LIMIT)
        lin = jnp.clip(up[:, D_FF:], -SWIGLU_LIMIT, SWIGLU_LIMIT)
        act = glu * jax.nn.sigmoid(SWIGLU_ALPHA * glu) * (lin + 1.0)
        y = jnp.dot(act.astype(BF16), wd_ref[0], preferred_element_type=F32) + bd_ref[0]
        ys_ref[...] = _pack_bf16_pairs(y)


def _experts(xs, tile_expert, n_valid, wu, bu, wd, bd):
    n_pad = xs.shape[0]
    t = MOE_TILE
    grid_spec = pltpu.PrefetchScalarGridSpec(
        num_scalar_prefetch=2,
        grid=(n_pad // t,),
        in_specs=[
            pl.BlockSpec((t, HALF), lambda i, te, nv: (i, 0)),
            pl.BlockSpec((1, D_MODEL, 2 * D_FF), lambda i, te, nv: (te[i], 0, 0)),
            pl.BlockSpec((1, 1, 2 * D_FF), lambda i, te, nv: (te[i], 0, 0)),
            pl.BlockSpec((1, D_FF, D_MODEL), lambda i, te, nv: (te[i], 0, 0)),
            pl.BlockSpec((1, 1, D_MODEL), lambda i, te, nv: (te[i], 0, 0)),
        ],
        out_specs=pl.BlockSpec((t, HALF), lambda i, te, nv: (i, 0)),
    )
    return pl.pallas_call(
        _experts_kernel,
        grid_spec=grid_spec,
        out_shape=jax.ShapeDtypeStruct((n_pad, HALF), U32),
        compiler_params=_params(("arbitrary",)),
        name="experts",
    )(tile_expert, n_valid, xs, wu, bu, wd, bd)


def _combine_kernel(y0_ref, y1_ref, y2_ref, y3_ref, w_ref, x_ref, mod_ref, g_ref, o_ref):
    w = w_ref[...]
    lo = None
    hi = None
    for k, y_ref in enumerate((y0_ref, y1_ref, y2_ref, y3_ref)):
        word = y_ref[...]
        wk = w[:, k:k + 1]
        lo_k = wk * _unpack_lo(word)
        hi_k = wk * _unpack_hi(word)
        lo = lo_k if lo is None else lo + lo_k
        hi = hi_k if hi is None else hi + hi_k
    yf = jnp.concatenate([lo, hi], axis=1)
    m = mod_ref[0]
    o_ref[...] = x_ref[...] + m[5:6, :] * _rms(yf, g_ref[...])


def _combine(yg, tw_t, xn, mod_l, post_g, b, s):
    n = xn.shape[0]
    tm = min(ROW_TILE, s)
    nb = s // tm
    nblk = n // tm
    row = lambda i: (i, 0)
    full = lambda i: (0, 0)
    y_specs = [pl.BlockSpec((tm, HALF), functools.partial(lambda i, k: (k * nblk + i, 0), k=k)) for k in range(TOP_K)]
    return pl.pallas_call(
        _combine_kernel,
        grid=(nblk,),
        in_specs=y_specs + [
            pl.BlockSpec((tm, TOP_K), row),
            pl.BlockSpec((tm, D_MODEL), row),
            pl.BlockSpec((1, 6, D_MODEL), lambda i: (i // nb, 0, 0)),
            pl.BlockSpec((1, D_MODEL), full),
        ],
        out_specs=pl.BlockSpec((tm, D_MODEL), row),
        out_shape=jax.ShapeDtypeStruct((n, D_MODEL), F32),
        compiler_params=_params(("parallel",)),
        name="combine",
    )(yg, yg, yg, yg, tw_t, xn, mod_l, post_g)


def _route(counts, tidx, rank, n_tiles):
    t = MOE_TILE
    cnt = counts[:, 0]
    padded = ((cnt + t - 1) // t) * t
    ends = jnp.cumsum(padded)
    starts = ends - padded
    experts = jnp.arange(N_EXPERTS, dtype=I32)[:, None, None]
    start_of = jnp.sum(jnp.where(tidx[None] == experts, starts[:, None, None], 0), axis=0)
    pos = start_of + rank
    tile_ends = ends // t
    n_valid = tile_ends[-1]
    tile = jnp.minimum(jnp.arange(n_tiles, dtype=I32), n_valid - 1)
    tile_expert = jnp.minimum(jnp.sum(tile_ends[None, :] <= tile[:, None], axis=1), N_EXPERTS - 1).astype(I32)
    return pos.astype(I32), tile_expert, n_valid.reshape(1).astype(I32)


def kernel(x, c, ada_w, ada_b, pre_mix_g, post_mix_g, w_in, q_norm_g, k_norm_g, ret_decay_fwd, ret_decay_bwd, ret_gn_g, ret_gn_b, w_out, pre_ffn_g, post_ffn_g, router_w, router_b, exp_w_up, exp_b_up, exp_w_down, exp_b_down):
    b, s, d = x.shape
    depth = ada_w.shape[0]
    n = b * s
    n_asg = n * TOP_K
    n_pad = n_asg + N_EXPERTS * MOE_TILE
    n_tiles = n_pad // MOE_TILE

    mod = _modulation(c, ada_w, ada_b)
    ropes = _rope_tables(s, HEAD_DIM) + _rope_tables(s, RET_DIM)
    x2 = x.reshape(n, d)
    for l in range(depth):
        mod_l = mod[l].reshape(b, 6, d)
        qg = jnp.tile(q_norm_g[l], LANES // HEAD_DIM).reshape(1, LANES)
        kg = jnp.tile(k_norm_g[l], LANES // HEAD_DIM).reshape(1, LANES)
        qa, kdt, vd, qr, kr, vr, gr = _inproj(x2, mod_l, pre_mix_g[l].reshape(1, d), w_in[l].astype(BF16),
                                              ropes, qg, kg, b, s)
        oa = _attention(qa, kdt, vd, b, s)
        dec = jnp.stack([ret_decay_fwd[l], ret_decay_bwd[l]]).astype(F32)
        orr = _retention(qr, kr, vr, gr, dec, ret_gn_g[l].reshape(1, RET_W), ret_gn_b[l].reshape(1, RET_W), b, s)
        xn, hp, tidx, tw, rank, counts = _outproj(
            oa, orr, w_out[l].astype(BF16), x2, mod_l, post_mix_g[l].reshape(1, d), pre_ffn_g[l].reshape(1, d),
            router_w[l].T.astype(BF16), router_b[l].reshape(N_EXPERTS, 1), b, s)
        pos, tile_expert, n_valid = _route(counts, tidx, rank, n_tiles)
        pos3 = pos.reshape(TOP_K, n // SC_CHUNK, SC_CHUNK).transpose(1, 0, 2)
        xs = _sc_scatter_rows(hp, pos3, n_pad)
        ys = _experts(xs, tile_expert, n_valid, exp_w_up[l].astype(BF16), exp_b_up[l].reshape(N_EXPERTS, 1, 2 * D_FF),
                      exp_w_down[l].astype(BF16), exp_b_down[l].reshape(N_EXPERTS, 1, d))
        yg = _sc_gather_rows(ys, pos.reshape(n_asg))
        x2 = _combine(yg, tw.T, xn, mod_l, post_ffn_g[l].reshape(1, d), b, s)
    return x2.reshape(b, s, d)
```

```python
import functools

import numpy as np
import jax
import jax.numpy as jnp
from jax import lax
from jax.experimental import pallas as pl
from jax.experimental.pallas import tpu as pltpu
from jax.experimental.pallas import tpu_sc as plsc

F32 = jnp.float32
BF16 = jnp.bfloat16
U32 = jnp.uint32
I32 = jnp.int32

D_MODEL = 1024
GRID_W = 64
ROPE_THETA = 10000.0
HEAD_DIM = 64
Q_HEADS = 8
KV_HEADS = 2
RET_HEADS = 4
RET_DIM = 128
RET_CHUNK = 128
ATTN_W = Q_HEADS * HEAD_DIM
KV_W = KV_HEADS * HEAD_DIM
RET_W = RET_HEADS * RET_DIM
IN_W = ATTN_W + 2 * KV_W + 4 * RET_W
N_EXPERTS = 32
TOP_K = 4
D_FF = D_MODEL
SWIGLU_LIMIT = 7.0
SWIGLU_ALPHA = 1.702
NORM_EPS = 1e-6
Q_SCALE = HEAD_DIM ** -0.5 * float(np.log2(np.e))
HALF = D_MODEL // 2
LANES = 128

ROW_TILE = 512
Q_TILE = 128
KV_TILE = 1024
MOE_TILE = 512
VMEM_LIMIT = 48 * 1024 * 1024
EXPERTS_VMEM_LIMIT = 56 * 1024 * 1024

SC_CORES = 2
SC_SUBCORES = 16
SC_WORKERS = SC_CORES * SC_SUBCORES
SC_CHUNK = 64


def _params(sem):
    return pltpu.CompilerParams(dimension_semantics=sem, vmem_limit_bytes=VMEM_LIMIT)


def _rms(x, g):
    return x * lax.rsqrt(jnp.mean(x * x, axis=-1, keepdims=True) + NORM_EPS) * g


def _pack_bf16_pairs(y):
    u = lax.bitcast_convert_type(y.astype(BF16).astype(F32), U32)
    return (u[:, :HALF] >> 16) | u[:, HALF:]


def _unpack_lo(w):
    return lax.bitcast_convert_type(w << 16, F32)


def _unpack_hi(w):
    return lax.bitcast_convert_type(w & jnp.uint32(0xFFFF0000), F32)


def _mod_kernel(c_ref, w_ref, b_ref, o_ref):
    c = c_ref[...]
    cond = c * jax.nn.sigmoid(c)
    o_ref[0] = jnp.dot(cond, w_ref[0], preferred_element_type=F32, precision=lax.Precision.HIGHEST) + b_ref[0]


def _modulation(c, ada_w, ada_b):
    depth, d, w6 = ada_w.shape
    b = c.shape[0]
    tn = 1536
    return pl.pallas_call(
        _mod_kernel,
        grid=(depth, w6 // tn),
        in_specs=[
            pl.BlockSpec((b, d), lambda l, j: (0, 0)),
            pl.BlockSpec((1, d, tn), lambda l, j: (l, 0, j)),
            pl.BlockSpec((1, 1, tn), lambda l, j: (l, 0, j)),
        ],
        out_specs=pl.BlockSpec((1, b, tn), lambda l, j: (l, 0, j)),
        out_shape=jax.ShapeDtypeStruct((depth, b, w6), F32),
        compiler_params=_params(("arbitrary", "arbitrary")),
        name="modulation",
    )(c, ada_w, ada_b.reshape(depth, 1, w6))


def _rope_tables(s, head_dim):
    quarter = head_dim // 4
    t = jnp.arange(s, dtype=F32)
    row = jnp.floor(t / GRID_W)
    col = t - row * GRID_W
    inv_freq = ROPE_THETA ** (-jnp.arange(quarter, dtype=F32) / quarter)
    lane = np.arange(LANES) % head_dim
    use_col = lane >= head_dim // 2
    second = (lane % (head_dim // 2)) >= quarter
    freq = inv_freq[lane % quarter]
    pos = jnp.where(use_col[None, :], col[:, None], row[:, None])
    ang = pos * freq[None, :]
    sign = jnp.where(second, 1.0, -1.0).astype(F32)
    return jnp.cos(ang), jnp.sin(ang) * sign[None, :]


def _rope(z, cos, sin_signed, quarter):
    lane = lax.broadcasted_iota(I32, z.shape, 1)
    first = (lane % (2 * quarter)) < quarter
    partner = jnp.where(first, pltpu.roll(z, LANES - quarter, 1), pltpu.roll(z, quarter, 1))
    return z * cos + partner * sin_signed


def _inproj_kernel(x_ref, mod_ref, g_ref, w_ref, ca_ref, sa_ref, cr_ref, sr_ref, qg_ref, kg_ref,
                   qa_ref, kdt_ref, vd_ref, qr_ref, kr_ref, vr_ref, gr_ref):
    x = x_ref[...]
    m = mod_ref[0]
    h = _rms(x, g_ref[...]) * (1.0 + m[1:2, :]) + m[0:1, :]
    hb = h.astype(BF16)
    tm = x.shape[0]

    def proj(c0):
        z = jnp.dot(hb, w_ref[:, c0:c0 + 2 * LANES], preferred_element_type=F32)
        return z[:, :LANES], z[:, LANES:]

    ri = lax.broadcasted_iota(I32, (LANES, LANES), 0) // HEAD_DIM
    ci = lax.broadcasted_iota(I32, (LANES, LANES), 1) // HEAD_DIM
    head_ones = (ri == ci).astype(BF16)
    ca, sa = ca_ref[...], sa_ref[...]
    cr, sr = cr_ref[...], sr_ref[...]
    lane = lax.broadcasted_iota(I32, (tm, LANES), 1)
    low = lane < HEAD_DIM

    def head_norm_rope(z, g):
        ss = jnp.dot((z * z).astype(BF16), head_ones, preferred_element_type=F32)
        zn = z * lax.rsqrt(ss * (1.0 / HEAD_DIM) + NORM_EPS) * g
        return _rope(zn, ca, sa, HEAD_DIM // 4)

    def store_pair(ref, j, fn, c0):
        for i, z in enumerate(proj(c0 + 2 * j * LANES)):
            ref[:, (2 * j + i) * LANES:(2 * j + i + 1) * LANES] = fn(z).astype(BF16)

    for j in range(ATTN_W // (2 * LANES)):
        store_pair(qa_ref, j, lambda z: head_norm_rope(z, qg_ref[...]) * Q_SCALE, 0)

    k, v = proj(ATTN_W)
    k = head_norm_rope(k, kg_ref[...])
    k_sw = pltpu.roll(k, HEAD_DIM, 1)
    kdt_ref[0, 0] = jnp.where(low, k, k_sw).T.astype(BF16)
    kdt_ref[0, 1] = jnp.where(low, k_sw, k).T.astype(BF16)
    vd_ref[0, 0] = jnp.where(low, v, 1.0).astype(BF16)
    vd_ref[0, 1] = jnp.where(low, pltpu.roll(v, HEAD_DIM, 1), 1.0).astype(BF16)

    base = ATTN_W + 2 * KV_W
    for j in range(RET_W // (2 * LANES)):
        store_pair(qr_ref, j, lambda z: _rope(z, cr, sr, RET_DIM // 4), base)
        store_pair(kr_ref, j, lambda z: _rope(z, cr, sr, RET_DIM // 4) * (RET_DIM ** -0.5), base + RET_W)
        store_pair(vr_ref, j, lambda z: z, base + 2 * RET_W)
        store_pair(gr_ref, j, lambda z: z * jax.nn.sigmoid(z), base + 3 * RET_W)


def _inproj(x2, mod_l, pre_g, w_in_b, ropes, qg, kg, b, s):
    n = x2.shape[0]
    tm = min(ROW_TILE, s)
    nb = s // tm
    ca, sa, cr, sr = ropes
    row = lambda i: (i, 0)
    full = lambda i: (0, 0)
    rope_spec = pl.BlockSpec((tm, LANES), lambda i: (i % nb, 0))
    wide = jax.ShapeDtypeStruct((n, RET_W), BF16)
    return pl.pallas_call(
        _inproj_kernel,
        grid=(n // tm,),
        in_specs=[
            pl.BlockSpec((tm, D_MODEL), row),
            pl.BlockSpec((1, 6, D_MODEL), lambda i: (i // nb, 0, 0)),
            pl.BlockSpec((1, D_MODEL), full),
            pl.BlockSpec((D_MODEL, IN_W), full),
            rope_spec, rope_spec, rope_spec, rope_spec,
            pl.BlockSpec((1, LANES), full),
            pl.BlockSpec((1, LANES), full),
        ],
        out_specs=[
            pl.BlockSpec((tm, ATTN_W), row),
            pl.BlockSpec((1, KV_HEADS, LANES, tm), lambda i: (i // nb, 0, 0, i % nb)),
            pl.BlockSpec((1, KV_HEADS, tm, LANES), lambda i: (i // nb, 0, i % nb, 0)),
            pl.BlockSpec((tm, RET_W), row),
            pl.BlockSpec((tm, RET_W), row),
            pl.BlockSpec((tm, RET_W), row),
            pl.BlockSpec((tm, RET_W), row),
        ],
        out_shape=[
            jax.ShapeDtypeStruct((n, ATTN_W), BF16),
            jax.ShapeDtypeStruct((b, KV_HEADS, LANES, s), BF16),
            jax.ShapeDtypeStruct((b, KV_HEADS, s, LANES), BF16),
            wide, wide, wide, wide,
        ],
        compiler_params=_params(("parallel",)),
        name="inproj",
    )(x2, mod_l, pre_g, w_in_b, ca, sa, cr, sr, qg, kg)


def _attn_kernel(q_ref, kt_ref, v_ref, o_ref, q_s, qn_s, m_ref, acc_ref, sa_ref, sb_ref, *, tq, tk):
    s = v_ref.shape[2]
    nk = s // tk
    nq = s // tq
    group = Q_HEADS // KV_HEADS
    lane = lax.broadcasted_iota(I32, (tq, LANES), 1)
    low = lane < HEAD_DIM
    zero = jnp.zeros((tq, LANES), BF16)

    def load_q(qi, dst):
        rows = pl.ds(pl.multiple_of(qi * tq, tq), tq)
        for p in range(group // 2):
            pair = q_ref[rows, p * LANES:(p + 1) * LANES]
            dst[2 * p] = jnp.where(low, pair, zero)
            dst[2 * p + 1] = jnp.where(low, zero, pair)

    def key_tile(j):
        return kt_ref[0, 0, :, j * tk:(j + 1) * tk]

    def step(j, cur_ref, nxt_ref, q_next, j_next):
        v = v_ref[0, 0, j * tk:(j + 1) * tk, :]
        kt = key_tile(j_next)
        nxt_ref[0] = jnp.dot(q_next[0], kt, preferred_element_type=F32)
        for g in range(group):
            sc = cur_ref[g]
            m_old = m_ref[g]
            m_new = jnp.maximum(m_old, jnp.max(sc, axis=-1, keepdims=True))
            p = jnp.exp2(sc - m_new)
            alpha = jnp.exp2(m_old - m_new)
            if g + 1 < group:
                nxt_ref[g + 1] = jnp.dot(q_next[g + 1], kt, preferred_element_type=F32)
            acc_ref[g] = alpha * acc_ref[g] + jnp.dot(p.astype(BF16), v, preferred_element_type=F32)
            m_ref[g] = m_new

    load_q(0, q_s)
    for g in range(group):
        sa_ref[g] = jnp.dot(q_s[g], key_tile(0), preferred_element_type=F32)

    def body(qi, carry):
        load_q(qi, q_s)
        load_q(jnp.minimum(qi + 1, nq - 1), qn_s)
        m_ref[...] = jnp.full(m_ref.shape, -jnp.inf, F32)
        acc_ref[...] = jnp.zeros(acc_ref.shape, F32)
        bufs = (sa_ref, sb_ref)
        for j in range(nk):
            last = j + 1 == nk
            step(j, bufs[j % 2], bufs[(j + 1) % 2], qn_s if last else q_s, 0 if last else j + 1)
        outs = []
        for g in range(group):
            acc = acc_ref[g]
            outs.append(acc / pltpu.roll(acc, HEAD_DIM, 1))
        rows = pl.ds(pl.multiple_of(qi * tq, tq), tq)
        for p in range(group // 2):
            pair = jnp.where(low, outs[2 * p], pltpu.roll(outs[2 * p + 1], HEAD_DIM, 1))
            o_ref[rows, p * LANES:(p + 1) * LANES] = pair.astype(BF16)
        return carry

    lax.fori_loop(0, nq, body, 0)


def _attention(qa, kdt, vd, b, s):
    n = qa.shape[0]
    tq = min(Q_TILE, s)
    tk = min(KV_TILE, s // 2)
    assert s % tq == 0 and s % (2 * tk) == 0 and tk % LANES == 0, (s, tq, tk)
    gw = ATTN_W // KV_HEADS
    group = Q_HEADS // KV_HEADS
    return pl.pallas_call(
        functools.partial(_attn_kernel, tq=tq, tk=tk),
        grid=(b, KV_HEADS),
        in_specs=[
            pl.BlockSpec((s, gw), lambda bi, h: (bi, h)),
            pl.BlockSpec((1, 1, LANES, s), lambda bi, h: (bi, h, 0, 0)),
            pl.BlockSpec((1, 1, s, LANES), lambda bi, h: (bi, h, 0, 0)),
        ],
        out_specs=pl.BlockSpec((s, gw), lambda bi, h: (bi, h)),
        out_shape=jax.ShapeDtypeStruct((n, ATTN_W), BF16),
        scratch_shapes=[
            pltpu.VMEM((group, tq, LANES), BF16),
            pltpu.VMEM((group, tq, LANES), BF16),
            pltpu.VMEM((group, tq, 1), F32),
            pltpu.VMEM((group, tq, LANES), F32),
            pltpu.VMEM((group, tq, tk), F32),
            pltpu.VMEM((group, tq, tk), F32),
        ],
        compiler_params=_params(("parallel", "parallel")),
        name="attention",
    )(qa, kdt, vd)


def _retention_kernel(dec_ref, q_ref, k_ref, v_ref, g_ref, gng_ref, gnb_ref, o_ref,
                      fbuf, bbuf, sf_ref, sb_ref):
    h = pl.program_id(1)
    c = RET_CHUNK
    s = q_ref.shape[1]
    nc = s // c
    lgf = jnp.full((1, 1), dec_ref[0, h], F32)
    lgb = jnp.full((1, 1), dec_ref[1, h], F32)
    ii = lax.broadcasted_iota(I32, (c, c), 0)
    jj = lax.broadcasted_iota(I32, (c, c), 1)
    diff = (ii - jj).astype(F32)
    d_f = jnp.where(diff >= 0, jnp.exp(lgf * jnp.maximum(diff, 0.0)), 0.0)
    d_b = jnp.where(diff <= 0, jnp.exp(lgb * jnp.maximum(-diff, 0.0)), 0.0)
    idx = lax.broadcasted_iota(I32, (c, 1), 0).astype(F32)
    xi_f = jnp.exp(lgf * (idx + 1.0))
    zeta_f = jnp.exp(lgf * (c - 1.0 - idx))
    xi_b = jnp.exp(lgb * (c - idx))
    zeta_b = jnp.exp(lgb * idx)
    cd_f = jnp.exp(lgf * c)
    cd_b = jnp.exp(lgb * c)
    sf_ref[...] = jnp.zeros(sf_ref.shape, F32)
    sb_ref[...] = jnp.zeros(sb_ref.shape, F32)
    nt = (((1,), (1,)), ((), ()))

    fwd = (d_f, xi_f, zeta_f, cd_f, sf_ref, fbuf)
    bwd = (d_b, xi_b, zeta_b, cd_b, sb_ref, bbuf)

    def body(i, carry):
        jobs = [(fwd, 2 * i), (bwd, nc - 1 - 2 * i), (fwd, 2 * i + 1), (bwd, nc - 2 - 2 * i)]
        offs = [pl.multiple_of(ci * c, c) for _, ci in jobs]
        qs = [q_ref[0, pl.ds(off, c), :] for off in offs]
        ks = [k_ref[0, pl.ds(off, c), :] for off in offs]
        vs = [v_ref[0, pl.ds(off, c), :] for off in offs]
        scores = [lax.dot_general(q, k, nt, preferred_element_type=F32) for q, k in zip(qs, ks)]
        incs = []
        for (d, _), k, v in zip(jobs, ks, vs):
            kzt = (k.astype(F32) * d[2]).T.astype(BF16)
            incs.append(jnp.dot(kzt, v, preferred_element_type=F32))
        states = [fwd[4][...], bwd[4][...]]
        cross = [jnp.dot(qs[n], states[n].astype(BF16), preferred_element_type=F32) for n in range(2)]
        states += [states[n] * jobs[n][0][3] + incs[n] for n in range(2)]
        cross += [jnp.dot(qs[n], states[n].astype(BF16), preferred_element_type=F32) for n in range(2, 4)]
        for n, ((d, _), off) in enumerate(zip(jobs, offs)):
            a = (scores[n] * d[0]).astype(BF16)
            d[5][pl.ds(off, c), :] = jnp.dot(a, vs[n], preferred_element_type=F32) + cross[n] * d[1]
        for n in range(2):
            jobs[n][0][4][...] = states[2 + n] * jobs[n][0][3] + incs[2 + n]
        return carry

    lax.fori_loop(0, nc // 2, body, 0)

    rows = min(512, s)

    def fin(t, carry):
        off = pl.multiple_of(t * rows, rows)
        r = fbuf[pl.ds(off, rows), :] + bbuf[pl.ds(off, rows), :]
        mu = jnp.mean(r, axis=-1, keepdims=True)
        d = r - mu
        var = jnp.mean(d * d, axis=-1, keepdims=True)
        y = d * lax.rsqrt(var + NORM_EPS) * gng_ref[...] + gnb_ref[...]
        o_ref[0, pl.ds(off, rows), :] = (g_ref[0, pl.ds(off, rows), :].astype(F32) * y).astype(BF16)
        return carry

    lax.fori_loop(0, s // rows, fin, 0)


def _retention(qr, kr, vr, gr, dec, gn_g, gn_b, b, s):
    shp = (b, s, RET_W)
    assert s % (2 * RET_CHUNK) == 0, s
    head = pl.BlockSpec((1, s, RET_DIM), lambda bi, h: (bi, 0, h))
    vec = pl.BlockSpec((1, RET_DIM), lambda bi, h: (0, h))
    out = pl.pallas_call(
        _retention_kernel,
        grid=(b, RET_HEADS),
        in_specs=[pl.BlockSpec(memory_space=pltpu.SMEM), head, head, head, head, vec, vec],
        out_specs=head,
        out_shape=jax.ShapeDtypeStruct(shp, BF16),
        scratch_shapes=[
            pltpu.VMEM((s, RET_DIM), F32),
            pltpu.VMEM((s, RET_DIM), F32),
            pltpu.VMEM((RET_DIM, RET_DIM), F32),
            pltpu.VMEM((RET_DIM, RET_DIM), F32),
        ],
        compiler_params=_params(("parallel", "parallel")),
        name="retention",
    )(dec, qr.reshape(shp), kr.reshape(shp), vr.reshape(shp), gr.reshape(shp), gn_g, gn_b)
    return out.reshape(b * s, RET_W)


def _outproj_kernel(oa_ref, or_ref, w_ref, x_ref, mod_ref, pg_ref, fg_ref, rw_ref, rb_ref,
                    xn_ref, hp_ref, ti_ref, tw_ref, rk_ref, cnt_ref, base_ref):
    @pl.when(pl.program_id(0) == 0)
    def _():
        base_ref[...] = jnp.zeros(base_ref.shape, F32)

    m = mod_ref[0]
    y = jnp.dot(oa_ref[...], w_ref[:ATTN_W, :], preferred_element_type=F32)
    y = y + jnp.dot(or_ref[...], w_ref[ATTN_W:, :], preferred_element_type=F32)
    xn = x_ref[...] + m[2:3, :] * _rms(y, pg_ref[...])
    xn_ref[...] = xn
    h = _rms(xn, fg_ref[...]) * (1.0 + m[4:5, :]) + m[3:4, :]
    hb = h.astype(BF16)
    u = lax.bitcast_convert_type(hb.astype(F32), U32)
    hp_ref[...] = (u[:, :HALF] >> 16) | u[:, HALF:]

    tm = hb.shape[0]
    nt = (((1,), (1,)), ((), ()))
    logits = lax.dot_general(rw_ref[...], hb, nt, preferred_element_type=F32) + rb_ref[...]
    e_iota = lax.broadcasted_iota(I32, (N_EXPERTS, tm), 0).astype(F32)
    vals, idxs = [], []
    cur = logits
    for _ in range(TOP_K):
        mx = jnp.max(cur, axis=0, keepdims=True)
        ik = jnp.min(jnp.where(cur == mx, e_iota, float(N_EXPERTS)), axis=0, keepdims=True)
        vals.append(mx)
        idxs.append(ik)
        cur = jnp.where(e_iota == ik, -jnp.inf, cur)
    v = jnp.concatenate(vals, axis=0)
    w = jnp.exp(v - v[0:1, :])
    tw_ref[...] = w / jnp.sum(w, axis=0, keepdims=True)
    ti_ref[...] = jnp.concatenate(idxs, axis=0).astype(I32)

    onehot = jnp.zeros((N_EXPERTS, tm), F32)
    for ik in idxs:
        onehot = onehot + (e_iota == ik).astype(F32)
    before = (lax.broadcasted_iota(I32, (tm, tm), 0) < lax.broadcasted_iota(I32, (tm, tm), 1)).astype(BF16)
    seen = jnp.dot(onehot.astype(BF16), before, preferred_element_type=F32) + base_ref[...]
    ranks = [jnp.sum(jnp.where(e_iota == ik, seen, 0.0), axis=0, keepdims=True) for ik in idxs]
    rk_ref[...] = jnp.concatenate(ranks, axis=0).astype(I32)
    total = base_ref[...] + jnp.sum(onehot, axis=1, keepdims=True)
    base_ref[...] = total
    cnt_ref[...] = jnp.broadcast_to(total, cnt_ref.shape).astype(I32)


def _outproj(oa, orr, w_out_b, x2, mod_l, post_g, ffn_g, rw_t, rb, b, s):
    n = x2.shape[0]
    tm = min(ROW_TILE, s)
    nb = s // tm
    row = lambda i: (i, 0)
    full = lambda i: (0, 0)
    col = lambda i: (0, i)
    return pl.pallas_call(
        _outproj_kernel,
        grid=(n // tm,),
        in_specs=[
            pl.BlockSpec((tm, ATTN_W), row),
            pl.BlockSpec((tm, RET_W), row),
            pl.BlockSpec((D_MODEL, D_MODEL), full),
            pl.BlockSpec((tm, D_MODEL), row),
            pl.BlockSpec((1, 6, D_MODEL), lambda i: (i // nb, 0, 0)),
            pl.BlockSpec((1, D_MODEL), full),
            pl.BlockSpec((1, D_MODEL), full),
            pl.BlockSpec((N_EXPERTS, D_MODEL), full),
            pl.BlockSpec((N_EXPERTS, 1), full),
        ],
        out_specs=[
            pl.BlockSpec((tm, D_MODEL), row),
            pl.BlockSpec((tm, HALF), row),
            pl.BlockSpec((TOP_K, tm), col),
            pl.BlockSpec((TOP_K, tm), col),
            pl.BlockSpec((TOP_K, tm), col),
            pl.BlockSpec((N_EXPERTS, LANES), full),
        ],
        out_shape=[
            jax.ShapeDtypeStruct((n, D_MODEL), F32),
            jax.ShapeDtypeStruct((n, HALF), U32),
            jax.ShapeDtypeStruct((TOP_K, n), I32),
            jax.ShapeDtypeStruct((TOP_K, n), F32),
            jax.ShapeDtypeStruct((TOP_K, n), I32),
            jax.ShapeDtypeStruct((N_EXPERTS, LANES), I32),
        ],
        scratch_shapes=[pltpu.VMEM((N_EXPERTS, 1), F32)],
        compiler_params=_params(("arbitrary",)),
        name="outproj_router",
    )(oa, orr, w_out_b, x2, mod_l, post_g, ffn_g, rw_t, rb)


def _sc_scatter_rows(rows, pos3, n_out):
    n, w = rows.shape
    nchunk, kk, c = pos3.shape
    per_w = nchunk // SC_WORKERS
    mesh = plsc.VectorSubcoreMesh(core_axis_name="c", subcore_axis_name="s")

    @functools.partial(
        pl.kernel, mesh=mesh,
        out_type=jax.ShapeDtypeStruct((n_out, w), rows.dtype),
        scratch_types=[pltpu.VMEM((kk, c), I32), pltpu.VMEM((c, w), rows.dtype)],
    )
    def k(rows_hbm, pos_hbm, out_hbm, idx_v, rows_v):
        wid = lax.axis_index("s") * SC_CORES + lax.axis_index("c")

        @pl.loop(0, per_w)
        def _(i):
            ch = wid * per_w + i
            pltpu.sync_copy(pos_hbm.at[ch], idx_v)
            pltpu.sync_copy(rows_hbm.at[pl.ds(ch * c, c)], rows_v)
            for j in range(kk):
                pltpu.sync_copy(rows_v, out_hbm.at[idx_v.at[j]])

    return k(rows, pos3)


def _sc_gather_rows(table, idx):
    w = table.shape[1]
    b = idx.shape[0]
    c = SC_CHUNK
    per_w = b // (SC_WORKERS * c)
    mesh = plsc.VectorSubcoreMesh(core_axis_name="c", subcore_axis_name="s")

    @functools.partial(
        pl.kernel, mesh=mesh,
        out_type=jax.ShapeDtypeStruct((b, w), table.dtype),
        scratch_types=[pltpu.VMEM((c,), I32), pltpu.VMEM((c, w), table.dtype), pltpu.SemaphoreType.DMA],
    )
    def k(table_hbm, idx_hbm, out_hbm, idx_v, rows_v, sem):
        wid = lax.axis_index("s") * SC_CORES + lax.axis_index("c")

        @pl.loop(0, per_w)
        def _(i):
            base = (wid * per_w + i) * c
            pltpu.sync_copy(idx_hbm.at[pl.ds(base, c)], idx_v)
            pltpu.async_copy(table_hbm.at[idx_v], rows_v, sem).wait()
            pltpu.sync_copy(rows_v, out_hbm.at[pl.ds(base, c)])

    return k(table, idx)


def _experts_kernel(te_ref, nv_ref, xs_ref, wu_ref, bu_ref, wd_ref, bd_ref, ys_ref, wu_b, wd_b):
    i = pl.program_id(0)
    live = i < nv_ref[0]
    new_expert = jnp.logical_or(i == 0, te_ref[i] != te_ref[jnp.maximum(i - 1, 0)])

    @pl.when(jnp.logical_and(live, new_expert))
    def _():
        wu_b[...] = wu_ref[0, 0].astype(BF16)
        wd_b[...] = wd_ref[0, 0].astype(BF16)

    @pl.when(live)
    def _():
        w = xs_ref[...]
        lo = _unpack_lo(w).astype(BF16)
        hi = _unpack_hi(w).astype(BF16)
        up = jnp.dot(lo, wu_b[:HALF, :], preferred_element_type=F32)
        up = up + jnp.dot(hi, wu_b[HALF:, :], preferred_element_type=F32) + bu_ref[0, 0]
        glu = jnp.minimum(up[:, :D_FF], SWIGLU_LIMIT)
        lin = jnp.clip(up[:, D_FF:], -SWIGLU_LIMIT, SWIGLU_LIMIT)
        act = glu * jax.nn.sigmoid(SWIGLU_ALPHA * glu) * (lin + 1.0)
        y = jnp.dot(act.astype(BF16), wd_b[...], preferred_element_type=F32) + bd_ref[0, 0]
        ys_ref[...] = _pack_bf16_pairs(y)


def _experts(xs, tile_expert, n_valid, layer, wu, bu, wd, bd):
    n_pad = xs.shape[0]
    t = MOE_TILE
    grid_spec = pltpu.PrefetchScalarGridSpec(
        num_scalar_prefetch=2,
        grid=(n_pad // t,),
        in_specs=[
            pl.BlockSpec((t, HALF), lambda i, te, nv: (i, 0)),
            pl.BlockSpec((1, 1, D_MODEL, 2 * D_FF), lambda i, te, nv: (layer, te[i], 0, 0)),
            pl.BlockSpec((1, 1, 1, 2 * D_FF), lambda i, te, nv: (layer, te[i], 0, 0)),
            pl.BlockSpec((1, 1, D_FF, D_MODEL), lambda i, te, nv: (layer, te[i], 0, 0)),
            pl.BlockSpec((1, 1, 1, D_MODEL), lambda i, te, nv: (layer, te[i], 0, 0)),
        ],
        out_specs=pl.BlockSpec((t, HALF), lambda i, te, nv: (i, 0)),
        scratch_shapes=[pltpu.VMEM((D_MODEL, 2 * D_FF), BF16), pltpu.VMEM((D_FF, D_MODEL), BF16)],
    )
    return pl.pallas_call(
        _experts_kernel,
        grid_spec=grid_spec,
        out_shape=jax.ShapeDtypeStruct((n_pad, HALF), U32),
        compiler_params=pltpu.CompilerParams(dimension_semantics=("arbitrary",), vmem_limit_bytes=EXPERTS_VMEM_LIMIT),
        name="experts",
    )(tile_expert, n_valid, xs, wu, bu, wd, bd)


def _combine_kernel(y0_ref, y1_ref, y2_ref, y3_ref, w_ref, x_ref, mod_ref, g_ref, o_ref):
    w = w_ref[...]
    lo = None
    hi = None
    for k, y_ref in enumerate((y0_ref, y1_ref, y2_ref, y3_ref)):
        word = y_ref[...]
        wk = w[:, k:k + 1]
        lo_k = wk * _unpack_lo(word)
        hi_k = wk * _unpack_hi(word)
        lo = lo_k if lo is None else lo + lo_k
        hi = hi_k if hi is None else hi + hi_k
    yf = jnp.concatenate([lo, hi], axis=1)
    m = mod_ref[0]
    o_ref[...] = x_ref[...] + m[5:6, :] * _rms(yf, g_ref[...])


def _combine(yg, tw_t, xn, mod_l, post_g, b, s):
    n = xn.shape[0]
    tm = min(ROW_TILE, s)
    nb = s // tm
    nblk = n // tm
    row = lambda i: (i, 0)
    full = lambda i: (0, 0)
    y_specs = [pl.BlockSpec((tm, HALF), functools.partial(lambda i, k: (k * nblk + i, 0), k=k)) for k in range(TOP_K)]
    return pl.pallas_call(
        _combine_kernel,
        grid=(nblk,),
        in_specs=y_specs + [
            pl.BlockSpec((tm, TOP_K), row),
            pl.BlockSpec((tm, D_MODEL), row),
            pl.BlockSpec((1, 6, D_MODEL), lambda i: (i // nb, 0, 0)),
            pl.BlockSpec((1, D_MODEL), full),
        ],
        out_specs=pl.BlockSpec((tm, D_MODEL), row),
        out_shape=jax.ShapeDtypeStruct((n, D_MODEL), F32),
        compiler_params=_params(("parallel",)),
        name="combine",
    )(yg, yg, yg, yg, tw_t, xn, mod_l, post_g)


def _route(counts, tidx, rank, n_tiles):
    t = MOE_TILE
    cnt = counts[:, 0]
    padded = ((cnt + t - 1) // t) * t
    ends = jnp.cumsum(padded)
    starts = ends - padded
    experts = jnp.arange(N_EXPERTS, dtype=I32)[:, None, None]
    start_of = jnp.sum(jnp.where(tidx[None] == experts, starts[:, None, None], 0), axis=0)
    pos = start_of + rank
    tile_ends = ends // t
    n_valid = tile_ends[-1]
    tile = jnp.minimum(jnp.arange(n_tiles, dtype=I32), n_valid - 1)
    tile_expert = jnp.minimum(jnp.sum(tile_ends[None, :] <= tile[:, None], axis=1), N_EXPERTS - 1).astype(I32)
    return pos.astype(I32), tile_expert, n_valid.reshape(1).astype(I32)


def kernel(x, c, ada_w, ada_b, pre_mix_g, post_mix_g, w_in, q_norm_g, k_norm_g, ret_decay_fwd, ret_decay_bwd, ret_gn_g, ret_gn_b, w_out, pre_ffn_g, post_ffn_g, router_w, router_b, exp_w_up, exp_b_up, exp_w_down, exp_b_down):
    b, s, d = x.shape
    depth = ada_w.shape[0]
    n = b * s
    n_asg = n * TOP_K
    n_pad = n_asg + N_EXPERTS * MOE_TILE
    n_tiles = n_pad // MOE_TILE

    mod = _modulation(c, ada_w, ada_b)
    ropes = _rope_tables(s, HEAD_DIM) + _rope_tables(s, RET_DIM)
    x2 = x.reshape(n, d)
    for l in range(depth):
        mod_l = mod[l].reshape(b, 6, d)
        qg = jnp.tile(q_norm_g[l], LANES // HEAD_DIM).reshape(1, LANES)
        kg = jnp.tile(k_norm_g[l], LANES // HEAD_DIM).reshape(1, LANES)
        qa, kdt, vd, qr, kr, vr, gr = _inproj(x2, mod_l, pre_mix_g[l].reshape(1, d), w_in[l].astype(BF16),
                                              ropes, qg, kg, b, s)
        oa = _attention(qa, kdt, vd, b, s)
        dec = jnp.stack([ret_decay_fwd[l], ret_decay_bwd[l]]).astype(F32)
        orr = _retention(qr, kr, vr, gr, dec, ret_gn_g[l].reshape(1, RET_W), ret_gn_b[l].reshape(1, RET_W), b, s)
        xn, hp, tidx, tw, rank, counts = _outproj(
            oa, orr, w_out[l].astype(BF16), x2, mod_l, post_mix_g[l].reshape(1, d), pre_ffn_g[l].reshape(1, d),
            router_w[l].T.astype(BF16), router_b[l].reshape(N_EXPERTS, 1), b, s)
        pos, tile_expert, n_valid = _route(counts, tidx, rank, n_tiles)
        pos3 = pos.reshape(TOP_K, n // SC_CHUNK, SC_CHUNK).transpose(1, 0, 2)
        xs = _sc_scatter_rows(hp, pos3, n_pad)
        ys = _experts(xs, tile_expert, n_valid, l, exp_w_up, exp_b_up.reshape(depth, N_EXPERTS, 1, 2 * D_FF),
                      exp_w_down, exp_b_down.reshape(depth, N_EXPERTS, 1, d))
        yg = _sc_gather_rows(ys, pos.reshape(n_asg))
        x2 = _combine(yg, tw.T, xn, mod_l, post_ffn_g[l].reshape(1, d), b, s)
    return x2.reshape(b, s, d)
```

```python
import functools

import numpy as np
import jax
import jax.numpy as jnp
from jax import lax
from jax.experimental import pallas as pl
from jax.experimental.pallas import tpu as pltpu
from jax.experimental.pallas import tpu_sc as plsc

F32 = jnp.float32
BF16 = jnp.bfloat16
U32 = jnp.uint32
I32 = jnp.int32

D_MODEL = 1024
GRID_W = 64
ROPE_THETA = 10000.0
HEAD_DIM = 64
Q_HEADS = 8
KV_HEADS = 2
RET_HEADS = 4
RET_DIM = 128
RET_CHUNK = 128
ATTN_W = Q_HEADS * HEAD_DIM
KV_W = KV_HEADS * HEAD_DIM
RET_W = RET_HEADS * RET_DIM
IN_W = ATTN_W + 2 * KV_W + 4 * RET_W
N_EXPERTS = 32
TOP_K = 4
D_FF = D_MODEL
SWIGLU_LIMIT = 7.0
SWIGLU_ALPHA = 1.702
NORM_EPS = 1e-6
Q_SCALE = HEAD_DIM ** -0.5 * float(np.log2(np.e))
HALF = D_MODEL // 2
LANES = 128

ROW_TILE = 512
Q_TILE = 128
KV_TILE = 1024
MOE_TILE = 512
VMEM_LIMIT = 48 * 1024 * 1024
EXPERTS_VMEM_LIMIT = 56 * 1024 * 1024

SC_CORES = 2
SC_SUBCORES = 16
SC_WORKERS = SC_CORES * SC_SUBCORES
SC_CHUNK = 64


def _params(sem):
    return pltpu.CompilerParams(dimension_semantics=sem, vmem_limit_bytes=VMEM_LIMIT)


def _rms(x, g):
    return x * lax.rsqrt(jnp.mean(x * x, axis=-1, keepdims=True) + NORM_EPS) * g


def _pack_bf16_pairs(y):
    u = lax.bitcast_convert_type(y.astype(BF16).astype(F32), U32)
    return (u[:, :HALF] >> 16) | u[:, HALF:]


def _unpack_lo(w):
    return lax.bitcast_convert_type(w << 16, F32)


def _unpack_hi(w):
    return lax.bitcast_convert_type(w & jnp.uint32(0xFFFF0000), F32)


def _mod_kernel(c_ref, w_ref, b_ref, o_ref):
    c = c_ref[...]
    cond = c * jax.nn.sigmoid(c)
    o_ref[0] = jnp.dot(cond, w_ref[0], preferred_element_type=F32, precision=lax.Precision.HIGHEST) + b_ref[0]


def _modulation(c, ada_w, ada_b):
    depth, d, w6 = ada_w.shape
    b = c.shape[0]
    tn = 1536
    return pl.pallas_call(
        _mod_kernel,
        grid=(depth, w6 // tn),
        in_specs=[
            pl.BlockSpec((b, d), lambda l, j: (0, 0)),
            pl.BlockSpec((1, d, tn), lambda l, j: (l, 0, j)),
            pl.BlockSpec((1, 1, tn), lambda l, j: (l, 0, j)),
        ],
        out_specs=pl.BlockSpec((1, b, tn), lambda l, j: (l, 0, j)),
        out_shape=jax.ShapeDtypeStruct((depth, b, w6), F32),
        compiler_params=_params(("arbitrary", "arbitrary")),
        name="modulation",
    )(c, ada_w, ada_b.reshape(depth, 1, w6))


def _rope_tables(s, head_dim):
    quarter = head_dim // 4
    t = jnp.arange(s, dtype=F32)
    row = jnp.floor(t / GRID_W)
    col = t - row * GRID_W
    inv_freq = ROPE_THETA ** (-jnp.arange(quarter, dtype=F32) / quarter)
    lane = np.arange(LANES) % head_dim
    use_col = lane >= head_dim // 2
    second = (lane % (head_dim // 2)) >= quarter
    freq = inv_freq[lane % quarter]
    pos = jnp.where(use_col[None, :], col[:, None], row[:, None])
    ang = pos * freq[None, :]
    sign = jnp.where(second, 1.0, -1.0).astype(F32)
    return jnp.cos(ang), jnp.sin(ang) * sign[None, :]


def _rope(z, cos, sin_signed, quarter):
    lane = lax.broadcasted_iota(I32, z.shape, 1)
    first = (lane % (2 * quarter)) < quarter
    partner = jnp.where(first, pltpu.roll(z, LANES - quarter, 1), pltpu.roll(z, quarter, 1))
    return z * cos + partner * sin_signed


def _inproj_kernel(x_ref, mod_ref, g_ref, w_ref, ca_ref, sa_ref, cr_ref, sr_ref, qg_ref, kg_ref,
                   qa_ref, kdt_ref, vd_ref, qr_ref, kr_ref, vr_ref, gr_ref):
    x = x_ref[...]
    m = mod_ref[0]
    h = _rms(x, g_ref[...]) * (1.0 + m[1:2, :]) + m[0:1, :]
    hb = h.astype(BF16)
    tm = x.shape[0]

    def proj(c0):
        z = jnp.dot(hb, w_ref[:, c0:c0 + 2 * LANES], preferred_element_type=F32)
        return z[:, :LANES], z[:, LANES:]

    ri = lax.broadcasted_iota(I32, (LANES, LANES), 0) // HEAD_DIM
    ci = lax.broadcasted_iota(I32, (LANES, LANES), 1) // HEAD_DIM
    head_ones = (ri == ci).astype(BF16)
    ca, sa = ca_ref[...], sa_ref[...]
    cr, sr = cr_ref[...], sr_ref[...]
    lane = lax.broadcasted_iota(I32, (tm, LANES), 1)
    low = lane < HEAD_DIM

    def head_norm_rope(z, g):
        ss = jnp.dot((z * z).astype(BF16), head_ones, preferred_element_type=F32)
        zn = z * lax.rsqrt(ss * (1.0 / HEAD_DIM) + NORM_EPS) * g
        return _rope(zn, ca, sa, HEAD_DIM // 4)

    def store_pair(ref, j, fn, c0):
        for i, z in enumerate(proj(c0 + 2 * j * LANES)):
            ref[:, (2 * j + i) * LANES:(2 * j + i + 1) * LANES] = fn(z).astype(BF16)

    for j in range(ATTN_W // (2 * LANES)):
        store_pair(qa_ref, j, lambda z: head_norm_rope(z, qg_ref[...]) * Q_SCALE, 0)

    k, v = proj(ATTN_W)
    k = head_norm_rope(k, kg_ref[...])
    k_sw = pltpu.roll(k, HEAD_DIM, 1)
    kdt_ref[0, 0] = jnp.where(low, k, k_sw).T.astype(BF16)
    kdt_ref[0, 1] = jnp.where(low, k_sw, k).T.astype(BF16)
    vd_ref[0, 0] = jnp.where(low, v, 1.0).astype(BF16)
    vd_ref[0, 1] = jnp.where(low, pltpu.roll(v, HEAD_DIM, 1), 1.0).astype(BF16)

    base = ATTN_W + 2 * KV_W
    for j in range(RET_W // (2 * LANES)):
        store_pair(qr_ref, j, lambda z: _rope(z, cr, sr, RET_DIM // 4), base)
        store_pair(kr_ref, j, lambda z: _rope(z, cr, sr, RET_DIM // 4) * (RET_DIM ** -0.5), base + RET_W)
        store_pair(vr_ref, j, lambda z: z, base + 2 * RET_W)
        store_pair(gr_ref, j, lambda z: z * jax.nn.sigmoid(z), base + 3 * RET_W)


def _inproj(x2, batch0, x_batch0, mod_l, pre_g, w_in_b, ropes, qg, kg, b, s):
    n = b * s
    tm = min(ROW_TILE, s)
    nb = s // tm
    ca, sa, cr, sr = ropes
    row = lambda i: (i, 0)
    full = lambda i: (0, 0)
    rope_spec = pl.BlockSpec((tm, LANES), lambda i: (i % nb, 0))
    wide = jax.ShapeDtypeStruct((n, RET_W), BF16)
    return pl.pallas_call(
        _inproj_kernel,
        grid=(n // tm,),
        in_specs=[
            pl.BlockSpec((tm, D_MODEL), lambda i: (x_batch0 * nb + i, 0)),
            pl.BlockSpec((1, 6, D_MODEL), lambda i: (batch0 + i // nb, 0, 0)),
            pl.BlockSpec((1, D_MODEL), full),
            pl.BlockSpec((D_MODEL, IN_W), full),
            rope_spec, rope_spec, rope_spec, rope_spec,
            pl.BlockSpec((1, LANES), full),
            pl.BlockSpec((1, LANES), full),
        ],
        out_specs=[
            pl.BlockSpec((tm, ATTN_W), row),
            pl.BlockSpec((1, KV_HEADS, LANES, tm), lambda i: (i // nb, 0, 0, i % nb)),
            pl.BlockSpec((1, KV_HEADS, tm, LANES), lambda i: (i // nb, 0, i % nb, 0)),
            pl.BlockSpec((tm, RET_W), row),
            pl.BlockSpec((tm, RET_W), row),
            pl.BlockSpec((tm, RET_W), row),
            pl.BlockSpec((tm, RET_W), row),
        ],
        out_shape=[
            jax.ShapeDtypeStruct((n, ATTN_W), BF16),
            jax.ShapeDtypeStruct((b, KV_HEADS, LANES, s), BF16),
            jax.ShapeDtypeStruct((b, KV_HEADS, s, LANES), BF16),
            wide, wide, wide, wide,
        ],
        compiler_params=_params(("parallel",)),
        name="inproj",
    )(x2, mod_l, pre_g, w_in_b, ca, sa, cr, sr, qg, kg)


def _attn_kernel(q_ref, kt_ref, v_ref, o_ref, q_s, qn_s, m_ref, acc_ref, sa_ref, sb_ref, *, tq, tk):
    s = v_ref.shape[2]
    nk = s // tk
    nq = s // tq
    group = Q_HEADS // KV_HEADS
    lane = lax.broadcasted_iota(I32, (tq, LANES), 1)
    low = lane < HEAD_DIM
    zero = jnp.zeros((tq, LANES), BF16)

    def load_q(qi, dst):
        rows = pl.ds(pl.multiple_of(qi * tq, tq), tq)
        for p in range(group // 2):
            pair = q_ref[rows, p * LANES:(p + 1) * LANES]
            dst[2 * p] = jnp.where(low, pair, zero)
            dst[2 * p + 1] = jnp.where(low, zero, pair)

    def key_tile(j):
        return kt_ref[0, 0, :, j * tk:(j + 1) * tk]

    def step(j, cur_ref, nxt_ref, q_next, j_next):
        v = v_ref[0, 0, j * tk:(j + 1) * tk, :]
        kt = key_tile(j_next)
        nxt_ref[0] = jnp.dot(q_next[0], kt, preferred_element_type=F32)
        for g in range(group):
            sc = cur_ref[g]
            m_old = m_ref[g]
            m_new = jnp.maximum(m_old, jnp.max(sc, axis=-1, keepdims=True))
            p = jnp.exp2(sc - m_new)
            alpha = jnp.exp2(m_old - m_new)
            if g + 1 < group:
                nxt_ref[g + 1] = jnp.dot(q_next[g + 1], kt, preferred_element_type=F32)
            acc_ref[g] = alpha * acc_ref[g] + jnp.dot(p.astype(BF16), v, preferred_element_type=F32)
            m_ref[g] = m_new

    load_q(0, q_s)
    for g in range(group):
        sa_ref[g] = jnp.dot(q_s[g], key_tile(0), preferred_element_type=F32)

    def body(qi, carry):
        load_q(qi, q_s)
        load_q(jnp.minimum(qi + 1, nq - 1), qn_s)
        m_ref[...] = jnp.full(m_ref.shape, -jnp.inf, F32)
        acc_ref[...] = jnp.zeros(acc_ref.shape, F32)
        bufs = (sa_ref, sb_ref)
        for j in range(nk):
            last = j + 1 == nk
            step(j, bufs[j % 2], bufs[(j + 1) % 2], qn_s if last else q_s, 0 if last else j + 1)
        outs = []
        for g in range(group):
            acc = acc_ref[g]
            outs.append(acc / pltpu.roll(acc, HEAD_DIM, 1))
        rows = pl.ds(pl.multiple_of(qi * tq, tq), tq)
        for p in range(group // 2):
            pair = jnp.where(low, outs[2 * p], pltpu.roll(outs[2 * p + 1], HEAD_DIM, 1))
            o_ref[rows, p * LANES:(p + 1) * LANES] = pair.astype(BF16)
        return carry

    lax.fori_loop(0, nq, body, 0)


def _attention(qa, kdt, vd, b, s):
    n = qa.shape[0]
    tq = min(Q_TILE, s)
    tk = min(KV_TILE, s // 2)
    assert s % tq == 0 and s % (2 * tk) == 0 and tk % LANES == 0, (s, tq, tk)
    gw = ATTN_W // KV_HEADS
    group = Q_HEADS // KV_HEADS
    return pl.pallas_call(
        functools.partial(_attn_kernel, tq=tq, tk=tk),
        grid=(b, KV_HEADS),
        in_specs=[
            pl.BlockSpec((s, gw), lambda bi, h: (bi, h)),
            pl.BlockSpec((1, 1, LANES, s), lambda bi, h: (bi, h, 0, 0)),
            pl.BlockSpec((1, 1, s, LANES), lambda bi, h: (bi, h, 0, 0)),
        ],
        out_specs=pl.BlockSpec((s, gw), lambda bi, h: (bi, h)),
        out_shape=jax.ShapeDtypeStruct((n, ATTN_W), BF16),
        scratch_shapes=[
            pltpu.VMEM((group, tq, LANES), BF16),
            pltpu.VMEM((group, tq, LANES), BF16),
            pltpu.VMEM((group, tq, 1), F32),
            pltpu.VMEM((group, tq, LANES), F32),
            pltpu.VMEM((group, tq, tk), F32),
            pltpu.VMEM((group, tq, tk), F32),
        ],
        compiler_params=_params(("parallel", "parallel")),
        name="attention",
    )(qa, kdt, vd)


def _retention_kernel(dec_ref, q_ref, k_ref, v_ref, g_ref, gng_ref, gnb_ref, o_ref,
                      fbuf, bbuf, sf_ref, sb_ref):
    h = pl.program_id(1)
    c = RET_CHUNK
    s = q_ref.shape[1]
    nc = s // c
    lgf = jnp.full((1, 1), dec_ref[0, h], F32)
    lgb = jnp.full((1, 1), dec_ref[1, h], F32)
    ii = lax.broadcasted_iota(I32, (c, c), 0)
    jj = lax.broadcasted_iota(I32, (c, c), 1)
    diff = (ii - jj).astype(F32)
    d_f = jnp.where(diff >= 0, jnp.exp(lgf * jnp.maximum(diff, 0.0)), 0.0)
    d_b = jnp.where(diff <= 0, jnp.exp(lgb * jnp.maximum(-diff, 0.0)), 0.0)
    idx = lax.broadcasted_iota(I32, (c, 1), 0).astype(F32)
    xi_f = jnp.exp(lgf * (idx + 1.0))
    zeta_f = jnp.exp(lgf * (c - 1.0 - idx))
    xi_b = jnp.exp(lgb * (c - idx))
    zeta_b = jnp.exp(lgb * idx)
    cd_f = jnp.exp(lgf * c)
    cd_b = jnp.exp(lgb * c)
    sf_ref[...] = jnp.zeros(sf_ref.shape, F32)
    sb_ref[...] = jnp.zeros(sb_ref.shape, F32)
    nt = (((1,), (1,)), ((), ()))

    fwd = (d_f, xi_f, zeta_f, cd_f, sf_ref, fbuf)
    bwd = (d_b, xi_b, zeta_b, cd_b, sb_ref, bbuf)

    def body(i, carry):
        jobs = [(fwd, 2 * i), (bwd, nc - 1 - 2 * i), (fwd, 2 * i + 1), (bwd, nc - 2 - 2 * i)]
        offs = [pl.multiple_of(ci * c, c) for _, ci in jobs]
        qs = [q_ref[0, pl.ds(off, c), :] for off in offs]
        ks = [k_ref[0, pl.ds(off, c), :] for off in offs]
        vs = [v_ref[0, pl.ds(off, c), :] for off in offs]
        scores = [lax.dot_general(q, k, nt, preferred_element_type=F32) for q, k in zip(qs, ks)]
        incs = []
        for (d, _), k, v in zip(jobs, ks, vs):
            kzt = (k.astype(F32) * d[2]).T.astype(BF16)
            incs.append(jnp.dot(kzt, v, preferred_element_type=F32))
        states = [fwd[4][...], bwd[4][...]]
        cross = [jnp.dot(qs[n], states[n].astype(BF16), preferred_element_type=F32) for n in range(2)]
        states += [states[n] * jobs[n][0][3] + incs[n] for n in range(2)]
        cross += [jnp.dot(qs[n], states[n].astype(BF16), preferred_element_type=F32) for n in range(2, 4)]
        for n, ((d, _), off) in enumerate(zip(jobs, offs)):
            a = (scores[n] * d[0]).astype(BF16)
            d[5][pl.ds(off, c), :] = jnp.dot(a, vs[n], preferred_element_type=F32) + cross[n] * d[1]
        for n in range(2):
            jobs[n][0][4][...] = states[2 + n] * jobs[n][0][3] + incs[2 + n]
        return carry

    lax.fori_loop(0, nc // 2, body, 0)

    rows = min(512, s)

    def fin(t, carry):
        off = pl.multiple_of(t * rows, rows)
        r = fbuf[pl.ds(off, rows), :] + bbuf[pl.ds(off, rows), :]
        mu = jnp.mean(r, axis=-1, keepdims=True)
        d = r - mu
        var = jnp.mean(d * d, axis=-1, keepdims=True)
        y = d * lax.rsqrt(var + NORM_EPS) * gng_ref[...] + gnb_ref[...]
        o_ref[0, pl.ds(off, rows), :] = (g_ref[0, pl.ds(off, rows), :].astype(F32) * y).astype(BF16)
        return carry

    lax.fori_loop(0, s // rows, fin, 0)


def _retention(qr, kr, vr, gr, dec, gn_g, gn_b, b, s):
    shp = (b, s, RET_W)
    assert s % (2 * RET_CHUNK) == 0, s
    head = pl.BlockSpec((1, s, RET_DIM), lambda bi, h: (bi, 0, h))
    vec = pl.BlockSpec((1, RET_DIM), lambda bi, h: (0, h))
    out = pl.pallas_call(
        _retention_kernel,
        grid=(b, RET_HEADS),
        in_specs=[pl.BlockSpec(memory_space=pltpu.SMEM), head, head, head, head, vec, vec],
        out_specs=head,
        out_shape=jax.ShapeDtypeStruct(shp, BF16),
        scratch_shapes=[
            pltpu.VMEM((s, RET_DIM), F32),
            pltpu.VMEM((s, RET_DIM), F32),
            pltpu.VMEM((RET_DIM, RET_DIM), F32),
            pltpu.VMEM((RET_DIM, RET_DIM), F32),
        ],
        compiler_params=_params(("parallel", "parallel")),
        name="retention",
    )(dec, qr.reshape(shp), kr.reshape(shp), vr.reshape(shp), gr.reshape(shp), gn_g, gn_b)
    return out.reshape(b * s, RET_W)


def _outproj_kernel(oa_ref, or_ref, w_ref, x_ref, mod_ref, pg_ref, fg_ref, rw_ref, rb_ref,
                    xn_ref, hp_ref, ti_ref, tw_ref, rk_ref, cnt_ref, base_ref):
    @pl.when(pl.program_id(0) == 0)
    def _():
        base_ref[...] = jnp.zeros(base_ref.shape, F32)

    m = mod_ref[0]
    y = jnp.dot(oa_ref[...], w_ref[:ATTN_W, :], preferred_element_type=F32)
    y = y + jnp.dot(or_ref[...], w_ref[ATTN_W:, :], preferred_element_type=F32)
    xn = x_ref[...] + m[2:3, :] * _rms(y, pg_ref[...])
    xn_ref[...] = xn
    h = _rms(xn, fg_ref[...]) * (1.0 + m[4:5, :]) + m[3:4, :]
    hb = h.astype(BF16)
    u = lax.bitcast_convert_type(hb.astype(F32), U32)
    hp_ref[...] = (u[:, :HALF] >> 16) | u[:, HALF:]

    tm = hb.shape[0]
    nt = (((1,), (1,)), ((), ()))
    logits = lax.dot_general(rw_ref[...], hb, nt, preferred_element_type=F32) + rb_ref[...]
    e_iota = lax.broadcasted_iota(I32, (N_EXPERTS, tm), 0).astype(F32)
    vals, idxs = [], []
    cur = logits
    for _ in range(TOP_K):
        mx = jnp.max(cur, axis=0, keepdims=True)
        ik = jnp.min(jnp.where(cur == mx, e_iota, float(N_EXPERTS)), axis=0, keepdims=True)
        vals.append(mx)
        idxs.append(ik)
        cur = jnp.where(e_iota == ik, -jnp.inf, cur)
    v = jnp.concatenate(vals, axis=0)
    w = jnp.exp(v - v[0:1, :])
    tw_ref[...] = w / jnp.sum(w, axis=0, keepdims=True)
    ti_ref[...] = jnp.concatenate(idxs, axis=0).astype(I32)

    onehot = jnp.zeros((N_EXPERTS, tm), F32)
    for ik in idxs:
        onehot = onehot + (e_iota == ik).astype(F32)
    before = (lax.broadcasted_iota(I32, (tm, tm), 0) < lax.broadcasted_iota(I32, (tm, tm), 1)).astype(BF16)
    seen = jnp.dot(onehot.astype(BF16), before, preferred_element_type=F32) + base_ref[...]
    ranks = [jnp.sum(jnp.where(e_iota == ik, seen, 0.0), axis=0, keepdims=True) for ik in idxs]
    rk_ref[...] = jnp.concatenate(ranks, axis=0).astype(I32)
    total = base_ref[...] + jnp.sum(onehot, axis=1, keepdims=True)
    base_ref[...] = total
    cnt_ref[...] = jnp.broadcast_to(total, cnt_ref.shape).astype(I32)


def _outproj(oa, orr, w_out_b, x2, batch0, x_batch0, mod_l, post_g, ffn_g, rw_t, rb, b, s):
    n = b * s
    tm = min(ROW_TILE, s)
    nb = s // tm
    row = lambda i: (i, 0)
    full = lambda i: (0, 0)
    col = lambda i: (0, i)
    return pl.pallas_call(
        _outproj_kernel,
        grid=(n // tm,),
        in_specs=[
            pl.BlockSpec((tm, ATTN_W), row),
            pl.BlockSpec((tm, RET_W), row),
            pl.BlockSpec((D_MODEL, D_MODEL), full),
            pl.BlockSpec((tm, D_MODEL), lambda i: (x_batch0 * nb + i, 0)),
            pl.BlockSpec((1, 6, D_MODEL), lambda i: (batch0 + i // nb, 0, 0)),
            pl.BlockSpec((1, D_MODEL), full),
            pl.BlockSpec((1, D_MODEL), full),
            pl.BlockSpec((N_EXPERTS, D_MODEL), full),
            pl.BlockSpec((N_EXPERTS, 1), full),
        ],
        out_specs=[
            pl.BlockSpec((tm, D_MODEL), row),
            pl.BlockSpec((tm, HALF), row),
            pl.BlockSpec((TOP_K, tm), col),
            pl.BlockSpec((TOP_K, tm), col),
            pl.BlockSpec((TOP_K, tm), col),
            pl.BlockSpec((N_EXPERTS, LANES), full),
        ],
        out_shape=[
            jax.ShapeDtypeStruct((n, D_MODEL), F32),
            jax.ShapeDtypeStruct((n, HALF), U32),
            jax.ShapeDtypeStruct((TOP_K, n), I32),
            jax.ShapeDtypeStruct((TOP_K, n), F32),
            jax.ShapeDtypeStruct((TOP_K, n), I32),
            jax.ShapeDtypeStruct((N_EXPERTS, LANES), I32),
        ],
        scratch_shapes=[pltpu.VMEM((N_EXPERTS, 1), F32)],
        compiler_params=_params(("arbitrary",)),
        name="outproj_router",
    )(oa, orr, w_out_b, x2, mod_l, post_g, ffn_g, rw_t, rb)


def _sc_scatter_rows(rows, pos3, n_out):
    n, w = rows.shape
    nchunk, kk, c = pos3.shape
    per_w = nchunk // SC_WORKERS
    mesh = plsc.VectorSubcoreMesh(core_axis_name="c", subcore_axis_name="s")

    @functools.partial(
        pl.kernel, mesh=mesh,
        out_type=jax.ShapeDtypeStruct((n_out, w), rows.dtype),
        scratch_types=[pltpu.VMEM((kk, c), I32), pltpu.VMEM((c, w), rows.dtype)],
    )
    def k(rows_hbm, pos_hbm, out_hbm, idx_v, rows_v):
        wid = lax.axis_index("s") * SC_CORES + lax.axis_index("c")

        @pl.loop(0, per_w)
        def _(i):
            ch = wid * per_w + i
            pltpu.sync_copy(pos_hbm.at[ch], idx_v)
            pltpu.sync_copy(rows_hbm.at[pl.ds(ch * c, c)], rows_v)
            for j in range(kk):
                pltpu.sync_copy(rows_v, out_hbm.at[idx_v.at[j]])

    return k(rows, pos3)


def _sc_gather_rows(table, idx):
    w = table.shape[1]
    b = idx.shape[0]
    c = SC_CHUNK
    per_w = b // (SC_WORKERS * c)
    mesh = plsc.VectorSubcoreMesh(core_axis_name="c", subcore_axis_name="s")

    @functools.partial(
        pl.kernel, mesh=mesh,
        out_type=jax.ShapeDtypeStruct((b, w), table.dtype),
        scratch_types=[pltpu.VMEM((c,), I32), pltpu.VMEM((c, w), table.dtype), pltpu.SemaphoreType.DMA],
    )
    def k(table_hbm, idx_hbm, out_hbm, idx_v, rows_v, sem):
        wid = lax.axis_index("s") * SC_CORES + lax.axis_index("c")

        @pl.loop(0, per_w)
        def _(i):
            base = (wid * per_w + i) * c
            pltpu.sync_copy(idx_hbm.at[pl.ds(base, c)], idx_v)
            pltpu.async_copy(table_hbm.at[idx_v], rows_v, sem).wait()
            pltpu.sync_copy(rows_v, out_hbm.at[pl.ds(base, c)])

    return k(table, idx)


def _experts_kernel(te_ref, nv_ref, xs_ref, wu_ref, bu_ref, wd_ref, bd_ref, ys_ref, wu_b, wd_b):
    i = pl.program_id(0)
    live = i < nv_ref[0]
    new_expert = jnp.logical_or(i == 0, te_ref[i] != te_ref[jnp.maximum(i - 1, 0)])

    @pl.when(jnp.logical_and(live, new_expert))
    def _():
        wu_b[...] = wu_ref[0, 0].astype(BF16)
        wd_b[...] = wd_ref[0, 0].astype(BF16)

    @pl.when(live)
    def _():
        w = xs_ref[...]
        lo = _unpack_lo(w).astype(BF16)
        hi = _unpack_hi(w).astype(BF16)
        up = jnp.dot(lo, wu_b[:HALF, :], preferred_element_type=F32)
        up = up + jnp.dot(hi, wu_b[HALF:, :], preferred_element_type=F32) + bu_ref[0, 0]
        glu = jnp.minimum(up[:, :D_FF], SWIGLU_LIMIT)
        lin = jnp.clip(up[:, D_FF:], -SWIGLU_LIMIT, SWIGLU_LIMIT)
        act = glu * jax.nn.sigmoid(SWIGLU_ALPHA * glu) * (lin + 1.0)
        y = jnp.dot(act.astype(BF16), wd_b[...], preferred_element_type=F32) + bd_ref[0, 0]
        ys_ref[...] = _pack_bf16_pairs(y)


def _experts(xs, tile_expert, n_valid, layer, wu, bu, wd, bd):
    n_pad = xs.shape[0]
    t = MOE_TILE
    grid_spec = pltpu.PrefetchScalarGridSpec(
        num_scalar_prefetch=2,
        grid=(n_pad // t,),
        in_specs=[
            pl.BlockSpec((t, HALF), lambda i, te, nv: (i, 0)),
            pl.BlockSpec((1, 1, D_MODEL, 2 * D_FF), lambda i, te, nv: (layer, te[i], 0, 0)),
            pl.BlockSpec((1, 1, 1, 2 * D_FF), lambda i, te, nv: (layer, te[i], 0, 0)),
            pl.BlockSpec((1, 1, D_FF, D_MODEL), lambda i, te, nv: (layer, te[i], 0, 0)),
            pl.BlockSpec((1, 1, 1, D_MODEL), lambda i, te, nv: (layer, te[i], 0, 0)),
        ],
        out_specs=pl.BlockSpec((t, HALF), lambda i, te, nv: (i, 0)),
        scratch_shapes=[pltpu.VMEM((D_MODEL, 2 * D_FF), BF16), pltpu.VMEM((D_FF, D_MODEL), BF16)],
    )
    return pl.pallas_call(
        _experts_kernel,
        grid_spec=grid_spec,
        out_shape=jax.ShapeDtypeStruct((n_pad, HALF), U32),
        compiler_params=pltpu.CompilerParams(dimension_semantics=("arbitrary",), vmem_limit_bytes=EXPERTS_VMEM_LIMIT),
        name="experts",
    )(tile_expert, n_valid, xs, wu, bu, wd, bd)


def _combine_kernel(y0_ref, y1_ref, y2_ref, y3_ref, w_ref, x_ref, mod_ref, g_ref, *rest):
    o_ref = rest[-1]
    w = w_ref[...]
    lo = None
    hi = None
    for k, y_ref in enumerate((y0_ref, y1_ref, y2_ref, y3_ref)):
        word = y_ref[...]
        wk = w[:, k:k + 1]
        lo_k = wk * _unpack_lo(word)
        hi_k = wk * _unpack_hi(word)
        lo = lo_k if lo is None else lo + lo_k
        hi = hi_k if hi is None else hi + hi_k
    yf = jnp.concatenate([lo, hi], axis=1)
    m = mod_ref[0]
    o_ref[...] = x_ref[...] + m[5:6, :] * _rms(yf, g_ref[...])


def _combine(yg, tw_t, xn, batch0, mod_l, post_g, b, s, out_batches=None, shared_out=None):
    n = b * s
    tm = min(ROW_TILE, s)
    nb = s // tm
    nblk = n // tm
    row = lambda i: (i, 0)
    full = lambda i: (0, 0)
    y_specs = [pl.BlockSpec((tm, HALF), functools.partial(lambda i, k: (k * nblk + i, 0), k=k)) for k in range(TOP_K)]
    out_rows, out_blk0 = (n, 0) if out_batches is None else (out_batches * s, batch0 * nb)
    extra_specs, extra_args, aliases = [], [], {}
    if shared_out is not None:
        extra_specs, extra_args, aliases = [pl.BlockSpec(memory_space=pl.ANY)], [shared_out], {8: 0}
    return pl.pallas_call(
        _combine_kernel,
        grid=(nblk,),
        in_specs=y_specs + [
            pl.BlockSpec((tm, TOP_K), row),
            pl.BlockSpec((tm, D_MODEL), row),
            pl.BlockSpec((1, 6, D_MODEL), lambda i: (batch0 + i // nb, 0, 0)),
            pl.BlockSpec((1, D_MODEL), full),
        ] + extra_specs,
        out_specs=pl.BlockSpec((tm, D_MODEL), lambda i: (out_blk0 + i, 0)),
        out_shape=jax.ShapeDtypeStruct((out_rows, D_MODEL), F32),
        input_output_aliases=aliases,
        compiler_params=_params(("parallel",)),
        name="combine",
    )(yg, yg, yg, yg, tw_t, xn, mod_l, post_g, *extra_args)


def _route(counts, tidx, rank, n_tiles):
    t = MOE_TILE
    cnt = counts[:, 0]
    padded = ((cnt + t - 1) // t) * t
    ends = jnp.cumsum(padded)
    starts = ends - padded
    experts = jnp.arange(N_EXPERTS, dtype=I32)[:, None, None]
    start_of = jnp.sum(jnp.where(tidx[None] == experts, starts[:, None, None], 0), axis=0)
    pos = start_of + rank
    tile_ends = ends // t
    n_valid = tile_ends[-1]
    tile = jnp.minimum(jnp.arange(n_tiles, dtype=I32), n_valid - 1)
    tile_expert = jnp.minimum(jnp.sum(tile_ends[None, :] <= tile[:, None], axis=1), N_EXPERTS - 1).astype(I32)
    return pos.astype(I32), tile_expert, n_valid.reshape(1).astype(I32)


def kernel(x, c, ada_w, ada_b, pre_mix_g, post_mix_g, w_in, q_norm_g, k_norm_g, ret_decay_fwd, ret_decay_bwd, ret_gn_g, ret_gn_b, w_out, pre_ffn_g, post_ffn_g, router_w, router_b, exp_w_up, exp_b_up, exp_w_down, exp_b_down):
    b, s, d = x.shape
    depth = ada_w.shape[0]
    groups = 2 if b % 2 == 0 else 1
    bg = b // groups
    n = bg * s
    n_asg = n * TOP_K
    n_pad = n_asg + N_EXPERTS * MOE_TILE
    n_tiles = n_pad // MOE_TILE

    mod = _modulation(c, ada_w, ada_b)
    ropes = _rope_tables(s, HEAD_DIM) + _rope_tables(s, RET_DIM)
    bu = exp_b_up.reshape(depth, N_EXPERTS, 1, 2 * D_FF)
    bd = exp_b_down.reshape(depth, N_EXPERTS, 1, d)
    xs_in = [x.reshape(b * s, d)] * groups
    x_b0 = [g * bg for g in range(groups)]
    out = None
    for l in range(depth):
        mod_l = mod[l].reshape(b, 6, d)
        qg = jnp.tile(q_norm_g[l], LANES // HEAD_DIM).reshape(1, LANES)
        kg = jnp.tile(k_norm_g[l], LANES // HEAD_DIM).reshape(1, LANES)
        w_in_b = w_in[l].astype(BF16)
        w_out_b = w_out[l].astype(BF16)
        rw_t = router_w[l].T.astype(BF16)
        dec = jnp.stack([ret_decay_fwd[l], ret_decay_bwd[l]]).astype(F32)
        last = l + 1 == depth
        proj = [_inproj(xs_in[g], g * bg, x_b0[g], mod_l, pre_mix_g[l].reshape(1, d), w_in_b, ropes, qg, kg, bg, s)
                for g in range(groups)]
        oa = [_attention(p[0], p[1], p[2], bg, s) for p in proj]
        orr = [_retention(p[3], p[4], p[5], p[6], dec, ret_gn_g[l].reshape(1, RET_W), ret_gn_b[l].reshape(1, RET_W),
                          bg, s) for p in proj]
        routed = []
        for g in range(groups):
            xn, hp, tidx, tw, rank, counts = _outproj(
                oa[g], orr[g], w_out_b, xs_in[g], g * bg, x_b0[g], mod_l, post_mix_g[l].reshape(1, d),
                pre_ffn_g[l].reshape(1, d), rw_t, router_b[l].reshape(N_EXPERTS, 1), bg, s)
            pos, tile_expert, n_valid = _route(counts, tidx, rank, n_tiles)
            pos3 = pos.reshape(TOP_K, n // SC_CHUNK, SC_CHUNK).transpose(1, 0, 2)
            routed.append((xn, tw, pos, tile_expert, n_valid, _sc_scatter_rows(hp, pos3, n_pad)))
        yg = []
        for xn, tw, pos, tile_expert, n_valid, xs in routed:
            ys = _experts(xs, tile_expert, n_valid, l, exp_w_up, bu, exp_w_down, bd)
            yg.append(_sc_gather_rows(ys, pos.reshape(n_asg)))
        nxt = []
        for g, (xn, tw, *_) in enumerate(routed):
            if last:
                out = _combine(yg[g], tw.T, xn, g * bg, mod_l, post_ffn_g[l].reshape(1, d), bg, s,
                               out_batches=b, shared_out=out)
            else:
                nxt.append(_combine(yg[g], tw.T, xn, g * bg, mod_l, post_ffn_g[l].reshape(1, d), bg, s))
        xs_in, x_b0 = nxt, [0] * groups
    return out.reshape(b, s, d)
```

```python
import functools

import numpy as np
import jax
import jax.numpy as jnp
from jax import lax
from jax.experimental import pallas as pl
from jax.experimental.pallas import tpu as pltpu
from jax.experimental.pallas import tpu_sc as plsc

F32 = jnp.float32
BF16 = jnp.bfloat16
U32 = jnp.uint32
I32 = jnp.int32

D_MODEL = 1024
GRID_W = 64
ROPE_THETA = 10000.0
HEAD_DIM = 64
Q_HEADS = 8
KV_HEADS = 2
RET_HEADS = 4
RET_DIM = 128
RET_CHUNK = 128
ATTN_W = Q_HEADS * HEAD_DIM
KV_W = KV_HEADS * HEAD_DIM
RET_W = RET_HEADS * RET_DIM
IN_W = ATTN_W + 2 * KV_W + 4 * RET_W
N_EXPERTS = 32
TOP_K = 4
D_FF = D_MODEL
SWIGLU_LIMIT = 7.0
SWIGLU_ALPHA = 1.702
NORM_EPS = 1e-6
Q_SCALE = HEAD_DIM ** -0.5 * float(np.log2(np.e))
HALF = D_MODEL // 2
LANES = 128

ROW_TILE = 512
Q_TILE = 128
KV_TILE = 1024
MOE_TILE = 512
VMEM_LIMIT = 48 * 1024 * 1024
EXPERTS_VMEM_LIMIT = 56 * 1024 * 1024

SC_CORES = 2
SC_SUBCORES = 16
SC_WORKERS = SC_CORES * SC_SUBCORES
SC_CHUNK = 64


def _params(sem):
    return pltpu.CompilerParams(dimension_semantics=sem, vmem_limit_bytes=VMEM_LIMIT)


def _rms(x, g):
    return x * lax.rsqrt(jnp.mean(x * x, axis=-1, keepdims=True) + NORM_EPS) * g


def _pack_bf16_pairs(y):
    u = lax.bitcast_convert_type(y.astype(BF16).astype(F32), U32)
    return (u[:, :HALF] >> 16) | u[:, HALF:]


def _unpack_lo(w):
    return lax.bitcast_convert_type(w << 16, F32)


def _unpack_hi(w):
    return lax.bitcast_convert_type(w & jnp.uint32(0xFFFF0000), F32)


def _mod_kernel(c_ref, w_ref, b_ref, o_ref):
    c = c_ref[...]
    cond = c * jax.nn.sigmoid(c)
    o_ref[0] = jnp.dot(cond, w_ref[0], preferred_element_type=F32, precision=lax.Precision.HIGHEST) + b_ref[0]


def _modulation(c, ada_w, ada_b):
    depth, d, w6 = ada_w.shape
    b = c.shape[0]
    tn = 1536
    return pl.pallas_call(
        _mod_kernel,
        grid=(depth, w6 // tn),
        in_specs=[
            pl.BlockSpec((b, d), lambda l, j: (0, 0)),
            pl.BlockSpec((1, d, tn), lambda l, j: (l, 0, j)),
            pl.BlockSpec((1, 1, tn), lambda l, j: (l, 0, j)),
        ],
        out_specs=pl.BlockSpec((1, b, tn), lambda l, j: (l, 0, j)),
        out_shape=jax.ShapeDtypeStruct((depth, b, w6), F32),
        compiler_params=_params(("arbitrary", "arbitrary")),
        name="modulation",
    )(c, ada_w, ada_b.reshape(depth, 1, w6))


def _rope_tables(s, head_dim):
    quarter = head_dim // 4
    t = jnp.arange(s, dtype=F32)
    row = jnp.floor(t / GRID_W)
    col = t - row * GRID_W
    inv_freq = ROPE_THETA ** (-jnp.arange(quarter, dtype=F32) / quarter)
    lane = np.arange(LANES) % head_dim
    use_col = lane >= head_dim // 2
    second = (lane % (head_dim // 2)) >= quarter
    freq = inv_freq[lane % quarter]
    pos = jnp.where(use_col[None, :], col[:, None], row[:, None])
    ang = pos * freq[None, :]
    sign = jnp.where(second, 1.0, -1.0).astype(F32)
    return jnp.cos(ang), jnp.sin(ang) * sign[None, :]


def _rope(z, cos, sin_signed, quarter):
    lane = lax.broadcasted_iota(I32, z.shape, 1)
    first = (lane % (2 * quarter)) < quarter
    partner = jnp.where(first, pltpu.roll(z, LANES - quarter, 1), pltpu.roll(z, quarter, 1))
    return z * cos + partner * sin_signed


def _inproj_kernel(x_ref, mod_ref, g_ref, w_ref, ca_ref, sa_ref, cr_ref, sr_ref, qg_ref, kg_ref,
                   qa_ref, kd_ref, vt_ref, qr_ref, kr_ref, vr_ref, gr_ref):
    x = x_ref[...]
    m = mod_ref[0]
    h = _rms(x, g_ref[...]) * (1.0 + m[1:2, :]) + m[0:1, :]
    hb = h.astype(BF16)
    tm = x.shape[0]

    def proj(c0):
        z = jnp.dot(hb, w_ref[:, c0:c0 + 2 * LANES], preferred_element_type=F32)
        return z[:, :LANES], z[:, LANES:]

    ri = lax.broadcasted_iota(I32, (LANES, LANES), 0) // HEAD_DIM
    ci = lax.broadcasted_iota(I32, (LANES, LANES), 1) // HEAD_DIM
    head_ones = (ri == ci).astype(BF16)
    ca, sa = ca_ref[...], sa_ref[...]
    cr, sr = cr_ref[...], sr_ref[...]
    lane = lax.broadcasted_iota(I32, (tm, LANES), 1)
    low = lane < HEAD_DIM

    def head_norm_rope(z, g):
        ss = jnp.dot((z * z).astype(BF16), head_ones, preferred_element_type=F32)
        zn = z * lax.rsqrt(ss * (1.0 / HEAD_DIM) + NORM_EPS) * g
        return _rope(zn, ca, sa, HEAD_DIM // 4)

    def store_pair(ref, j, fn, c0):
        for i, z in enumerate(proj(c0 + 2 * j * LANES)):
            ref[:, (2 * j + i) * LANES:(2 * j + i + 1) * LANES] = fn(z).astype(BF16)

    for j in range(ATTN_W // (2 * LANES)):
        store_pair(qa_ref, j, lambda z: head_norm_rope(z, qg_ref[...]) * Q_SCALE, 0)

    k, v = proj(ATTN_W)
    k = head_norm_rope(k, kg_ref[...])
    k_sw = pltpu.roll(k, HEAD_DIM, 1)
    kd_ref[0, 0] = jnp.where(low, k, k_sw).astype(BF16)
    kd_ref[0, 1] = jnp.where(low, k_sw, k).astype(BF16)
    vt_ref[0, 0] = jnp.where(low, v, 1.0).T.astype(BF16)
    vt_ref[0, 1] = jnp.where(low, pltpu.roll(v, HEAD_DIM, 1), 1.0).T.astype(BF16)

    base = ATTN_W + 2 * KV_W
    for j in range(RET_W // (2 * LANES)):
        store_pair(qr_ref, j, lambda z: _rope(z, cr, sr, RET_DIM // 4), base)
        store_pair(kr_ref, j, lambda z: _rope(z, cr, sr, RET_DIM // 4) * (RET_DIM ** -0.5), base + RET_W)
        store_pair(vr_ref, j, lambda z: z, base + 2 * RET_W)
        store_pair(gr_ref, j, lambda z: z * jax.nn.sigmoid(z), base + 3 * RET_W)


def _inproj(x2, batch0, x_batch0, mod_l, pre_g, w_in_b, ropes, qg, kg, b, s):
    n = b * s
    tm = min(ROW_TILE, s)
    nb = s // tm
    ca, sa, cr, sr = ropes
    row = lambda i: (i, 0)
    full = lambda i: (0, 0)
    rope_spec = pl.BlockSpec((tm, LANES), lambda i: (i % nb, 0))
    wide = jax.ShapeDtypeStruct((n, RET_W), BF16)
    return pl.pallas_call(
        _inproj_kernel,
        grid=(n // tm,),
        in_specs=[
            pl.BlockSpec((tm, D_MODEL), lambda i: (x_batch0 * nb + i, 0)),
            pl.BlockSpec((1, 6, D_MODEL), lambda i: (batch0 + i // nb, 0, 0)),
            pl.BlockSpec((1, D_MODEL), full),
            pl.BlockSpec((D_MODEL, IN_W), full),
            rope_spec, rope_spec, rope_spec, rope_spec,
            pl.BlockSpec((1, LANES), full),
            pl.BlockSpec((1, LANES), full),
        ],
        out_specs=[
            pl.BlockSpec((tm, ATTN_W), row),
            pl.BlockSpec((1, KV_HEADS, tm, LANES), lambda i: (i // nb, 0, i % nb, 0)),
            pl.BlockSpec((1, KV_HEADS, LANES, tm), lambda i: (i // nb, 0, 0, i % nb)),
            pl.BlockSpec((tm, RET_W), row),
            pl.BlockSpec((tm, RET_W), row),
            pl.BlockSpec((tm, RET_W), row),
            pl.BlockSpec((tm, RET_W), row),
        ],
        out_shape=[
            jax.ShapeDtypeStruct((n, ATTN_W), BF16),
            jax.ShapeDtypeStruct((b, KV_HEADS, s, LANES), BF16),
            jax.ShapeDtypeStruct((b, KV_HEADS, LANES, s), BF16),
            wide, wide, wide, wide,
        ],
        compiler_params=_params(("parallel",)),
        name="inproj",
    )(x2, mod_l, pre_g, w_in_b, ca, sa, cr, sr, qg, kg)


def _attn_kernel(q_ref, kd_ref, vt_ref, o_ref, q_s, qn_s, m_ref, acc_ref, s0_ref, s1_ref, *, tq, tk):
    s = kd_ref.shape[2]
    nk = s // tk
    nq = s // tq
    pairs = (Q_HEADS // KV_HEADS) // 2
    top = lax.broadcasted_iota(I32, (LANES, tq), 0) < HEAD_DIM

    def load_q(qi, dst):
        rows = pl.ds(pl.multiple_of(qi * tq, tq), tq)
        for p in range(pairs):
            qt = q_ref[rows, p * LANES:(p + 1) * LANES].astype(F32).T
            dst[p, :, :tq] = jnp.where(top, qt, 0.0).astype(BF16)
            dst[p, :, tq:] = jnp.where(top, 0.0, qt).astype(BF16)

    def keys(j):
        return kd_ref[0, 0, j * tk:(j + 1) * tk, :]

    def step(j, cur_ref, nxt_ref, q_next, j_next):
        vt = vt_ref[0, 0, :, j * tk:(j + 1) * tk]
        kd = keys(j_next)
        nxt_ref[0] = jnp.dot(kd, q_next[0], preferred_element_type=F32)
        for p in range(pairs):
            sc = cur_ref[p]
            m_old = m_ref[p]
            m_new = jnp.maximum(m_old, jnp.max(sc, axis=0, keepdims=True))
            pt = jnp.exp2(sc - m_new)
            alpha = jnp.exp2(m_old - m_new)
            if p + 1 < pairs:
                nxt_ref[p + 1] = jnp.dot(kd, q_next[p + 1], preferred_element_type=F32)
            acc_ref[p] = alpha * acc_ref[p] + jnp.dot(vt, pt.astype(BF16), preferred_element_type=F32)
            m_ref[p] = m_new

    bufs = (s0_ref, s1_ref)
    load_q(0, q_s)
    for p in range(pairs):
        s0_ref[p] = jnp.dot(keys(0), q_s[p], preferred_element_type=F32)

    def body(qi, carry):
        load_q(qi, q_s)
        load_q(jnp.minimum(qi + 1, nq - 1), qn_s)
        m_ref[...] = jnp.full(m_ref.shape, -jnp.inf, F32)
        acc_ref[...] = jnp.zeros(acc_ref.shape, F32)
        for j in range(nk):
            last = j + 1 == nk
            step(j, bufs[j % len(bufs)], bufs[(j + 1) % len(bufs)], qn_s if last else q_s, 0 if last else j + 1)
        rows = pl.ds(pl.multiple_of(qi * tq, tq), tq)
        for p in range(pairs):
            acc = acc_ref[p]
            o_t = acc[:HEAD_DIM] / acc[HEAD_DIM:]
            both = jnp.concatenate([o_t[:, :tq], o_t[:, tq:]], axis=0)
            o_ref[rows, p * LANES:(p + 1) * LANES] = both.T.astype(BF16)
        return carry

    lax.fori_loop(0, nq, body, 0)


def _attention(qa, kd, vt, b, s):
    n = qa.shape[0]
    tq = Q_TILE
    tk = min(KV_TILE, s // 2)
    assert tq == LANES and s % tq == 0 and s % (2 * tk) == 0 and tk % LANES == 0, (s, tq, tk)
    gw = ATTN_W // KV_HEADS
    pairs = (Q_HEADS // KV_HEADS) // 2
    return pl.pallas_call(
        functools.partial(_attn_kernel, tq=tq, tk=tk),
        grid=(b, KV_HEADS),
        in_specs=[
            pl.BlockSpec((s, gw), lambda bi, h: (bi, h)),
            pl.BlockSpec((1, 1, s, LANES), lambda bi, h: (bi, h, 0, 0)),
            pl.BlockSpec((1, 1, LANES, s), lambda bi, h: (bi, h, 0, 0)),
        ],
        out_specs=pl.BlockSpec((s, gw), lambda bi, h: (bi, h)),
        out_shape=jax.ShapeDtypeStruct((n, ATTN_W), BF16),
        scratch_shapes=[
            pltpu.VMEM((pairs, LANES, 2 * tq), BF16),
            pltpu.VMEM((pairs, LANES, 2 * tq), BF16),
            pltpu.VMEM((pairs, 1, 2 * tq), F32),
            pltpu.VMEM((pairs, LANES, 2 * tq), F32),
            pltpu.VMEM((pairs, tk, 2 * tq), F32),
            pltpu.VMEM((pairs, tk, 2 * tq), F32),
        ],
        compiler_params=_params(("parallel", "parallel")),
        name="attention",
    )(qa, kd, vt)


def _retention_kernel(dec_ref, q_ref, k_ref, v_ref, g_ref, gng_ref, gnb_ref, o_ref,
                      fbuf, bbuf, sf_ref, sb_ref):
    h = pl.program_id(1)
    c = RET_CHUNK
    s = q_ref.shape[1]
    nc = s // c
    lgf = jnp.full((1, 1), dec_ref[0, h], F32)
    lgb = jnp.full((1, 1), dec_ref[1, h], F32)
    ii = lax.broadcasted_iota(I32, (c, c), 0)
    jj = lax.broadcasted_iota(I32, (c, c), 1)
    diff = (ii - jj).astype(F32)
    d_f = jnp.where(diff >= 0, jnp.exp(lgf * jnp.maximum(diff, 0.0)), 0.0)
    d_b = jnp.where(diff <= 0, jnp.exp(lgb * jnp.maximum(-diff, 0.0)), 0.0)
    idx = lax.broadcasted_iota(I32, (c, 1), 0).astype(F32)
    xi_f = jnp.exp(lgf * (idx + 1.0))
    zeta_f = jnp.exp(lgf * (c - 1.0 - idx))
    xi_b = jnp.exp(lgb * (c - idx))
    zeta_b = jnp.exp(lgb * idx)
    cd_f = jnp.exp(lgf * c)
    cd_b = jnp.exp(lgb * c)
    sf_ref[...] = jnp.zeros(sf_ref.shape, F32)
    sb_ref[...] = jnp.zeros(sb_ref.shape, F32)
    nt = (((1,), (1,)), ((), ()))

    fwd = (d_f, xi_f, zeta_f, cd_f, sf_ref, fbuf)
    bwd = (d_b, xi_b, zeta_b, cd_b, sb_ref, bbuf)

    def body(i, carry):
        jobs = [(fwd, 2 * i), (bwd, nc - 1 - 2 * i), (fwd, 2 * i + 1), (bwd, nc - 2 - 2 * i)]
        offs = [pl.multiple_of(ci * c, c) for _, ci in jobs]
        qs = [q_ref[0, pl.ds(off, c), :] for off in offs]
        ks = [k_ref[0, pl.ds(off, c), :] for off in offs]
        vs = [v_ref[0, pl.ds(off, c), :] for off in offs]
        scores = [lax.dot_general(q, k, nt, preferred_element_type=F32) for q, k in zip(qs, ks)]
        incs = []
        for (d, _), k, v in zip(jobs, ks, vs):
            kzt = (k.astype(F32) * d[2]).T.astype(BF16)
            incs.append(jnp.dot(kzt, v, preferred_element_type=F32))
        states = [fwd[4][...], bwd[4][...]]
        cross = [jnp.dot(qs[n], states[n].astype(BF16), preferred_element_type=F32) for n in range(2)]
        states += [states[n] * jobs[n][0][3] + incs[n] for n in range(2)]
        cross += [jnp.dot(qs[n], states[n].astype(BF16), preferred_element_type=F32) for n in range(2, 4)]
        for n, ((d, _), off) in enumerate(zip(jobs, offs)):
            a = (scores[n] * d[0]).astype(BF16)
            d[5][pl.ds(off, c), :] = jnp.dot(a, vs[n], preferred_element_type=F32) + cross[n] * d[1]
        for n in range(2):
            jobs[n][0][4][...] = states[2 + n] * jobs[n][0][3] + incs[2 + n]
        return carry

    lax.fori_loop(0, nc // 2, body, 0)

    rows = min(512, s)

    def fin(t, carry):
        off = pl.multiple_of(t * rows, rows)
        r = fbuf[pl.ds(off, rows), :] + bbuf[pl.ds(off, rows), :]
        mu = jnp.mean(r, axis=-1, keepdims=True)
        d = r - mu
        var = jnp.mean(d * d, axis=-1, keepdims=True)
        y = d * lax.rsqrt(var + NORM_EPS) * gng_ref[...] + gnb_ref[...]
        o_ref[0, pl.ds(off, rows), :] = (g_ref[0, pl.ds(off, rows), :].astype(F32) * y).astype(BF16)
        return carry

    lax.fori_loop(0, s // rows, fin, 0)


def _retention(qr, kr, vr, gr, dec, gn_g, gn_b, b, s):
    shp = (b, s, RET_W)
    assert s % (2 * RET_CHUNK) == 0, s
    head = pl.BlockSpec((1, s, RET_DIM), lambda bi, h: (bi, 0, h))
    vec = pl.BlockSpec((1, RET_DIM), lambda bi, h: (0, h))
    out = pl.pallas_call(
        _retention_kernel,
        grid=(b, RET_HEADS),
        in_specs=[pl.BlockSpec(memory_space=pltpu.SMEM), head, head, head, head, vec, vec],
        out_specs=head,
        out_shape=jax.ShapeDtypeStruct(shp, BF16),
        scratch_shapes=[
            pltpu.VMEM((s, RET_DIM), F32),
            pltpu.VMEM((s, RET_DIM), F32),
            pltpu.VMEM((RET_DIM, RET_DIM), F32),
            pltpu.VMEM((RET_DIM, RET_DIM), F32),
        ],
        compiler_params=_params(("parallel", "parallel")),
        name="retention",
    )(dec, qr.reshape(shp), kr.reshape(shp), vr.reshape(shp), gr.reshape(shp), gn_g, gn_b)
    return out.reshape(b * s, RET_W)


def _outproj_kernel(oa_ref, or_ref, w_ref, x_ref, mod_ref, pg_ref, fg_ref, rw_ref, rb_ref,
                    xn_ref, hp_ref, ti_ref, tw_ref, rk_ref, cnt_ref, base_ref):
    @pl.when(pl.program_id(0) == 0)
    def _():
        base_ref[...] = jnp.zeros(base_ref.shape, F32)

    m = mod_ref[0]
    y = jnp.dot(oa_ref[...], w_ref[:ATTN_W, :], preferred_element_type=F32)
    y = y + jnp.dot(or_ref[...], w_ref[ATTN_W:, :], preferred_element_type=F32)
    xn = x_ref[...] + m[2:3, :] * _rms(y, pg_ref[...])
    xn_ref[...] = xn
    h = _rms(xn, fg_ref[...]) * (1.0 + m[4:5, :]) + m[3:4, :]
    hb = h.astype(BF16)
    u = lax.bitcast_convert_type(hb.astype(F32), U32)
    hp_ref[...] = (u[:, :HALF] >> 16) | u[:, HALF:]

    tm = hb.shape[0]
    nt = (((1,), (1,)), ((), ()))
    logits = lax.dot_general(rw_ref[...], hb, nt, preferred_element_type=F32) + rb_ref[...]
    e_iota = lax.broadcasted_iota(I32, (N_EXPERTS, tm), 0).astype(F32)
    vals, idxs = [], []
    cur = logits
    for _ in range(TOP_K):
        mx = jnp.max(cur, axis=0, keepdims=True)
        ik = jnp.min(jnp.where(cur == mx, e_iota, float(N_EXPERTS)), axis=0, keepdims=True)
        vals.append(mx)
        idxs.append(ik)
        cur = jnp.where(e_iota == ik, -jnp.inf, cur)
    v = jnp.concatenate(vals, axis=0)
    w = jnp.exp(v - v[0:1, :])
    tw_ref[...] = w / jnp.sum(w, axis=0, keepdims=True)
    ti_ref[...] = jnp.concatenate(idxs, axis=0).astype(I32)

    onehot = jnp.zeros((N_EXPERTS, tm), F32)
    for ik in idxs:
        onehot = onehot + (e_iota == ik).astype(F32)
    before = (lax.broadcasted_iota(I32, (tm, tm), 0) < lax.broadcasted_iota(I32, (tm, tm), 1)).astype(BF16)
    seen = jnp.dot(onehot.astype(BF16), before, preferred_element_type=F32) + base_ref[...]
    ranks = [jnp.sum(jnp.where(e_iota == ik, seen, 0.0), axis=0, keepdims=True) for ik in idxs]
    rk_ref[...] = jnp.concatenate(ranks, axis=0).astype(I32)
    total = base_ref[...] + jnp.sum(onehot, axis=1, keepdims=True)
    base_ref[...] = total
    cnt_ref[...] = jnp.broadcast_to(total, cnt_ref.shape).astype(I32)


def _outproj(oa, orr, w_out_b, x2, batch0, x_batch0, mod_l, post_g, ffn_g, rw_t, rb, b, s):
    n = b * s
    tm = min(ROW_TILE, s)
    nb = s // tm
    row = lambda i: (i, 0)
    full = lambda i: (0, 0)
    col = lambda i: (0, i)
    return pl.pallas_call(
        _outproj_kernel,
        grid=(n // tm,),
        in_specs=[
            pl.BlockSpec((tm, ATTN_W), row),
            pl.BlockSpec((tm, RET_W), row),
            pl.BlockSpec((D_MODEL, D_MODEL), full),
            pl.BlockSpec((tm, D_MODEL), lambda i: (x_batch0 * nb + i, 0)),
            pl.BlockSpec((1, 6, D_MODEL), lambda i: (batch0 + i // nb, 0, 0)),
            pl.BlockSpec((1, D_MODEL), full),
            pl.BlockSpec((1, D_MODEL), full),
            pl.BlockSpec((N_EXPERTS, D_MODEL), full),
            pl.BlockSpec((N_EXPERTS, 1), full),
        ],
        out_specs=[
            pl.BlockSpec((tm, D_MODEL), row),
            pl.BlockSpec((tm, HALF), row),
            pl.BlockSpec((TOP_K, tm), col),
            pl.BlockSpec((TOP_K, tm), col),
            pl.BlockSpec((TOP_K, tm), col),
            pl.BlockSpec((N_EXPERTS, LANES), full),
        ],
        out_shape=[
            jax.ShapeDtypeStruct((n, D_MODEL), F32),
            jax.ShapeDtypeStruct((n, HALF), U32),
            jax.ShapeDtypeStruct((TOP_K, n), I32),
            jax.ShapeDtypeStruct((TOP_K, n), F32),
            jax.ShapeDtypeStruct((TOP_K, n), I32),
            jax.ShapeDtypeStruct((N_EXPERTS, LANES), I32),
        ],
        scratch_shapes=[pltpu.VMEM((N_EXPERTS, 1), F32)],
        compiler_params=_params(("arbitrary",)),
        name="outproj_router",
    )(oa, orr, w_out_b, x2, mod_l, post_g, ffn_g, rw_t, rb)


def _sc_scatter_rows(rows, pos3, n_out):
    n, w = rows.shape
    nchunk, kk, c = pos3.shape
    per_w = nchunk // SC_WORKERS
    mesh = plsc.VectorSubcoreMesh(core_axis_name="c", subcore_axis_name="s")

    @functools.partial(
        pl.kernel, mesh=mesh,
        out_type=jax.ShapeDtypeStruct((n_out, w), rows.dtype),
        scratch_types=[pltpu.VMEM((kk, c), I32), pltpu.VMEM((c, w), rows.dtype)],
    )
    def k(rows_hbm, pos_hbm, out_hbm, idx_v, rows_v):
        wid = lax.axis_index("s") * SC_CORES + lax.axis_index("c")

        @pl.loop(0, per_w)
        def _(i):
            ch = wid * per_w + i
            pltpu.sync_copy(pos_hbm.at[ch], idx_v)
            pltpu.sync_copy(rows_hbm.at[pl.ds(ch * c, c)], rows_v)
            for j in range(kk):
                pltpu.sync_copy(rows_v, out_hbm.at[idx_v.at[j]])

    return k(rows, pos3)


def _sc_gather_rows(table, idx):
    w = table.shape[1]
    b = idx.shape[0]
    c = SC_CHUNK
    per_w = b // (SC_WORKERS * c)
    mesh = plsc.VectorSubcoreMesh(core_axis_name="c", subcore_axis_name="s")

    @functools.partial(
        pl.kernel, mesh=mesh,
        out_type=jax.ShapeDtypeStruct((b, w), table.dtype),
        scratch_types=[pltpu.VMEM((c,), I32), pltpu.VMEM((c, w), table.dtype), pltpu.SemaphoreType.DMA],
    )
    def k(table_hbm, idx_hbm, out_hbm, idx_v, rows_v, sem):
        wid = lax.axis_index("s") * SC_CORES + lax.axis_index("c")

        @pl.loop(0, per_w)
        def _(i):
            base = (wid * per_w + i) * c
            pltpu.sync_copy(idx_hbm.at[pl.ds(base, c)], idx_v)
            pltpu.async_copy(table_hbm.at[idx_v], rows_v, sem).wait()
            pltpu.sync_copy(rows_v, out_hbm.at[pl.ds(base, c)])

    return k(table, idx)


def _experts_kernel(te_ref, nv_ref, xs_ref, wu_ref, bu_ref, wd_ref, bd_ref, ys_ref, wu_b, wd_b):
    i = pl.program_id(0)
    live = i < nv_ref[0]
    new_expert = jnp.logical_or(i == 0, te_ref[i] != te_ref[jnp.maximum(i - 1, 0)])

    @pl.when(jnp.logical_and(live, new_expert))
    def _():
        wu_b[...] = wu_ref[0, 0].astype(BF16)
        wd_b[...] = wd_ref[0, 0].astype(BF16)

    @pl.when(live)
    def _():
        w = xs_ref[...]
        lo = _unpack_lo(w).astype(BF16)
        hi = _unpack_hi(w).astype(BF16)
        up = jnp.dot(lo, wu_b[:HALF, :], preferred_element_type=F32)
        up = up + jnp.dot(hi, wu_b[HALF:, :], preferred_element_type=F32) + bu_ref[0, 0]
        glu = jnp.minimum(up[:, :D_FF], SWIGLU_LIMIT)
        lin = jnp.clip(up[:, D_FF:], -SWIGLU_LIMIT, SWIGLU_LIMIT)
        act = glu * jax.nn.sigmoid(SWIGLU_ALPHA * glu) * (lin + 1.0)
        y = jnp.dot(act.astype(BF16), wd_b[...], preferred_element_type=F32) + bd_ref[0, 0]
        ys_ref[...] = _pack_bf16_pairs(y)


def _experts(xs, tile_expert, n_valid, layer, wu, bu, wd, bd):
    n_pad = xs.shape[0]
    t = MOE_TILE
    grid_spec = pltpu.PrefetchScalarGridSpec(
        num_scalar_prefetch=2,
        grid=(n_pad // t,),
        in_specs=[
            pl.BlockSpec((t, HALF), lambda i, te, nv: (i, 0)),
            pl.BlockSpec((1, 1, D_MODEL, 2 * D_FF), lambda i, te, nv: (layer, te[i], 0, 0)),
            pl.BlockSpec((1, 1, 1, 2 * D_FF), lambda i, te, nv: (layer, te[i], 0, 0)),
            pl.BlockSpec((1, 1, D_FF, D_MODEL), lambda i, te, nv: (layer, te[i], 0, 0)),
            pl.BlockSpec((1, 1, 1, D_MODEL), lambda i, te, nv: (layer, te[i], 0, 0)),
        ],
        out_specs=pl.BlockSpec((t, HALF), lambda i, te, nv: (i, 0)),
        scratch_shapes=[pltpu.VMEM((D_MODEL, 2 * D_FF), BF16), pltpu.VMEM((D_FF, D_MODEL), BF16)],
    )
    return pl.pallas_call(
        _experts_kernel,
        grid_spec=grid_spec,
        out_shape=jax.ShapeDtypeStruct((n_pad, HALF), U32),
        compiler_params=pltpu.CompilerParams(dimension_semantics=("arbitrary",), vmem_limit_bytes=EXPERTS_VMEM_LIMIT),
        name="experts",
    )(tile_expert, n_valid, xs, wu, bu, wd, bd)


def _combine_kernel(y0_ref, y1_ref, y2_ref, y3_ref, w_ref, x_ref, mod_ref, g_ref, *rest):
    o_ref = rest[-1]
    w = w_ref[...]
    lo = None
    hi = None
    for k, y_ref in enumerate((y0_ref, y1_ref, y2_ref, y3_ref)):
        word = y_ref[...]
        wk = w[:, k:k + 1]
        lo_k = wk * _unpack_lo(word)
        hi_k = wk * _unpack_hi(word)
        lo = lo_k if lo is None else lo + lo_k
        hi = hi_k if hi is None else hi + hi_k
    yf = jnp.concatenate([lo, hi], axis=1)
    m = mod_ref[0]
    o_ref[...] = x_ref[...] + m[5:6, :] * _rms(yf, g_ref[...])


def _combine(yg, tw_t, xn, batch0, mod_l, post_g, b, s, out_batches=None, shared_out=None):
    n = b * s
    tm = min(ROW_TILE, s)
    nb = s // tm
    nblk = n // tm
    row = lambda i: (i, 0)
    full = lambda i: (0, 0)
    y_specs = [pl.BlockSpec((tm, HALF), functools.partial(lambda i, k: (k * nblk + i, 0), k=k)) for k in range(TOP_K)]
    out_rows, out_blk0 = (n, 0) if out_batches is None else (out_batches * s, batch0 * nb)
    extra_specs, extra_args, aliases = [], [], {}
    if shared_out is not None:
        extra_specs, extra_args, aliases = [pl.BlockSpec(memory_space=pl.ANY)], [shared_out], {8: 0}
    return pl.pallas_call(
        _combine_kernel,
        grid=(nblk,),
        in_specs=y_specs + [
            pl.BlockSpec((tm, TOP_K), row),
            pl.BlockSpec((tm, D_MODEL), row),
            pl.BlockSpec((1, 6, D_MODEL), lambda i: (batch0 + i // nb, 0, 0)),
            pl.BlockSpec((1, D_MODEL), full),
        ] + extra_specs,
        out_specs=pl.BlockSpec((tm, D_MODEL), lambda i: (out_blk0 + i, 0)),
        out_shape=jax.ShapeDtypeStruct((out_rows, D_MODEL), F32),
        input_output_aliases=aliases,
        compiler_params=_params(("parallel",)),
        name="combine",
    )(yg, yg, yg, yg, tw_t, xn, mod_l, post_g, *extra_args)


def _route(counts, tidx, rank, n_tiles):
    t = MOE_TILE
    cnt = counts[:, 0]
    padded = ((cnt + t - 1) // t) * t
    ends = jnp.cumsum(padded)
    starts = ends - padded
    experts = jnp.arange(N_EXPERTS, dtype=I32)[:, None, None]
    start_of = jnp.sum(jnp.where(tidx[None] == experts, starts[:, None, None], 0), axis=0)
    pos = start_of + rank
    tile_ends = ends // t
    n_valid = tile_ends[-1]
    tile = jnp.minimum(jnp.arange(n_tiles, dtype=I32), n_valid - 1)
    tile_expert = jnp.minimum(jnp.sum(tile_ends[None, :] <= tile[:, None], axis=1), N_EXPERTS - 1).astype(I32)
    return pos.astype(I32), tile_expert, n_valid.reshape(1).astype(I32)


def kernel(x, c, ada_w, ada_b, pre_mix_g, post_mix_g, w_in, q_norm_g, k_norm_g, ret_decay_fwd, ret_decay_bwd, ret_gn_g, ret_gn_b, w_out, pre_ffn_g, post_ffn_g, router_w, router_b, exp_w_up, exp_b_up, exp_w_down, exp_b_down):
    b, s, d = x.shape
    depth = ada_w.shape[0]
    groups = 2 if b % 2 == 0 else 1
    bg = b // groups
    n = bg * s
    n_asg = n * TOP_K
    n_pad = n_asg + N_EXPERTS * MOE_TILE
    n_tiles = n_pad // MOE_TILE

    mod = _modulation(c, ada_w, ada_b)
    ropes = _rope_tables(s, HEAD_DIM) + _rope_tables(s, RET_DIM)
    bu = exp_b_up.reshape(depth, N_EXPERTS, 1, 2 * D_FF)
    bd = exp_b_down.reshape(depth, N_EXPERTS, 1, d)
    xs_in = [x.reshape(b * s, d)] * groups
    x_b0 = [g * bg for g in range(groups)]
    out = None
    for l in range(depth):
        mod_l = mod[l].reshape(b, 6, d)
        qg = jnp.tile(q_norm_g[l], LANES // HEAD_DIM).reshape(1, LANES)
        kg = jnp.tile(k_norm_g[l], LANES // HEAD_DIM).reshape(1, LANES)
        w_in_b = w_in[l].astype(BF16)
        w_out_b = w_out[l].astype(BF16)
        rw_t = router_w[l].T.astype(BF16)
        dec = jnp.stack([ret_decay_fwd[l], ret_decay_bwd[l]]).astype(F32)
        last = l + 1 == depth
        proj = [_inproj(xs_in[g], g * bg, x_b0[g], mod_l, pre_mix_g[l].reshape(1, d), w_in_b, ropes, qg, kg, bg, s)
                for g in range(groups)]
        oa = [_attention(p[0], p[1], p[2], bg, s) for p in proj]
        orr = [_retention(p[3], p[4], p[5], p[6], dec, ret_gn_g[l].reshape(1, RET_W), ret_gn_b[l].reshape(1, RET_W),
                          bg, s) for p in proj]
        routed = []
        for g in range(groups):
            xn, hp, tidx, tw, rank, counts = _outproj(
                oa[g], orr[g], w_out_b, xs_in[g], g * bg, x_b0[g], mod_l, post_mix_g[l].reshape(1, d),
                pre_ffn_g[l].reshape(1, d), rw_t, router_b[l].reshape(N_EXPERTS, 1), bg, s)
            pos, tile_expert, n_valid = _route(counts, tidx, rank, n_tiles)
            pos3 = pos.reshape(TOP_K, n // SC_CHUNK, SC_CHUNK).transpose(1, 0, 2)
            routed.append((xn, tw, pos, tile_expert, n_valid, _sc_scatter_rows(hp, pos3, n_pad)))
        yg = []
        for xn, tw, pos, tile_expert, n_valid, xs in routed:
            ys = _experts(xs, tile_expert, n_valid, l, exp_w_up, bu, exp_w_down, bd)
            yg.append(_sc_gather_rows(ys, pos.reshape(n_asg)))
        nxt = []
        for g, (xn, tw, *_) in enumerate(routed):
            if last:
                out = _combine(yg[g], tw.T, xn, g * bg, mod_l, post_ffn_g[l].reshape(1, d), bg, s,
                               out_batches=b, shared_out=out)
            else:
                nxt.append(_combine(yg[g], tw.T, xn, g * bg, mod_l, post_ffn_g[l].reshape(1, d), bg, s))
        xs_in, x_b0 = nxt, [0] * groups
    return out.reshape(b, s, d)
```

```python
import functools

import numpy as np
import jax
import jax.numpy as jnp
from jax import lax
from jax.experimental import pallas as pl
from jax.experimental.pallas import tpu as pltpu
from jax.experimental.pallas import tpu_sc as plsc

F32 = jnp.float32
BF16 = jnp.bfloat16
U32 = jnp.uint32
I32 = jnp.int32
FP8 = jnp.float8_e4m3fn

D_MODEL = 1024
GRID_W = 64
ROPE_THETA = 10000.0
HEAD_DIM = 64
Q_HEADS = 8
KV_HEADS = 2
RET_HEADS = 4
RET_DIM = 128
RET_CHUNK = 128
ATTN_W = Q_HEADS * HEAD_DIM
KV_W = KV_HEADS * HEAD_DIM
RET_W = RET_HEADS * RET_DIM
IN_W = ATTN_W + 2 * KV_W + 4 * RET_W
N_EXPERTS = 32
TOP_K = 4
D_FF = D_MODEL
SWIGLU_LIMIT = 7.0
SWIGLU_ALPHA = 1.702
NORM_EPS = 1e-6
Q_SCALE = HEAD_DIM ** -0.5 * float(np.log2(np.e))
P_SHIFT = 8.0
HALF = D_MODEL // 2
LANES = 128

ROW_TILE = 512
Q_TILE = 128
KV_TILE = 1024
MOE_TILE = 512
VMEM_LIMIT = 48 * 1024 * 1024
EXPERTS_VMEM_LIMIT = 56 * 1024 * 1024

SC_CORES = 2
SC_SUBCORES = 16
SC_WORKERS = SC_CORES * SC_SUBCORES
SC_CHUNK = 64


def _params(sem):
    return pltpu.CompilerParams(dimension_semantics=sem, vmem_limit_bytes=VMEM_LIMIT)


def _rms(x, g):
    return x * lax.rsqrt(jnp.mean(x * x, axis=-1, keepdims=True) + NORM_EPS) * g


def _pack_bf16_pairs(y):
    u = lax.bitcast_convert_type(y.astype(BF16).astype(F32), U32)
    return (u[:, :HALF] >> 16) | u[:, HALF:]


def _unpack_lo(w):
    return lax.bitcast_convert_type(w << 16, F32)


def _unpack_hi(w):
    return lax.bitcast_convert_type(w & jnp.uint32(0xFFFF0000), F32)


def _mod_kernel(c_ref, w_ref, b_ref, o_ref):
    c = c_ref[...]
    cond = c * jax.nn.sigmoid(c)
    o_ref[0] = jnp.dot(cond, w_ref[0], preferred_element_type=F32, precision=lax.Precision.HIGHEST) + b_ref[0]


def _modulation(c, ada_w, ada_b):
    depth, d, w6 = ada_w.shape
    b = c.shape[0]
    tn = 1536
    return pl.pallas_call(
        _mod_kernel,
        grid=(depth, w6 // tn),
        in_specs=[
            pl.BlockSpec((b, d), lambda l, j: (0, 0)),
            pl.BlockSpec((1, d, tn), lambda l, j: (l, 0, j)),
            pl.BlockSpec((1, 1, tn), lambda l, j: (l, 0, j)),
        ],
        out_specs=pl.BlockSpec((1, b, tn), lambda l, j: (l, 0, j)),
        out_shape=jax.ShapeDtypeStruct((depth, b, w6), F32),
        compiler_params=_params(("arbitrary", "arbitrary")),
        name="modulation",
    )(c, ada_w, ada_b.reshape(depth, 1, w6))


def _rope_tables(s, head_dim):
    quarter = head_dim // 4
    t = jnp.arange(s, dtype=F32)
    row = jnp.floor(t / GRID_W)
    col = t - row * GRID_W
    inv_freq = ROPE_THETA ** (-jnp.arange(quarter, dtype=F32) / quarter)
    lane = np.arange(LANES) % head_dim
    use_col = lane >= head_dim // 2
    second = (lane % (head_dim // 2)) >= quarter
    freq = inv_freq[lane % quarter]
    pos = jnp.where(use_col[None, :], col[:, None], row[:, None])
    ang = pos * freq[None, :]
    sign = jnp.where(second, 1.0, -1.0).astype(F32)
    return jnp.cos(ang), jnp.sin(ang) * sign[None, :]


def _rope(z, cos, sin_signed, quarter):
    lane = lax.broadcasted_iota(I32, z.shape, 1)
    first = (lane % (2 * quarter)) < quarter
    partner = jnp.where(first, pltpu.roll(z, LANES - quarter, 1), pltpu.roll(z, quarter, 1))
    return z * cos + partner * sin_signed


def _inproj_kernel(x_ref, mod_ref, g_ref, w_ref, ca_ref, sa_ref, cr_ref, sr_ref, qg_ref, kg_ref,
                   qa_ref, kd_ref, vt_ref, qr_ref, kr_ref, vr_ref, gr_ref):
    x = x_ref[...]
    m = mod_ref[0]
    h = _rms(x, g_ref[...]) * (1.0 + m[1:2, :]) + m[0:1, :]
    hb = h.astype(BF16)
    tm = x.shape[0]

    def proj(c0):
        z = jnp.dot(hb, w_ref[:, c0:c0 + 2 * LANES], preferred_element_type=F32)
        return z[:, :LANES], z[:, LANES:]

    ri = lax.broadcasted_iota(I32, (LANES, LANES), 0) // HEAD_DIM
    ci = lax.broadcasted_iota(I32, (LANES, LANES), 1) // HEAD_DIM
    head_ones = (ri == ci).astype(BF16)
    ca, sa = ca_ref[...], sa_ref[...]
    cr, sr = cr_ref[...], sr_ref[...]
    lane = lax.broadcasted_iota(I32, (tm, LANES), 1)
    low = lane < HEAD_DIM

    def head_norm_rope(z, g):
        ss = jnp.dot((z * z).astype(BF16), head_ones, preferred_element_type=F32)
        zn = z * lax.rsqrt(ss * (1.0 / HEAD_DIM) + NORM_EPS) * g
        return _rope(zn, ca, sa, HEAD_DIM // 4)

    def store_pair(ref, j, fn, c0):
        for i, z in enumerate(proj(c0 + 2 * j * LANES)):
            ref[:, (2 * j + i) * LANES:(2 * j + i + 1) * LANES] = fn(z).astype(BF16)

    for j in range(ATTN_W // (2 * LANES)):
        store_pair(qa_ref, j, lambda z: head_norm_rope(z, qg_ref[...]) * Q_SCALE, 0)

    k, v = proj(ATTN_W)
    k = head_norm_rope(k, kg_ref[...])
    k_sw = pltpu.roll(k, HEAD_DIM, 1)
    kd_ref[0, 0] = jnp.where(low, k, k_sw).astype(FP8)
    kd_ref[0, 1] = jnp.where(low, k_sw, k).astype(FP8)
    vt_ref[0, 0] = jnp.where(low, v, 1.0).T.astype(FP8)
    vt_ref[0, 1] = jnp.where(low, pltpu.roll(v, HEAD_DIM, 1), 1.0).T.astype(FP8)

    base = ATTN_W + 2 * KV_W
    for j in range(RET_W // (2 * LANES)):
        store_pair(qr_ref, j, lambda z: _rope(z, cr, sr, RET_DIM // 4), base)
        store_pair(kr_ref, j, lambda z: _rope(z, cr, sr, RET_DIM // 4) * (RET_DIM ** -0.5), base + RET_W)
        store_pair(vr_ref, j, lambda z: z, base + 2 * RET_W)
        store_pair(gr_ref, j, lambda z: z * jax.nn.sigmoid(z), base + 3 * RET_W)


def _inproj(x2, batch0, x_batch0, mod_l, pre_g, w_in_b, ropes, qg, kg, b, s):
    n = b * s
    tm = min(ROW_TILE, s)
    nb = s // tm
    ca, sa, cr, sr = ropes
    row = lambda i: (i, 0)
    full = lambda i: (0, 0)
    rope_spec = pl.BlockSpec((tm, LANES), lambda i: (i % nb, 0))
    wide = jax.ShapeDtypeStruct((n, RET_W), BF16)
    return pl.pallas_call(
        _inproj_kernel,
        grid=(n // tm,),
        in_specs=[
            pl.BlockSpec((tm, D_MODEL), lambda i: (x_batch0 * nb + i, 0)),
            pl.BlockSpec((1, 6, D_MODEL), lambda i: (batch0 + i // nb, 0, 0)),
            pl.BlockSpec((1, D_MODEL), full),
            pl.BlockSpec((D_MODEL, IN_W), full),
            rope_spec, rope_spec, rope_spec, rope_spec,
            pl.BlockSpec((1, LANES), full),
            pl.BlockSpec((1, LANES), full),
        ],
        out_specs=[
            pl.BlockSpec((tm, ATTN_W), row),
            pl.BlockSpec((1, KV_HEADS, tm, LANES), lambda i: (i // nb, 0, i % nb, 0)),
            pl.BlockSpec((1, KV_HEADS, LANES, tm), lambda i: (i // nb, 0, 0, i % nb)),
            pl.BlockSpec((tm, RET_W), row),
            pl.BlockSpec((tm, RET_W), row),
            pl.BlockSpec((tm, RET_W), row),
            pl.BlockSpec((tm, RET_W), row),
        ],
        out_shape=[
            jax.ShapeDtypeStruct((n, ATTN_W), BF16),
            jax.ShapeDtypeStruct((b, KV_HEADS, s, LANES), FP8),
            jax.ShapeDtypeStruct((b, KV_HEADS, LANES, s), FP8),
            wide, wide, wide, wide,
        ],
        compiler_params=_params(("parallel",)),
        name="inproj",
    )(x2, mod_l, pre_g, w_in_b, ca, sa, cr, sr, qg, kg)


def _attn_kernel(q_ref, kd_ref, vt_ref, o_ref, q_s, qn_s, m_ref, acc_ref, s0_ref, s1_ref, *, tq, tk):
    s = kd_ref.shape[2]
    nk = s // tk
    nq = s // tq
    pairs = (Q_HEADS // KV_HEADS) // 2
    top = lax.broadcasted_iota(I32, (LANES, tq), 0) < HEAD_DIM

    def load_q(qi, dst):
        rows = pl.ds(pl.multiple_of(qi * tq, tq), tq)
        for p in range(pairs):
            qt = q_ref[rows, p * LANES:(p + 1) * LANES].astype(F32).T
            dst[p, :, :tq] = jnp.where(top, qt, 0.0).astype(FP8)
            dst[p, :, tq:] = jnp.where(top, 0.0, qt).astype(FP8)

    def keys(j):
        return kd_ref[0, 0, j * tk:(j + 1) * tk, :]

    def step(j, cur_ref, nxt_ref, q_next, j_next):
        vt = vt_ref[0, 0, :, j * tk:(j + 1) * tk]
        kd = keys(j_next)
        nxt_ref[0] = jnp.dot(kd, q_next[0], preferred_element_type=F32)
        for p in range(pairs):
            sc = cur_ref[p]
            m_old = m_ref[p]
            m_new = jnp.maximum(m_old, jnp.max(sc, axis=0, keepdims=True))
            pt = jnp.exp2(sc - (m_new - P_SHIFT))
            alpha = jnp.exp2(m_old - m_new)
            if p + 1 < pairs:
                nxt_ref[p + 1] = jnp.dot(kd, q_next[p + 1], preferred_element_type=F32)
            acc_ref[p] = alpha * acc_ref[p] + jnp.dot(vt, pt.astype(FP8), preferred_element_type=F32)
            m_ref[p] = m_new

    bufs = (s0_ref, s1_ref)
    load_q(0, q_s)
    for p in range(pairs):
        s0_ref[p] = jnp.dot(keys(0), q_s[p], preferred_element_type=F32)

    def body(qi, carry):
        load_q(qi, q_s)
        load_q(jnp.minimum(qi + 1, nq - 1), qn_s)
        m_ref[...] = jnp.full(m_ref.shape, -jnp.inf, F32)
        acc_ref[...] = jnp.zeros(acc_ref.shape, F32)
        for j in range(nk):
            last = j + 1 == nk
            step(j, bufs[j % len(bufs)], bufs[(j + 1) % len(bufs)], qn_s if last else q_s, 0 if last else j + 1)
        rows = pl.ds(pl.multiple_of(qi * tq, tq), tq)
        for p in range(pairs):
            acc = acc_ref[p]
            o_t = acc[:HEAD_DIM] / acc[HEAD_DIM:]
            both = jnp.concatenate([o_t[:, :tq], o_t[:, tq:]], axis=0)
            o_ref[rows, p * LANES:(p + 1) * LANES] = both.T.astype(BF16)
        return carry

    lax.fori_loop(0, nq, body, 0)


def _attention(qa, kd, vt, b, s):
    n = qa.shape[0]
    tq = Q_TILE
    tk = min(KV_TILE, s // 2)
    assert tq == LANES and s % tq == 0 and s % (2 * tk) == 0 and tk % LANES == 0, (s, tq, tk)
    gw = ATTN_W // KV_HEADS
    pairs = (Q_HEADS // KV_HEADS) // 2
    return pl.pallas_call(
        functools.partial(_attn_kernel, tq=tq, tk=tk),
        grid=(b, KV_HEADS),
        in_specs=[
            pl.BlockSpec((s, gw), lambda bi, h: (bi, h)),
            pl.BlockSpec((1, 1, s, LANES), lambda bi, h: (bi, h, 0, 0)),
            pl.BlockSpec((1, 1, LANES, s), lambda bi, h: (bi, h, 0, 0)),
        ],
        out_specs=pl.BlockSpec((s, gw), lambda bi, h: (bi, h)),
        out_shape=jax.ShapeDtypeStruct((n, ATTN_W), BF16),
        scratch_shapes=[
            pltpu.VMEM((pairs, LANES, 2 * tq), FP8),
            pltpu.VMEM((pairs, LANES, 2 * tq), FP8),
            pltpu.VMEM((pairs, 1, 2 * tq), F32),
            pltpu.VMEM((pairs, LANES, 2 * tq), F32),
            pltpu.VMEM((pairs, tk, 2 * tq), F32),
            pltpu.VMEM((pairs, tk, 2 * tq), F32),
        ],
        compiler_params=_params(("parallel", "parallel")),
        name="attention",
    )(qa, kd, vt)


def _retention_kernel(dec_ref, q_ref, k_ref, v_ref, g_ref, gng_ref, gnb_ref, o_ref,
                      fbuf, bbuf, sf_ref, sb_ref):
    h = pl.program_id(1)
    c = RET_CHUNK
    s = q_ref.shape[1]
    nc = s // c
    lgf = jnp.full((1, 1), dec_ref[0, h], F32)
    lgb = jnp.full((1, 1), dec_ref[1, h], F32)
    ii = lax.broadcasted_iota(I32, (c, c), 0)
    jj = lax.broadcasted_iota(I32, (c, c), 1)
    diff = (ii - jj).astype(F32)
    d_f = jnp.where(diff >= 0, jnp.exp(lgf * jnp.maximum(diff, 0.0)), 0.0)
    d_b = jnp.where(diff <= 0, jnp.exp(lgb * jnp.maximum(-diff, 0.0)), 0.0)
    idx = lax.broadcasted_iota(I32, (c, 1), 0).astype(F32)
    xi_f = jnp.exp(lgf * (idx + 1.0))
    zeta_f = jnp.exp(lgf * (c - 1.0 - idx))
    xi_b = jnp.exp(lgb * (c - idx))
    zeta_b = jnp.exp(lgb * idx)
    cd_f = jnp.exp(lgf * c)
    cd_b = jnp.exp(lgb * c)
    sf_ref[...] = jnp.zeros(sf_ref.shape, F32)
    sb_ref[...] = jnp.zeros(sb_ref.shape, F32)
    nt = (((1,), (1,)), ((), ()))

    fwd = (d_f, xi_f, zeta_f, cd_f, sf_ref, fbuf)
    bwd = (d_b, xi_b, zeta_b, cd_b, sb_ref, bbuf)

    def body(i, carry):
        jobs = [(fwd, 2 * i), (bwd, nc - 1 - 2 * i), (fwd, 2 * i + 1), (bwd, nc - 2 - 2 * i)]
        offs = [pl.multiple_of(ci * c, c) for _, ci in jobs]
        qs = [q_ref[0, pl.ds(off, c), :] for off in offs]
        ks = [k_ref[0, pl.ds(off, c), :] for off in offs]
        vs = [v_ref[0, pl.ds(off, c), :] for off in offs]
        scores = [lax.dot_general(q, k, nt, preferred_element_type=F32) for q, k in zip(qs, ks)]
        incs = []
        for (d, _), k, v in zip(jobs, ks, vs):
            kzt = (k.astype(F32) * d[2]).T.astype(BF16)
            incs.append(jnp.dot(kzt, v, preferred_element_type=F32))
        states = [fwd[4][...], bwd[4][...]]
        cross = [jnp.dot(qs[n], states[n].astype(BF16), preferred_element_type=F32) for n in range(2)]
        states += [states[n] * jobs[n][0][3] + incs[n] for n in range(2)]
        cross += [jnp.dot(qs[n], states[n].astype(BF16), preferred_element_type=F32) for n in range(2, 4)]
        for n, ((d, _), off) in enumerate(zip(jobs, offs)):
            a = (scores[n] * d[0]).astype(BF16)
            d[5][pl.ds(off, c), :] = jnp.dot(a, vs[n], preferred_element_type=F32) + cross[n] * d[1]
        for n in range(2):
            jobs[n][0][4][...] = states[2 + n] * jobs[n][0][3] + incs[2 + n]
        return carry

    lax.fori_loop(0, nc // 2, body, 0)

    rows = min(512, s)

    def fin(t, carry):
        off = pl.multiple_of(t * rows, rows)
        r = fbuf[pl.ds(off, rows), :] + bbuf[pl.ds(off, rows), :]
        mu = jnp.mean(r, axis=-1, keepdims=True)
        d = r - mu
        var = jnp.mean(d * d, axis=-1, keepdims=True)
        y = d * lax.rsqrt(var + NORM_EPS) * gng_ref[...] + gnb_ref[...]
        o_ref[0, pl.ds(off, rows), :] = (g_ref[0, pl.ds(off, rows), :].astype(F32) * y).astype(BF16)
        return carry

    lax.fori_loop(0, s // rows, fin, 0)


def _retention(qr, kr, vr, gr, dec, gn_g, gn_b, b, s):
    shp = (b, s, RET_W)
    assert s % (2 * RET_CHUNK) == 0, s
    head = pl.BlockSpec((1, s, RET_DIM), lambda bi, h: (bi, 0, h))
    vec = pl.BlockSpec((1, RET_DIM), lambda bi, h: (0, h))
    out = pl.pallas_call(
        _retention_kernel,
        grid=(b, RET_HEADS),
        in_specs=[pl.BlockSpec(memory_space=pltpu.SMEM), head, head, head, head, vec, vec],
        out_specs=head,
        out_shape=jax.ShapeDtypeStruct(shp, BF16),
        scratch_shapes=[
            pltpu.VMEM((s, RET_DIM), F32),
            pltpu.VMEM((s, RET_DIM), F32),
            pltpu.VMEM((RET_DIM, RET_DIM), F32),
            pltpu.VMEM((RET_DIM, RET_DIM), F32),
        ],
        compiler_params=_params(("parallel", "parallel")),
        name="retention",
    )(dec, qr.reshape(shp), kr.reshape(shp), vr.reshape(shp), gr.reshape(shp), gn_g, gn_b)
    return out.reshape(b * s, RET_W)


def _outproj_kernel(oa_ref, or_ref, w_ref, x_ref, mod_ref, pg_ref, fg_ref, rw_ref, rb_ref,
                    xn_ref, hp_ref, ti_ref, tw_ref, rk_ref, cnt_ref, base_ref):
    @pl.when(pl.program_id(0) == 0)
    def _():
        base_ref[...] = jnp.zeros(base_ref.shape, F32)

    m = mod_ref[0]
    y = jnp.dot(oa_ref[...], w_ref[:ATTN_W, :], preferred_element_type=F32)
    y = y + jnp.dot(or_ref[...], w_ref[ATTN_W:, :], preferred_element_type=F32)
    xn = x_ref[...] + m[2:3, :] * _rms(y, pg_ref[...])
    xn_ref[...] = xn
    h = _rms(xn, fg_ref[...]) * (1.0 + m[4:5, :]) + m[3:4, :]
    hb = h.astype(BF16)
    u = lax.bitcast_convert_type(hb.astype(F32), U32)
    hp_ref[...] = (u[:, :HALF] >> 16) | u[:, HALF:]

    tm = hb.shape[0]
    nt = (((1,), (1,)), ((), ()))
    logits = lax.dot_general(rw_ref[...], hb, nt, preferred_element_type=F32) + rb_ref[...]
    e_iota = lax.broadcasted_iota(I32, (N_EXPERTS, tm), 0).astype(F32)
    vals, idxs = [], []
    cur = logits
    for _ in range(TOP_K):
        mx = jnp.max(cur, axis=0, keepdims=True)
        ik = jnp.min(jnp.where(cur == mx, e_iota, float(N_EXPERTS)), axis=0, keepdims=True)
        vals.append(mx)
        idxs.append(ik)
        cur = jnp.where(e_iota == ik, -jnp.inf, cur)
    v = jnp.concatenate(vals, axis=0)
    w = jnp.exp(v - v[0:1, :])
    tw_ref[...] = w / jnp.sum(w, axis=0, keepdims=True)
    ti_ref[...] = jnp.concatenate(idxs, axis=0).astype(I32)

    onehot = jnp.zeros((N_EXPERTS, tm), F32)
    for ik in idxs:
        onehot = onehot + (e_iota == ik).astype(F32)
    before = (lax.broadcasted_iota(I32, (tm, tm), 0) < lax.broadcasted_iota(I32, (tm, tm), 1)).astype(BF16)
    seen = jnp.dot(onehot.astype(BF16), before, preferred_element_type=F32) + base_ref[...]
    ranks = [jnp.sum(jnp.where(e_iota == ik, seen, 0.0), axis=0, keepdims=True) for ik in idxs]
    rk_ref[...] = jnp.concatenate(ranks, axis=0).astype(I32)
    total = base_ref[...] + jnp.sum(onehot, axis=1, keepdims=True)
    base_ref[...] = total
    cnt_ref[...] = jnp.broadcast_to(total, cnt_ref.shape).astype(I32)


def _outproj(oa, orr, w_out_b, x2, batch0, x_batch0, mod_l, post_g, ffn_g, rw_t, rb, b, s):
    n = b * s
    tm = min(ROW_TILE, s)
    nb = s // tm
    row = lambda i: (i, 0)
    full = lambda i: (0, 0)
    col = lambda i: (0, i)
    return pl.pallas_call(
        _outproj_kernel,
        grid=(n // tm,),
        in_specs=[
            pl.BlockSpec((tm, ATTN_W), row),
            pl.BlockSpec((tm, RET_W), row),
            pl.BlockSpec((D_MODEL, D_MODEL), full),
            pl.BlockSpec((tm, D_MODEL), lambda i: (x_batch0 * nb + i, 0)),
            pl.BlockSpec((1, 6, D_MODEL), lambda i: (batch0 + i // nb, 0, 0)),
            pl.BlockSpec((1, D_MODEL), full),
            pl.BlockSpec((1, D_MODEL), full),
            pl.BlockSpec((N_EXPERTS, D_MODEL), full),
            pl.BlockSpec((N_EXPERTS, 1), full),
        ],
        out_specs=[
            pl.BlockSpec((tm, D_MODEL), row),
            pl.BlockSpec((tm, HALF), row),
            pl.BlockSpec((TOP_K, tm), col),
            pl.BlockSpec((TOP_K, tm), col),
            pl.BlockSpec((TOP_K, tm), col),
            pl.BlockSpec((N_EXPERTS, LANES), full),
        ],
        out_shape=[
            jax.ShapeDtypeStruct((n, D_MODEL), F32),
            jax.ShapeDtypeStruct((n, HALF), U32),
            jax.ShapeDtypeStruct((TOP_K, n), I32),
            jax.ShapeDtypeStruct((TOP_K, n), F32),
            jax.ShapeDtypeStruct((TOP_K, n), I32),
            jax.ShapeDtypeStruct((N_EXPERTS, LANES), I32),
        ],
        scratch_shapes=[pltpu.VMEM((N_EXPERTS, 1), F32)],
        compiler_params=_params(("arbitrary",)),
        name="outproj_router",
    )(oa, orr, w_out_b, x2, mod_l, post_g, ffn_g, rw_t, rb)


def _sc_scatter_rows(rows, pos3, n_out):
    n, w = rows.shape
    nchunk, kk, c = pos3.shape
    per_w = nchunk // SC_WORKERS
    mesh = plsc.VectorSubcoreMesh(core_axis_name="c", subcore_axis_name="s")

    @functools.partial(
        pl.kernel, mesh=mesh,
        out_type=jax.ShapeDtypeStruct((n_out, w), rows.dtype),
        scratch_types=[pltpu.VMEM((kk, c), I32), pltpu.VMEM((c, w), rows.dtype)],
    )
    def k(rows_hbm, pos_hbm, out_hbm, idx_v, rows_v):
        wid = lax.axis_index("s") * SC_CORES + lax.axis_index("c")

        @pl.loop(0, per_w)
        def _(i):
            ch = wid * per_w + i
            pltpu.sync_copy(pos_hbm.at[ch], idx_v)
            pltpu.sync_copy(rows_hbm.at[pl.ds(ch * c, c)], rows_v)
            for j in range(kk):
                pltpu.sync_copy(rows_v, out_hbm.at[idx_v.at[j]])

    return k(rows, pos3)


def _sc_gather_rows(table, idx):
    w = table.shape[1]
    b = idx.shape[0]
    c = SC_CHUNK
    per_w = b // (SC_WORKERS * c)
    mesh = plsc.VectorSubcoreMesh(core_axis_name="c", subcore_axis_name="s")

    @functools.partial(
        pl.kernel, mesh=mesh,
        out_type=jax.ShapeDtypeStruct((b, w), table.dtype),
        scratch_types=[pltpu.VMEM((c,), I32), pltpu.VMEM((c, w), table.dtype), pltpu.SemaphoreType.DMA],
    )
    def k(table_hbm, idx_hbm, out_hbm, idx_v, rows_v, sem):
        wid = lax.axis_index("s") * SC_CORES + lax.axis_index("c")

        @pl.loop(0, per_w)
        def _(i):
            base = (wid * per_w + i) * c
            pltpu.sync_copy(idx_hbm.at[pl.ds(base, c)], idx_v)
            pltpu.async_copy(table_hbm.at[idx_v], rows_v, sem).wait()
            pltpu.sync_copy(rows_v, out_hbm.at[pl.ds(base, c)])

    return k(table, idx)


def _experts_kernel(te_ref, nv_ref, xs_ref, wu_ref, bu_ref, wd_ref, bd_ref, ys_ref, wu_b, wd_b):
    i = pl.program_id(0)
    live = i < nv_ref[0]
    new_expert = jnp.logical_or(i == 0, te_ref[i] != te_ref[jnp.maximum(i - 1, 0)])

    @pl.when(jnp.logical_and(live, new_expert))
    def _():
        wu_b[...] = wu_ref[0, 0].astype(BF16)
        wd_b[...] = wd_ref[0, 0].astype(BF16)

    @pl.when(live)
    def _():
        w = xs_ref[...]
        lo = _unpack_lo(w).astype(BF16)
        hi = _unpack_hi(w).astype(BF16)
        up = jnp.dot(lo, wu_b[:HALF, :], preferred_element_type=F32)
        up = up + jnp.dot(hi, wu_b[HALF:, :], preferred_element_type=F32) + bu_ref[0, 0]
        glu = jnp.minimum(up[:, :D_FF], SWIGLU_LIMIT)
        lin = jnp.clip(up[:, D_FF:], -SWIGLU_LIMIT, SWIGLU_LIMIT)
        act = glu * jax.nn.sigmoid(SWIGLU_ALPHA * glu) * (lin + 1.0)
        y = jnp.dot(act.astype(BF16), wd_b[...], preferred_element_type=F32) + bd_ref[0, 0]
        ys_ref[...] = _pack_bf16_pairs(y)


def _experts(xs, tile_expert, n_valid, layer, wu, bu, wd, bd):
    n_pad = xs.shape[0]
    t = MOE_TILE
    grid_spec = pltpu.PrefetchScalarGridSpec(
        num_scalar_prefetch=2,
        grid=(n_pad // t,),
        in_specs=[
            pl.BlockSpec((t, HALF), lambda i, te, nv: (i, 0)),
            pl.BlockSpec((1, 1, D_MODEL, 2 * D_FF), lambda i, te, nv: (layer, te[i], 0, 0)),
            pl.BlockSpec((1, 1, 1, 2 * D_FF), lambda i, te, nv: (layer, te[i], 0, 0)),
            pl.BlockSpec((1, 1, D_FF, D_MODEL), lambda i, te, nv: (layer, te[i], 0, 0)),
            pl.BlockSpec((1, 1, 1, D_MODEL), lambda i, te, nv: (layer, te[i], 0, 0)),
        ],
        out_specs=pl.BlockSpec((t, HALF), lambda i, te, nv: (i, 0)),
        scratch_shapes=[pltpu.VMEM((D_MODEL, 2 * D_FF), BF16), pltpu.VMEM((D_FF, D_MODEL), BF16)],
    )
    return pl.pallas_call(
        _experts_kernel,
        grid_spec=grid_spec,
        out_shape=jax.ShapeDtypeStruct((n_pad, HALF), U32),
        compiler_params=pltpu.CompilerParams(dimension_semantics=("arbitrary",), vmem_limit_bytes=EXPERTS_VMEM_LIMIT),
        name="experts",
    )(tile_expert, n_valid, xs, wu, bu, wd, bd)


def _combine_kernel(y0_ref, y1_ref, y2_ref, y3_ref, w_ref, x_ref, mod_ref, g_ref, *rest):
    o_ref = rest[-1]
    w = w_ref[...]
    lo = None
    hi = None
    for k, y_ref in enumerate((y0_ref, y1_ref, y2_ref, y3_ref)):
        word = y_ref[...]
        wk = w[:, k:k + 1]
        lo_k = wk * _unpack_lo(word)
        hi_k = wk * _unpack_hi(word)
        lo = lo_k if lo is None else lo + lo_k
        hi = hi_k if hi is None else hi + hi_k
    yf = jnp.concatenate([lo, hi], axis=1)
    m = mod_ref[0]
    o_ref[...] = x_ref[...] + m[5:6, :] * _rms(yf, g_ref[...])


def _combine(yg, tw_t, xn, batch0, mod_l, post_g, b, s, out_batches=None, shared_out=None):
    n = b * s
    tm = min(ROW_TILE, s)
    nb = s // tm
    nblk = n // tm
    row = lambda i: (i, 0)
    full = lambda i: (0, 0)
    y_specs = [pl.BlockSpec((tm, HALF), functools.partial(lambda i, k: (k * nblk + i, 0), k=k)) for k in range(TOP_K)]
    out_rows, out_blk0 = (n, 0) if out_batches is None else (out_batches * s, batch0 * nb)
    extra_specs, extra_args, aliases = [], [], {}
    if shared_out is not None:
        extra_specs, extra_args, aliases = [pl.BlockSpec(memory_space=pl.ANY)], [shared_out], {8: 0}
    return pl.pallas_call(
        _combine_kernel,
        grid=(nblk,),
        in_specs=y_specs + [
            pl.BlockSpec((tm, TOP_K), row),
            pl.BlockSpec((tm, D_MODEL), row),
            pl.BlockSpec((1, 6, D_MODEL), lambda i: (batch0 + i // nb, 0, 0)),
            pl.BlockSpec((1, D_MODEL), full),
        ] + extra_specs,
        out_specs=pl.BlockSpec((tm, D_MODEL), lambda i: (out_blk0 + i, 0)),
        out_shape=jax.ShapeDtypeStruct((out_rows, D_MODEL), F32),
        input_output_aliases=aliases,
        compiler_params=_params(("parallel",)),
        name="combine",
    )(yg, yg, yg, yg, tw_t, xn, mod_l, post_g, *extra_args)


def _route(counts, tidx, rank, n_tiles):
    t = MOE_TILE
    cnt = counts[:, 0]
    padded = ((cnt + t - 1) // t) * t
    ends = jnp.cumsum(padded)
    starts = ends - padded
    experts = jnp.arange(N_EXPERTS, dtype=I32)[:, None, None]
    start_of = jnp.sum(jnp.where(tidx[None] == experts, starts[:, None, None], 0), axis=0)
    pos = start_of + rank
    tile_ends = ends // t
    n_valid = tile_ends[-1]
    tile = jnp.minimum(jnp.arange(n_tiles, dtype=I32), n_valid - 1)
    tile_expert = jnp.minimum(jnp.sum(tile_ends[None, :] <= tile[:, None], axis=1), N_EXPERTS - 1).astype(I32)
    return pos.astype(I32), tile_expert, n_valid.reshape(1).astype(I32)


def kernel(x, c, ada_w, ada_b, pre_mix_g, post_mix_g, w_in, q_norm_g, k_norm_g, ret_decay_fwd, ret_decay_bwd, ret_gn_g, ret_gn_b, w_out, pre_ffn_g, post_ffn_g, router_w, router_b, exp_w_up, exp_b_up, exp_w_down, exp_b_down):
    b, s, d = x.shape
    depth = ada_w.shape[0]
    groups = 2 if b % 2 == 0 else 1
    bg = b // groups
    n = bg * s
    n_asg = n * TOP_K
    n_pad = n_asg + N_EXPERTS * MOE_TILE
    n_tiles = n_pad // MOE_TILE

    mod = _modulation(c, ada_w, ada_b)
    ropes = _rope_tables(s, HEAD_DIM) + _rope_tables(s, RET_DIM)
    bu = exp_b_up.reshape(depth, N_EXPERTS, 1, 2 * D_FF)
    bd = exp_b_down.reshape(depth, N_EXPERTS, 1, d)
    xs_in = [x.reshape(b * s, d)] * groups
    x_b0 = [g * bg for g in range(groups)]
    out = None
    for l in range(depth):
        mod_l = mod[l].reshape(b, 6, d)
        qg = jnp.tile(q_norm_g[l], LANES // HEAD_DIM).reshape(1, LANES)
        kg = jnp.tile(k_norm_g[l], LANES // HEAD_DIM).reshape(1, LANES)
        w_in_b = w_in[l].astype(BF16)
        w_out_b = w_out[l].astype(BF16)
        rw_t = router_w[l].T.astype(BF16)
        dec = jnp.stack([ret_decay_fwd[l], ret_decay_bwd[l]]).astype(F32)
        last = l + 1 == depth
        proj = [_inproj(xs_in[g], g * bg, x_b0[g], mod_l, pre_mix_g[l].reshape(1, d), w_in_b, ropes, qg, kg, bg, s)
                for g in range(groups)]
        oa = [_attention(p[0], p[1], p[2], bg, s) for p in proj]
        orr = [_retention(p[3], p[4], p[5], p[6], dec, ret_gn_g[l].reshape(1, RET_W), ret_gn_b[l].reshape(1, RET_W),
                          bg, s) for p in proj]
        routed = []
        for g in range(groups):
            xn, hp, tidx, tw, rank, counts = _outproj(
                oa[g], orr[g], w_out_b, xs_in[g], g * bg, x_b0[g], mod_l, post_mix_g[l].reshape(1, d),
                pre_ffn_g[l].reshape(1, d), rw_t, router_b[l].reshape(N_EXPERTS, 1), bg, s)
            pos, tile_expert, n_valid = _route(counts, tidx, rank, n_tiles)
            pos3 = pos.reshape(TOP_K, n // SC_CHUNK, SC_CHUNK).transpose(1, 0, 2)
            routed.append((xn, tw, pos, tile_expert, n_valid, _sc_scatter_rows(hp, pos3, n_pad)))
        yg = []
        for xn, tw, pos, tile_expert, n_valid, xs in routed:
            ys = _experts(xs, tile_expert, n_valid, l, exp_w_up, bu, exp_w_down, bd)
            yg.append(_sc_gather_rows(ys, pos.reshape(n_asg)))
        nxt = []
        for g, (xn, tw, *_) in enumerate(routed):
            if last:
                out = _combine(yg[g], tw.T, xn, g * bg, mod_l, post_ffn_g[l].reshape(1, d), bg, s,
                               out_batches=b, shared_out=out)
            else:
                nxt.append(_combine(yg[g], tw.T, xn, g * bg, mod_l, post_ffn_g[l].reshape(1, d), bg, s))
        xs_in, x_b0 = nxt, [0] * groups
    return out.reshape(b, s, d)
```

```python
import functools

import numpy as np
import jax
import jax.numpy as jnp
from jax import lax
from jax.experimental import pallas as pl
from jax.experimental.pallas import tpu as pltpu
from jax.experimental.pallas import tpu_sc as plsc

F32 = jnp.float32
BF16 = jnp.bfloat16
U32 = jnp.uint32
I32 = jnp.int32
FP8 = jnp.float8_e4m3fn

D_MODEL = 1024
GRID_W = 64
ROPE_THETA = 10000.0
HEAD_DIM = 64
Q_HEADS = 8
KV_HEADS = 2
RET_HEADS = 4
RET_DIM = 128
RET_CHUNK = 128
ATTN_W = Q_HEADS * HEAD_DIM
KV_W = KV_HEADS * HEAD_DIM
RET_W = RET_HEADS * RET_DIM
IN_W = ATTN_W + 2 * KV_W + 4 * RET_W
N_EXPERTS = 32
TOP_K = 4
D_FF = D_MODEL
SWIGLU_LIMIT = 7.0
SWIGLU_ALPHA = 1.702
NORM_EPS = 1e-6
Q_SCALE = HEAD_DIM ** -0.5 * float(np.log2(np.e))
V_RANGE = 224.0
P_SHIFT = 8.0
HALF = D_MODEL // 2
LANES = 128

ROW_TILE = 512
Q_TILE = 128
KV_TILE = 1024
MOE_TILE = 512
VMEM_LIMIT = 48 * 1024 * 1024
EXPERTS_VMEM_LIMIT = 56 * 1024 * 1024

SC_CORES = 2
SC_SUBCORES = 16
SC_WORKERS = SC_CORES * SC_SUBCORES
SC_CHUNK = 64


def _params(sem):
    return pltpu.CompilerParams(dimension_semantics=sem, vmem_limit_bytes=VMEM_LIMIT)


def _rms(x, g):
    return x * lax.rsqrt(jnp.mean(x * x, axis=-1, keepdims=True) + NORM_EPS) * g


def _pack_bf16_pairs(y):
    u = lax.bitcast_convert_type(y.astype(BF16).astype(F32), U32)
    return (u[:, :HALF] >> 16) | u[:, HALF:]


def _unpack_lo(w):
    return lax.bitcast_convert_type(w << 16, F32)


def _unpack_hi(w):
    return lax.bitcast_convert_type(w & jnp.uint32(0xFFFF0000), F32)


def _mod_kernel(c_ref, w_ref, b_ref, o_ref):
    c = c_ref[...]
    cond = c * jax.nn.sigmoid(c)
    o_ref[0] = jnp.dot(cond, w_ref[0], preferred_element_type=F32, precision=lax.Precision.HIGHEST) + b_ref[0]


def _modulation(c, ada_w, ada_b):
    depth, d, w6 = ada_w.shape
    b = c.shape[0]
    tn = 1536
    return pl.pallas_call(
        _mod_kernel,
        grid=(depth, w6 // tn),
        in_specs=[
            pl.BlockSpec((b, d), lambda l, j: (0, 0)),
            pl.BlockSpec((1, d, tn), lambda l, j: (l, 0, j)),
            pl.BlockSpec((1, 1, tn), lambda l, j: (l, 0, j)),
        ],
        out_specs=pl.BlockSpec((1, b, tn), lambda l, j: (l, 0, j)),
        out_shape=jax.ShapeDtypeStruct((depth, b, w6), F32),
        compiler_params=_params(("arbitrary", "arbitrary")),
        name="modulation",
    )(c, ada_w, ada_b.reshape(depth, 1, w6))


def _rope_tables(s, head_dim):
    quarter = head_dim // 4
    t = jnp.arange(s, dtype=F32)
    row = jnp.floor(t / GRID_W)
    col = t - row * GRID_W
    inv_freq = ROPE_THETA ** (-jnp.arange(quarter, dtype=F32) / quarter)
    lane = np.arange(LANES) % head_dim
    use_col = lane >= head_dim // 2
    second = (lane % (head_dim // 2)) >= quarter
    freq = inv_freq[lane % quarter]
    pos = jnp.where(use_col[None, :], col[:, None], row[:, None])
    ang = pos * freq[None, :]
    sign = jnp.where(second, 1.0, -1.0).astype(F32)
    return jnp.cos(ang), jnp.sin(ang) * sign[None, :]


def _rope(z, cos, sin_signed, quarter):
    lane = lax.broadcasted_iota(I32, z.shape, 1)
    first = (lane % (2 * quarter)) < quarter
    partner = jnp.where(first, pltpu.roll(z, LANES - quarter, 1), pltpu.roll(z, quarter, 1))
    return z * cos + partner * sin_signed


def _inproj_kernel(x_ref, mod_ref, g_ref, w_ref, ca_ref, sa_ref, cr_ref, sr_ref, qg_ref, kg_ref, vs_ref,
                   qa_ref, kd_ref, vt_ref, qr_ref, kr_ref, vr_ref, gr_ref):
    x = x_ref[...]
    m = mod_ref[0]
    h = _rms(x, g_ref[...]) * (1.0 + m[1:2, :]) + m[0:1, :]
    hb = h.astype(BF16)
    tm = x.shape[0]

    def proj(c0):
        z = jnp.dot(hb, w_ref[:, c0:c0 + 2 * LANES], preferred_element_type=F32)
        return z[:, :LANES], z[:, LANES:]

    ri = lax.broadcasted_iota(I32, (LANES, LANES), 0) // HEAD_DIM
    ci = lax.broadcasted_iota(I32, (LANES, LANES), 1) // HEAD_DIM
    head_ones = (ri == ci).astype(BF16)
    ca, sa = ca_ref[...], sa_ref[...]
    cr, sr = cr_ref[...], sr_ref[...]
    lane = lax.broadcasted_iota(I32, (tm, LANES), 1)
    low = lane < HEAD_DIM

    def head_norm_rope(z, g):
        ss = jnp.dot((z * z).astype(BF16), head_ones, preferred_element_type=F32)
        zn = z * lax.rsqrt(ss * (1.0 / HEAD_DIM) + NORM_EPS) * g
        return _rope(zn, ca, sa, HEAD_DIM // 4)

    def store_pair(ref, j, fn, c0):
        for i, z in enumerate(proj(c0 + 2 * j * LANES)):
            ref[:, (2 * j + i) * LANES:(2 * j + i + 1) * LANES] = fn(z).astype(BF16)

    for j in range(ATTN_W // (2 * LANES)):
        store_pair(qa_ref, j, lambda z: head_norm_rope(z, qg_ref[...]) * Q_SCALE, 0)

    k, v = proj(ATTN_W)
    k = head_norm_rope(k, kg_ref[...])
    k_sw = pltpu.roll(k, HEAD_DIM, 1)
    v = v * vs_ref[0]
    kd_ref[0, 0] = jnp.where(low, k, k_sw).astype(FP8)
    kd_ref[0, 1] = jnp.where(low, k_sw, k).astype(FP8)
    vt_ref[0, 0] = jnp.where(low, v, 1.0).T.astype(FP8)
    vt_ref[0, 1] = jnp.where(low, pltpu.roll(v, HEAD_DIM, 1), 1.0).T.astype(FP8)

    base = ATTN_W + 2 * KV_W
    for j in range(RET_W // (2 * LANES)):
        store_pair(qr_ref, j, lambda z: _rope(z, cr, sr, RET_DIM // 4), base)
        store_pair(kr_ref, j, lambda z: _rope(z, cr, sr, RET_DIM // 4) * (RET_DIM ** -0.5), base + RET_W)
        store_pair(vr_ref, j, lambda z: z, base + 2 * RET_W)
        store_pair(gr_ref, j, lambda z: z * jax.nn.sigmoid(z), base + 3 * RET_W)


def _inproj(x2, batch0, x_batch0, mod_l, pre_g, w_in_b, ropes, qg, kg, v_scale, b, s):
    n = b * s
    tm = min(ROW_TILE, s)
    nb = s // tm
    ca, sa, cr, sr = ropes
    row = lambda i: (i, 0)
    full = lambda i: (0, 0)
    rope_spec = pl.BlockSpec((tm, LANES), lambda i: (i % nb, 0))
    wide = jax.ShapeDtypeStruct((n, RET_W), BF16)
    return pl.pallas_call(
        _inproj_kernel,
        grid=(n // tm,),
        in_specs=[
            pl.BlockSpec((tm, D_MODEL), lambda i: (x_batch0 * nb + i, 0)),
            pl.BlockSpec((1, 6, D_MODEL), lambda i: (batch0 + i // nb, 0, 0)),
            pl.BlockSpec((1, D_MODEL), full),
            pl.BlockSpec((D_MODEL, IN_W), full),
            rope_spec, rope_spec, rope_spec, rope_spec,
            pl.BlockSpec((1, LANES), full),
            pl.BlockSpec((1, LANES), full),
            pl.BlockSpec((1, 1, LANES), lambda i: (batch0 + i // nb, 0, 0)),
        ],
        out_specs=[
            pl.BlockSpec((tm, ATTN_W), row),
            pl.BlockSpec((1, KV_HEADS, tm, LANES), lambda i: (i // nb, 0, i % nb, 0)),
            pl.BlockSpec((1, KV_HEADS, LANES, tm), lambda i: (i // nb, 0, 0, i % nb)),
            pl.BlockSpec((tm, RET_W), row),
            pl.BlockSpec((tm, RET_W), row),
            pl.BlockSpec((tm, RET_W), row),
            pl.BlockSpec((tm, RET_W), row),
        ],
        out_shape=[
            jax.ShapeDtypeStruct((n, ATTN_W), BF16),
            jax.ShapeDtypeStruct((b, KV_HEADS, s, LANES), FP8),
            jax.ShapeDtypeStruct((b, KV_HEADS, LANES, s), FP8),
            wide, wide, wide, wide,
        ],
        compiler_params=_params(("parallel",)),
        name="inproj",
    )(x2, mod_l, pre_g, w_in_b, ca, sa, cr, sr, qg, kg, v_scale)


def _attn_kernel(q_ref, kd_ref, vt_ref, fac_ref, o_ref, q_s, qn_s, m_ref, acc_ref, s0_ref, s1_ref, *, tq, tk):
    s = kd_ref.shape[2]
    nk = s // tk
    nq = s // tq
    pairs = (Q_HEADS // KV_HEADS) // 2
    top = lax.broadcasted_iota(I32, (LANES, tq), 0) < HEAD_DIM
    unscale = jnp.concatenate([fac_ref[0, 0:1, :]] * (2 * tq // LANES), axis=1)
    gain = fac_ref[0, 1:2, 0:1]

    def load_q(qi, dst):
        rows = pl.ds(pl.multiple_of(qi * tq, tq), tq)
        for p in range(pairs):
            qt = q_ref[rows, p * LANES:(p + 1) * LANES].astype(F32).T
            dst[p, :, :tq] = jnp.where(top, qt, 0.0).astype(FP8)
            dst[p, :, tq:] = jnp.where(top, 0.0, qt).astype(FP8)

    def keys(j):
        return kd_ref[0, 0, j * tk:(j + 1) * tk, :]

    def step(j, cur_ref, nxt_ref, q_next, j_next):
        vt = vt_ref[0, 0, :, j * tk:(j + 1) * tk]
        kd = keys(j_next)
        nxt_ref[0] = jnp.dot(kd, q_next[0], preferred_element_type=F32)
        for p in range(pairs):
            sc = cur_ref[p]
            m_old = m_ref[p]
            m_new = jnp.maximum(m_old, jnp.max(sc, axis=0, keepdims=True))
            pt = jnp.exp2((sc - (m_new - P_SHIFT / gain)) * gain)
            alpha = jnp.exp2((m_old - m_new) * gain)
            if p + 1 < pairs:
                nxt_ref[p + 1] = jnp.dot(kd, q_next[p + 1], preferred_element_type=F32)
            acc_ref[p] = alpha * acc_ref[p] + jnp.dot(vt, pt.astype(FP8), preferred_element_type=F32)
            m_ref[p] = m_new

    bufs = (s0_ref, s1_ref)
    load_q(0, q_s)
    for p in range(pairs):
        s0_ref[p] = jnp.dot(keys(0), q_s[p], preferred_element_type=F32)

    def body(qi, carry):
        load_q(qi, q_s)
        load_q(jnp.minimum(qi + 1, nq - 1), qn_s)
        m_ref[...] = jnp.full(m_ref.shape, -jnp.inf, F32)
        acc_ref[...] = jnp.zeros(acc_ref.shape, F32)
        for j in range(nk):
            last = j + 1 == nk
            step(j, bufs[j % len(bufs)], bufs[(j + 1) % len(bufs)], qn_s if last else q_s, 0 if last else j + 1)
        rows = pl.ds(pl.multiple_of(qi * tq, tq), tq)
        for p in range(pairs):
            acc = acc_ref[p]
            o_t = acc[:HEAD_DIM] / acc[HEAD_DIM:] * unscale
            both = jnp.concatenate([o_t[:, :tq], o_t[:, tq:]], axis=0)
            o_ref[rows, p * LANES:(p + 1) * LANES] = both.T.astype(BF16)
        return carry

    lax.fori_loop(0, nq, body, 0)


def _attention(qa, kd, vt, fac, batch0, b, s):
    n = qa.shape[0]
    tq = Q_TILE
    tk = min(KV_TILE, s // 2)
    assert tq == LANES and s % tq == 0 and s % (2 * tk) == 0 and tk % LANES == 0, (s, tq, tk)
    gw = ATTN_W // KV_HEADS
    pairs = (Q_HEADS // KV_HEADS) // 2
    return pl.pallas_call(
        functools.partial(_attn_kernel, tq=tq, tk=tk),
        grid=(b, KV_HEADS),
        in_specs=[
            pl.BlockSpec((s, gw), lambda bi, h: (bi, h)),
            pl.BlockSpec((1, 1, s, LANES), lambda bi, h: (bi, h, 0, 0)),
            pl.BlockSpec((1, 1, LANES, s), lambda bi, h: (bi, h, 0, 0)),
            pl.BlockSpec((1, 2, LANES), lambda bi, h: (batch0 + bi, 0, 0)),
        ],
        out_specs=pl.BlockSpec((s, gw), lambda bi, h: (bi, h)),
        out_shape=jax.ShapeDtypeStruct((n, ATTN_W), BF16),
        scratch_shapes=[
            pltpu.VMEM((pairs, LANES, 2 * tq), FP8),
            pltpu.VMEM((pairs, LANES, 2 * tq), FP8),
            pltpu.VMEM((pairs, 1, 2 * tq), F32),
            pltpu.VMEM((pairs, LANES, 2 * tq), F32),
            pltpu.VMEM((pairs, tk, 2 * tq), F32),
            pltpu.VMEM((pairs, tk, 2 * tq), F32),
        ],
        compiler_params=_params(("parallel", "parallel")),
        name="attention",
    )(qa, kd, vt, fac)


def _retention_kernel(dec_ref, q_ref, k_ref, v_ref, g_ref, gng_ref, gnb_ref, o_ref,
                      fbuf, bbuf, sf_ref, sb_ref):
    h = pl.program_id(1)
    c = RET_CHUNK
    s = q_ref.shape[1]
    nc = s // c
    lgf = jnp.full((1, 1), dec_ref[0, h], F32)
    lgb = jnp.full((1, 1), dec_ref[1, h], F32)
    ii = lax.broadcasted_iota(I32, (c, c), 0)
    jj = lax.broadcasted_iota(I32, (c, c), 1)
    diff = (ii - jj).astype(F32)
    d_f = jnp.where(diff >= 0, jnp.exp(lgf * jnp.maximum(diff, 0.0)), 0.0)
    d_b = jnp.where(diff <= 0, jnp.exp(lgb * jnp.maximum(-diff, 0.0)), 0.0)
    idx = lax.broadcasted_iota(I32, (c, 1), 0).astype(F32)
    xi_f = jnp.exp(lgf * (idx + 1.0))
    zeta_f = jnp.exp(lgf * (c - 1.0 - idx))
    xi_b = jnp.exp(lgb * (c - idx))
    zeta_b = jnp.exp(lgb * idx)
    cd_f = jnp.exp(lgf * c)
    cd_b = jnp.exp(lgb * c)
    sf_ref[...] = jnp.zeros(sf_ref.shape, F32)
    sb_ref[...] = jnp.zeros(sb_ref.shape, F32)
    nt = (((1,), (1,)), ((), ()))

    fwd = (d_f, xi_f, zeta_f, cd_f, sf_ref, fbuf)
    bwd = (d_b, xi_b, zeta_b, cd_b, sb_ref, bbuf)

    def body(i, carry):
        jobs = [(fwd, 2 * i), (bwd, nc - 1 - 2 * i), (fwd, 2 * i + 1), (bwd, nc - 2 - 2 * i)]
        offs = [pl.multiple_of(ci * c, c) for _, ci in jobs]
        qs = [q_ref[0, pl.ds(off, c), :] for off in offs]
        ks = [k_ref[0, pl.ds(off, c), :] for off in offs]
        vs = [v_ref[0, pl.ds(off, c), :] for off in offs]
        scores = [lax.dot_general(q, k, nt, preferred_element_type=F32) for q, k in zip(qs, ks)]
        incs = []
        for (d, _), k, v in zip(jobs, ks, vs):
            kzt = (k.astype(F32) * d[2]).T.astype(BF16)
            incs.append(jnp.dot(kzt, v, preferred_element_type=F32))
        states = [fwd[4][...], bwd[4][...]]
        cross = [jnp.dot(qs[n], states[n].astype(BF16), preferred_element_type=F32) for n in range(2)]
        states += [states[n] * jobs[n][0][3] + incs[n] for n in range(2)]
        cross += [jnp.dot(qs[n], states[n].astype(BF16), preferred_element_type=F32) for n in range(2, 4)]
        for n, ((d, _), off) in enumerate(zip(jobs, offs)):
            a = (scores[n] * d[0]).astype(BF16)
            d[5][pl.ds(off, c), :] = jnp.dot(a, vs[n], preferred_element_type=F32) + cross[n] * d[1]
        for n in range(2):
            jobs[n][0][4][...] = states[2 + n] * jobs[n][0][3] + incs[2 + n]
        return carry

    lax.fori_loop(0, nc // 2, body, 0)

    rows = min(512, s)

    def fin(t, carry):
        off = pl.multiple_of(t * rows, rows)
        r = fbuf[pl.ds(off, rows), :] + bbuf[pl.ds(off, rows), :]
        mu = jnp.mean(r, axis=-1, keepdims=True)
        d = r - mu
        var = jnp.mean(d * d, axis=-1, keepdims=True)
        y = d * lax.rsqrt(var + NORM_EPS) * gng_ref[...] + gnb_ref[...]
        o_ref[0, pl.ds(off, rows), :] = (g_ref[0, pl.ds(off, rows), :].astype(F32) * y).astype(BF16)
        return carry

    lax.fori_loop(0, s // rows, fin, 0)


def _retention(qr, kr, vr, gr, dec, gn_g, gn_b, b, s):
    shp = (b, s, RET_W)
    assert s % (2 * RET_CHUNK) == 0, s
    head = pl.BlockSpec((1, s, RET_DIM), lambda bi, h: (bi, 0, h))
    vec = pl.BlockSpec((1, RET_DIM), lambda bi, h: (0, h))
    out = pl.pallas_call(
        _retention_kernel,
        grid=(b, RET_HEADS),
        in_specs=[pl.BlockSpec(memory_space=pltpu.SMEM), head, head, head, head, vec, vec],
        out_specs=head,
        out_shape=jax.ShapeDtypeStruct(shp, BF16),
        scratch_shapes=[
            pltpu.VMEM((s, RET_DIM), F32),
            pltpu.VMEM((s, RET_DIM), F32),
            pltpu.VMEM((RET_DIM, RET_DIM), F32),
            pltpu.VMEM((RET_DIM, RET_DIM), F32),
        ],
        compiler_params=_params(("parallel", "parallel")),
        name="retention",
    )(dec, qr.reshape(shp), kr.reshape(shp), vr.reshape(shp), gr.reshape(shp), gn_g, gn_b)
    return out.reshape(b * s, RET_W)


def _outproj_kernel(oa_ref, or_ref, w_ref, x_ref, mod_ref, pg_ref, fg_ref, rw_ref, rb_ref,
                    xn_ref, hp_ref, ti_ref, tw_ref, rk_ref, cnt_ref, base_ref):
    @pl.when(pl.program_id(0) == 0)
    def _():
        base_ref[...] = jnp.zeros(base_ref.shape, F32)

    m = mod_ref[0]
    y = jnp.dot(oa_ref[...], w_ref[:ATTN_W, :], preferred_element_type=F32)
    y = y + jnp.dot(or_ref[...], w_ref[ATTN_W:, :], preferred_element_type=F32)
    xn = x_ref[...] + m[2:3, :] * _rms(y, pg_ref[...])
    xn_ref[...] = xn
    h = _rms(xn, fg_ref[...]) * (1.0 + m[4:5, :]) + m[3:4, :]
    hb = h.astype(BF16)
    u = lax.bitcast_convert_type(hb.astype(F32), U32)
    hp_ref[...] = (u[:, :HALF] >> 16) | u[:, HALF:]

    tm = hb.shape[0]
    nt = (((1,), (1,)), ((), ()))
    logits = lax.dot_general(rw_ref[...], hb, nt, preferred_element_type=F32) + rb_ref[...]
    e_iota = lax.broadcasted_iota(I32, (N_EXPERTS, tm), 0).astype(F32)
    vals, idxs = [], []
    cur = logits
    for _ in range(TOP_K):
        mx = jnp.max(cur, axis=0, keepdims=True)
        ik = jnp.min(jnp.where(cur == mx, e_iota, float(N_EXPERTS)), axis=0, keepdims=True)
        vals.append(mx)
        idxs.append(ik)
        cur = jnp.where(e_iota == ik, -jnp.inf, cur)
    v = jnp.concatenate(vals, axis=0)
    w = jnp.exp(v - v[0:1, :])
    tw_ref[...] = w / jnp.sum(w, axis=0, keepdims=True)
    ti_ref[...] = jnp.concatenate(idxs, axis=0).astype(I32)

    onehot = jnp.zeros((N_EXPERTS, tm), F32)
    for ik in idxs:
        onehot = onehot + (e_iota == ik).astype(F32)
    before = (lax.broadcasted_iota(I32, (tm, tm), 0) < lax.broadcasted_iota(I32, (tm, tm), 1)).astype(BF16)
    seen = jnp.dot(onehot.astype(BF16), before, preferred_element_type=F32) + base_ref[...]
    ranks = [jnp.sum(jnp.where(e_iota == ik, seen, 0.0), axis=0, keepdims=True) for ik in idxs]
    rk_ref[...] = jnp.concatenate(ranks, axis=0).astype(I32)
    total = base_ref[...] + jnp.sum(onehot, axis=1, keepdims=True)
    base_ref[...] = total
    cnt_ref[...] = jnp.broadcast_to(total, cnt_ref.shape).astype(I32)


def _outproj(oa, orr, w_out_b, x2, batch0, x_batch0, mod_l, post_g, ffn_g, rw_t, rb, b, s):
    n = b * s
    tm = min(ROW_TILE, s)
    nb = s // tm
    row = lambda i: (i, 0)
    full = lambda i: (0, 0)
    col = lambda i: (0, i)
    return pl.pallas_call(
        _outproj_kernel,
        grid=(n // tm,),
        in_specs=[
            pl.BlockSpec((tm, ATTN_W), row),
            pl.BlockSpec((tm, RET_W), row),
            pl.BlockSpec((D_MODEL, D_MODEL), full),
            pl.BlockSpec((tm, D_MODEL), lambda i: (x_batch0 * nb + i, 0)),
            pl.BlockSpec((1, 6, D_MODEL), lambda i: (batch0 + i // nb, 0, 0)),
            pl.BlockSpec((1, D_MODEL), full),
            pl.BlockSpec((1, D_MODEL), full),
            pl.BlockSpec((N_EXPERTS, D_MODEL), full),
            pl.BlockSpec((N_EXPERTS, 1), full),
        ],
        out_specs=[
            pl.BlockSpec((tm, D_MODEL), row),
            pl.BlockSpec((tm, HALF), row),
            pl.BlockSpec((TOP_K, tm), col),
            pl.BlockSpec((TOP_K, tm), col),
            pl.BlockSpec((TOP_K, tm), col),
            pl.BlockSpec((N_EXPERTS, LANES), full),
        ],
        out_shape=[
            jax.ShapeDtypeStruct((n, D_MODEL), F32),
            jax.ShapeDtypeStruct((n, HALF), U32),
            jax.ShapeDtypeStruct((TOP_K, n), I32),
            jax.ShapeDtypeStruct((TOP_K, n), F32),
            jax.ShapeDtypeStruct((TOP_K, n), I32),
            jax.ShapeDtypeStruct((N_EXPERTS, LANES), I32),
        ],
        scratch_shapes=[pltpu.VMEM((N_EXPERTS, 1), F32)],
        compiler_params=_params(("arbitrary",)),
        name="outproj_router",
    )(oa, orr, w_out_b, x2, mod_l, post_g, ffn_g, rw_t, rb)


def _sc_scatter_rows(rows, pos3, n_out):
    n, w = rows.shape
    nchunk, kk, c = pos3.shape
    per_w = nchunk // SC_WORKERS
    mesh = plsc.VectorSubcoreMesh(core_axis_name="c", subcore_axis_name="s")

    @functools.partial(
        pl.kernel, mesh=mesh,
        out_type=jax.ShapeDtypeStruct((n_out, w), rows.dtype),
        scratch_types=[pltpu.VMEM((kk, c), I32), pltpu.VMEM((c, w), rows.dtype)],
    )
    def k(rows_hbm, pos_hbm, out_hbm, idx_v, rows_v):
        wid = lax.axis_index("s") * SC_CORES + lax.axis_index("c")

        @pl.loop(0, per_w)
        def _(i):
            ch = wid * per_w + i
            pltpu.sync_copy(pos_hbm.at[ch], idx_v)
            pltpu.sync_copy(rows_hbm.at[pl.ds(ch * c, c)], rows_v)
            for j in range(kk):
                pltpu.sync_copy(rows_v, out_hbm.at[idx_v.at[j]])

    return k(rows, pos3)


def _sc_gather_rows(table, idx):
    w = table.shape[1]
    b = idx.shape[0]
    c = SC_CHUNK
    per_w = b // (SC_WORKERS * c)
    mesh = plsc.VectorSubcoreMesh(core_axis_name="c", subcore_axis_name="s")

    @functools.partial(
        pl.kernel, mesh=mesh,
        out_type=jax.ShapeDtypeStruct((b, w), table.dtype),
        scratch_types=[pltpu.VMEM((c,), I32), pltpu.VMEM((c, w), table.dtype), pltpu.SemaphoreType.DMA],
    )
    def k(table_hbm, idx_hbm, out_hbm, idx_v, rows_v, sem):
        wid = lax.axis_index("s") * SC_CORES + lax.axis_index("c")

        @pl.loop(0, per_w)
        def _(i):
            base = (wid * per_w + i) * c
            pltpu.sync_copy(idx_hbm.at[pl.ds(base, c)], idx_v)
            pltpu.async_copy(table_hbm.at[idx_v], rows_v, sem).wait()
            pltpu.sync_copy(rows_v, out_hbm.at[pl.ds(base, c)])

    return k(table, idx)


def _experts_kernel(te_ref, nv_ref, xs_ref, wu_ref, bu_ref, wd_ref, bd_ref, ys_ref, wu_b, wd_b):
    i = pl.program_id(0)
    live = i < nv_ref[0]
    new_expert = jnp.logical_or(i == 0, te_ref[i] != te_ref[jnp.maximum(i - 1, 0)])

    @pl.when(jnp.logical_and(live, new_expert))
    def _():
        wu_b[...] = wu_ref[0, 0].astype(BF16)
        wd_b[...] = wd_ref[0, 0].astype(BF16)

    @pl.when(live)
    def _():
        w = xs_ref[...]
        lo = _unpack_lo(w).astype(BF16)
        hi = _unpack_hi(w).astype(BF16)
        up = jnp.dot(lo, wu_b[:HALF, :], preferred_element_type=F32)
        up = up + jnp.dot(hi, wu_b[HALF:, :], preferred_element_type=F32) + bu_ref[0, 0]
        glu = jnp.minimum(up[:, :D_FF], SWIGLU_LIMIT)
        lin = jnp.clip(up[:, D_FF:], -SWIGLU_LIMIT, SWIGLU_LIMIT)
        act = glu * jax.nn.sigmoid(SWIGLU_ALPHA * glu) * (lin + 1.0)
        y = jnp.dot(act.astype(BF16), wd_b[...], preferred_element_type=F32) + bd_ref[0, 0]
        ys_ref[...] = _pack_bf16_pairs(y)


def _experts(xs, tile_expert, n_valid, layer, wu, bu, wd, bd):
    n_pad = xs.shape[0]
    t = MOE_TILE
    grid_spec = pltpu.PrefetchScalarGridSpec(
        num_scalar_prefetch=2,
        grid=(n_pad // t,),
        in_specs=[
            pl.BlockSpec((t, HALF), lambda i, te, nv: (i, 0)),
            pl.BlockSpec((1, 1, D_MODEL, 2 * D_FF), lambda i, te, nv: (layer, te[i], 0, 0)),
            pl.BlockSpec((1, 1, 1, 2 * D_FF), lambda i, te, nv: (layer, te[i], 0, 0)),
            pl.BlockSpec((1, 1, D_FF, D_MODEL), lambda i, te, nv: (layer, te[i], 0, 0)),
            pl.BlockSpec((1, 1, 1, D_MODEL), lambda i, te, nv: (layer, te[i], 0, 0)),
        ],
        out_specs=pl.BlockSpec((t, HALF), lambda i, te, nv: (i, 0)),
        scratch_shapes=[pltpu.VMEM((D_MODEL, 2 * D_FF), BF16), pltpu.VMEM((D_FF, D_MODEL), BF16)],
    )
    return pl.pallas_call(
        _experts_kernel,
        grid_spec=grid_spec,
        out_shape=jax.ShapeDtypeStruct((n_pad, HALF), U32),
        compiler_params=pltpu.CompilerParams(dimension_semantics=("arbitrary",), vmem_limit_bytes=EXPERTS_VMEM_LIMIT),
        name="experts",
    )(tile_expert, n_valid, xs, wu, bu, wd, bd)


def _combine_kernel(y0_ref, y1_ref, y2_ref, y3_ref, w_ref, x_ref, mod_ref, g_ref, *rest):
    o_ref = rest[-1]
    w = w_ref[...]
    lo = None
    hi = None
    for k, y_ref in enumerate((y0_ref, y1_ref, y2_ref, y3_ref)):
        word = y_ref[...]
        wk = w[:, k:k + 1]
        lo_k = wk * _unpack_lo(word)
        hi_k = wk * _unpack_hi(word)
        lo = lo_k if lo is None else lo + lo_k
        hi = hi_k if hi is None else hi + hi_k
    yf = jnp.concatenate([lo, hi], axis=1)
    m = mod_ref[0]
    o_ref[...] = x_ref[...] + m[5:6, :] * _rms(yf, g_ref[...])


def _combine(yg, tw_t, xn, batch0, mod_l, post_g, b, s, out_batches=None, shared_out=None):
    n = b * s
    tm = min(ROW_TILE, s)
    nb = s // tm
    nblk = n // tm
    row = lambda i: (i, 0)
    full = lambda i: (0, 0)
    y_specs = [pl.BlockSpec((tm, HALF), functools.partial(lambda i, k: (k * nblk + i, 0), k=k)) for k in range(TOP_K)]
    out_rows, out_blk0 = (n, 0) if out_batches is None else (out_batches * s, batch0 * nb)
    extra_specs, extra_args, aliases = [], [], {}
    if shared_out is not None:
        extra_specs, extra_args, aliases = [pl.BlockSpec(memory_space=pl.ANY)], [shared_out], {8: 0}
    return pl.pallas_call(
        _combine_kernel,
        grid=(nblk,),
        in_specs=y_specs + [
            pl.BlockSpec((tm, TOP_K), row),
            pl.BlockSpec((tm, D_MODEL), row),
            pl.BlockSpec((1, 6, D_MODEL), lambda i: (batch0 + i // nb, 0, 0)),
            pl.BlockSpec((1, D_MODEL), full),
        ] + extra_specs,
        out_specs=pl.BlockSpec((tm, D_MODEL), lambda i: (out_blk0 + i, 0)),
        out_shape=jax.ShapeDtypeStruct((out_rows, D_MODEL), F32),
        input_output_aliases=aliases,
        compiler_params=_params(("parallel",)),
        name="combine",
    )(yg, yg, yg, yg, tw_t, xn, mod_l, post_g, *extra_args)


def _route(counts, tidx, rank, n_tiles):
    t = MOE_TILE
    cnt = counts[:, 0]
    padded = ((cnt + t - 1) // t) * t
    ends = jnp.cumsum(padded)
    starts = ends - padded
    experts = jnp.arange(N_EXPERTS, dtype=I32)[:, None, None]
    start_of = jnp.sum(jnp.where(tidx[None] == experts, starts[:, None, None], 0), axis=0)
    pos = start_of + rank
    tile_ends = ends // t
    n_valid = tile_ends[-1]
    tile = jnp.minimum(jnp.arange(n_tiles, dtype=I32), n_valid - 1)
    tile_expert = jnp.minimum(jnp.sum(tile_ends[None, :] <= tile[:, None], axis=1), N_EXPERTS - 1).astype(I32)
    return pos.astype(I32), tile_expert, n_valid.reshape(1).astype(I32)


def _attention_ranges(mod_l, pre_g, w_in_l, q_gain, k_gain):
    d = pre_g.shape[0]
    gq = jnp.max(jnp.abs(q_gain))
    gk = jnp.max(jnp.abs(k_gain))
    gq = jnp.where(gq > 0, gq, 1.0)
    gk = jnp.where(gk > 0, gk, 1.0)
    reps = LANES // HEAD_DIM
    qg = jnp.tile(q_gain / gq, reps).reshape(1, LANES)
    kg = jnp.tile(k_gain / gk, reps).reshape(1, LANES)
    shift, scale = mod_l[:, 0, :], mod_l[:, 1, :]
    h_norm = d ** 0.5 * jnp.max(jnp.abs(pre_g[None, :] * (1.0 + scale)), axis=1) + jnp.linalg.norm(shift, axis=1)
    w_v = w_in_l[:, ATTN_W + KV_W:ATTN_W + 2 * KV_W]
    bound = 1.02 * h_norm * jnp.max(jnp.linalg.norm(w_v, axis=0))
    v_scale = V_RANGE / jnp.maximum(bound, 1e-30)
    nb = mod_l.shape[0]
    lanes = lambda t: jnp.broadcast_to(t[:, None], (nb, LANES))
    fac = jnp.stack([lanes(1.0 / v_scale), jnp.full((nb, LANES), gq * gk, F32)], axis=1)
    return qg, kg, lanes(v_scale)[:, None, :], fac


def kernel(x, c, ada_w, ada_b, pre_mix_g, post_mix_g, w_in, q_norm_g, k_norm_g, ret_decay_fwd, ret_decay_bwd, ret_gn_g, ret_gn_b, w_out, pre_ffn_g, post_ffn_g, router_w, router_b, exp_w_up, exp_b_up, exp_w_down, exp_b_down):
    b, s, d = x.shape
    depth = ada_w.shape[0]
    groups = 2 if b % 2 == 0 else 1
    bg = b // groups
    n = bg * s
    n_asg = n * TOP_K
    n_pad = n_asg + N_EXPERTS * MOE_TILE
    n_tiles = n_pad // MOE_TILE

    mod = _modulation(c, ada_w, ada_b)
    ropes = _rope_tables(s, HEAD_DIM) + _rope_tables(s, RET_DIM)
    bu = exp_b_up.reshape(depth, N_EXPERTS, 1, 2 * D_FF)
    bd = exp_b_down.reshape(depth, N_EXPERTS, 1, d)
    xs_in = [x.reshape(b * s, d)] * groups
    x_b0 = [g * bg for g in range(groups)]
    out = None
    for l in range(depth):
        mod_l = mod[l].reshape(b, 6, d)
        qg, kg, v_scale, fac = _attention_ranges(mod_l, pre_mix_g[l], w_in[l], q_norm_g[l], k_norm_g[l])
        w_in_b = w_in[l].astype(BF16)
        w_out_b = w_out[l].astype(BF16)
        rw_t = router_w[l].T.astype(BF16)
        dec = jnp.stack([ret_decay_fwd[l], ret_decay_bwd[l]]).astype(F32)
        last = l + 1 == depth
        proj = [_inproj(xs_in[g], g * bg, x_b0[g], mod_l, pre_mix_g[l].reshape(1, d), w_in_b, ropes, qg, kg, v_scale,
                        bg, s) for g in range(groups)]
        oa = [_attention(p[0], p[1], p[2], fac, g * bg, bg, s) for g, p in enumerate(proj)]
        orr = [_retention(p[3], p[4], p[5], p[6], dec, ret_gn_g[l].reshape(1, RET_W), ret_gn_b[l].reshape(1, RET_W),
                          bg, s) for p in proj]
        routed = []
        for g in range(groups):
            xn, hp, tidx, tw, rank, counts = _outproj(
                oa[g], orr[g], w_out_b, xs_in[g], g * bg, x_b0[g], mod_l, post_mix_g[l].reshape(1, d),
                pre_ffn_g[l].reshape(1, d), rw_t, router_b[l].reshape(N_EXPERTS, 1), bg, s)
            pos, tile_expert, n_valid = _route(counts, tidx, rank, n_tiles)
            pos3 = pos.reshape(TOP_K, n // SC_CHUNK, SC_CHUNK).transpose(1, 0, 2)
            routed.append((xn, tw, pos, tile_expert, n_valid, _sc_scatter_rows(hp, pos3, n_pad)))
        yg = []
        for xn, tw, pos, tile_expert, n_valid, xs in routed:
            ys = _experts(xs, tile_expert, n_valid, l, exp_w_up, bu, exp_w_down, bd)
            yg.append(_sc_gather_rows(ys, pos.reshape(n_asg)))
        nxt = []
        for g, (xn, tw, *_) in enumerate(routed):
            if last:
                out = _combine(yg[g], tw.T, xn, g * bg, mod_l, post_ffn_g[l].reshape(1, d), bg, s,
                               out_batches=b, shared_out=out)
            else:
                nxt.append(_combine(yg[g], tw.T, xn, g * bg, mod_l, post_ffn_g[l].reshape(1, d), bg, s))
        xs_in, x_b0 = nxt, [0] * groups
    return out.reshape(b, s, d)
```

```python
import functools

import numpy as np
import jax
import jax.numpy as jnp
from jax import lax
from jax.experimental import pallas as pl
from jax.experimental.pallas import tpu as pltpu
from jax.experimental.pallas import tpu_sc as plsc

F32 = jnp.float32
BF16 = jnp.bfloat16
U32 = jnp.uint32
I32 = jnp.int32
FP8 = jnp.float8_e4m3fn

D_MODEL = 1024
GRID_W = 64
ROPE_THETA = 10000.0
HEAD_DIM = 64
Q_HEADS = 8
KV_HEADS = 2
RET_HEADS = 4
RET_DIM = 128
RET_CHUNK = 128
ATTN_W = Q_HEADS * HEAD_DIM
KV_W = KV_HEADS * HEAD_DIM
RET_W = RET_HEADS * RET_DIM
IN_W = ATTN_W + 2 * KV_W + 4 * RET_W
N_EXPERTS = 32
TOP_K = 4
D_FF = D_MODEL
SWIGLU_LIMIT = 7.0
SWIGLU_ALPHA = 1.702
NORM_EPS = 1e-6
Q_SCALE = HEAD_DIM ** -0.5 * float(np.log2(np.e))
V_RANGE = 224.0
P_SHIFT = 8.0
HALF = D_MODEL // 2
LANES = 128

ROW_TILE = 512
Q_TILE = 128
Q_TILES_PER_TRIP = 2
KV_TILE = 1024
MOE_TILE = 512
VMEM_LIMIT = 48 * 1024 * 1024
EXPERTS_VMEM_LIMIT = 56 * 1024 * 1024

SC_CORES = 2
SC_SUBCORES = 16
SC_WORKERS = SC_CORES * SC_SUBCORES
SC_CHUNK = 64


def _params(sem):
    return pltpu.CompilerParams(dimension_semantics=sem, vmem_limit_bytes=VMEM_LIMIT)


def _rms(x, g):
    return x * lax.rsqrt(jnp.mean(x * x, axis=-1, keepdims=True) + NORM_EPS) * g


def _pack_bf16_pairs(y):
    u = lax.bitcast_convert_type(y.astype(BF16).astype(F32), U32)
    return (u[:, :HALF] >> 16) | u[:, HALF:]


def _unpack_lo(w):
    return lax.bitcast_convert_type(w << 16, F32)


def _unpack_hi(w):
    return lax.bitcast_convert_type(w & jnp.uint32(0xFFFF0000), F32)


def _mod_kernel(c_ref, w_ref, b_ref, o_ref):
    c = c_ref[...]
    cond = c * jax.nn.sigmoid(c)
    o_ref[0] = jnp.dot(cond, w_ref[0], preferred_element_type=F32, precision=lax.Precision.HIGHEST) + b_ref[0]


def _modulation(c, ada_w, ada_b):
    depth, d, w6 = ada_w.shape
    b = c.shape[0]
    tn = 1536
    return pl.pallas_call(
        _mod_kernel,
        grid=(depth, w6 // tn),
        in_specs=[
            pl.BlockSpec((b, d), lambda l, j: (0, 0)),
            pl.BlockSpec((1, d, tn), lambda l, j: (l, 0, j)),
            pl.BlockSpec((1, 1, tn), lambda l, j: (l, 0, j)),
        ],
        out_specs=pl.BlockSpec((1, b, tn), lambda l, j: (l, 0, j)),
        out_shape=jax.ShapeDtypeStruct((depth, b, w6), F32),
        compiler_params=_params(("arbitrary", "arbitrary")),
        name="modulation",
    )(c, ada_w, ada_b.reshape(depth, 1, w6))


def _rope_tables(s, head_dim):
    quarter = head_dim // 4
    t = jnp.arange(s, dtype=F32)
    row = jnp.floor(t / GRID_W)
    col = t - row * GRID_W
    inv_freq = ROPE_THETA ** (-jnp.arange(quarter, dtype=F32) / quarter)
    lane = np.arange(LANES) % head_dim
    use_col = lane >= head_dim // 2
    second = (lane % (head_dim // 2)) >= quarter
    freq = inv_freq[lane % quarter]
    pos = jnp.where(use_col[None, :], col[:, None], row[:, None])
    ang = pos * freq[None, :]
    sign = jnp.where(second, 1.0, -1.0).astype(F32)
    return jnp.cos(ang), jnp.sin(ang) * sign[None, :]


def _rope(z, cos, sin_signed, quarter):
    lane = lax.broadcasted_iota(I32, z.shape, 1)
    first = (lane % (2 * quarter)) < quarter
    partner = jnp.where(first, pltpu.roll(z, LANES - quarter, 1), pltpu.roll(z, quarter, 1))
    return z * cos + partner * sin_signed


def _inproj_kernel(x_ref, mod_ref, g_ref, w_ref, ca_ref, sa_ref, cr_ref, sr_ref, qg_ref, kg_ref, vs_ref,
                   qa_ref, kd_ref, vt_ref, qr_ref, kr_ref, vr_ref, gr_ref):
    x = x_ref[...]
    m = mod_ref[0]
    h = _rms(x, g_ref[...]) * (1.0 + m[1:2, :]) + m[0:1, :]
    hb = h.astype(BF16)
    tm = x.shape[0]

    def proj(c0):
        z = jnp.dot(hb, w_ref[:, c0:c0 + 2 * LANES], preferred_element_type=F32)
        return z[:, :LANES], z[:, LANES:]

    ri = lax.broadcasted_iota(I32, (LANES, LANES), 0) // HEAD_DIM
    ci = lax.broadcasted_iota(I32, (LANES, LANES), 1) // HEAD_DIM
    head_ones = (ri == ci).astype(BF16)
    ca, sa = ca_ref[...], sa_ref[...]
    cr, sr = cr_ref[...], sr_ref[...]
    lane = lax.broadcasted_iota(I32, (tm, LANES), 1)
    low = lane < HEAD_DIM

    def head_norm_rope(z, g):
        ss = jnp.dot((z * z).astype(BF16), head_ones, preferred_element_type=F32)
        zn = z * lax.rsqrt(ss * (1.0 / HEAD_DIM) + NORM_EPS) * g
        return _rope(zn, ca, sa, HEAD_DIM // 4)

    def store_pair(ref, j, fn, c0):
        for i, z in enumerate(proj(c0 + 2 * j * LANES)):
            ref[:, (2 * j + i) * LANES:(2 * j + i + 1) * LANES] = fn(z).astype(BF16)

    for j in range(ATTN_W // (2 * LANES)):
        store_pair(qa_ref, j, lambda z: head_norm_rope(z, qg_ref[...]) * Q_SCALE, 0)

    k, v = proj(ATTN_W)
    k = head_norm_rope(k, kg_ref[...])
    k_sw = pltpu.roll(k, HEAD_DIM, 1)
    v = v * vs_ref[0]
    kd_ref[0, 0] = jnp.where(low, k, k_sw).astype(FP8)
    kd_ref[0, 1] = jnp.where(low, k_sw, k).astype(FP8)
    vt_ref[0, 0] = jnp.where(low, v, 1.0).T.astype(FP8)
    vt_ref[0, 1] = jnp.where(low, pltpu.roll(v, HEAD_DIM, 1), 1.0).T.astype(FP8)

    base = ATTN_W + 2 * KV_W
    for j in range(RET_W // (2 * LANES)):
        store_pair(qr_ref, j, lambda z: _rope(z, cr, sr, RET_DIM // 4), base)
        store_pair(kr_ref, j, lambda z: _rope(z, cr, sr, RET_DIM // 4) * (RET_DIM ** -0.5), base + RET_W)
        store_pair(vr_ref, j, lambda z: z, base + 2 * RET_W)
        store_pair(gr_ref, j, lambda z: z * jax.nn.sigmoid(z), base + 3 * RET_W)


def _inproj(x2, batch0, x_batch0, mod_l, pre_g, w_in_b, ropes, qg, kg, v_scale, b, s):
    n = b * s
    tm = min(ROW_TILE, s)
    nb = s // tm
    ca, sa, cr, sr = ropes
    row = lambda i: (i, 0)
    full = lambda i: (0, 0)
    rope_spec = pl.BlockSpec((tm, LANES), lambda i: (i % nb, 0))
    wide = jax.ShapeDtypeStruct((n, RET_W), BF16)
    return pl.pallas_call(
        _inproj_kernel,
        grid=(n // tm,),
        in_specs=[
            pl.BlockSpec((tm, D_MODEL), lambda i: (x_batch0 * nb + i, 0)),
            pl.BlockSpec((1, 6, D_MODEL), lambda i: (batch0 + i // nb, 0, 0)),
            pl.BlockSpec((1, D_MODEL), full),
            pl.BlockSpec((D_MODEL, IN_W), full),
            rope_spec, rope_spec, rope_spec, rope_spec,
            pl.BlockSpec((1, LANES), full),
            pl.BlockSpec((1, LANES), full),
            pl.BlockSpec((1, 1, LANES), lambda i: (batch0 + i // nb, 0, 0)),
        ],
        out_specs=[
            pl.BlockSpec((tm, ATTN_W), row),
            pl.BlockSpec((1, KV_HEADS, tm, LANES), lambda i: (i // nb, 0, i % nb, 0)),
            pl.BlockSpec((1, KV_HEADS, LANES, tm), lambda i: (i // nb, 0, 0, i % nb)),
            pl.BlockSpec((tm, RET_W), row),
            pl.BlockSpec((tm, RET_W), row),
            pl.BlockSpec((tm, RET_W), row),
            pl.BlockSpec((tm, RET_W), row),
        ],
        out_shape=[
            jax.ShapeDtypeStruct((n, ATTN_W), BF16),
            jax.ShapeDtypeStruct((b, KV_HEADS, s, LANES), FP8),
            jax.ShapeDtypeStruct((b, KV_HEADS, LANES, s), FP8),
            wide, wide, wide, wide,
        ],
        compiler_params=_params(("parallel",)),
        name="inproj",
    )(x2, mod_l, pre_g, w_in_b, ca, sa, cr, sr, qg, kg, v_scale)


def _attn_kernel(q_ref, kd_ref, vt_ref, fac_ref, o_ref, q_s, qn_s, m_ref, acc_ref, s0_ref, s1_ref, *, tq, tk, tiles):
    s = kd_ref.shape[2]
    nk = s // tk
    nq = s // tq
    pairs = (Q_HEADS // KV_HEADS) // 2
    top = lax.broadcasted_iota(I32, (LANES, tq), 0) < HEAD_DIM
    unscale = jnp.concatenate([fac_ref[0, 0:1, :]] * (2 * tq // LANES), axis=1)
    gain = fac_ref[0, 1:2, 0:1]

    chains = tiles * pairs

    def load_q(trip, dst):
        for u in range(tiles):
            rows = pl.ds(pl.multiple_of((trip * tiles + u) * tq, tq), tq)
            for p in range(pairs):
                qt = q_ref[rows, p * LANES:(p + 1) * LANES].astype(F32).T
                dst[u * pairs + p, :, :tq] = jnp.where(top, qt, 0.0).astype(FP8)
                dst[u * pairs + p, :, tq:] = jnp.where(top, 0.0, qt).astype(FP8)

    def keys(j):
        return kd_ref[0, 0, j * tk:(j + 1) * tk, :]

    def step(j, cur_ref, nxt_ref, q_next, j_next):
        vt = vt_ref[0, 0, :, j * tk:(j + 1) * tk]
        kd = keys(j_next)
        nxt_ref[0] = jnp.dot(kd, q_next[0], preferred_element_type=F32)
        for c in range(chains):
            sc = cur_ref[c]
            m_old = m_ref[c]
            m_new = jnp.maximum(m_old, jnp.max(sc, axis=0, keepdims=True))
            pt = jnp.exp2((sc - (m_new - P_SHIFT / gain)) * gain)
            alpha = jnp.exp2((m_old - m_new) * gain)
            if c + 1 < chains:
                nxt_ref[c + 1] = jnp.dot(kd, q_next[c + 1], preferred_element_type=F32)
            acc_ref[c] = alpha * acc_ref[c] + jnp.dot(vt, pt.astype(FP8), preferred_element_type=F32)
            m_ref[c] = m_new

    bufs = (s0_ref, s1_ref)
    trips = nq // tiles
    load_q(0, q_s)
    for c in range(chains):
        s0_ref[c] = jnp.dot(keys(0), q_s[c], preferred_element_type=F32)

    def body(trip, carry):
        load_q(trip, q_s)
        load_q(jnp.minimum(trip + 1, trips - 1), qn_s)
        m_ref[...] = jnp.full(m_ref.shape, -jnp.inf, F32)
        acc_ref[...] = jnp.zeros(acc_ref.shape, F32)
        for j in range(nk):
            last = j + 1 == nk
            step(j, bufs[j % len(bufs)], bufs[(j + 1) % len(bufs)], qn_s if last else q_s, 0 if last else j + 1)
        for c in range(chains):
            u, p = divmod(c, pairs)
            rows = pl.ds(pl.multiple_of((trip * tiles + u) * tq, tq), tq)
            acc = acc_ref[c]
            o_t = acc[:HEAD_DIM] / acc[HEAD_DIM:] * unscale
            both = jnp.concatenate([o_t[:, :tq], o_t[:, tq:]], axis=0)
            o_ref[rows, p * LANES:(p + 1) * LANES] = both.T.astype(BF16)
        return carry

    lax.fori_loop(0, trips, body, 0)


def _attention(qa, kd, vt, fac, batch0, b, s):
    n = qa.shape[0]
    tq = Q_TILE
    tk = min(KV_TILE, s // 2)
    tiles = Q_TILES_PER_TRIP
    assert tq == LANES and s % (tiles * tq) == 0 and s % (2 * tk) == 0 and tk % LANES == 0, (s, tq, tk)
    gw = ATTN_W // KV_HEADS
    chains = tiles * ((Q_HEADS // KV_HEADS) // 2)
    return pl.pallas_call(
        functools.partial(_attn_kernel, tq=tq, tk=tk, tiles=tiles),
        grid=(b, KV_HEADS),
        in_specs=[
            pl.BlockSpec((s, gw), lambda bi, h: (bi, h)),
            pl.BlockSpec((1, 1, s, LANES), lambda bi, h: (bi, h, 0, 0)),
            pl.BlockSpec((1, 1, LANES, s), lambda bi, h: (bi, h, 0, 0)),
            pl.BlockSpec((1, 2, LANES), lambda bi, h: (batch0 + bi, 0, 0)),
        ],
        out_specs=pl.BlockSpec((s, gw), lambda bi, h: (bi, h)),
        out_shape=jax.ShapeDtypeStruct((n, ATTN_W), BF16),
        scratch_shapes=[
            pltpu.VMEM((chains, LANES, 2 * tq), FP8),
            pltpu.VMEM((chains, LANES, 2 * tq), FP8),
            pltpu.VMEM((chains, 1, 2 * tq), F32),
            pltpu.VMEM((chains, LANES, 2 * tq), F32),
            pltpu.VMEM((chains, tk, 2 * tq), F32),
            pltpu.VMEM((chains, tk, 2 * tq), F32),
        ],
        compiler_params=_params(("parallel", "parallel")),
        name="attention",
    )(qa, kd, vt, fac)


def _retention_kernel(dec_ref, q_ref, k_ref, v_ref, g_ref, gng_ref, gnb_ref, o_ref,
                      fbuf, bbuf, sf_ref, sb_ref):
    h = pl.program_id(1)
    c = RET_CHUNK
    s = q_ref.shape[1]
    nc = s // c
    lgf = jnp.full((1, 1), dec_ref[0, h], F32)
    lgb = jnp.full((1, 1), dec_ref[1, h], F32)
    ii = lax.broadcasted_iota(I32, (c, c), 0)
    jj = lax.broadcasted_iota(I32, (c, c), 1)
    diff = (ii - jj).astype(F32)
    d_f = jnp.where(diff >= 0, jnp.exp(lgf * jnp.maximum(diff, 0.0)), 0.0)
    d_b = jnp.where(diff <= 0, jnp.exp(lgb * jnp.maximum(-diff, 0.0)), 0.0)
    idx = lax.broadcasted_iota(I32, (c, 1), 0).astype(F32)
    xi_f = jnp.exp(lgf * (idx + 1.0))
    zeta_f = jnp.exp(lgf * (c - 1.0 - idx))
    xi_b = jnp.exp(lgb * (c - idx))
    zeta_b = jnp.exp(lgb * idx)
    cd_f = jnp.exp(lgf * c)
    cd_b = jnp.exp(lgb * c)
    sf_ref[...] = jnp.zeros(sf_ref.shape, F32)
    sb_ref[...] = jnp.zeros(sb_ref.shape, F32)
    nt = (((1,), (1,)), ((), ()))

    fwd = (d_f, xi_f, zeta_f, cd_f, sf_ref, fbuf)
    bwd = (d_b, xi_b, zeta_b, cd_b, sb_ref, bbuf)

    def body(i, carry):
        jobs = [(fwd, 2 * i), (bwd, nc - 1 - 2 * i), (fwd, 2 * i + 1), (bwd, nc - 2 - 2 * i)]
        offs = [pl.multiple_of(ci * c, c) for _, ci in jobs]
        qs = [q_ref[0, pl.ds(off, c), :] for off in offs]
        ks = [k_ref[0, pl.ds(off, c), :] for off in offs]
        vs = [v_ref[0, pl.ds(off, c), :] for off in offs]
        scores = [lax.dot_general(q, k, nt, preferred_element_type=F32) for q, k in zip(qs, ks)]
        incs = []
        for (d, _), k, v in zip(jobs, ks, vs):
            kzt = (k.astype(F32) * d[2]).T.astype(BF16)
            incs.append(jnp.dot(kzt, v, preferred_element_type=F32))
        states = [fwd[4][...], bwd[4][...]]
        cross = [jnp.dot(qs[n], states[n].astype(BF16), preferred_element_type=F32) for n in range(2)]
        states += [states[n] * jobs[n][0][3] + incs[n] for n in range(2)]
        cross += [jnp.dot(qs[n], states[n].astype(BF16), preferred_element_type=F32) for n in range(2, 4)]
        for n, ((d, _), off) in enumerate(zip(jobs, offs)):
            a = (scores[n] * d[0]).astype(BF16)
            d[5][pl.ds(off, c), :] = jnp.dot(a, vs[n], preferred_element_type=F32) + cross[n] * d[1]
        for n in range(2):
            jobs[n][0][4][...] = states[2 + n] * jobs[n][0][3] + incs[2 + n]
        return carry

    lax.fori_loop(0, nc // 2, body, 0)

    rows = min(512, s)

    def fin(t, carry):
        off = pl.multiple_of(t * rows, rows)
        r = fbuf[pl.ds(off, rows), :] + bbuf[pl.ds(off, rows), :]
        mu = jnp.mean(r, axis=-1, keepdims=True)
        d = r - mu
        var = jnp.mean(d * d, axis=-1, keepdims=True)
        y = d * lax.rsqrt(var + NORM_EPS) * gng_ref[...] + gnb_ref[...]
        o_ref[0, pl.ds(off, rows), :] = (g_ref[0, pl.ds(off, rows), :].astype(F32) * y).astype(BF16)
        return carry

    lax.fori_loop(0, s // rows, fin, 0)


def _retention(qr, kr, vr, gr, dec, gn_g, gn_b, b, s):
    shp = (b, s, RET_W)
    assert s % (2 * RET_CHUNK) == 0, s
    head = pl.BlockSpec((1, s, RET_DIM), lambda bi, h: (bi, 0, h))
    vec = pl.BlockSpec((1, RET_DIM), lambda bi, h: (0, h))
    out = pl.pallas_call(
        _retention_kernel,
        grid=(b, RET_HEADS),
        in_specs=[pl.BlockSpec(memory_space=pltpu.SMEM), head, head, head, head, vec, vec],
        out_specs=head,
        out_shape=jax.ShapeDtypeStruct(shp, BF16),
        scratch_shapes=[
            pltpu.VMEM((s, RET_DIM), F32),
            pltpu.VMEM((s, RET_DIM), F32),
            pltpu.VMEM((RET_DIM, RET_DIM), F32),
            pltpu.VMEM((RET_DIM, RET_DIM), F32),
        ],
        compiler_params=_params(("parallel", "parallel")),
        name="retention",
    )(dec, qr.reshape(shp), kr.reshape(shp), vr.reshape(shp), gr.reshape(shp), gn_g, gn_b)
    return out.reshape(b * s, RET_W)


def _outproj_kernel(oa_ref, or_ref, w_ref, x_ref, mod_ref, pg_ref, fg_ref, rw_ref, rb_ref,
                    xn_ref, hp_ref, ti_ref, tw_ref, rk_ref, cnt_ref, base_ref):
    @pl.when(pl.program_id(0) == 0)
    def _():
        base_ref[...] = jnp.zeros(base_ref.shape, F32)

    m = mod_ref[0]
    y = jnp.dot(oa_ref[...], w_ref[:ATTN_W, :], preferred_element_type=F32)
    y = y + jnp.dot(or_ref[...], w_ref[ATTN_W:, :], preferred_element_type=F32)
    xn = x_ref[...] + m[2:3, :] * _rms(y, pg_ref[...])
    xn_ref[...] = xn
    h = _rms(xn, fg_ref[...]) * (1.0 + m[4:5, :]) + m[3:4, :]
    hb = h.astype(BF16)
    u = lax.bitcast_convert_type(hb.astype(F32), U32)
    hp_ref[...] = (u[:, :HALF] >> 16) | u[:, HALF:]

    tm = hb.shape[0]
    nt = (((1,), (1,)), ((), ()))
    logits = lax.dot_general(rw_ref[...], hb, nt, preferred_element_type=F32) + rb_ref[...]
    e_iota = lax.broadcasted_iota(I32, (N_EXPERTS, tm), 0).astype(F32)
    vals, idxs = [], []
    cur = logits
    for _ in range(TOP_K):
        mx = jnp.max(cur, axis=0, keepdims=True)
        ik = jnp.min(jnp.where(cur == mx, e_iota, float(N_EXPERTS)), axis=0, keepdims=True)
        vals.append(mx)
        idxs.append(ik)
        cur = jnp.where(e_iota == ik, -jnp.inf, cur)
    v = jnp.concatenate(vals, axis=0)
    w = jnp.exp(v - v[0:1, :])
    tw_ref[...] = w / jnp.sum(w, axis=0, keepdims=True)
    ti_ref[...] = jnp.concatenate(idxs, axis=0).astype(I32)

    onehot = jnp.zeros((N_EXPERTS, tm), F32)
    for ik in idxs:
        onehot = onehot + (e_iota == ik).astype(F32)
    before = (lax.broadcasted_iota(I32, (tm, tm), 0) < lax.broadcasted_iota(I32, (tm, tm), 1)).astype(BF16)
    seen = jnp.dot(onehot.astype(BF16), before, preferred_element_type=F32) + base_ref[...]
    ranks = [jnp.sum(jnp.where(e_iota == ik, seen, 0.0), axis=0, keepdims=True) for ik in idxs]
    rk_ref[...] = jnp.concatenate(ranks, axis=0).astype(I32)
    total = base_ref[...] + jnp.sum(onehot, axis=1, keepdims=True)
    base_ref[...] = total
    cnt_ref[...] = jnp.broadcast_to(total, cnt_ref.shape).astype(I32)


def _outproj(oa, orr, w_out_b, x2, batch0, x_batch0, mod_l, post_g, ffn_g, rw_t, rb, b, s):
    n = b * s
    tm = min(ROW_TILE, s)
    nb = s // tm
    row = lambda i: (i, 0)
    full = lambda i: (0, 0)
    col = lambda i: (0, i)
    return pl.pallas_call(
        _outproj_kernel,
        grid=(n // tm,),
        in_specs=[
            pl.BlockSpec((tm, ATTN_W), row),
            pl.BlockSpec((tm, RET_W), row),
            pl.BlockSpec((D_MODEL, D_MODEL), full),
            pl.BlockSpec((tm, D_MODEL), lambda i: (x_batch0 * nb + i, 0)),
            pl.BlockSpec((1, 6, D_MODEL), lambda i: (batch0 + i // nb, 0, 0)),
            pl.BlockSpec((1, D_MODEL), full),
            pl.BlockSpec((1, D_MODEL), full),
            pl.BlockSpec((N_EXPERTS, D_MODEL), full),
            pl.BlockSpec((N_EXPERTS, 1), full),
        ],
        out_specs=[
            pl.BlockSpec((tm, D_MODEL), row),
            pl.BlockSpec((tm, HALF), row),
            pl.BlockSpec((TOP_K, tm), col),
            pl.BlockSpec((TOP_K, tm), col),
            pl.BlockSpec((TOP_K, tm), col),
            pl.BlockSpec((N_EXPERTS, LANES), full),
        ],
        out_shape=[
            jax.ShapeDtypeStruct((n, D_MODEL), F32),
            jax.ShapeDtypeStruct((n, HALF), U32),
            jax.ShapeDtypeStruct((TOP_K, n), I32),
            jax.ShapeDtypeStruct((TOP_K, n), F32),
            jax.ShapeDtypeStruct((TOP_K, n), I32),
            jax.ShapeDtypeStruct((N_EXPERTS, LANES), I32),
        ],
        scratch_shapes=[pltpu.VMEM((N_EXPERTS, 1), F32)],
        compiler_params=_params(("arbitrary",)),
        name="outproj_router",
    )(oa, orr, w_out_b, x2, mod_l, post_g, ffn_g, rw_t, rb)


def _sc_scatter_rows(rows, pos3, n_out):
    n, w = rows.shape
    nchunk, kk, c = pos3.shape
    per_w = nchunk // SC_WORKERS
    mesh = plsc.VectorSubcoreMesh(core_axis_name="c", subcore_axis_name="s")

    @functools.partial(
        pl.kernel, mesh=mesh,
        out_type=jax.ShapeDtypeStruct((n_out, w), rows.dtype),
        scratch_types=[pltpu.VMEM((kk, c), I32), pltpu.VMEM((c, w), rows.dtype)],
    )
    def k(rows_hbm, pos_hbm, out_hbm, idx_v, rows_v):
        wid = lax.axis_index("s") * SC_CORES + lax.axis_index("c")

        @pl.loop(0, per_w)
        def _(i):
            ch = wid * per_w + i
            pltpu.sync_copy(pos_hbm.at[ch], idx_v)
            pltpu.sync_copy(rows_hbm.at[pl.ds(ch * c, c)], rows_v)
            for j in range(kk):
                pltpu.sync_copy(rows_v, out_hbm.at[idx_v.at[j]])

    return k(rows, pos3)


def _sc_gather_rows(table, idx):
    w = table.shape[1]
    b = idx.shape[0]
    c = SC_CHUNK
    per_w = b // (SC_WORKERS * c)
    mesh = plsc.VectorSubcoreMesh(core_axis_name="c", subcore_axis_name="s")

    @functools.partial(
        pl.kernel, mesh=mesh,
        out_type=jax.ShapeDtypeStruct((b, w), table.dtype),
        scratch_types=[pltpu.VMEM((c,), I32), pltpu.VMEM((c, w), table.dtype), pltpu.SemaphoreType.DMA],
    )
    def k(table_hbm, idx_hbm, out_hbm, idx_v, rows_v, sem):
        wid = lax.axis_index("s") * SC_CORES + lax.axis_index("c")

        @pl.loop(0, per_w)
        def _(i):
            base = (wid * per_w + i) * c
            pltpu.sync_copy(idx_hbm.at[pl.ds(base, c)], idx_v)
            pltpu.async_copy(table_hbm.at[idx_v], rows_v, sem).wait()
            pltpu.sync_copy(rows_v, out_hbm.at[pl.ds(base, c)])

    return k(table, idx)


def _experts_kernel(te_ref, nv_ref, xs_ref, wu_ref, bu_ref, wd_ref, bd_ref, ys_ref, wu_b, wd_b):
    i = pl.program_id(0)
    live = i < nv_ref[0]
    new_expert = jnp.logical_or(i == 0, te_ref[i] != te_ref[jnp.maximum(i - 1, 0)])

    @pl.when(jnp.logical_and(live, new_expert))
    def _():
        wu_b[...] = wu_ref[0, 0].astype(BF16)
        wd_b[...] = wd_ref[0, 0].astype(BF16)

    @pl.when(live)
    def _():
        w = xs_ref[...]
        lo = _unpack_lo(w).astype(BF16)
        hi = _unpack_hi(w).astype(BF16)
        up = jnp.dot(lo, wu_b[:HALF, :], preferred_element_type=F32)
        up = up + jnp.dot(hi, wu_b[HALF:, :], preferred_element_type=F32) + bu_ref[0, 0]
        glu = jnp.minimum(up[:, :D_FF], SWIGLU_LIMIT)
        lin = jnp.clip(up[:, D_FF:], -SWIGLU_LIMIT, SWIGLU_LIMIT)
        act = glu * jax.nn.sigmoid(SWIGLU_ALPHA * glu) * (lin + 1.0)
        y = jnp.dot(act.astype(BF16), wd_b[...], preferred_element_type=F32) + bd_ref[0, 0]
        ys_ref[...] = _pack_bf16_pairs(y)


def _experts(xs, tile_expert, n_valid, layer, wu, bu, wd, bd):
    n_pad = xs.shape[0]
    t = MOE_TILE
    grid_spec = pltpu.PrefetchScalarGridSpec(
        num_scalar_prefetch=2,
        grid=(n_pad // t,),
        in_specs=[
            pl.BlockSpec((t, HALF), lambda i, te, nv: (i, 0)),
            pl.BlockSpec((1, 1, D_MODEL, 2 * D_FF), lambda i, te, nv: (layer, te[i], 0, 0)),
            pl.BlockSpec((1, 1, 1, 2 * D_FF), lambda i, te, nv: (layer, te[i], 0, 0)),
            pl.BlockSpec((1, 1, D_FF, D_MODEL), lambda i, te, nv: (layer, te[i], 0, 0)),
            pl.BlockSpec((1, 1, 1, D_MODEL), lambda i, te, nv: (layer, te[i], 0, 0)),
        ],
        out_specs=pl.BlockSpec((t, HALF), lambda i, te, nv: (i, 0)),
        scratch_shapes=[pltpu.VMEM((D_MODEL, 2 * D_FF), BF16), pltpu.VMEM((D_FF, D_MODEL), BF16)],
    )
    return pl.pallas_call(
        _experts_kernel,
        grid_spec=grid_spec,
        out_shape=jax.ShapeDtypeStruct((n_pad, HALF), U32),
        compiler_params=pltpu.CompilerParams(dimension_semantics=("arbitrary",), vmem_limit_bytes=EXPERTS_VMEM_LIMIT),
        name="experts",
    )(tile_expert, n_valid, xs, wu, bu, wd, bd)


def _combine_kernel(y0_ref, y1_ref, y2_ref, y3_ref, w_ref, x_ref, mod_ref, g_ref, *rest):
    o_ref = rest[-1]
    w = w_ref[...]
    lo = None
    hi = None
    for k, y_ref in enumerate((y0_ref, y1_ref, y2_ref, y3_ref)):
        word = y_ref[...]
        wk = w[:, k:k + 1]
        lo_k = wk * _unpack_lo(word)
        hi_k = wk * _unpack_hi(word)
        lo = lo_k if lo is None else lo + lo_k
        hi = hi_k if hi is None else hi + hi_k
    yf = jnp.concatenate([lo, hi], axis=1)
    m = mod_ref[0]
    o_ref[...] = x_ref[...] + m[5:6, :] * _rms(yf, g_ref[...])


def _combine(yg, tw_t, xn, batch0, mod_l, post_g, b, s, out_batches=None, shared_out=None):
    n = b * s
    tm = min(ROW_TILE, s)
    nb = s // tm
    nblk = n // tm
    row = lambda i: (i, 0)
    full = lambda i: (0, 0)
    y_specs = [pl.BlockSpec((tm, HALF), functools.partial(lambda i, k: (k * nblk + i, 0), k=k)) for k in range(TOP_K)]
    out_rows, out_blk0 = (n, 0) if out_batches is None else (out_batches * s, batch0 * nb)
    extra_specs, extra_args, aliases = [], [], {}
    if shared_out is not None:
        extra_specs, extra_args, aliases = [pl.BlockSpec(memory_space=pl.ANY)], [shared_out], {8: 0}
    return pl.pallas_call(
        _combine_kernel,
        grid=(nblk,),
        in_specs=y_specs + [
            pl.BlockSpec((tm, TOP_K), row),
            pl.BlockSpec((tm, D_MODEL), row),
            pl.BlockSpec((1, 6, D_MODEL), lambda i: (batch0 + i // nb, 0, 0)),
            pl.BlockSpec((1, D_MODEL), full),
        ] + extra_specs,
        out_specs=pl.BlockSpec((tm, D_MODEL), lambda i: (out_blk0 + i, 0)),
        out_shape=jax.ShapeDtypeStruct((out_rows, D_MODEL), F32),
        input_output_aliases=aliases,
        compiler_params=_params(("parallel",)),
        name="combine",
    )(yg, yg, yg, yg, tw_t, xn, mod_l, post_g, *extra_args)


def _route(counts, tidx, rank, n_tiles):
    t = MOE_TILE
    cnt = counts[:, 0]
    padded = ((cnt + t - 1) // t) * t
    ends = jnp.cumsum(padded)
    starts = ends - padded
    experts = jnp.arange(N_EXPERTS, dtype=I32)[:, None, None]
    start_of = jnp.sum(jnp.where(tidx[None] == experts, starts[:, None, None], 0), axis=0)
    pos = start_of + rank
    tile_ends = ends // t
    n_valid = tile_ends[-1]
    tile = jnp.minimum(jnp.arange(n_tiles, dtype=I32), n_valid - 1)
    tile_expert = jnp.minimum(jnp.sum(tile_ends[None, :] <= tile[:, None], axis=1), N_EXPERTS - 1).astype(I32)
    return pos.astype(I32), tile_expert, n_valid.reshape(1).astype(I32)


def _attention_ranges(mod_l, pre_g, w_in_l, q_gain, k_gain):
    d = pre_g.shape[0]
    gq = jnp.max(jnp.abs(q_gain))
    gk = jnp.max(jnp.abs(k_gain))
    gq = jnp.where(gq > 0, gq, 1.0)
    gk = jnp.where(gk > 0, gk, 1.0)
    reps = LANES // HEAD_DIM
    qg = jnp.tile(q_gain / gq, reps).reshape(1, LANES)
    kg = jnp.tile(k_gain / gk, reps).reshape(1, LANES)
    shift, scale = mod_l[:, 0, :], mod_l[:, 1, :]
    h_norm = d ** 0.5 * jnp.max(jnp.abs(pre_g[None, :] * (1.0 + scale)), axis=1) + jnp.linalg.norm(shift, axis=1)
    w_v = w_in_l[:, ATTN_W + KV_W:ATTN_W + 2 * KV_W]
    bound = 1.02 * h_norm * jnp.max(jnp.linalg.norm(w_v, axis=0))
    v_scale = V_RANGE / jnp.maximum(bound, 1e-30)
    nb = mod_l.shape[0]
    lanes = lambda t: jnp.broadcast_to(t[:, None], (nb, LANES))
    fac = jnp.stack([lanes(1.0 / v_scale), jnp.full((nb, LANES), gq * gk, F32)], axis=1)
    return qg, kg, lanes(v_scale)[:, None, :], fac


def kernel(x, c, ada_w, ada_b, pre_mix_g, post_mix_g, w_in, q_norm_g, k_norm_g, ret_decay_fwd, ret_decay_bwd, ret_gn_g, ret_gn_b, w_out, pre_ffn_g, post_ffn_g, router_w, router_b, exp_w_up, exp_b_up, exp_w_down, exp_b_down):
    b, s, d = x.shape
    depth = ada_w.shape[0]
    groups = 2 if b % 2 == 0 else 1
    bg = b // groups
    n = bg * s
    n_asg = n * TOP_K
    n_pad = n_asg + N_EXPERTS * MOE_TILE
    n_tiles = n_pad // MOE_TILE

    mod = _modulation(c, ada_w, ada_b)
    ropes = _rope_tables(s, HEAD_DIM) + _rope_tables(s, RET_DIM)
    bu = exp_b_up.reshape(depth, N_EXPERTS, 1, 2 * D_FF)
    bd = exp_b_down.reshape(depth, N_EXPERTS, 1, d)
    xs_in = [x.reshape(b * s, d)] * groups
    x_b0 = [g * bg for g in range(groups)]
    out = None
    for l in range(depth):
        mod_l = mod[l].reshape(b, 6, d)
        qg, kg, v_scale, fac = _attention_ranges(mod_l, pre_mix_g[l], w_in[l], q_norm_g[l], k_norm_g[l])
        w_in_b = w_in[l].astype(BF16)
        w_out_b = w_out[l].astype(BF16)
        rw_t = router_w[l].T.astype(BF16)
        dec = jnp.stack([ret_decay_fwd[l], ret_decay_bwd[l]]).astype(F32)
        last = l + 1 == depth
        proj = [_inproj(xs_in[g], g * bg, x_b0[g], mod_l, pre_mix_g[l].reshape(1, d), w_in_b, ropes, qg, kg, v_scale,
                        bg, s) for g in range(groups)]
        oa = [_attention(p[0], p[1], p[2], fac, g * bg, bg, s) for g, p in enumerate(proj)]
        orr = [_retention(p[3], p[4], p[5], p[6], dec, ret_gn_g[l].reshape(1, RET_W), ret_gn_b[l].reshape(1, RET_W),
                          bg, s) for p in proj]
        routed = []
        for g in range(groups):
            xn, hp, tidx, tw, rank, counts = _outproj(
                oa[g], orr[g], w_out_b, xs_in[g], g * bg, x_b0[g], mod_l, post_mix_g[l].reshape(1, d),
                pre_ffn_g[l].reshape(1, d), rw_t, router_b[l].reshape(N_EXPERTS, 1), bg, s)
            pos, tile_expert, n_valid = _route(counts, tidx, rank, n_tiles)
            pos3 = pos.reshape(TOP_K, n // SC_CHUNK, SC_CHUNK).transpose(1, 0, 2)
            routed.append((xn, tw, pos, tile_expert, n_valid, _sc_scatter_rows(hp, pos3, n_pad)))
        yg = []
        for xn, tw, pos, tile_expert, n_valid, xs in routed:
            ys = _experts(xs, tile_expert, n_valid, l, exp_w_up, bu, exp_w_down, bd)
            yg.append(_sc_gather_rows(ys, pos.reshape(n_asg)))
        nxt = []
        for g, (xn, tw, *_) in enumerate(routed):
            if last:
                out = _combine(yg[g], tw.T, xn, g * bg, mod_l, post_ffn_g[l].reshape(1, d), bg, s,
                               out_batches=b, shared_out=out)
            else:
                nxt.append(_combine(yg[g], tw.T, xn, g * bg, mod_l, post_ffn_g[l].reshape(1, d), bg, s))
        xs_in, x_b0 = nxt, [0] * groups
    return out.reshape(b, s, d)
```

```python
import functools

import numpy as np
import jax
import jax.numpy as jnp
from jax import lax
from jax.experimental import pallas as pl
from jax.experimental.pallas import tpu as pltpu
from jax.experimental.pallas import tpu_sc as plsc

F32 = jnp.float32
BF16 = jnp.bfloat16
U32 = jnp.uint32
I32 = jnp.int32
FP8 = jnp.float8_e4m3fn

D_MODEL = 1024
GRID_W = 64
ROPE_THETA = 10000.0
HEAD_DIM = 64
Q_HEADS = 8
KV_HEADS = 2
RET_HEADS = 4
RET_DIM = 128
RET_CHUNK = 128
ATTN_W = Q_HEADS * HEAD_DIM
KV_W = KV_HEADS * HEAD_DIM
RET_W = RET_HEADS * RET_DIM
IN_W = ATTN_W + 2 * KV_W + 4 * RET_W
N_EXPERTS = 32
TOP_K = 4
D_FF = D_MODEL
SWIGLU_LIMIT = 7.0
SWIGLU_ALPHA = 1.702
NORM_EPS = 1e-6
Q_SCALE = HEAD_DIM ** -0.5 * float(np.log2(np.e))
V_RANGE = 224.0
P_SHIFT = 8.0
HALF = D_MODEL // 2
LANES = 128

ROW_TILE = 512
Q_TILE = 128
Q_TILES_PER_TRIP = 2
KV_TILE = 1024
MOE_TILE = 512
VMEM_LIMIT = 48 * 1024 * 1024
EXPERTS_VMEM_LIMIT = 56 * 1024 * 1024

SC_CORES = 2
SC_SUBCORES = 16
SC_WORKERS = SC_CORES * SC_SUBCORES
SC_CHUNK = 64


def _params(sem):
    return pltpu.CompilerParams(dimension_semantics=sem, vmem_limit_bytes=VMEM_LIMIT)


def _rms(x, g):
    return x * lax.rsqrt(jnp.mean(x * x, axis=-1, keepdims=True) + NORM_EPS) * g


def _pack_bf16_pairs(y):
    u = lax.bitcast_convert_type(y.astype(BF16).astype(F32), U32)
    return (u[:, :HALF] >> 16) | u[:, HALF:]


def _unpack_lo(w):
    return lax.bitcast_convert_type(w << 16, F32)


def _unpack_hi(w):
    return lax.bitcast_convert_type(w & jnp.uint32(0xFFFF0000), F32)


def _mod_kernel(c_ref, w_ref, b_ref, o_ref):
    c = c_ref[...]
    cond = c * jax.nn.sigmoid(c)
    o_ref[0] = jnp.dot(cond, w_ref[0], preferred_element_type=F32, precision=lax.Precision.HIGHEST) + b_ref[0]


def _modulation(c, ada_w, ada_b):
    depth, d, w6 = ada_w.shape
    b = c.shape[0]
    tn = 1536
    return pl.pallas_call(
        _mod_kernel,
        grid=(depth, w6 // tn),
        in_specs=[
            pl.BlockSpec((b, d), lambda l, j: (0, 0)),
            pl.BlockSpec((1, d, tn), lambda l, j: (l, 0, j)),
            pl.BlockSpec((1, 1, tn), lambda l, j: (l, 0, j)),
        ],
        out_specs=pl.BlockSpec((1, b, tn), lambda l, j: (l, 0, j)),
        out_shape=jax.ShapeDtypeStruct((depth, b, w6), F32),
        compiler_params=_params(("arbitrary", "arbitrary")),
        name="modulation",
    )(c, ada_w, ada_b.reshape(depth, 1, w6))


def _rope_tables(s, head_dim):
    quarter = head_dim // 4
    t = jnp.arange(s, dtype=F32)
    row = jnp.floor(t / GRID_W)
    col = t - row * GRID_W
    inv_freq = ROPE_THETA ** (-jnp.arange(quarter, dtype=F32) / quarter)
    lane = np.arange(LANES) % head_dim
    use_col = lane >= head_dim // 2
    second = (lane % (head_dim // 2)) >= quarter
    freq = inv_freq[lane % quarter]
    pos = jnp.where(use_col[None, :], col[:, None], row[:, None])
    ang = pos * freq[None, :]
    sign = jnp.where(second, 1.0, -1.0).astype(F32)
    return jnp.cos(ang), jnp.sin(ang) * sign[None, :]


def _rope(z, cos, sin_signed, quarter):
    lane = lax.broadcasted_iota(I32, z.shape, 1)
    first = (lane % (2 * quarter)) < quarter
    partner = jnp.where(first, pltpu.roll(z, LANES - quarter, 1), pltpu.roll(z, quarter, 1))
    return z * cos + partner * sin_signed


def _inproj_kernel(x_ref, mod_ref, g_ref, w_ref, ca_ref, sa_ref, cr_ref, sr_ref, qg_ref, kg_ref, vs_ref,
                   qa_ref, kd_ref, vt_ref, qr_ref, kr_ref, vr_ref, gr_ref):
    x = x_ref[...]
    m = mod_ref[0]
    h = _rms(x, g_ref[...]) * (1.0 + m[1:2, :]) + m[0:1, :]
    hb = h.astype(BF16)
    tm = x.shape[0]

    def proj(c0):
        z = jnp.dot(hb, w_ref[:, c0:c0 + 2 * LANES], preferred_element_type=F32)
        return z[:, :LANES], z[:, LANES:]

    ri = lax.broadcasted_iota(I32, (LANES, LANES), 0) // HEAD_DIM
    ci = lax.broadcasted_iota(I32, (LANES, LANES), 1) // HEAD_DIM
    head_ones = (ri == ci).astype(BF16)
    ca, sa = ca_ref[...], sa_ref[...]
    cr, sr = cr_ref[...], sr_ref[...]
    lane = lax.broadcasted_iota(I32, (tm, LANES), 1)
    low = lane < HEAD_DIM

    def head_norm_rope(z, g):
        ss = jnp.dot((z * z).astype(BF16), head_ones, preferred_element_type=F32)
        zn = z * lax.rsqrt(ss * (1.0 / HEAD_DIM) + NORM_EPS) * g
        return _rope(zn, ca, sa, HEAD_DIM // 4)

    def store_pair(ref, j, fn, c0):
        for i, z in enumerate(proj(c0 + 2 * j * LANES)):
            ref[:, (2 * j + i) * LANES:(2 * j + i + 1) * LANES] = fn(z).astype(BF16)

    for j in range(ATTN_W // (2 * LANES)):
        store_pair(qa_ref, j, lambda z: head_norm_rope(z, qg_ref[...]) * Q_SCALE, 0)

    k, v = proj(ATTN_W)
    k = head_norm_rope(k, kg_ref[...])
    k_sw = pltpu.roll(k, HEAD_DIM, 1)
    v = v * vs_ref[0]
    kd_ref[0, 0] = jnp.where(low, k, k_sw).astype(BF16)
    kd_ref[0, 1] = jnp.where(low, k_sw, k).astype(BF16)
    vt_ref[0, 0] = jnp.where(low, v, 1.0).T.astype(FP8)
    vt_ref[0, 1] = jnp.where(low, pltpu.roll(v, HEAD_DIM, 1), 1.0).T.astype(FP8)

    base = ATTN_W + 2 * KV_W
    for j in range(RET_W // (2 * LANES)):
        store_pair(qr_ref, j, lambda z: _rope(z, cr, sr, RET_DIM // 4), base)
        store_pair(kr_ref, j, lambda z: _rope(z, cr, sr, RET_DIM // 4) * (RET_DIM ** -0.5), base + RET_W)
        store_pair(vr_ref, j, lambda z: z, base + 2 * RET_W)
        store_pair(gr_ref, j, lambda z: z * jax.nn.sigmoid(z), base + 3 * RET_W)


def _inproj(x2, batch0, x_batch0, mod_l, pre_g, w_in_b, ropes, qg, kg, v_scale, b, s):
    n = b * s
    tm = min(ROW_TILE, s)
    nb = s // tm
    ca, sa, cr, sr = ropes
    row = lambda i: (i, 0)
    full = lambda i: (0, 0)
    rope_spec = pl.BlockSpec((tm, LANES), lambda i: (i % nb, 0))
    wide = jax.ShapeDtypeStruct((n, RET_W), BF16)
    return pl.pallas_call(
        _inproj_kernel,
        grid=(n // tm,),
        in_specs=[
            pl.BlockSpec((tm, D_MODEL), lambda i: (x_batch0 * nb + i, 0)),
            pl.BlockSpec((1, 6, D_MODEL), lambda i: (batch0 + i // nb, 0, 0)),
            pl.BlockSpec((1, D_MODEL), full),
            pl.BlockSpec((D_MODEL, IN_W), full),
            rope_spec, rope_spec, rope_spec, rope_spec,
            pl.BlockSpec((1, LANES), full),
            pl.BlockSpec((1, LANES), full),
            pl.BlockSpec((1, 1, LANES), lambda i: (batch0 + i // nb, 0, 0)),
        ],
        out_specs=[
            pl.BlockSpec((tm, ATTN_W), row),
            pl.BlockSpec((1, KV_HEADS, tm, LANES), lambda i: (i // nb, 0, i % nb, 0)),
            pl.BlockSpec((1, KV_HEADS, LANES, tm), lambda i: (i // nb, 0, 0, i % nb)),
            pl.BlockSpec((tm, RET_W), row),
            pl.BlockSpec((tm, RET_W), row),
            pl.BlockSpec((tm, RET_W), row),
            pl.BlockSpec((tm, RET_W), row),
        ],
        out_shape=[
            jax.ShapeDtypeStruct((n, ATTN_W), BF16),
            jax.ShapeDtypeStruct((b, KV_HEADS, s, LANES), BF16),
            jax.ShapeDtypeStruct((b, KV_HEADS, LANES, s), FP8),
            wide, wide, wide, wide,
        ],
        compiler_params=_params(("parallel",)),
        name="inproj",
    )(x2, mod_l, pre_g, w_in_b, ca, sa, cr, sr, qg, kg, v_scale)


def _attn_kernel(q_ref, kd_ref, vt_ref, fac_ref, o_ref, q_s, qn_s, m_ref, acc_ref, s0_ref, s1_ref,
                 *, tq, tk, tiles):
    s = kd_ref.shape[2]
    nk = s // tk
    nq = s // tq
    pairs = (Q_HEADS // KV_HEADS) // 2
    top = lax.broadcasted_iota(I32, (LANES, tq), 0) < HEAD_DIM
    unscale = jnp.concatenate([fac_ref[0]] * (2 * tq // LANES), axis=1)

    chains = tiles * pairs

    def load_q(trip, dst):
        for u in range(tiles):
            rows = pl.ds(pl.multiple_of((trip * tiles + u) * tq, tq), tq)
            for p in range(pairs):
                qt = q_ref[rows, p * LANES:(p + 1) * LANES].astype(F32).T
                dst[u * pairs + p, :, :tq] = jnp.where(top, qt, 0.0).astype(BF16)
                dst[u * pairs + p, :, tq:] = jnp.where(top, 0.0, qt).astype(BF16)

    def keys(j):
        return kd_ref[0, 0, j * tk:(j + 1) * tk, :]

    def step(j, cur_ref, nxt_ref, q_next, j_next):
        vt = vt_ref[0, 0, :, j * tk:(j + 1) * tk]
        kd = keys(j_next)
        nxt_ref[0] = jnp.dot(kd, q_next[0], preferred_element_type=F32)
        for c in range(chains):
            sc = cur_ref[c]
            m_old = m_ref[c]
            m_new = jnp.maximum(m_old, jnp.max(sc, axis=0, keepdims=True))
            pt = jnp.exp2(sc - (m_new - P_SHIFT))
            alpha = jnp.exp2(m_old - m_new)
            if c + 1 < chains:
                nxt_ref[c + 1] = jnp.dot(kd, q_next[c + 1], preferred_element_type=F32)
            acc_ref[c] = alpha * acc_ref[c] + jnp.dot(vt, pt.astype(FP8), preferred_element_type=F32)
            m_ref[c] = m_new

    bufs = (s0_ref, s1_ref)
    trips = nq // tiles
    load_q(0, q_s)
    for c in range(chains):
        s0_ref[c] = jnp.dot(keys(0), q_s[c], preferred_element_type=F32)

    def body(trip, carry):
        load_q(trip, q_s)
        load_q(jnp.minimum(trip + 1, trips - 1), qn_s)
        m_ref[...] = jnp.full(m_ref.shape, -jnp.inf, F32)
        acc_ref[...] = jnp.zeros(acc_ref.shape, F32)
        for j in range(nk):
            last = j + 1 == nk
            step(j, bufs[j % len(bufs)], bufs[(j + 1) % len(bufs)], qn_s if last else q_s, 0 if last else j + 1)
        for c in range(chains):
            u, p = divmod(c, pairs)
            rows = pl.ds(pl.multiple_of((trip * tiles + u) * tq, tq), tq)
            acc = acc_ref[c]
            o_t = acc[:HEAD_DIM] / acc[HEAD_DIM:] * unscale
            both = jnp.concatenate([o_t[:, :tq], o_t[:, tq:]], axis=0)
            o_ref[rows, p * LANES:(p + 1) * LANES] = both.T.astype(BF16)
        return carry

    lax.fori_loop(0, trips, body, 0)


def _attention(qa, kd, vt, fac, batch0, b, s):
    n = qa.shape[0]
    tq = Q_TILE
    tk = min(KV_TILE, s // 2)
    tiles = Q_TILES_PER_TRIP
    assert tq == LANES and s % (tiles * tq) == 0 and s % (2 * tk) == 0 and tk % LANES == 0, (s, tq, tk)
    gw = ATTN_W // KV_HEADS
    chains = tiles * ((Q_HEADS // KV_HEADS) // 2)
    return pl.pallas_call(
        functools.partial(_attn_kernel, tq=tq, tk=tk, tiles=tiles),
        grid=(b, KV_HEADS),
        in_specs=[
            pl.BlockSpec((s, gw), lambda bi, h: (bi, h)),
            pl.BlockSpec((1, 1, s, LANES), lambda bi, h: (bi, h, 0, 0)),
            pl.BlockSpec((1, 1, LANES, s), lambda bi, h: (bi, h, 0, 0)),
            pl.BlockSpec((1, 1, LANES), lambda bi, h: (batch0 + bi, 0, 0)),
        ],
        out_specs=pl.BlockSpec((s, gw), lambda bi, h: (bi, h)),
        out_shape=jax.ShapeDtypeStruct((n, ATTN_W), BF16),
        scratch_shapes=[
            pltpu.VMEM((chains, LANES, 2 * tq), BF16),
            pltpu.VMEM((chains, LANES, 2 * tq), BF16),
            pltpu.VMEM((chains, 1, 2 * tq), F32),
            pltpu.VMEM((chains, LANES, 2 * tq), F32),
            pltpu.VMEM((chains, tk, 2 * tq), F32),
            pltpu.VMEM((chains, tk, 2 * tq), F32),
        ],
        compiler_params=_params(("parallel", "parallel")),
        name="attention",
    )(qa, kd, vt, fac)


def _retention_kernel(dec_ref, q_ref, k_ref, v_ref, g_ref, gng_ref, gnb_ref, o_ref,
                      fbuf, bbuf, sf_ref, sb_ref):
    h = pl.program_id(1)
    c = RET_CHUNK
    s = q_ref.shape[1]
    nc = s // c
    lgf = jnp.full((1, 1), dec_ref[0, h], F32)
    lgb = jnp.full((1, 1), dec_ref[1, h], F32)
    ii = lax.broadcasted_iota(I32, (c, c), 0)
    jj = lax.broadcasted_iota(I32, (c, c), 1)
    diff = (ii - jj).astype(F32)
    d_f = jnp.where(diff >= 0, jnp.exp(lgf * jnp.maximum(diff, 0.0)), 0.0)
    d_b = jnp.where(diff <= 0, jnp.exp(lgb * jnp.maximum(-diff, 0.0)), 0.0)
    idx = lax.broadcasted_iota(I32, (c, 1), 0).astype(F32)
    xi_f = jnp.exp(lgf * (idx + 1.0))
    zeta_f = jnp.exp(lgf * (c - 1.0 - idx))
    xi_b = jnp.exp(lgb * (c - idx))
    zeta_b = jnp.exp(lgb * idx)
    cd_f = jnp.exp(lgf * c)
    cd_b = jnp.exp(lgb * c)
    sf_ref[...] = jnp.zeros(sf_ref.shape, F32)
    sb_ref[...] = jnp.zeros(sb_ref.shape, F32)
    nt = (((1,), (1,)), ((), ()))

    fwd = (d_f, xi_f, zeta_f, cd_f, sf_ref, fbuf)
    bwd = (d_b, xi_b, zeta_b, cd_b, sb_ref, bbuf)

    def body(i, carry):
        jobs = [(fwd, 2 * i), (bwd, nc - 1 - 2 * i), (fwd, 2 * i + 1), (bwd, nc - 2 - 2 * i)]
        offs = [pl.multiple_of(ci * c, c) for _, ci in jobs]
        qs = [q_ref[0, pl.ds(off, c), :] for off in offs]
        ks = [k_ref[0, pl.ds(off, c), :] for off in offs]
        vs = [v_ref[0, pl.ds(off, c), :] for off in offs]
        scores = [lax.dot_general(q, k, nt, preferred_element_type=F32) for q, k in zip(qs, ks)]
        incs = []
        for (d, _), k, v in zip(jobs, ks, vs):
            kzt = (k.astype(F32) * d[2]).T.astype(BF16)
            incs.append(jnp.dot(kzt, v, preferred_element_type=F32))
        states = [fwd[4][...], bwd[4][...]]
        cross = [jnp.dot(qs[n], states[n].astype(BF16), preferred_element_type=F32) for n in range(2)]
        states += [states[n] * jobs[n][0][3] + incs[n] for n in range(2)]
        cross += [jnp.dot(qs[n], states[n].astype(BF16), preferred_element_type=F32) for n in range(2, 4)]
        for n, ((d, _), off) in enumerate(zip(jobs, offs)):
            a = (scores[n] * d[0]).astype(BF16)
            d[5][pl.ds(off, c), :] = jnp.dot(a, vs[n], preferred_element_type=F32) + cross[n] * d[1]
        for n in range(2):
            jobs[n][0][4][...] = states[2 + n] * jobs[n][0][3] + incs[2 + n]
        return carry

    lax.fori_loop(0, nc // 2, body, 0)

    rows = min(512, s)

    def fin(t, carry):
        off = pl.multiple_of(t * rows, rows)
        r = fbuf[pl.ds(off, rows), :] + bbuf[pl.ds(off, rows), :]
        mu = jnp.mean(r, axis=-1, keepdims=True)
        d = r - mu
        var = jnp.mean(d * d, axis=-1, keepdims=True)
        y = d * lax.rsqrt(var + NORM_EPS) * gng_ref[...] + gnb_ref[...]
        o_ref[0, pl.ds(off, rows), :] = (g_ref[0, pl.ds(off, rows), :].astype(F32) * y).astype(BF16)
        return carry

    lax.fori_loop(0, s // rows, fin, 0)


def _retention(qr, kr, vr, gr, dec, gn_g, gn_b, b, s):
    shp = (b, s, RET_W)
    assert s % (2 * RET_CHUNK) == 0, s
    head = pl.BlockSpec((1, s, RET_DIM), lambda bi, h: (bi, 0, h))
    vec = pl.BlockSpec((1, RET_DIM), lambda bi, h: (0, h))
    out = pl.pallas_call(
        _retention_kernel,
        grid=(b, RET_HEADS),
        in_specs=[pl.BlockSpec(memory_space=pltpu.SMEM), head, head, head, head, vec, vec],
        out_specs=head,
        out_shape=jax.ShapeDtypeStruct(shp, BF16),
        scratch_shapes=[
            pltpu.VMEM((s, RET_DIM), F32),
            pltpu.VMEM((s, RET_DIM), F32),
            pltpu.VMEM((RET_DIM, RET_DIM), F32),
            pltpu.VMEM((RET_DIM, RET_DIM), F32),
        ],
        compiler_params=_params(("parallel", "parallel")),
        name="retention",
    )(dec, qr.reshape(shp), kr.reshape(shp), vr.reshape(shp), gr.reshape(shp), gn_g, gn_b)
    return out.reshape(b * s, RET_W)


def _outproj_kernel(oa_ref, or_ref, w_ref, x_ref, mod_ref, pg_ref, fg_ref, rw_ref, rb_ref,
                    xn_ref, hp_ref, ti_ref, tw_ref, rk_ref, cnt_ref, base_ref):
    @pl.when(pl.program_id(0) == 0)
    def _():
        base_ref[...] = jnp.zeros(base_ref.shape, F32)

    m = mod_ref[0]
    y = jnp.dot(oa_ref[...], w_ref[:ATTN_W, :], preferred_element_type=F32)
    y = y + jnp.dot(or_ref[...], w_ref[ATTN_W:, :], preferred_element_type=F32)
    xn = x_ref[...] + m[2:3, :] * _rms(y, pg_ref[...])
    xn_ref[...] = xn
    h = _rms(xn, fg_ref[...]) * (1.0 + m[4:5, :]) + m[3:4, :]
    hb = h.astype(BF16)
    u = lax.bitcast_convert_type(hb.astype(F32), U32)
    hp_ref[...] = (u[:, :HALF] >> 16) | u[:, HALF:]

    tm = hb.shape[0]
    nt = (((1,), (1,)), ((), ()))
    logits = lax.dot_general(rw_ref[...], hb, nt, preferred_element_type=F32) + rb_ref[...]
    e_iota = lax.broadcasted_iota(I32, (N_EXPERTS, tm), 0).astype(F32)
    vals, idxs = [], []
    cur = logits
    for _ in range(TOP_K):
        mx = jnp.max(cur, axis=0, keepdims=True)
        ik = jnp.min(jnp.where(cur == mx, e_iota, float(N_EXPERTS)), axis=0, keepdims=True)
        vals.append(mx)
        idxs.append(ik)
        cur = jnp.where(e_iota == ik, -jnp.inf, cur)
    v = jnp.concatenate(vals, axis=0)
    w = jnp.exp(v - v[0:1, :])
    tw_ref[...] = w / jnp.sum(w, axis=0, keepdims=True)
    ti_ref[...] = jnp.concatenate(idxs, axis=0).astype(I32)

    onehot = jnp.zeros((N_EXPERTS, tm), F32)
    for ik in idxs:
        onehot = onehot + (e_iota == ik).astype(F32)
    before = (lax.broadcasted_iota(I32, (tm, tm), 0) < lax.broadcasted_iota(I32, (tm, tm), 1)).astype(BF16)
    seen = jnp.dot(onehot.astype(BF16), before, preferred_element_type=F32) + base_ref[...]
    ranks = [jnp.sum(jnp.where(e_iota == ik, seen, 0.0), axis=0, keepdims=True) for ik in idxs]
    rk_ref[...] = jnp.concatenate(ranks, axis=0).astype(I32)
    total = base_ref[...] + jnp.sum(onehot, axis=1, keepdims=True)
    base_ref[...] = total
    cnt_ref[...] = jnp.broadcast_to(total, cnt_ref.shape).astype(I32)


def _outproj(oa, orr, w_out_b, x2, batch0, x_batch0, mod_l, post_g, ffn_g, rw_t, rb, b, s):
    n = b * s
    tm = min(ROW_TILE, s)
    nb = s // tm
    row = lambda i: (i, 0)
    full = lambda i: (0, 0)
    col = lambda i: (0, i)
    return pl.pallas_call(
        _outproj_kernel,
        grid=(n // tm,),
        in_specs=[
            pl.BlockSpec((tm, ATTN_W), row),
            pl.BlockSpec((tm, RET_W), row),
            pl.BlockSpec((D_MODEL, D_MODEL), full),
            pl.BlockSpec((tm, D_MODEL), lambda i: (x_batch0 * nb + i, 0)),
            pl.BlockSpec((1, 6, D_MODEL), lambda i: (batch0 + i // nb, 0, 0)),
            pl.BlockSpec((1, D_MODEL), full),
            pl.BlockSpec((1, D_MODEL), full),
            pl.BlockSpec((N_EXPERTS, D_MODEL), full),
            pl.BlockSpec((N_EXPERTS, 1), full),
        ],
        out_specs=[
            pl.BlockSpec((tm, D_MODEL), row),
            pl.BlockSpec((tm, HALF), row),
            pl.BlockSpec((TOP_K, tm), col),
            pl.BlockSpec((TOP_K, tm), col),
            pl.BlockSpec((TOP_K, tm), col),
            pl.BlockSpec((N_EXPERTS, LANES), full),
        ],
        out_shape=[
            jax.ShapeDtypeStruct((n, D_MODEL), F32),
            jax.ShapeDtypeStruct((n, HALF), U32),
            jax.ShapeDtypeStruct((TOP_K, n), I32),
            jax.ShapeDtypeStruct((TOP_K, n), F32),
            jax.ShapeDtypeStruct((TOP_K, n), I32),
            jax.ShapeDtypeStruct((N_EXPERTS, LANES), I32),
        ],
        scratch_shapes=[pltpu.VMEM((N_EXPERTS, 1), F32)],
        compiler_params=_params(("arbitrary",)),
        name="outproj_router",
    )(oa, orr, w_out_b, x2, mod_l, post_g, ffn_g, rw_t, rb)


def _sc_scatter_rows(rows, pos3, n_out):
    n, w = rows.shape
    nchunk, kk, c = pos3.shape
    per_w = nchunk // SC_WORKERS
    mesh = plsc.VectorSubcoreMesh(core_axis_name="c", subcore_axis_name="s")

    @functools.partial(
        pl.kernel, mesh=mesh,
        out_type=jax.ShapeDtypeStruct((n_out, w), rows.dtype),
        scratch_types=[pltpu.VMEM((kk, c), I32), pltpu.VMEM((c, w), rows.dtype)],
    )
    def k(rows_hbm, pos_hbm, out_hbm, idx_v, rows_v):
        wid = lax.axis_index("s") * SC_CORES + lax.axis_index("c")

        @pl.loop(0, per_w)
        def _(i):
            ch = wid * per_w + i
            pltpu.sync_copy(pos_hbm.at[ch], idx_v)
            pltpu.sync_copy(rows_hbm.at[pl.ds(ch * c, c)], rows_v)
            for j in range(kk):
                pltpu.sync_copy(rows_v, out_hbm.at[idx_v.at[j]])

    return k(rows, pos3)


def _sc_gather_rows(table, idx):
    w = table.shape[1]
    b = idx.shape[0]
    c = SC_CHUNK
    per_w = b // (SC_WORKERS * c)
    mesh = plsc.VectorSubcoreMesh(core_axis_name="c", subcore_axis_name="s")

    @functools.partial(
        pl.kernel, mesh=mesh,
        out_type=jax.ShapeDtypeStruct((b, w), table.dtype),
        scratch_types=[pltpu.VMEM((c,), I32), pltpu.VMEM((c, w), table.dtype), pltpu.SemaphoreType.DMA],
    )
    def k(table_hbm, idx_hbm, out_hbm, idx_v, rows_v, sem):
        wid = lax.axis_index("s") * SC_CORES + lax.axis_index("c")

        @pl.loop(0, per_w)
        def _(i):
            base = (wid * per_w + i) * c
            pltpu.sync_copy(idx_hbm.at[pl.ds(base, c)], idx_v)
            pltpu.async_copy(table_hbm.at[idx_v], rows_v, sem).wait()
            pltpu.sync_copy(rows_v, out_hbm.at[pl.ds(base, c)])

    return k(table, idx)


def _experts_kernel(te_ref, nv_ref, xs_ref, wu_ref, bu_ref, wd_ref, bd_ref, ys_ref, wu_b, wd_b):
    i = pl.program_id(0)
    live = i < nv_ref[0]
    new_expert = jnp.logical_or(i == 0, te_ref[i] != te_ref[jnp.maximum(i - 1, 0)])

    @pl.when(jnp.logical_and(live, new_expert))
    def _():
        wu_b[...] = wu_ref[0, 0].astype(BF16)
        wd_b[...] = wd_ref[0, 0].astype(BF16)

    @pl.when(live)
    def _():
        w = xs_ref[...]
        lo = _unpack_lo(w).astype(BF16)
        hi = _unpack_hi(w).astype(BF16)
        up = jnp.dot(lo, wu_b[:HALF, :], preferred_element_type=F32)
        up = up + jnp.dot(hi, wu_b[HALF:, :], preferred_element_type=F32) + bu_ref[0, 0]
        glu = jnp.minimum(up[:, :D_FF], SWIGLU_LIMIT)
        lin = jnp.clip(up[:, D_FF:], -SWIGLU_LIMIT, SWIGLU_LIMIT)
        act = glu * jax.nn.sigmoid(SWIGLU_ALPHA * glu) * (lin + 1.0)
        y = jnp.dot(act.astype(BF16), wd_b[...], preferred_element_type=F32) + bd_ref[0, 0]
        ys_ref[...] = _pack_bf16_pairs(y)


def _experts(xs, tile_expert, n_valid, layer, wu, bu, wd, bd):
    n_pad = xs.shape[0]
    t = MOE_TILE
    grid_spec = pltpu.PrefetchScalarGridSpec(
        num_scalar_prefetch=2,
        grid=(n_pad // t,),
        in_specs=[
            pl.BlockSpec((t, HALF), lambda i, te, nv: (i, 0)),
            pl.BlockSpec((1, 1, D_MODEL, 2 * D_FF), lambda i, te, nv: (layer, te[i], 0, 0)),
            pl.BlockSpec((1, 1, 1, 2 * D_FF), lambda i, te, nv: (layer, te[i], 0, 0)),
            pl.BlockSpec((1, 1, D_FF, D_MODEL), lambda i, te, nv: (layer, te[i], 0, 0)),
            pl.BlockSpec((1, 1, 1, D_MODEL), lambda i, te, nv: (layer, te[i], 0, 0)),
        ],
        out_specs=pl.BlockSpec((t, HALF), lambda i, te, nv: (i, 0)),
        scratch_shapes=[pltpu.VMEM((D_MODEL, 2 * D_FF), BF16), pltpu.VMEM((D_FF, D_MODEL), BF16)],
    )
    return pl.pallas_call(
        _experts_kernel,
        grid_spec=grid_spec,
        out_shape=jax.ShapeDtypeStruct((n_pad, HALF), U32),
        compiler_params=pltpu.CompilerParams(dimension_semantics=("arbitrary",), vmem_limit_bytes=EXPERTS_VMEM_LIMIT),
        name="experts",
    )(tile_expert, n_valid, xs, wu, bu, wd, bd)


def _combine_kernel(y0_ref, y1_ref, y2_ref, y3_ref, w_ref, x_ref, mod_ref, g_ref, *rest):
    o_ref = rest[-1]
    w = w_ref[...]
    lo = None
    hi = None
    for k, y_ref in enumerate((y0_ref, y1_ref, y2_ref, y3_ref)):
        word = y_ref[...]
        wk = w[:, k:k + 1]
        lo_k = wk * _unpack_lo(word)
        hi_k = wk * _unpack_hi(word)
        lo = lo_k if lo is None else lo + lo_k
        hi = hi_k if hi is None else hi + hi_k
    yf = jnp.concatenate([lo, hi], axis=1)
    m = mod_ref[0]
    o_ref[...] = x_ref[...] + m[5:6, :] * _rms(yf, g_ref[...])


def _combine(yg, tw_t, xn, batch0, mod_l, post_g, b, s, out_batches=None, shared_out=None):
    n = b * s
    tm = min(ROW_TILE, s)
    nb = s // tm
    nblk = n // tm
    row = lambda i: (i, 0)
    full = lambda i: (0, 0)
    y_specs = [pl.BlockSpec((tm, HALF), functools.partial(lambda i, k: (k * nblk + i, 0), k=k)) for k in range(TOP_K)]
    out_rows, out_blk0 = (n, 0) if out_batches is None else (out_batches * s, batch0 * nb)
    extra_specs, extra_args, aliases = [], [], {}
    if shared_out is not None:
        extra_specs, extra_args, aliases = [pl.BlockSpec(memory_space=pl.ANY)], [shared_out], {8: 0}
    return pl.pallas_call(
        _combine_kernel,
        grid=(nblk,),
        in_specs=y_specs + [
            pl.BlockSpec((tm, TOP_K), row),
            pl.BlockSpec((tm, D_MODEL), row),
            pl.BlockSpec((1, 6, D_MODEL), lambda i: (batch0 + i // nb, 0, 0)),
            pl.BlockSpec((1, D_MODEL), full),
        ] + extra_specs,
        out_specs=pl.BlockSpec((tm, D_MODEL), lambda i: (out_blk0 + i, 0)),
        out_shape=jax.ShapeDtypeStruct((out_rows, D_MODEL), F32),
        input_output_aliases=aliases,
        compiler_params=_params(("parallel",)),
        name="combine",
    )(yg, yg, yg, yg, tw_t, xn, mod_l, post_g, *extra_args)


def _route(counts, tidx, rank, n_tiles):
    t = MOE_TILE
    cnt = counts[:, 0]
    padded = ((cnt + t - 1) // t) * t
    ends = jnp.cumsum(padded)
    starts = ends - padded
    experts = jnp.arange(N_EXPERTS, dtype=I32)[:, None, None]
    start_of = jnp.sum(jnp.where(tidx[None] == experts, starts[:, None, None], 0), axis=0)
    pos = start_of + rank
    tile_ends = ends // t
    n_valid = tile_ends[-1]
    tile = jnp.minimum(jnp.arange(n_tiles, dtype=I32), n_valid - 1)
    tile_expert = jnp.minimum(jnp.sum(tile_ends[None, :] <= tile[:, None], axis=1), N_EXPERTS - 1).astype(I32)
    return pos.astype(I32), tile_expert, n_valid.reshape(1).astype(I32)


def _value_range_scale(mod_l, pre_g, w_in_l):
    d = pre_g.shape[0]
    shift, scale = mod_l[:, 0, :], mod_l[:, 1, :]
    h_norm = d ** 0.5 * jnp.max(jnp.abs(pre_g[None, :] * (1.0 + scale)), axis=1) + jnp.linalg.norm(shift, axis=1)
    w_v = w_in_l[:, ATTN_W + KV_W:ATTN_W + 2 * KV_W]
    bound = 1.02 * h_norm * jnp.max(jnp.linalg.norm(w_v, axis=0))
    v_scale = V_RANGE / jnp.maximum(bound, 1e-30)
    lanes = lambda t: jnp.broadcast_to(t[:, None, None], (mod_l.shape[0], 1, LANES))
    return lanes(v_scale), lanes(1.0 / v_scale)


def kernel(x, c, ada_w, ada_b, pre_mix_g, post_mix_g, w_in, q_norm_g, k_norm_g, ret_decay_fwd, ret_decay_bwd, ret_gn_g, ret_gn_b, w_out, pre_ffn_g, post_ffn_g, router_w, router_b, exp_w_up, exp_b_up, exp_w_down, exp_b_down):
    b, s, d = x.shape
    depth = ada_w.shape[0]
    groups = 2 if b % 2 == 0 else 1
    bg = b // groups
    n = bg * s
    n_asg = n * TOP_K
    n_pad = n_asg + N_EXPERTS * MOE_TILE
    n_tiles = n_pad // MOE_TILE

    mod = _modulation(c, ada_w, ada_b)
    ropes = _rope_tables(s, HEAD_DIM) + _rope_tables(s, RET_DIM)
    bu = exp_b_up.reshape(depth, N_EXPERTS, 1, 2 * D_FF)
    bd = exp_b_down.reshape(depth, N_EXPERTS, 1, d)
    xs_in = [x.reshape(b * s, d)] * groups
    x_b0 = [g * bg for g in range(groups)]
    out = None
    for l in range(depth):
        mod_l = mod[l].reshape(b, 6, d)
        qg = jnp.tile(q_norm_g[l], LANES // HEAD_DIM).reshape(1, LANES)
        kg = jnp.tile(k_norm_g[l], LANES // HEAD_DIM).reshape(1, LANES)
        v_scale, fac = _value_range_scale(mod_l, pre_mix_g[l], w_in[l])
        w_in_b = w_in[l].astype(BF16)
        w_out_b = w_out[l].astype(BF16)
        rw_t = router_w[l].T.astype(BF16)
        dec = jnp.stack([ret_decay_fwd[l], ret_decay_bwd[l]]).astype(F32)
        last = l + 1 == depth
        proj = [_inproj(xs_in[g], g * bg, x_b0[g], mod_l, pre_mix_g[l].reshape(1, d), w_in_b, ropes, qg, kg, v_scale,
                        bg, s) for g in range(groups)]
        oa = [_attention(p[0], p[1], p[2], fac, g * bg, bg, s) for g, p in enumerate(proj)]
        orr = [_retention(p[3], p[4], p[5], p[6], dec, ret_gn_g[l].reshape(1, RET_W), ret_gn_b[l].reshape(1, RET_W),
                          bg, s) for p in proj]
        routed = []
        for g in range(groups):
            xn, hp, tidx, tw, rank, counts = _outproj(
                oa[g], orr[g], w_out_b, xs_in[g], g * bg, x_b0[g], mod_l, post_mix_g[l].reshape(1, d),
                pre_ffn_g[l].reshape(1, d), rw_t, router_b[l].reshape(N_EXPERTS, 1), bg, s)
            pos, tile_expert, n_valid = _route(counts, tidx, rank, n_tiles)
            pos3 = pos.reshape(TOP_K, n // SC_CHUNK, SC_CHUNK).transpose(1, 0, 2)
            routed.append((xn, tw, pos, tile_expert, n_valid, _sc_scatter_rows(hp, pos3, n_pad)))
        yg = []
        for xn, tw, pos, tile_expert, n_valid, xs in routed:
            ys = _experts(xs, tile_expert, n_valid, l, exp_w_up, bu, exp_w_down, bd)
            yg.append(_sc_gather_rows(ys, pos.reshape(n_asg)))
        nxt = []
        for g, (xn, tw, *_) in enumerate(routed):
            if last:
                out = _combine(yg[g], tw.T, xn, g * bg, mod_l, post_ffn_g[l].reshape(1, d), bg, s,
                               out_batches=b, shared_out=out)
            else:
                nxt.append(_combine(yg[g], tw.T, xn, g * bg, mod_l, post_ffn_g[l].reshape(1, d), bg, s))
        xs_in, x_b0 = nxt, [0] * groups
    return out.reshape(b, s, d)
```

```python
import functools

import numpy as np
import jax
import jax.numpy as jnp
from jax import lax
from jax.experimental import pallas as pl
from jax.experimental.pallas import tpu as pltpu
from jax.experimental.pallas import tpu_sc as plsc

F32 = jnp.float32
BF16 = jnp.bfloat16
U32 = jnp.uint32
I32 = jnp.int32
FP8 = jnp.float8_e4m3fn

D_MODEL = 1024
GRID_W = 64
ROPE_THETA = 10000.0
HEAD_DIM = 64
Q_HEADS = 8
KV_HEADS = 2
RET_HEADS = 4
RET_DIM = 128
RET_CHUNK = 128
ATTN_W = Q_HEADS * HEAD_DIM
KV_W = KV_HEADS * HEAD_DIM
RET_W = RET_HEADS * RET_DIM
IN_W = ATTN_W + 2 * KV_W + 4 * RET_W
N_EXPERTS = 32
TOP_K = 4
D_FF = D_MODEL
SWIGLU_LIMIT = 7.0
SWIGLU_ALPHA = 1.702
NORM_EPS = 1e-6
Q_SCALE = HEAD_DIM ** -0.5 * float(np.log2(np.e))
V_RANGE = 224.0
P_SHIFT = 8.0
HALF = D_MODEL // 2
LANES = 128

ROW_TILE = 512
Q_TILE = 128
Q_TILES_PER_TRIP = 2
KV_TILE = 1024
MOE_TILE = 512
VMEM_LIMIT = 48 * 1024 * 1024
EXPERTS_VMEM_LIMIT = 56 * 1024 * 1024

SC_CORES = 2
SC_SUBCORES = 16
SC_WORKERS = SC_CORES * SC_SUBCORES
SC_CHUNK = 64


def _params(sem):
    return pltpu.CompilerParams(dimension_semantics=sem, vmem_limit_bytes=VMEM_LIMIT)


def _rms(x, g):
    return x * lax.rsqrt(jnp.mean(x * x, axis=-1, keepdims=True) + NORM_EPS) * g


def _pack_bf16_pairs(y):
    u = lax.bitcast_convert_type(y.astype(BF16).astype(F32), U32)
    return (u[:, :HALF] >> 16) | u[:, HALF:]


def _unpack_lo(w):
    return lax.bitcast_convert_type(w << 16, F32)


def _unpack_hi(w):
    return lax.bitcast_convert_type(w & jnp.uint32(0xFFFF0000), F32)


def _mod_kernel(c_ref, w_ref, b_ref, o_ref):
    c = c_ref[...]
    cond = c * jax.nn.sigmoid(c)
    o_ref[0] = jnp.dot(cond, w_ref[0], preferred_element_type=F32, precision=lax.Precision.HIGHEST) + b_ref[0]


def _modulation(c, ada_w, ada_b):
    depth, d, w6 = ada_w.shape
    b = c.shape[0]
    tn = 1536
    return pl.pallas_call(
        _mod_kernel,
        grid=(depth, w6 // tn),
        in_specs=[
            pl.BlockSpec((b, d), lambda l, j: (0, 0)),
            pl.BlockSpec((1, d, tn), lambda l, j: (l, 0, j)),
            pl.BlockSpec((1, 1, tn), lambda l, j: (l, 0, j)),
        ],
        out_specs=pl.BlockSpec((1, b, tn), lambda l, j: (l, 0, j)),
        out_shape=jax.ShapeDtypeStruct((depth, b, w6), F32),
        compiler_params=_params(("arbitrary", "arbitrary")),
        name="modulation",
    )(c, ada_w, ada_b.reshape(depth, 1, w6))


def _rope_tables(s, head_dim):
    quarter = head_dim // 4
    t = jnp.arange(s, dtype=F32)
    row = jnp.floor(t / GRID_W)
    col = t - row * GRID_W
    inv_freq = ROPE_THETA ** (-jnp.arange(quarter, dtype=F32) / quarter)
    lane = np.arange(LANES) % head_dim
    use_col = lane >= head_dim // 2
    second = (lane % (head_dim // 2)) >= quarter
    freq = inv_freq[lane % quarter]
    pos = jnp.where(use_col[None, :], col[:, None], row[:, None])
    ang = pos * freq[None, :]
    sign = jnp.where(second, 1.0, -1.0).astype(F32)
    return jnp.cos(ang), jnp.sin(ang) * sign[None, :]


def _rope(z, cos, sin_signed, quarter):
    lane = lax.broadcasted_iota(I32, z.shape, 1)
    first = (lane % (2 * quarter)) < quarter
    partner = jnp.where(first, pltpu.roll(z, LANES - quarter, 1), pltpu.roll(z, quarter, 1))
    return z * cos + partner * sin_signed


def _inproj_kernel(x_ref, mod_ref, g_ref, w_ref, ca_ref, sa_ref, cr_ref, sr_ref, qg_ref, kg_ref, vs_ref,
                   qa_ref, kd_ref, vt_ref, qr_ref, kr_ref, vr_ref, gr_ref):
    x = x_ref[...]
    m = mod_ref[0]
    h = _rms(x, g_ref[...]) * (1.0 + m[1:2, :]) + m[0:1, :]
    hb = h.astype(BF16)
    tm = x.shape[0]

    def proj(c0):
        z = jnp.dot(hb, w_ref[:, c0:c0 + 2 * LANES], preferred_element_type=F32)
        return z[:, :LANES], z[:, LANES:]

    ri = lax.broadcasted_iota(I32, (LANES, LANES), 0) // HEAD_DIM
    ci = lax.broadcasted_iota(I32, (LANES, LANES), 1) // HEAD_DIM
    head_ones = (ri == ci).astype(BF16)
    ca, sa = ca_ref[...], sa_ref[...]
    cr, sr = cr_ref[...], sr_ref[...]
    lane = lax.broadcasted_iota(I32, (tm, LANES), 1)
    low = lane < HEAD_DIM

    def head_norm_rope(z, g):
        ss = jnp.dot((z * z).astype(BF16), head_ones, preferred_element_type=F32)
        zn = z * lax.rsqrt(ss * (1.0 / HEAD_DIM) + NORM_EPS) * g
        return _rope(zn, ca, sa, HEAD_DIM // 4)

    def store_pair(ref, j, fn):
        def finish(zs):
            for i, z in enumerate(zs):
                ref[:, (2 * j + i) * LANES:(2 * j + i + 1) * LANES] = fn(z).astype(BF16)
        return finish

    def store_kv(zs):
        k, v = zs
        k = head_norm_rope(k, kg_ref[...])
        k_sw = pltpu.roll(k, HEAD_DIM, 1)
        v = v * vs_ref[0]
        kd_ref[0, 0] = jnp.where(low, k, k_sw).astype(BF16)
        kd_ref[0, 1] = jnp.where(low, k_sw, k).astype(BF16)
        vt_ref[0, 0] = jnp.where(low, v, 1.0).T.astype(FP8)
        vt_ref[0, 1] = jnp.where(low, pltpu.roll(v, HEAD_DIM, 1), 1.0).T.astype(FP8)

    jobs = [(2 * j * LANES, store_pair(qa_ref, j, lambda z: head_norm_rope(z, qg_ref[...]) * Q_SCALE))
            for j in range(ATTN_W // (2 * LANES))]
    jobs.append((ATTN_W, store_kv))
    base = ATTN_W + 2 * KV_W
    for j in range(RET_W // (2 * LANES)):
        off = 2 * j * LANES
        jobs += [
            (base + off, store_pair(qr_ref, j, lambda z: _rope(z, cr, sr, RET_DIM // 4))),
            (base + RET_W + off, store_pair(kr_ref, j, lambda z: _rope(z, cr, sr, RET_DIM // 4) * (RET_DIM ** -0.5))),
            (base + 2 * RET_W + off, store_pair(vr_ref, j, lambda z: z)),
            (base + 3 * RET_W + off, store_pair(gr_ref, j, lambda z: z * jax.nn.sigmoid(z))),
        ]
    zs = proj(jobs[0][0])
    for i, (_, finish) in enumerate(jobs):
        nxt = proj(jobs[i + 1][0]) if i + 1 < len(jobs) else None
        finish(zs)
        zs = nxt


def _inproj(x2, batch0, x_batch0, mod_l, pre_g, w_in_b, ropes, qg, kg, v_scale, b, s):
    n = b * s
    tm = min(ROW_TILE, s)
    nb = s // tm
    ca, sa, cr, sr = ropes
    row = lambda i: (i, 0)
    full = lambda i: (0, 0)
    rope_spec = pl.BlockSpec((tm, LANES), lambda i: (i % nb, 0))
    wide = jax.ShapeDtypeStruct((n, RET_W), BF16)
    return pl.pallas_call(
        _inproj_kernel,
        grid=(n // tm,),
        in_specs=[
            pl.BlockSpec((tm, D_MODEL), lambda i: (x_batch0 * nb + i, 0)),
            pl.BlockSpec((1, 6, D_MODEL), lambda i: (batch0 + i // nb, 0, 0)),
            pl.BlockSpec((1, D_MODEL), full),
            pl.BlockSpec((D_MODEL, IN_W), full),
            rope_spec, rope_spec, rope_spec, rope_spec,
            pl.BlockSpec((1, LANES), full),
            pl.BlockSpec((1, LANES), full),
            pl.BlockSpec((1, 1, LANES), lambda i: (batch0 + i // nb, 0, 0)),
        ],
        out_specs=[
            pl.BlockSpec((tm, ATTN_W), row),
            pl.BlockSpec((1, KV_HEADS, tm, LANES), lambda i: (i // nb, 0, i % nb, 0)),
            pl.BlockSpec((1, KV_HEADS, LANES, tm), lambda i: (i // nb, 0, 0, i % nb)),
            pl.BlockSpec((tm, RET_W), row),
            pl.BlockSpec((tm, RET_W), row),
            pl.BlockSpec((tm, RET_W), row),
            pl.BlockSpec((tm, RET_W), row),
        ],
        out_shape=[
            jax.ShapeDtypeStruct((n, ATTN_W), BF16),
            jax.ShapeDtypeStruct((b, KV_HEADS, s, LANES), BF16),
            jax.ShapeDtypeStruct((b, KV_HEADS, LANES, s), FP8),
            wide, wide, wide, wide,
        ],
        compiler_params=_params(("parallel",)),
        name="inproj",
    )(x2, mod_l, pre_g, w_in_b, ca, sa, cr, sr, qg, kg, v_scale)


def _attn_kernel(q_ref, kd_ref, vt_ref, fac_ref, o_ref, q_s, qn_s, m_ref, acc_ref, s0_ref, s1_ref,
                 *, tq, tk, tiles):
    s = kd_ref.shape[2]
    nk = s // tk
    nq = s // tq
    pairs = (Q_HEADS // KV_HEADS) // 2
    top = lax.broadcasted_iota(I32, (LANES, tq), 0) < HEAD_DIM
    unscale = jnp.concatenate([fac_ref[0]] * (2 * tq // LANES), axis=1)

    chains = tiles * pairs

    def load_q(trip, dst):
        for u in range(tiles):
            rows = pl.ds(pl.multiple_of((trip * tiles + u) * tq, tq), tq)
            for p in range(pairs):
                qt = q_ref[rows, p * LANES:(p + 1) * LANES].astype(F32).T
                dst[u * pairs + p, :, :tq] = jnp.where(top, qt, 0.0).astype(BF16)
                dst[u * pairs + p, :, tq:] = jnp.where(top, 0.0, qt).astype(BF16)

    def keys(j):
        return kd_ref[0, 0, j * tk:(j + 1) * tk, :]

    def step(j, cur_ref, nxt_ref, q_next, j_next):
        vt = vt_ref[0, 0, :, j * tk:(j + 1) * tk]
        kd = keys(j_next)
        nxt_ref[0] = jnp.dot(kd, q_next[0], preferred_element_type=F32)
        for c in range(chains):
            sc = cur_ref[c]
            m_old = m_ref[c]
            m_new = jnp.maximum(m_old, jnp.max(sc, axis=0, keepdims=True))
            pt = jnp.exp2(sc - (m_new - P_SHIFT))
            alpha = jnp.exp2(m_old - m_new)
            if c + 1 < chains:
                nxt_ref[c + 1] = jnp.dot(kd, q_next[c + 1], preferred_element_type=F32)
            acc_ref[c] = alpha * acc_ref[c] + jnp.dot(vt, pt.astype(FP8), preferred_element_type=F32)
            m_ref[c] = m_new

    bufs = (s0_ref, s1_ref)
    trips = nq // tiles
    load_q(0, q_s)
    for c in range(chains):
        s0_ref[c] = jnp.dot(keys(0), q_s[c], preferred_element_type=F32)

    def body(trip, carry):
        load_q(trip, q_s)
        load_q(jnp.minimum(trip + 1, trips - 1), qn_s)
        m_ref[...] = jnp.full(m_ref.shape, -jnp.inf, F32)
        acc_ref[...] = jnp.zeros(acc_ref.shape, F32)
        for j in range(nk):
            last = j + 1 == nk
            step(j, bufs[j % len(bufs)], bufs[(j + 1) % len(bufs)], qn_s if last else q_s, 0 if last else j + 1)
        for c in range(chains):
            u, p = divmod(c, pairs)
            rows = pl.ds(pl.multiple_of((trip * tiles + u) * tq, tq), tq)
            acc = acc_ref[c]
            o_t = acc[:HEAD_DIM] / acc[HEAD_DIM:] * unscale
            both = jnp.concatenate([o_t[:, :tq], o_t[:, tq:]], axis=0)
            o_ref[rows, p * LANES:(p + 1) * LANES] = both.T.astype(BF16)
        return carry

    lax.fori_loop(0, trips, body, 0)


def _attention(qa, kd, vt, fac, batch0, b, s):
    n = qa.shape[0]
    tq = Q_TILE
    tk = min(KV_TILE, s // 2)
    tiles = Q_TILES_PER_TRIP
    assert tq == LANES and s % (tiles * tq) == 0 and s % (2 * tk) == 0 and tk % LANES == 0, (s, tq, tk)
    gw = ATTN_W // KV_HEADS
    chains = tiles * ((Q_HEADS // KV_HEADS) // 2)
    return pl.pallas_call(
        functools.partial(_attn_kernel, tq=tq, tk=tk, tiles=tiles),
        grid=(b, KV_HEADS),
        in_specs=[
            pl.BlockSpec((s, gw), lambda bi, h: (bi, h)),
            pl.BlockSpec((1, 1, s, LANES), lambda bi, h: (bi, h, 0, 0)),
            pl.BlockSpec((1, 1, LANES, s), lambda bi, h: (bi, h, 0, 0)),
            pl.BlockSpec((1, 1, LANES), lambda bi, h: (batch0 + bi, 0, 0)),
        ],
        out_specs=pl.BlockSpec((s, gw), lambda bi, h: (bi, h)),
        out_shape=jax.ShapeDtypeStruct((n, ATTN_W), BF16),
        scratch_shapes=[
            pltpu.VMEM((chains, LANES, 2 * tq), BF16),
            pltpu.VMEM((chains, LANES, 2 * tq), BF16),
            pltpu.VMEM((chains, 1, 2 * tq), F32),
            pltpu.VMEM((chains, LANES, 2 * tq), F32),
            pltpu.VMEM((chains, tk, 2 * tq), F32),
            pltpu.VMEM((chains, tk, 2 * tq), F32),
        ],
        compiler_params=_params(("parallel", "parallel")),
        name="attention",
    )(qa, kd, vt, fac)


def _retention_kernel(dec_ref, q_ref, k_ref, v_ref, g_ref, gng_ref, gnb_ref, o_ref,
                      fbuf, bbuf, sf_ref, sb_ref):
    h = pl.program_id(1)
    c = RET_CHUNK
    s = q_ref.shape[1]
    nc = s // c
    lgf = jnp.full((1, 1), dec_ref[0, h], F32)
    lgb = jnp.full((1, 1), dec_ref[1, h], F32)
    ii = lax.broadcasted_iota(I32, (c, c), 0)
    jj = lax.broadcasted_iota(I32, (c, c), 1)
    diff = (ii - jj).astype(F32)
    d_f = jnp.where(diff >= 0, jnp.exp(lgf * jnp.maximum(diff, 0.0)), 0.0)
    d_b = jnp.where(diff <= 0, jnp.exp(lgb * jnp.maximum(-diff, 0.0)), 0.0)
    idx = lax.broadcasted_iota(I32, (c, 1), 0).astype(F32)
    xi_f = jnp.exp(lgf * (idx + 1.0))
    zeta_f = jnp.exp(lgf * (c - 1.0 - idx))
    xi_b = jnp.exp(lgb * (c - idx))
    zeta_b = jnp.exp(lgb * idx)
    cd_f = jnp.exp(lgf * c)
    cd_b = jnp.exp(lgb * c)
    sf_ref[...] = jnp.zeros(sf_ref.shape, F32)
    sb_ref[...] = jnp.zeros(sb_ref.shape, F32)
    nt = (((1,), (1,)), ((), ()))

    fwd = (d_f, xi_f, zeta_f, cd_f, sf_ref, fbuf)
    bwd = (d_b, xi_b, zeta_b, cd_b, sb_ref, bbuf)

    def body(i, carry):
        jobs = [(fwd, 2 * i), (bwd, nc - 1 - 2 * i), (fwd, 2 * i + 1), (bwd, nc - 2 - 2 * i)]
        offs = [pl.multiple_of(ci * c, c) for _, ci in jobs]
        qs = [q_ref[0, pl.ds(off, c), :] for off in offs]
        ks = [k_ref[0, pl.ds(off, c), :] for off in offs]
        vs = [v_ref[0, pl.ds(off, c), :] for off in offs]
        scores = [lax.dot_general(q, k, nt, preferred_element_type=F32) for q, k in zip(qs, ks)]
        incs = []
        for (d, _), k, v in zip(jobs, ks, vs):
            kzt = (k.astype(F32) * d[2]).T.astype(BF16)
            incs.append(jnp.dot(kzt, v, preferred_element_type=F32))
        states = [fwd[4][...], bwd[4][...]]
        cross = [jnp.dot(qs[n], states[n].astype(BF16), preferred_element_type=F32) for n in range(2)]
        states += [states[n] * jobs[n][0][3] + incs[n] for n in range(2)]
        cross += [jnp.dot(qs[n], states[n].astype(BF16), preferred_element_type=F32) for n in range(2, 4)]
        for n, ((d, _), off) in enumerate(zip(jobs, offs)):
            a = (scores[n] * d[0]).astype(BF16)
            d[5][pl.ds(off, c), :] = jnp.dot(a, vs[n], preferred_element_type=F32) + cross[n] * d[1]
        for n in range(2):
            jobs[n][0][4][...] = states[2 + n] * jobs[n][0][3] + incs[2 + n]
        return carry

    lax.fori_loop(0, nc // 2, body, 0)

    rows = min(512, s)

    def fin(t, carry):
        off = pl.multiple_of(t * rows, rows)
        r = fbuf[pl.ds(off, rows), :] + bbuf[pl.ds(off, rows), :]
        mu = jnp.mean(r, axis=-1, keepdims=True)
        d = r - mu
        var = jnp.mean(d * d, axis=-1, keepdims=True)
        y = d * lax.rsqrt(var + NORM_EPS) * gng_ref[...] + gnb_ref[...]
        o_ref[0, pl.ds(off, rows), :] = (g_ref[0, pl.ds(off, rows), :].astype(F32) * y).astype(BF16)
        return carry

    lax.fori_loop(0, s // rows, fin, 0)


def _retention(qr, kr, vr, gr, dec, gn_g, gn_b, b, s):
    shp = (b, s, RET_W)
    assert s % (2 * RET_CHUNK) == 0, s
    head = pl.BlockSpec((1, s, RET_DIM), lambda bi, h: (bi, 0, h))
    vec = pl.BlockSpec((1, RET_DIM), lambda bi, h: (0, h))
    out = pl.pallas_call(
        _retention_kernel,
        grid=(b, RET_HEADS),
        in_specs=[pl.BlockSpec(memory_space=pltpu.SMEM), head, head, head, head, vec, vec],
        out_specs=head,
        out_shape=jax.ShapeDtypeStruct(shp, BF16),
        scratch_shapes=[
            pltpu.VMEM((s, RET_DIM), F32),
            pltpu.VMEM((s, RET_DIM), F32),
            pltpu.VMEM((RET_DIM, RET_DIM), F32),
            pltpu.VMEM((RET_DIM, RET_DIM), F32),
        ],
        compiler_params=_params(("parallel", "parallel")),
        name="retention",
    )(dec, qr.reshape(shp), kr.reshape(shp), vr.reshape(shp), gr.reshape(shp), gn_g, gn_b)
    return out.reshape(b * s, RET_W)


def _outproj_kernel(oa_ref, or_ref, w_ref, x_ref, mod_ref, pg_ref, fg_ref, rw_ref, rb_ref,
                    xn_ref, hp_ref, ti_ref, tw_ref, rk_ref, cnt_ref, base_ref):
    @pl.when(pl.program_id(0) == 0)
    def _():
        base_ref[...] = jnp.zeros(base_ref.shape, F32)

    m = mod_ref[0]
    tm = x_ref.shape[0]
    nt = (((1,), (1,)), ((), ()))
    halves = [slice(0, tm // 2), slice(tm // 2, tm)]
    ys = [jnp.dot(oa_ref[r, :], w_ref[:ATTN_W, :], preferred_element_type=F32)
          + jnp.dot(or_ref[r, :], w_ref[ATTN_W:, :], preferred_element_type=F32) for r in halves]
    parts = []
    for r, y in zip(halves, ys):
        xn = x_ref[r, :] + m[2:3, :] * _rms(y, pg_ref[...])
        xn_ref[r, :] = xn
        h = _rms(xn, fg_ref[...]) * (1.0 + m[4:5, :]) + m[3:4, :]
        hb = h.astype(BF16)
        u = lax.bitcast_convert_type(hb.astype(F32), U32)
        hp_ref[r, :] = (u[:, :HALF] >> 16) | u[:, HALF:]
        parts.append(lax.dot_general(rw_ref[...], hb, nt, preferred_element_type=F32))
    logits = jnp.concatenate(parts, axis=1) + rb_ref[...]
    e_iota = lax.broadcasted_iota(I32, (N_EXPERTS, tm), 0).astype(F32)
    vals, idxs = [], []
    cur = logits
    for _ in range(TOP_K):
        mx = jnp.max(cur, axis=0, keepdims=True)
        ik = jnp.min(jnp.where(cur == mx, e_iota, float(N_EXPERTS)), axis=0, keepdims=True)
        vals.append(mx)
        idxs.append(ik)
        cur = jnp.where(e_iota == ik, -jnp.inf, cur)
    v = jnp.concatenate(vals, axis=0)
    w = jnp.exp(v - v[0:1, :])
    tw_ref[...] = w / jnp.sum(w, axis=0, keepdims=True)
    ti_ref[...] = jnp.concatenate(idxs, axis=0).astype(I32)

    onehot = jnp.zeros((N_EXPERTS, tm), F32)
    for ik in idxs:
        onehot = onehot + (e_iota == ik).astype(F32)
    before = (lax.broadcasted_iota(I32, (tm, tm), 0) < lax.broadcasted_iota(I32, (tm, tm), 1)).astype(BF16)
    seen = jnp.dot(onehot.astype(BF16), before, preferred_element_type=F32) + base_ref[...]
    ranks = [jnp.sum(jnp.where(e_iota == ik, seen, 0.0), axis=0, keepdims=True) for ik in idxs]
    rk_ref[...] = jnp.concatenate(ranks, axis=0).astype(I32)
    total = base_ref[...] + jnp.sum(onehot, axis=1, keepdims=True)
    base_ref[...] = total
    cnt_ref[...] = jnp.broadcast_to(total, cnt_ref.shape).astype(I32)


def _outproj(oa, orr, w_out_b, x2, batch0, x_batch0, mod_l, post_g, ffn_g, rw_t, rb, b, s):
    n = b * s
    tm = min(ROW_TILE, s)
    nb = s // tm
    row = lambda i: (i, 0)
    full = lambda i: (0, 0)
    col = lambda i: (0, i)
    return pl.pallas_call(
        _outproj_kernel,
        grid=(n // tm,),
        in_specs=[
            pl.BlockSpec((tm, ATTN_W), row),
            pl.BlockSpec((tm, RET_W), row),
            pl.BlockSpec((D_MODEL, D_MODEL), full),
            pl.BlockSpec((tm, D_MODEL), lambda i: (x_batch0 * nb + i, 0)),
            pl.BlockSpec((1, 6, D_MODEL), lambda i: (batch0 + i // nb, 0, 0)),
            pl.BlockSpec((1, D_MODEL), full),
            pl.BlockSpec((1, D_MODEL), full),
            pl.BlockSpec((N_EXPERTS, D_MODEL), full),
            pl.BlockSpec((N_EXPERTS, 1), full),
        ],
        out_specs=[
            pl.BlockSpec((tm, D_MODEL), row),
            pl.BlockSpec((tm, HALF), row),
            pl.BlockSpec((TOP_K, tm), col),
            pl.BlockSpec((TOP_K, tm), col),
            pl.BlockSpec((TOP_K, tm), col),
            pl.BlockSpec((N_EXPERTS, LANES), full),
        ],
        out_shape=[
            jax.ShapeDtypeStruct((n, D_MODEL), F32),
            jax.ShapeDtypeStruct((n, HALF), U32),
            jax.ShapeDtypeStruct((TOP_K, n), I32),
            jax.ShapeDtypeStruct((TOP_K, n), F32),
            jax.ShapeDtypeStruct((TOP_K, n), I32),
            jax.ShapeDtypeStruct((N_EXPERTS, LANES), I32),
        ],
        scratch_shapes=[pltpu.VMEM((N_EXPERTS, 1), F32)],
        compiler_params=_params(("arbitrary",)),
        name="outproj_router",
    )(oa, orr, w_out_b, x2, mod_l, post_g, ffn_g, rw_t, rb)


def _sc_scatter_rows(rows, pos3, n_out):
    n, w = rows.shape
    nchunk, kk, c = pos3.shape
    per_w = nchunk // SC_WORKERS
    mesh = plsc.VectorSubcoreMesh(core_axis_name="c", subcore_axis_name="s")

    @functools.partial(
        pl.kernel, mesh=mesh,
        out_type=jax.ShapeDtypeStruct((n_out, w), rows.dtype),
        scratch_types=[pltpu.VMEM((kk, c), I32), pltpu.VMEM((c, w), rows.dtype)],
    )
    def k(rows_hbm, pos_hbm, out_hbm, idx_v, rows_v):
        wid = lax.axis_index("s") * SC_CORES + lax.axis_index("c")

        @pl.loop(0, per_w)
        def _(i):
            ch = wid * per_w + i
            pltpu.sync_copy(pos_hbm.at[ch], idx_v)
            pltpu.sync_copy(rows_hbm.at[pl.ds(ch * c, c)], rows_v)
            for j in range(kk):
                pltpu.sync_copy(rows_v, out_hbm.at[idx_v.at[j]])

    return k(rows, pos3)


def _sc_gather_rows(table, idx):
    w = table.shape[1]
    b = idx.shape[0]
    c = SC_CHUNK
    per_w = b // (SC_WORKERS * c)
    mesh = plsc.VectorSubcoreMesh(core_axis_name="c", subcore_axis_name="s")

    @functools.partial(
        pl.kernel, mesh=mesh,
        out_type=jax.ShapeDtypeStruct((b, w), table.dtype),
        scratch_types=[pltpu.VMEM((c,), I32), pltpu.VMEM((c, w), table.dtype), pltpu.SemaphoreType.DMA],
    )
    def k(table_hbm, idx_hbm, out_hbm, idx_v, rows_v, sem):
        wid = lax.axis_index("s") * SC_CORES + lax.axis_index("c")

        @pl.loop(0, per_w)
        def _(i):
            base = (wid * per_w + i) * c
            pltpu.sync_copy(idx_hbm.at[pl.ds(base, c)], idx_v)
            pltpu.async_copy(table_hbm.at[idx_v], rows_v, sem).wait()
            pltpu.sync_copy(rows_v, out_hbm.at[pl.ds(base, c)])

    return k(table, idx)


def _experts_kernel(te_ref, nv_ref, xs_ref, wu_ref, bu_ref, wd_ref, bd_ref, ys_ref, wu_b, wd_b):
    i = pl.program_id(0)
    live = i < nv_ref[0]
    new_expert = jnp.logical_or(i == 0, te_ref[i] != te_ref[jnp.maximum(i - 1, 0)])

    @pl.when(jnp.logical_and(live, new_expert))
    def _():
        wu_b[...] = wu_ref[0, 0].astype(BF16)
        wd_b[...] = wd_ref[0, 0].astype(BF16)

    @pl.when(live)
    def _():
        t = xs_ref.shape[0]
        halves = [slice(0, t // 2), slice(t // 2, t)]
        ups = []
        for r in halves:
            w = xs_ref[r, :]
            lo = _unpack_lo(w).astype(BF16)
            hi = _unpack_hi(w).astype(BF16)
            ups.append(jnp.dot(lo, wu_b[:HALF, :], preferred_element_type=F32)
                       + jnp.dot(hi, wu_b[HALF:, :], preferred_element_type=F32))
        for r, up in zip(halves, ups):
            up = up + bu_ref[0, 0]
            glu = jnp.minimum(up[:, :D_FF], SWIGLU_LIMIT)
            lin = jnp.clip(up[:, D_FF:], -SWIGLU_LIMIT, SWIGLU_LIMIT)
            act = glu * jax.nn.sigmoid(SWIGLU_ALPHA * glu) * (lin + 1.0)
            y = jnp.dot(act.astype(BF16), wd_b[...], preferred_element_type=F32) + bd_ref[0, 0]
            ys_ref[r, :] = _pack_bf16_pairs(y)


def _experts(xs, tile_expert, n_valid, layer, wu, bu, wd, bd):
    n_pad = xs.shape[0]
    t = MOE_TILE
    grid_spec = pltpu.PrefetchScalarGridSpec(
        num_scalar_prefetch=2,
        grid=(n_pad // t,),
        in_specs=[
            pl.BlockSpec((t, HALF), lambda i, te, nv: (i, 0)),
            pl.BlockSpec((1, 1, D_MODEL, 2 * D_FF), lambda i, te, nv: (layer, te[i], 0, 0)),
            pl.BlockSpec((1, 1, 1, 2 * D_FF), lambda i, te, nv: (layer, te[i], 0, 0)),
            pl.BlockSpec((1, 1, D_FF, D_MODEL), lambda i, te, nv: (layer, te[i], 0, 0)),
            pl.BlockSpec((1, 1, 1, D_MODEL), lambda i, te, nv: (layer, te[i], 0, 0)),
        ],
        out_specs=pl.BlockSpec((t, HALF), lambda i, te, nv: (i, 0)),
        scratch_shapes=[pltpu.VMEM((D_MODEL, 2 * D_FF), BF16), pltpu.VMEM((D_FF, D_MODEL), BF16)],
    )
    return pl.pallas_call(
        _experts_kernel,
        grid_spec=grid_spec,
        out_shape=jax.ShapeDtypeStruct((n_pad, HALF), U32),
        compiler_params=pltpu.CompilerParams(dimension_semantics=("arbitrary",), vmem_limit_bytes=EXPERTS_VMEM_LIMIT),
        name="experts",
    )(tile_expert, n_valid, xs, wu, bu, wd, bd)


def _combine_kernel(y0_ref, y1_ref, y2_ref, y3_ref, w_ref, x_ref, mod_ref, g_ref, *rest):
    o_ref = rest[-1]
    w = w_ref[...]
    lo = None
    hi = None
    for k, y_ref in enumerate((y0_ref, y1_ref, y2_ref, y3_ref)):
        word = y_ref[...]
        wk = w[:, k:k + 1]
        lo_k = wk * _unpack_lo(word)
        hi_k = wk * _unpack_hi(word)
        lo = lo_k if lo is None else lo + lo_k
        hi = hi_k if hi is None else hi + hi_k
    yf = jnp.concatenate([lo, hi], axis=1)
    m = mod_ref[0]
    o_ref[...] = x_ref[...] + m[5:6, :] * _rms(yf, g_ref[...])


def _combine(yg, tw_t, xn, batch0, mod_l, post_g, b, s, out_batches=None, shared_out=None):
    n = b * s
    tm = min(ROW_TILE, s)
    nb = s // tm
    nblk = n // tm
    row = lambda i: (i, 0)
    full = lambda i: (0, 0)
    y_specs = [pl.BlockSpec((tm, HALF), functools.partial(lambda i, k: (k * nblk + i, 0), k=k)) for k in range(TOP_K)]
    out_rows, out_blk0 = (n, 0) if out_batches is None else (out_batches * s, batch0 * nb)
    extra_specs, extra_args, aliases = [], [], {}
    if shared_out is not None:
        extra_specs, extra_args, aliases = [pl.BlockSpec(memory_space=pl.ANY)], [shared_out], {8: 0}
    return pl.pallas_call(
        _combine_kernel,
        grid=(nblk,),
        in_specs=y_specs + [
            pl.BlockSpec((tm, TOP_K), row),
            pl.BlockSpec((tm, D_MODEL), row),
            pl.BlockSpec((1, 6, D_MODEL), lambda i: (batch0 + i // nb, 0, 0)),
            pl.BlockSpec((1, D_MODEL), full),
        ] + extra_specs,
        out_specs=pl.BlockSpec((tm, D_MODEL), lambda i: (out_blk0 + i, 0)),
        out_shape=jax.ShapeDtypeStruct((out_rows, D_MODEL), F32),
        input_output_aliases=aliases,
        compiler_params=_params(("parallel",)),
        name="combine",
    )(yg, yg, yg, yg, tw_t, xn, mod_l, post_g, *extra_args)


def _route(counts, tidx, rank, n_tiles):
    t = MOE_TILE
    cnt = counts[:, 0]
    padded = ((cnt + t - 1) // t) * t
    ends = jnp.cumsum(padded)
    starts = ends - padded
    experts = jnp.arange(N_EXPERTS, dtype=I32)[:, None, None]
    start_of = jnp.sum(jnp.where(tidx[None] == experts, starts[:, None, None], 0), axis=0)
    pos = start_of + rank
    tile_ends = ends // t
    n_valid = tile_ends[-1]
    tile = jnp.minimum(jnp.arange(n_tiles, dtype=I32), n_valid - 1)
    tile_expert = jnp.minimum(jnp.sum(tile_ends[None, :] <= tile[:, None], axis=1), N_EXPERTS - 1).astype(I32)
    return pos.astype(I32), tile_expert, n_valid.reshape(1).astype(I32)


def _value_range_scale(mod_l, pre_g, w_in_l):
    d = pre_g.shape[0]
    shift, scale = mod_l[:, 0, :], mod_l[:, 1, :]
    h_norm = d ** 0.5 * jnp.max(jnp.abs(pre_g[None, :] * (1.0 + scale)), axis=1) + jnp.linalg.norm(shift, axis=1)
    w_v = w_in_l[:, ATTN_W + KV_W:ATTN_W + 2 * KV_W]
    bound = 1.02 * h_norm * jnp.max(jnp.linalg.norm(w_v, axis=0))
    v_scale = V_RANGE / jnp.maximum(bound, 1e-30)
    lanes = lambda t: jnp.broadcast_to(t[:, None, None], (mod_l.shape[0], 1, LANES))
    return lanes(v_scale), lanes(1.0 / v_scale)


def kernel(x, c, ada_w, ada_b, pre_mix_g, post_mix_g, w_in, q_norm_g, k_norm_g, ret_decay_fwd, ret_decay_bwd, ret_gn_g, ret_gn_b, w_out, pre_ffn_g, post_ffn_g, router_w, router_b, exp_w_up, exp_b_up, exp_w_down, exp_b_down):
    b, s, d = x.shape
    depth = ada_w.shape[0]
    groups = 2 if b % 2 == 0 else 1
    bg = b // groups
    n = bg * s
    n_asg = n * TOP_K
    n_pad = n_asg + N_EXPERTS * MOE_TILE
    n_tiles = n_pad // MOE_TILE

    mod = _modulation(c, ada_w, ada_b)
    ropes = _rope_tables(s, HEAD_DIM) + _rope_tables(s, RET_DIM)
    bu = exp_b_up.reshape(depth, N_EXPERTS, 1, 2 * D_FF)
    bd = exp_b_down.reshape(depth, N_EXPERTS, 1, d)
    xs_in = [x.reshape(b * s, d)] * groups
    x_b0 = [g * bg for g in range(groups)]
    out = None
    for l in range(depth):
        mod_l = mod[l].reshape(b, 6, d)
        qg = jnp.tile(q_norm_g[l], LANES // HEAD_DIM).reshape(1, LANES)
        kg = jnp.tile(k_norm_g[l], LANES // HEAD_DIM).reshape(1, LANES)
        v_scale, fac = _value_range_scale(mod_l, pre_mix_g[l], w_in[l])
        w_in_b = w_in[l].astype(BF16)
        w_out_b = w_out[l].astype(BF16)
        rw_t = router_w[l].T.astype(BF16)
        dec = jnp.stack([ret_decay_fwd[l], ret_decay_bwd[l]]).astype(F32)
        last = l + 1 == depth
        proj = [_inproj(xs_in[g], g * bg, x_b0[g], mod_l, pre_mix_g[l].reshape(1, d), w_in_b, ropes, qg, kg, v_scale,
                        bg, s) for g in range(groups)]
        oa = [_attention(p[0], p[1], p[2], fac, g * bg, bg, s) for g, p in enumerate(proj)]
        orr = [_retention(p[3], p[4], p[5], p[6], dec, ret_gn_g[l].reshape(1, RET_W), ret_gn_b[l].reshape(1, RET_W),
                          bg, s) for p in proj]
        routed = []
        for g in range(groups):
            xn, hp, tidx, tw, rank, counts = _outproj(
                oa[g], orr[g], w_out_b, xs_in[g], g * bg, x_b0[g], mod_l, post_mix_g[l].reshape(1, d),
                pre_ffn_g[l].reshape(1, d), rw_t, router_b[l].reshape(N_EXPERTS, 1), bg, s)
            pos, tile_expert, n_valid = _route(counts, tidx, rank, n_tiles)
            pos3 = pos.reshape(TOP_K, n // SC_CHUNK, SC_CHUNK).transpose(1, 0, 2)
            routed.append((xn, tw, pos, tile_expert, n_valid, _sc_scatter_rows(hp, pos3, n_pad)))
        yg = []
        for xn, tw, pos, tile_expert, n_valid, xs in routed:
            ys = _experts(xs, tile_expert, n_valid, l, exp_w_up, bu, exp_w_down, bd)
            yg.append(_sc_gather_rows(ys, pos.reshape(n_asg)))
        nxt = []
        for g, (xn, tw, *_) in enumerate(routed):
            if last:
                out = _combine(yg[g], tw.T, xn, g * bg, mod_l, post_ffn_g[l].reshape(1, d), bg, s,
                               out_batches=b, shared_out=out)
            else:
                nxt.append(_combine(yg[g], tw.T, xn, g * bg, mod_l, post_ffn_g[l].reshape(1, d), bg, s))
        xs_in, x_b0 = nxt, [0] * groups
    return out.reshape(b, s, d)
```

```python
import functools

import numpy as np
import jax
import jax.numpy as jnp
from jax import lax
from jax.experimental import pallas as pl
from jax.experimental.pallas import tpu as pltpu
from jax.experimental.pallas import tpu_sc as plsc

F32 = jnp.float32
BF16 = jnp.bfloat16
U32 = jnp.uint32
I32 = jnp.int32
FP8 = jnp.float8_e4m3fn

D_MODEL = 1024
GRID_W = 64
ROPE_THETA = 10000.0
HEAD_DIM = 64
Q_HEADS = 8
KV_HEADS = 2
RET_HEADS = 4
RET_DIM = 128
RET_CHUNK = 128
ATTN_W = Q_HEADS * HEAD_DIM
KV_W = KV_HEADS * HEAD_DIM
RET_W = RET_HEADS * RET_DIM
IN_W = ATTN_W + 2 * KV_W + 4 * RET_W
N_EXPERTS = 32
TOP_K = 4
D_FF = D_MODEL
SWIGLU_LIMIT = 7.0
SWIGLU_ALPHA = 1.702
NORM_EPS = 1e-6
Q_SCALE = HEAD_DIM ** -0.5 * float(np.log2(np.e))
V_RANGE = 224.0
P_SHIFT = 8.0
HALF = D_MODEL // 2
LANES = 128

ROW_TILE = 512
Q_TILE = 128
Q_TILES_PER_TRIP = 2
KV_TILE = 1024
MOE_TILE = 512
RET_CHUNKS_PER_TRIP = 8
VMEM_LIMIT = 48 * 1024 * 1024
EXPERTS_VMEM_LIMIT = 56 * 1024 * 1024

SC_CORES = 2
SC_SUBCORES = 16
SC_WORKERS = SC_CORES * SC_SUBCORES
SC_CHUNK = 64


def _params(sem):
    return pltpu.CompilerParams(dimension_semantics=sem, vmem_limit_bytes=VMEM_LIMIT)


def _rms(x, g):
    return x * lax.rsqrt(jnp.mean(x * x, axis=-1, keepdims=True) + NORM_EPS) * g


def _pack_bf16_pairs(y):
    u = lax.bitcast_convert_type(y.astype(BF16).astype(F32), U32)
    return (u[:, :HALF] >> 16) | u[:, HALF:]


def _unpack_lo(w):
    return lax.bitcast_convert_type(w << 16, F32)


def _unpack_hi(w):
    return lax.bitcast_convert_type(w & jnp.uint32(0xFFFF0000), F32)


def _mod_kernel(c_ref, w_ref, b_ref, o_ref):
    c = c_ref[...]
    cond = c * jax.nn.sigmoid(c)
    o_ref[0] = jnp.dot(cond, w_ref[0], preferred_element_type=F32, precision=lax.Precision.HIGHEST) + b_ref[0]


def _modulation(c, ada_w, ada_b):
    depth, d, w6 = ada_w.shape
    b = c.shape[0]
    tn = 1536
    return pl.pallas_call(
        _mod_kernel,
        grid=(depth, w6 // tn),
        in_specs=[
            pl.BlockSpec((b, d), lambda l, j: (0, 0)),
            pl.BlockSpec((1, d, tn), lambda l, j: (l, 0, j)),
            pl.BlockSpec((1, 1, tn), lambda l, j: (l, 0, j)),
        ],
        out_specs=pl.BlockSpec((1, b, tn), lambda l, j: (l, 0, j)),
        out_shape=jax.ShapeDtypeStruct((depth, b, w6), F32),
        compiler_params=_params(("arbitrary", "arbitrary")),
        name="modulation",
    )(c, ada_w, ada_b.reshape(depth, 1, w6))


def _rope_tables(s, head_dim):
    quarter = head_dim // 4
    t = jnp.arange(s, dtype=F32)
    row = jnp.floor(t / GRID_W)
    col = t - row * GRID_W
    inv_freq = ROPE_THETA ** (-jnp.arange(quarter, dtype=F32) / quarter)
    lane = np.arange(LANES) % head_dim
    use_col = lane >= head_dim // 2
    second = (lane % (head_dim // 2)) >= quarter
    freq = inv_freq[lane % quarter]
    pos = jnp.where(use_col[None, :], col[:, None], row[:, None])
    ang = pos * freq[None, :]
    sign = jnp.where(second, 1.0, -1.0).astype(F32)
    return jnp.cos(ang), jnp.sin(ang) * sign[None, :]


def _rope(z, cos, sin_signed, quarter):
    lane = lax.broadcasted_iota(I32, z.shape, 1)
    first = (lane % (2 * quarter)) < quarter
    partner = jnp.where(first, pltpu.roll(z, LANES - quarter, 1), pltpu.roll(z, quarter, 1))
    return z * cos + partner * sin_signed


def _inproj_kernel(x_ref, mod_ref, g_ref, w_ref, ca_ref, sa_ref, cr_ref, sr_ref, qg_ref, kg_ref, vs_ref,
                   qa_ref, kd_ref, vt_ref, qr_ref, kr_ref, vr_ref, gr_ref):
    x = x_ref[...]
    m = mod_ref[0]
    h = _rms(x, g_ref[...]) * (1.0 + m[1:2, :]) + m[0:1, :]
    hb = h.astype(BF16)
    tm = x.shape[0]

    def proj(c0):
        z = jnp.dot(hb, w_ref[:, c0:c0 + 2 * LANES], preferred_element_type=F32)
        return z[:, :LANES], z[:, LANES:]

    ri = lax.broadcasted_iota(I32, (LANES, LANES), 0) // HEAD_DIM
    ci = lax.broadcasted_iota(I32, (LANES, LANES), 1) // HEAD_DIM
    head_ones = (ri == ci).astype(BF16)
    ca, sa = ca_ref[...], sa_ref[...]
    cr, sr = cr_ref[...], sr_ref[...]
    lane = lax.broadcasted_iota(I32, (tm, LANES), 1)
    low = lane < HEAD_DIM

    def head_norm_rope(z, g):
        ss = jnp.dot((z * z).astype(BF16), head_ones, preferred_element_type=F32)
        zn = z * lax.rsqrt(ss * (1.0 / HEAD_DIM) + NORM_EPS) * g
        return _rope(zn, ca, sa, HEAD_DIM // 4)

    def store_pair(ref, j, fn):
        def finish(zs):
            for i, z in enumerate(zs):
                ref[:, (2 * j + i) * LANES:(2 * j + i + 1) * LANES] = fn(z).astype(BF16)
        return finish

    def store_kv(zs):
        k, v = zs
        k = head_norm_rope(k, kg_ref[...])
        k_sw = pltpu.roll(k, HEAD_DIM, 1)
        v = v * vs_ref[0]
        kd_ref[0, 0] = jnp.where(low, k, k_sw).astype(BF16)
        kd_ref[0, 1] = jnp.where(low, k_sw, k).astype(BF16)
        vt_ref[0, 0] = jnp.where(low, v, 1.0).T.astype(FP8)
        vt_ref[0, 1] = jnp.where(low, pltpu.roll(v, HEAD_DIM, 1), 1.0).T.astype(FP8)

    jobs = [(2 * j * LANES, store_pair(qa_ref, j, lambda z: head_norm_rope(z, qg_ref[...]) * Q_SCALE))
            for j in range(ATTN_W // (2 * LANES))]
    jobs.append((ATTN_W, store_kv))
    base = ATTN_W + 2 * KV_W
    for j in range(RET_W // (2 * LANES)):
        off = 2 * j * LANES
        jobs += [
            (base + off, store_pair(qr_ref, j, lambda z: _rope(z, cr, sr, RET_DIM // 4))),
            (base + RET_W + off, store_pair(kr_ref, j, lambda z: _rope(z, cr, sr, RET_DIM // 4) * (RET_DIM ** -0.5))),
            (base + 2 * RET_W + off, store_pair(vr_ref, j, lambda z: z)),
            (base + 3 * RET_W + off, store_pair(gr_ref, j, lambda z: z * jax.nn.sigmoid(z))),
        ]
    zs = proj(jobs[0][0])
    for i, (_, finish) in enumerate(jobs):
        nxt = proj(jobs[i + 1][0]) if i + 1 < len(jobs) else None
        finish(zs)
        zs = nxt


def _inproj(x2, batch0, x_batch0, mod_l, pre_g, w_in_b, ropes, qg, kg, v_scale, b, s):
    n = b * s
    tm = min(ROW_TILE, s)
    nb = s // tm
    ca, sa, cr, sr = ropes
    row = lambda i: (i, 0)
    full = lambda i: (0, 0)
    rope_spec = pl.BlockSpec((tm, LANES), lambda i: (i % nb, 0))
    wide = jax.ShapeDtypeStruct((n, RET_W), BF16)
    return pl.pallas_call(
        _inproj_kernel,
        grid=(n // tm,),
        in_specs=[
            pl.BlockSpec((tm, D_MODEL), lambda i: (x_batch0 * nb + i, 0)),
            pl.BlockSpec((1, 6, D_MODEL), lambda i: (batch0 + i // nb, 0, 0)),
            pl.BlockSpec((1, D_MODEL), full),
            pl.BlockSpec((D_MODEL, IN_W), full),
            rope_spec, rope_spec, rope_spec, rope_spec,
            pl.BlockSpec((1, LANES), full),
            pl.BlockSpec((1, LANES), full),
            pl.BlockSpec((1, 1, LANES), lambda i: (batch0 + i // nb, 0, 0)),
        ],
        out_specs=[
            pl.BlockSpec((tm, ATTN_W), row),
            pl.BlockSpec((1, KV_HEADS, tm, LANES), lambda i: (i // nb, 0, i % nb, 0)),
            pl.BlockSpec((1, KV_HEADS, LANES, tm), lambda i: (i // nb, 0, 0, i % nb)),
            pl.BlockSpec((tm, RET_W), row),
            pl.BlockSpec((tm, RET_W), row),
            pl.BlockSpec((tm, RET_W), row),
            pl.BlockSpec((tm, RET_W), row),
        ],
        out_shape=[
            jax.ShapeDtypeStruct((n, ATTN_W), BF16),
            jax.ShapeDtypeStruct((b, KV_HEADS, s, LANES), BF16),
            jax.ShapeDtypeStruct((b, KV_HEADS, LANES, s), FP8),
            wide, wide, wide, wide,
        ],
        compiler_params=_params(("parallel",)),
        name="inproj",
    )(x2, mod_l, pre_g, w_in_b, ca, sa, cr, sr, qg, kg, v_scale)


def _attn_kernel(q_ref, kd_ref, vt_ref, fac_ref, o_ref, q_s, qn_s, m_ref, acc_ref, s0_ref, s1_ref,
                 *, tq, tk, tiles):
    s = kd_ref.shape[2]
    nk = s // tk
    nq = s // tq
    pairs = (Q_HEADS // KV_HEADS) // 2
    top = lax.broadcasted_iota(I32, (LANES, tq), 0) < HEAD_DIM
    unscale = jnp.concatenate([fac_ref[0]] * (2 * tq // LANES), axis=1)

    chains = tiles * pairs

    def load_q(trip, dst):
        for u in range(tiles):
            rows = pl.ds(pl.multiple_of((trip * tiles + u) * tq, tq), tq)
            for p in range(pairs):
                qt = q_ref[rows, p * LANES:(p + 1) * LANES].astype(F32).T
                dst[u * pairs + p, :, :tq] = jnp.where(top, qt, 0.0).astype(BF16)
                dst[u * pairs + p, :, tq:] = jnp.where(top, 0.0, qt).astype(BF16)

    def keys(j):
        return kd_ref[0, 0, j * tk:(j + 1) * tk, :]

    def step(j, cur_ref, nxt_ref, q_next, j_next):
        vt = vt_ref[0, 0, :, j * tk:(j + 1) * tk]
        kd = keys(j_next)
        nxt_ref[0] = jnp.dot(kd, q_next[0], preferred_element_type=F32)
        for c in range(chains):
            sc = cur_ref[c]
            m_old = m_ref[c]
            m_new = jnp.maximum(m_old, jnp.max(sc, axis=0, keepdims=True))
            pt = jnp.exp2(sc - (m_new - P_SHIFT))
            alpha = jnp.exp2(m_old - m_new)
            if c + 1 < chains:
                nxt_ref[c + 1] = jnp.dot(kd, q_next[c + 1], preferred_element_type=F32)
            acc_ref[c] = alpha * acc_ref[c] + jnp.dot(vt, pt.astype(FP8), preferred_element_type=F32)
            m_ref[c] = m_new

    bufs = (s0_ref, s1_ref)
    trips = nq // tiles
    load_q(0, q_s)
    for c in range(chains):
        s0_ref[c] = jnp.dot(keys(0), q_s[c], preferred_element_type=F32)

    def body(trip, carry):
        load_q(trip, q_s)
        load_q(jnp.minimum(trip + 1, trips - 1), qn_s)
        m_ref[...] = jnp.full(m_ref.shape, -jnp.inf, F32)
        acc_ref[...] = jnp.zeros(acc_ref.shape, F32)
        for j in range(nk):
            last = j + 1 == nk
            step(j, bufs[j % len(bufs)], bufs[(j + 1) % len(bufs)], qn_s if last else q_s, 0 if last else j + 1)
        for c in range(chains):
            u, p = divmod(c, pairs)
            rows = pl.ds(pl.multiple_of((trip * tiles + u) * tq, tq), tq)
            acc = acc_ref[c]
            o_t = acc[:HEAD_DIM] / acc[HEAD_DIM:] * unscale
            both = jnp.concatenate([o_t[:, :tq], o_t[:, tq:]], axis=0)
            o_ref[rows, p * LANES:(p + 1) * LANES] = both.T.astype(BF16)
        return carry

    lax.fori_loop(0, trips, body, 0)


def _attention(qa, kd, vt, fac, batch0, b, s):
    n = qa.shape[0]
    tq = Q_TILE
    tk = min(KV_TILE, s // 2)
    tiles = Q_TILES_PER_TRIP
    assert tq == LANES and s % (tiles * tq) == 0 and s % (2 * tk) == 0 and tk % LANES == 0, (s, tq, tk)
    gw = ATTN_W // KV_HEADS
    chains = tiles * ((Q_HEADS // KV_HEADS) // 2)
    return pl.pallas_call(
        functools.partial(_attn_kernel, tq=tq, tk=tk, tiles=tiles),
        grid=(b, KV_HEADS),
        in_specs=[
            pl.BlockSpec((s, gw), lambda bi, h: (bi, h)),
            pl.BlockSpec((1, 1, s, LANES), lambda bi, h: (bi, h, 0, 0)),
            pl.BlockSpec((1, 1, LANES, s), lambda bi, h: (bi, h, 0, 0)),
            pl.BlockSpec((1, 1, LANES), lambda bi, h: (batch0 + bi, 0, 0)),
        ],
        out_specs=pl.BlockSpec((s, gw), lambda bi, h: (bi, h)),
        out_shape=jax.ShapeDtypeStruct((n, ATTN_W), BF16),
        scratch_shapes=[
            pltpu.VMEM((chains, LANES, 2 * tq), BF16),
            pltpu.VMEM((chains, LANES, 2 * tq), BF16),
            pltpu.VMEM((chains, 1, 2 * tq), F32),
            pltpu.VMEM((chains, LANES, 2 * tq), F32),
            pltpu.VMEM((chains, tk, 2 * tq), F32),
            pltpu.VMEM((chains, tk, 2 * tq), F32),
        ],
        compiler_params=_params(("parallel", "parallel")),
        name="attention",
    )(qa, kd, vt, fac)


def _retention_kernel(dec_ref, q_ref, k_ref, v_ref, g_ref, gng_ref, gnb_ref, o_ref,
                      fbuf, bbuf, sf_ref, sb_ref):
    h = pl.program_id(1)
    c = RET_CHUNK
    s = q_ref.shape[1]
    nc = s // c
    lgf = jnp.full((1, 1), dec_ref[0, h], F32)
    lgb = jnp.full((1, 1), dec_ref[1, h], F32)
    ii = lax.broadcasted_iota(I32, (c, c), 0)
    jj = lax.broadcasted_iota(I32, (c, c), 1)
    diff = (ii - jj).astype(F32)
    d_f = jnp.where(diff >= 0, jnp.exp(lgf * jnp.maximum(diff, 0.0)), 0.0)
    d_b = jnp.where(diff <= 0, jnp.exp(lgb * jnp.maximum(-diff, 0.0)), 0.0)
    idx = lax.broadcasted_iota(I32, (c, 1), 0).astype(F32)
    xi_f = jnp.exp(lgf * (idx + 1.0))
    zeta_f = jnp.exp(lgf * (c - 1.0 - idx))
    xi_b = jnp.exp(lgb * (c - idx))
    zeta_b = jnp.exp(lgb * idx)
    cd_f = jnp.exp(lgf * c)
    cd_b = jnp.exp(lgb * c)
    sf_ref[...] = jnp.zeros(sf_ref.shape, F32)
    sb_ref[...] = jnp.zeros(sb_ref.shape, F32)
    nt = (((1,), (1,)), ((), ()))

    fwd = (d_f, xi_f, zeta_f, cd_f, sf_ref, fbuf)
    bwd = (d_b, xi_b, zeta_b, cd_b, sb_ref, bbuf)

    per = min(RET_CHUNKS_PER_TRIP, nc)

    def body(i, carry):
        jobs = []
        for n in range(per):
            jobs += [(fwd, per * i + n), (bwd, nc - 1 - per * i - n)]
        offs = [pl.multiple_of(ci * c, c) for _, ci in jobs]
        qs = [q_ref[0, pl.ds(off, c), :] for off in offs]
        ks = [k_ref[0, pl.ds(off, c), :] for off in offs]
        vs = [v_ref[0, pl.ds(off, c), :] for off in offs]
        scores = [lax.dot_general(q, k, nt, preferred_element_type=F32) for q, k in zip(qs, ks)]
        incs = []
        for (d, _), k, v in zip(jobs, ks, vs):
            kzt = (k.astype(F32) * d[2]).T.astype(BF16)
            incs.append(jnp.dot(kzt, v, preferred_element_type=F32))
        states = [fwd[4][...], bwd[4][...]]
        cross = []
        for n, (d, _) in enumerate(jobs):
            cross.append(jnp.dot(qs[n], states[n].astype(BF16), preferred_element_type=F32))
            states.append(states[n] * d[3] + incs[n])
        for n, ((d, _), off) in enumerate(zip(jobs, offs)):
            a = (scores[n] * d[0]).astype(BF16)
            d[5][pl.ds(off, c), :] = jnp.dot(a, vs[n], preferred_element_type=F32) + cross[n] * d[1]
        fwd[4][...] = states[2 * per]
        bwd[4][...] = states[2 * per + 1]
        return carry

    lax.fori_loop(0, nc // per, body, 0)

    rows = min(1024, s)

    def fin(t, carry):
        off = pl.multiple_of(t * rows, rows)
        r = fbuf[pl.ds(off, rows), :] + bbuf[pl.ds(off, rows), :]
        mu = jnp.mean(r, axis=-1, keepdims=True)
        d = r - mu
        var = jnp.mean(d * d, axis=-1, keepdims=True)
        y = d * lax.rsqrt(var + NORM_EPS) * gng_ref[...] + gnb_ref[...]
        o_ref[0, pl.ds(off, rows), :] = (g_ref[0, pl.ds(off, rows), :].astype(F32) * y).astype(BF16)
        return carry

    lax.fori_loop(0, s // rows, fin, 0)


def _retention(qr, kr, vr, gr, dec, gn_g, gn_b, b, s):
    shp = (b, s, RET_W)
    assert (s // RET_CHUNK) % min(RET_CHUNKS_PER_TRIP, s // RET_CHUNK) == 0, s
    head = pl.BlockSpec((1, s, RET_DIM), lambda bi, h: (bi, 0, h))
    vec = pl.BlockSpec((1, RET_DIM), lambda bi, h: (0, h))
    out = pl.pallas_call(
        _retention_kernel,
        grid=(b, RET_HEADS),
        in_specs=[pl.BlockSpec(memory_space=pltpu.SMEM), head, head, head, head, vec, vec],
        out_specs=head,
        out_shape=jax.ShapeDtypeStruct(shp, BF16),
        scratch_shapes=[
            pltpu.VMEM((s, RET_DIM), F32),
            pltpu.VMEM((s, RET_DIM), F32),
            pltpu.VMEM((RET_DIM, RET_DIM), F32),
            pltpu.VMEM((RET_DIM, RET_DIM), F32),
        ],
        compiler_params=_params(("parallel", "parallel")),
        name="retention",
    )(dec, qr.reshape(shp), kr.reshape(shp), vr.reshape(shp), gr.reshape(shp), gn_g, gn_b)
    return out.reshape(b * s, RET_W)


def _outproj_kernel(oa_ref, or_ref, w_ref, x_ref, mod_ref, pg_ref, fg_ref, rw_ref, rb_ref,
                    xn_ref, hp_ref, ti_ref, tw_ref, rk_ref, cnt_ref, base_ref):
    @pl.when(pl.program_id(0) == 0)
    def _():
        base_ref[...] = jnp.zeros(base_ref.shape, F32)

    m = mod_ref[0]
    tm = x_ref.shape[0]
    nt = (((1,), (1,)), ((), ()))
    halves = [slice(0, tm // 2), slice(tm // 2, tm)]
    ys = [jnp.dot(oa_ref[r, :], w_ref[:ATTN_W, :], preferred_element_type=F32)
          + jnp.dot(or_ref[r, :], w_ref[ATTN_W:, :], preferred_element_type=F32) for r in halves]
    parts = []
    for r, y in zip(halves, ys):
        xn = x_ref[r, :] + m[2:3, :] * _rms(y, pg_ref[...])
        xn_ref[r, :] = xn
        h = _rms(xn, fg_ref[...]) * (1.0 + m[4:5, :]) + m[3:4, :]
        hb = h.astype(BF16)
        u = lax.bitcast_convert_type(hb.astype(F32), U32)
        hp_ref[r, :] = (u[:, :HALF] >> 16) | u[:, HALF:]
        parts.append(lax.dot_general(rw_ref[...], hb, nt, preferred_element_type=F32))
    logits = jnp.concatenate(parts, axis=1) + rb_ref[...]
    e_iota = lax.broadcasted_iota(I32, (N_EXPERTS, tm), 0).astype(F32)
    vals, idxs = [], []
    cur = logits
    for _ in range(TOP_K):
        mx = jnp.max(cur, axis=0, keepdims=True)
        ik = jnp.min(jnp.where(cur == mx, e_iota, float(N_EXPERTS)), axis=0, keepdims=True)
        vals.append(mx)
        idxs.append(ik)
        cur = jnp.where(e_iota == ik, -jnp.inf, cur)
    v = jnp.concatenate(vals, axis=0)
    w = jnp.exp(v - v[0:1, :])
    tw_ref[...] = w / jnp.sum(w, axis=0, keepdims=True)
    ti_ref[...] = jnp.concatenate(idxs, axis=0).astype(I32)

    onehot = jnp.zeros((N_EXPERTS, tm), F32)
    for ik in idxs:
        onehot = onehot + (e_iota == ik).astype(F32)
    before = (lax.broadcasted_iota(I32, (tm, tm), 0) < lax.broadcasted_iota(I32, (tm, tm), 1)).astype(BF16)
    seen = jnp.dot(onehot.astype(BF16), before, preferred_element_type=F32) + base_ref[...]
    ranks = [jnp.sum(jnp.where(e_iota == ik, seen, 0.0), axis=0, keepdims=True) for ik in idxs]
    rk_ref[...] = jnp.concatenate(ranks, axis=0).astype(I32)
    total = base_ref[...] + jnp.sum(onehot, axis=1, keepdims=True)
    base_ref[...] = total
    cnt_ref[...] = jnp.broadcast_to(total, cnt_ref.shape).astype(I32)


def _outproj(oa, orr, w_out_b, x2, batch0, x_batch0, mod_l, post_g, ffn_g, rw_t, rb, b, s):
    n = b * s
    tm = min(ROW_TILE, s)
    nb = s // tm
    row = lambda i: (i, 0)
    full = lambda i: (0, 0)
    col = lambda i: (0, i)
    return pl.pallas_call(
        _outproj_kernel,
        grid=(n // tm,),
        in_specs=[
            pl.BlockSpec((tm, ATTN_W), row),
            pl.BlockSpec((tm, RET_W), row),
            pl.BlockSpec((D_MODEL, D_MODEL), full),
            pl.BlockSpec((tm, D_MODEL), lambda i: (x_batch0 * nb + i, 0)),
            pl.BlockSpec((1, 6, D_MODEL), lambda i: (batch0 + i // nb, 0, 0)),
            pl.BlockSpec((1, D_MODEL), full),
            pl.BlockSpec((1, D_MODEL), full),
            pl.BlockSpec((N_EXPERTS, D_MODEL), full),
            pl.BlockSpec((N_EXPERTS, 1), full),
        ],
        out_specs=[
            pl.BlockSpec((tm, D_MODEL), row),
            pl.BlockSpec((tm, HALF), row),
            pl.BlockSpec((TOP_K, tm), col),
            pl.BlockSpec((TOP_K, tm), col),
            pl.BlockSpec((TOP_K, tm), col),
            pl.BlockSpec((N_EXPERTS, LANES), full),
        ],
        out_shape=[
            jax.ShapeDtypeStruct((n, D_MODEL), F32),
            jax.ShapeDtypeStruct((n, HALF), U32),
            jax.ShapeDtypeStruct((TOP_K, n), I32),
            jax.ShapeDtypeStruct((TOP_K, n), F32),
            jax.ShapeDtypeStruct((TOP_K, n), I32),
            jax.ShapeDtypeStruct((N_EXPERTS, LANES), I32),
        ],
        scratch_shapes=[pltpu.VMEM((N_EXPERTS, 1), F32)],
        compiler_params=_params(("arbitrary",)),
        name="outproj_router",
    )(oa, orr, w_out_b, x2, mod_l, post_g, ffn_g, rw_t, rb)


def _sc_scatter_rows(rows, pos3, n_out):
    n, w = rows.shape
    nchunk, kk, c = pos3.shape
    per_w = nchunk // SC_WORKERS
    mesh = plsc.VectorSubcoreMesh(core_axis_name="c", subcore_axis_name="s")

    @functools.partial(
        pl.kernel, mesh=mesh,
        out_type=jax.ShapeDtypeStruct((n_out, w), rows.dtype),
        scratch_types=[pltpu.VMEM((kk, c), I32), pltpu.VMEM((c, w), rows.dtype)],
    )
    def k(rows_hbm, pos_hbm, out_hbm, idx_v, rows_v):
        wid = lax.axis_index("s") * SC_CORES + lax.axis_index("c")

        @pl.loop(0, per_w)
        def _(i):
            ch = wid * per_w + i
            pltpu.sync_copy(pos_hbm.at[ch], idx_v)
            pltpu.sync_copy(rows_hbm.at[pl.ds(ch * c, c)], rows_v)
            for j in range(kk):
                pltpu.sync_copy(rows_v, out_hbm.at[idx_v.at[j]])

    return k(rows, pos3)


def _sc_gather_rows(table, idx):
    w = table.shape[1]
    b = idx.shape[0]
    c = SC_CHUNK
    per_w = b // (SC_WORKERS * c)
    mesh = plsc.VectorSubcoreMesh(core_axis_name="c", subcore_axis_name="s")

    @functools.partial(
        pl.kernel, mesh=mesh,
        out_type=jax.ShapeDtypeStruct((b, w), table.dtype),
        scratch_types=[pltpu.VMEM((c,), I32), pltpu.VMEM((c, w), table.dtype), pltpu.SemaphoreType.DMA],
    )
    def k(table_hbm, idx_hbm, out_hbm, idx_v, rows_v, sem):
        wid = lax.axis_index("s") * SC_CORES + lax.axis_index("c")

        @pl.loop(0, per_w)
        def _(i):
            base = (wid * per_w + i) * c
            pltpu.sync_copy(idx_hbm.at[pl.ds(base, c)], idx_v)
            pltpu.async_copy(table_hbm.at[idx_v], rows_v, sem).wait()
            pltpu.sync_copy(rows_v, out_hbm.at[pl.ds(base, c)])

    return k(table, idx)


def _experts_kernel(te_ref, nv_ref, xs_ref, wu_ref, bu_ref, wd_ref, bd_ref, ys_ref, wu_b, wd_b):
    i = pl.program_id(0)
    live = i < nv_ref[0]
    new_expert = jnp.logical_or(i == 0, te_ref[i] != te_ref[jnp.maximum(i - 1, 0)])

    @pl.when(jnp.logical_and(live, new_expert))
    def _():
        wu_b[...] = wu_ref[0, 0].astype(BF16)
        wd_b[...] = wd_ref[0, 0].astype(BF16)

    @pl.when(live)
    def _():
        t = xs_ref.shape[0]
        halves = [slice(0, t // 2), slice(t // 2, t)]
        ups = []
        for r in halves:
            w = xs_ref[r, :]
            lo = _unpack_lo(w).astype(BF16)
            hi = _unpack_hi(w).astype(BF16)
            ups.append(jnp.dot(lo, wu_b[:HALF, :], preferred_element_type=F32)
                       + jnp.dot(hi, wu_b[HALF:, :], preferred_element_type=F32))
        for r, up in zip(halves, ups):
            up = up + bu_ref[0, 0]
            glu = jnp.minimum(up[:, :D_FF], SWIGLU_LIMIT)
            lin = jnp.clip(up[:, D_FF:], -SWIGLU_LIMIT, SWIGLU_LIMIT)
            act = glu * jax.nn.sigmoid(SWIGLU_ALPHA * glu) * (lin + 1.0)
            y = jnp.dot(act.astype(BF16), wd_b[...], preferred_element_type=F32) + bd_ref[0, 0]
            ys_ref[r, :] = _pack_bf16_pairs(y)


def _experts(xs, tile_expert, n_valid, layer, wu, bu, wd, bd):
    n_pad = xs.shape[0]
    t = MOE_TILE
    grid_spec = pltpu.PrefetchScalarGridSpec(
        num_scalar_prefetch=2,
        grid=(n_pad // t,),
        in_specs=[
            pl.BlockSpec((t, HALF), lambda i, te, nv: (i, 0)),
            pl.BlockSpec((1, 1, D_MODEL, 2 * D_FF), lambda i, te, nv: (layer, te[i], 0, 0)),
            pl.BlockSpec((1, 1, 1, 2 * D_FF), lambda i, te, nv: (layer, te[i], 0, 0)),
            pl.BlockSpec((1, 1, D_FF, D_MODEL), lambda i, te, nv: (layer, te[i], 0, 0)),
            pl.BlockSpec((1, 1, 1, D_MODEL), lambda i, te, nv: (layer, te[i], 0, 0)),
        ],
        out_specs=pl.BlockSpec((t, HALF), lambda i, te, nv: (i, 0)),
        scratch_shapes=[pltpu.VMEM((D_MODEL, 2 * D_FF), BF16), pltpu.VMEM((D_FF, D_MODEL), BF16)],
    )
    return pl.pallas_call(
        _experts_kernel,
        grid_spec=grid_spec,
        out_shape=jax.ShapeDtypeStruct((n_pad, HALF), U32),
        compiler_params=pltpu.CompilerParams(dimension_semantics=("arbitrary",), vmem_limit_bytes=EXPERTS_VMEM_LIMIT),
        name="experts",
    )(tile_expert, n_valid, xs, wu, bu, wd, bd)


def _combine_kernel(y0_ref, y1_ref, y2_ref, y3_ref, w_ref, x_ref, mod_ref, g_ref, *rest):
    o_ref = rest[-1]
    w = w_ref[...]
    lo = None
    hi = None
    for k, y_ref in enumerate((y0_ref, y1_ref, y2_ref, y3_ref)):
        word = y_ref[...]
        wk = w[:, k:k + 1]
        lo_k = wk * _unpack_lo(word)
        hi_k = wk * _unpack_hi(word)
        lo = lo_k if lo is None else lo + lo_k
        hi = hi_k if hi is None else hi + hi_k
    yf = jnp.concatenate([lo, hi], axis=1)
    m = mod_ref[0]
    o_ref[...] = x_ref[...] + m[5:6, :] * _rms(yf, g_ref[...])


def _combine(yg, tw_t, xn, batch0, mod_l, post_g, b, s, out_batches=None, shared_out=None):
    n = b * s
    tm = min(ROW_TILE, s)
    nb = s // tm
    nblk = n // tm
    row = lambda i: (i, 0)
    full = lambda i: (0, 0)
    y_specs = [pl.BlockSpec((tm, HALF), functools.partial(lambda i, k: (k * nblk + i, 0), k=k)) for k in range(TOP_K)]
    out_rows, out_blk0 = (n, 0) if out_batches is None else (out_batches * s, batch0 * nb)
    extra_specs, extra_args, aliases = [], [], {}
    if shared_out is not None:
        extra_specs, extra_args, aliases = [pl.BlockSpec(memory_space=pl.ANY)], [shared_out], {8: 0}
    return pl.pallas_call(
        _combine_kernel,
        grid=(nblk,),
        in_specs=y_specs + [
            pl.BlockSpec((tm, TOP_K), row),
            pl.BlockSpec((tm, D_MODEL), row),
            pl.BlockSpec((1, 6, D_MODEL), lambda i: (batch0 + i // nb, 0, 0)),
            pl.BlockSpec((1, D_MODEL), full),
        ] + extra_specs,
        out_specs=pl.BlockSpec((tm, D_MODEL), lambda i: (out_blk0 + i, 0)),
        out_shape=jax.ShapeDtypeStruct((out_rows, D_MODEL), F32),
        input_output_aliases=aliases,
        compiler_params=_params(("parallel",)),
        name="combine",
    )(yg, yg, yg, yg, tw_t, xn, mod_l, post_g, *extra_args)


def _route(counts, tidx, rank, n_tiles):
    t = MOE_TILE
    cnt = counts[:, 0]
    padded = ((cnt + t - 1) // t) * t
    ends = jnp.cumsum(padded)
    starts = ends - padded
    experts = jnp.arange(N_EXPERTS, dtype=I32)[:, None, None]
    start_of = jnp.sum(jnp.where(tidx[None] == experts, starts[:, None, None], 0), axis=0)
    pos = start_of + rank
    tile_ends = ends // t
    n_valid = tile_ends[-1]
    tile = jnp.minimum(jnp.arange(n_tiles, dtype=I32), n_valid - 1)
    tile_expert = jnp.minimum(jnp.sum(tile_ends[None, :] <= tile[:, None], axis=1), N_EXPERTS - 1).astype(I32)
    return pos.astype(I32), tile_expert, n_valid.reshape(1).astype(I32)


def _value_range_scale(mod_l, pre_g, w_in_l):
    d = pre_g.shape[0]
    shift, scale = mod_l[:, 0, :], mod_l[:, 1, :]
    h_norm = d ** 0.5 * jnp.max(jnp.abs(pre_g[None, :] * (1.0 + scale)), axis=1) + jnp.linalg.norm(shift, axis=1)
    w_v = w_in_l[:, ATTN_W + KV_W:ATTN_W + 2 * KV_W]
    bound = 1.02 * h_norm * jnp.max(jnp.linalg.norm(w_v, axis=0))
    v_scale = V_RANGE / jnp.maximum(bound, 1e-30)
    lanes = lambda t: jnp.broadcast_to(t[:, None, None], (mod_l.shape[0], 1, LANES))
    return lanes(v_scale), lanes(1.0 / v_scale)


def kernel(x, c, ada_w, ada_b, pre_mix_g, post_mix_g, w_in, q_norm_g, k_norm_g, ret_decay_fwd, ret_decay_bwd, ret_gn_g, ret_gn_b, w_out, pre_ffn_g, post_ffn_g, router_w, router_b, exp_w_up, exp_b_up, exp_w_down, exp_b_down):
    b, s, d = x.shape
    depth = ada_w.shape[0]
    groups = 2 if b % 2 == 0 else 1
    bg = b // groups
    n = bg * s
    n_asg = n * TOP_K
    n_pad = n_asg + N_EXPERTS * MOE_TILE
    n_tiles = n_pad // MOE_TILE

    mod = _modulation(c, ada_w, ada_b)
    ropes = _rope_tables(s, HEAD_DIM) + _rope_tables(s, RET_DIM)
    bu = exp_b_up.reshape(depth, N_EXPERTS, 1, 2 * D_FF)
    bd = exp_b_down.reshape(depth, N_EXPERTS, 1, d)
    xs_in = [x.reshape(b * s, d)] * groups
    x_b0 = [g * bg for g in range(groups)]
    out = None
    for l in range(depth):
        mod_l = mod[l].reshape(b, 6, d)
        qg = jnp.tile(q_norm_g[l], LANES // HEAD_DIM).reshape(1, LANES)
        kg = jnp.tile(k_norm_g[l], LANES // HEAD_DIM).reshape(1, LANES)
        v_scale, fac = _value_range_scale(mod_l, pre_mix_g[l], w_in[l])
        w_in_b = w_in[l].astype(BF16)
        w_out_b = w_out[l].astype(BF16)
        rw_t = router_w[l].T.astype(BF16)
        dec = jnp.stack([ret_decay_fwd[l], ret_decay_bwd[l]]).astype(F32)
        last = l + 1 == depth
        proj = [_inproj(xs_in[g], g * bg, x_b0[g], mod_l, pre_mix_g[l].reshape(1, d), w_in_b, ropes, qg, kg, v_scale,
                        bg, s) for g in range(groups)]
        oa = [_attention(p[0], p[1], p[2], fac, g * bg, bg, s) for g, p in enumerate(proj)]
        orr = [_retention(p[3], p[4], p[5], p[6], dec, ret_gn_g[l].reshape(1, RET_W), ret_gn_b[l].reshape(1, RET_W),
                          bg, s) for p in proj]
        routed = []
        for g in range(groups):
            xn, hp, tidx, tw, rank, counts = _outproj(
                oa[g], orr[g], w_out_b, xs_in[g], g * bg, x_b0[g], mod_l, post_mix_g[l].reshape(1, d),
                pre_ffn_g[l].reshape(1, d), rw_t, router_b[l].reshape(N_EXPERTS, 1), bg, s)
            pos, tile_expert, n_valid = _route(counts, tidx, rank, n_tiles)
            pos3 = pos.reshape(TOP_K, n // SC_CHUNK, SC_CHUNK).transpose(1, 0, 2)
            routed.append((xn, tw, pos, tile_expert, n_valid, _sc_scatter_rows(hp, pos3, n_pad)))
        yg = []
        for xn, tw, pos, tile_expert, n_valid, xs in routed:
            ys = _experts(xs, tile_expert, n_valid, l, exp_w_up, bu, exp_w_down, bd)
            yg.append(_sc_gather_rows(ys, pos.reshape(n_asg)))
        nxt = []
        for g, (xn, tw, *_) in enumerate(routed):
            if last:
                out = _combine(yg[g], tw.T, xn, g * bg, mod_l, post_ffn_g[l].reshape(1, d), bg, s,
                               out_batches=b, shared_out=out)
            else:
                nxt.append(_combine(yg[g], tw.T, xn, g * bg, mod_l, post_ffn_g[l].reshape(1, d), bg, s))
        xs_in, x_b0 = nxt, [0] * groups
    return out.reshape(b, s, d)
```

```python
import functools

import numpy as np
import jax
import jax.numpy as jnp
from jax import lax
from jax.experimental import pallas as pl
from jax.experimental.pallas import tpu as pltpu
from jax.experimental.pallas import tpu_sc as plsc

F32 = jnp.float32
BF16 = jnp.bfloat16
U32 = jnp.uint32
I32 = jnp.int32
FP8 = jnp.float8_e4m3fn

D_MODEL = 1024
GRID_W = 64
ROPE_THETA = 10000.0
HEAD_DIM = 64
Q_HEADS = 8
KV_HEADS = 2
RET_HEADS = 4
RET_DIM = 128
RET_CHUNK = 128
ATTN_W = Q_HEADS * HEAD_DIM
KV_W = KV_HEADS * HEAD_DIM
RET_W = RET_HEADS * RET_DIM
IN_W = ATTN_W + 2 * KV_W + 4 * RET_W
N_EXPERTS = 32
TOP_K = 4
D_FF = D_MODEL
SWIGLU_LIMIT = 7.0
SWIGLU_ALPHA = 1.702
NORM_EPS = 1e-6
Q_SCALE = HEAD_DIM ** -0.5 * float(np.log2(np.e))
V_RANGE = 224.0
P_SHIFT = 8.0
HALF = D_MODEL // 2
LANES = 128

ROW_TILE = 512
Q_TILE = 128
Q_TILES_PER_TRIP = 2
KV_TILE = 1024
MOE_TILE = 512
RET_CHUNKS_PER_TRIP = 8
VMEM_LIMIT = 48 * 1024 * 1024
EXPERTS_VMEM_LIMIT = 56 * 1024 * 1024

SC_CORES = 2
SC_SUBCORES = 16
SC_WORKERS = SC_CORES * SC_SUBCORES
SC_CHUNK = 64


def _params(sem):
    return pltpu.CompilerParams(dimension_semantics=sem, vmem_limit_bytes=VMEM_LIMIT)


def _rms(x, g):
    return x * lax.rsqrt(jnp.mean(x * x, axis=-1, keepdims=True) + NORM_EPS) * g


def _pack_bf16_pairs(y):
    u = lax.bitcast_convert_type(y.astype(BF16).astype(F32), U32)
    return (u[:, :HALF] >> 16) | u[:, HALF:]


def _unpack_lo(w):
    return lax.bitcast_convert_type(w << 16, F32)


def _unpack_hi(w):
    return lax.bitcast_convert_type(w & jnp.uint32(0xFFFF0000), F32)


def _mod_kernel(c_ref, w_ref, b_ref, o_ref):
    c = c_ref[...]
    cond = c * jax.nn.sigmoid(c)
    o_ref[0] = jnp.dot(cond, w_ref[0], preferred_element_type=F32, precision=lax.Precision.HIGHEST) + b_ref[0]


def _modulation(c, ada_w, ada_b):
    depth, d, w6 = ada_w.shape
    b = c.shape[0]
    tn = 1536
    return pl.pallas_call(
        _mod_kernel,
        grid=(depth, w6 // tn),
        in_specs=[
            pl.BlockSpec((b, d), lambda l, j: (0, 0)),
            pl.BlockSpec((1, d, tn), lambda l, j: (l, 0, j)),
            pl.BlockSpec((1, 1, tn), lambda l, j: (l, 0, j)),
        ],
        out_specs=pl.BlockSpec((1, b, tn), lambda l, j: (l, 0, j)),
        out_shape=jax.ShapeDtypeStruct((depth, b, w6), F32),
        compiler_params=_params(("arbitrary", "arbitrary")),
        name="modulation",
    )(c, ada_w, ada_b.reshape(depth, 1, w6))


def _rope_tables(s, head_dim):
    quarter = head_dim // 4
    t = jnp.arange(s, dtype=F32)
    row = jnp.floor(t / GRID_W)
    col = t - row * GRID_W
    inv_freq = ROPE_THETA ** (-jnp.arange(quarter, dtype=F32) / quarter)
    lane = np.arange(LANES) % head_dim
    use_col = lane >= head_dim // 2
    second = (lane % (head_dim // 2)) >= quarter
    freq = inv_freq[lane % quarter]
    pos = jnp.where(use_col[None, :], col[:, None], row[:, None])
    ang = pos * freq[None, :]
    sign = jnp.where(second, 1.0, -1.0).astype(F32)
    return jnp.cos(ang), jnp.sin(ang) * sign[None, :]


def _rope(z, cos, sin_signed, quarter):
    lane = lax.broadcasted_iota(I32, z.shape, 1)
    first = (lane % (2 * quarter)) < quarter
    partner = jnp.where(first, pltpu.roll(z, LANES - quarter, 1), pltpu.roll(z, quarter, 1))
    return z * cos + partner * sin_signed


def _inproj_kernel(x_ref, mod_ref, g_ref, w_ref, ca_ref, sa_ref, cr_ref, sr_ref, qg_ref, kg_ref, vs_ref,
                   qa_ref, kd_ref, vt_ref, qr_ref, kr_ref, vr_ref, gr_ref):
    x = x_ref[...]
    m = mod_ref[0]
    h = _rms(x, g_ref[...]) * (1.0 + m[1:2, :]) + m[0:1, :]
    hb = h.astype(BF16)
    tm = x.shape[0]

    def proj(c0):
        z = jnp.dot(hb, w_ref[:, c0:c0 + 2 * LANES], preferred_element_type=F32)
        return z[:, :LANES], z[:, LANES:]

    ri = lax.broadcasted_iota(I32, (LANES, LANES), 0) // HEAD_DIM
    ci = lax.broadcasted_iota(I32, (LANES, LANES), 1) // HEAD_DIM
    head_ones = (ri == ci).astype(BF16)
    ca, sa = ca_ref[...], sa_ref[...]
    cr, sr = cr_ref[...], sr_ref[...]
    lane = lax.broadcasted_iota(I32, (tm, LANES), 1)
    low = lane < HEAD_DIM

    def head_norm_rope(z, g):
        ss = jnp.dot((z * z).astype(BF16), head_ones, preferred_element_type=F32)
        zn = z * lax.rsqrt(ss * (1.0 / HEAD_DIM) + NORM_EPS) * g
        return _rope(zn, ca, sa, HEAD_DIM // 4)

    def store_pair(ref, j, fn):
        def finish(zs):
            for i, z in enumerate(zs):
                ref[:, (2 * j + i) * LANES:(2 * j + i + 1) * LANES] = fn(z).astype(BF16)
        return finish

    def store_kv(zs):
        k, v = zs
        k = head_norm_rope(k, kg_ref[...])
        k_sw = pltpu.roll(k, HEAD_DIM, 1)
        v = v * vs_ref[0]
        kd_ref[0, 0] = jnp.where(low, k, k_sw).astype(BF16)
        kd_ref[0, 1] = jnp.where(low, k_sw, k).astype(BF16)
        vt_ref[0, 0] = jnp.where(low, v, 1.0).T.astype(FP8)
        vt_ref[0, 1] = jnp.where(low, pltpu.roll(v, HEAD_DIM, 1), 1.0).T.astype(FP8)

    jobs = [(2 * j * LANES, store_pair(qa_ref, j, lambda z: head_norm_rope(z, qg_ref[...]) * Q_SCALE))
            for j in range(ATTN_W // (2 * LANES))]
    jobs.append((ATTN_W, store_kv))
    base = ATTN_W + 2 * KV_W
    for j in range(RET_W // (2 * LANES)):
        off = 2 * j * LANES
        jobs += [
            (base + off, store_pair(qr_ref, j, lambda z: _rope(z, cr, sr, RET_DIM // 4))),
            (base + RET_W + off, store_pair(kr_ref, j, lambda z: _rope(z, cr, sr, RET_DIM // 4) * (RET_DIM ** -0.5))),
            (base + 2 * RET_W + off, store_pair(vr_ref, j, lambda z: z)),
            (base + 3 * RET_W + off, store_pair(gr_ref, j, lambda z: z * jax.nn.sigmoid(z))),
        ]
    zs = proj(jobs[0][0])
    for i, (_, finish) in enumerate(jobs):
        nxt = proj(jobs[i + 1][0]) if i + 1 < len(jobs) else None
        finish(zs)
        zs = nxt


def _inproj(x2, batch0, x_batch0, mod_l, pre_g, w_in_b, ropes, qg, kg, v_scale, b, s):
    n = b * s
    tm = min(ROW_TILE, s)
    nb = s // tm
    ca, sa, cr, sr = ropes
    row = lambda i: (i, 0)
    full = lambda i: (0, 0)
    rope_spec = pl.BlockSpec((tm, LANES), lambda i: (i % nb, 0))
    wide = jax.ShapeDtypeStruct((n, RET_W), BF16)
    return pl.pallas_call(
        _inproj_kernel,
        grid=(n // tm,),
        in_specs=[
            pl.BlockSpec((tm, D_MODEL), lambda i: (x_batch0 * nb + i, 0)),
            pl.BlockSpec((1, 6, D_MODEL), lambda i: (batch0 + i // nb, 0, 0)),
            pl.BlockSpec((1, D_MODEL), full),
            pl.BlockSpec((D_MODEL, IN_W), full),
            rope_spec, rope_spec, rope_spec, rope_spec,
            pl.BlockSpec((1, LANES), full),
            pl.BlockSpec((1, LANES), full),
            pl.BlockSpec((1, 1, LANES), lambda i: (batch0 + i // nb, 0, 0)),
        ],
        out_specs=[
            pl.BlockSpec((tm, ATTN_W), row),
            pl.BlockSpec((1, KV_HEADS, tm, LANES), lambda i: (i // nb, 0, i % nb, 0)),
            pl.BlockSpec((1, KV_HEADS, LANES, tm), lambda i: (i // nb, 0, 0, i % nb)),
            pl.BlockSpec((tm, RET_W), row),
            pl.BlockSpec((tm, RET_W), row),
            pl.BlockSpec((tm, RET_W), row),
            pl.BlockSpec((tm, RET_W), row),
        ],
        out_shape=[
            jax.ShapeDtypeStruct((n, ATTN_W), BF16),
            jax.ShapeDtypeStruct((b, KV_HEADS, s, LANES), BF16),
            jax.ShapeDtypeStruct((b, KV_HEADS, LANES, s), FP8),
            wide, wide, wide, wide,
        ],
        compiler_params=_params(("parallel",)),
        name="inproj",
    )(x2, mod_l, pre_g, w_in_b, ca, sa, cr, sr, qg, kg, v_scale)


def _attn_kernel(q_ref, kd_ref, vt_ref, fac_ref, o_ref, q_s, qn_s, m_ref, acc_ref, s0_ref, s1_ref,
                 *, tq, tk, tiles):
    s = kd_ref.shape[2]
    nk = s // tk
    nq = s // tq
    pairs = (Q_HEADS // KV_HEADS) // 2
    top = lax.broadcasted_iota(I32, (LANES, tq), 0) < HEAD_DIM
    unscale = jnp.concatenate([fac_ref[0]] * (2 * tq // LANES), axis=1)

    chains = tiles * pairs

    def load_q(trip, dst):
        for u in range(tiles):
            rows = pl.ds(pl.multiple_of((trip * tiles + u) * tq, tq), tq)
            for p in range(pairs):
                qt = q_ref[rows, p * LANES:(p + 1) * LANES].astype(F32).T
                dst[u * pairs + p, :, :tq] = jnp.where(top, qt, 0.0).astype(BF16)
                dst[u * pairs + p, :, tq:] = jnp.where(top, 0.0, qt).astype(BF16)

    def keys(j):
        return kd_ref[0, 0, j * tk:(j + 1) * tk, :]

    def step(j, cur_ref, nxt_ref, q_next, j_next):
        vt = vt_ref[0, 0, :, j * tk:(j + 1) * tk]
        kd = keys(j_next)
        nxt_ref[0] = jnp.dot(kd, q_next[0], preferred_element_type=F32)
        for c in range(chains):
            sc = cur_ref[c]
            m_old = m_ref[c]
            m_new = jnp.maximum(m_old, jnp.max(sc, axis=0, keepdims=True))
            pt = jnp.exp2(sc - (m_new - P_SHIFT))
            alpha = jnp.exp2(m_old - m_new)
            if c + 1 < chains:
                nxt_ref[c + 1] = jnp.dot(kd, q_next[c + 1], preferred_element_type=F32)
            acc_ref[c] = alpha * acc_ref[c] + jnp.dot(vt, pt.astype(FP8), preferred_element_type=F32)
            m_ref[c] = m_new

    bufs = (s0_ref, s1_ref)
    trips = nq // tiles
    load_q(0, q_s)
    for c in range(chains):
        s0_ref[c] = jnp.dot(keys(0), q_s[c], preferred_element_type=F32)

    def body(trip, carry):
        load_q(trip, q_s)
        load_q(jnp.minimum(trip + 1, trips - 1), qn_s)
        m_ref[...] = jnp.full(m_ref.shape, -jnp.inf, F32)
        acc_ref[...] = jnp.zeros(acc_ref.shape, F32)
        for j in range(nk):
            last = j + 1 == nk
            step(j, bufs[j % len(bufs)], bufs[(j + 1) % len(bufs)], qn_s if last else q_s, 0 if last else j + 1)
        for c in range(chains):
            u, p = divmod(c, pairs)
            rows = pl.ds(pl.multiple_of((trip * tiles + u) * tq, tq), tq)
            acc = acc_ref[c]
            o_t = acc[:HEAD_DIM] / acc[HEAD_DIM:] * unscale
            both = jnp.concatenate([o_t[:, :tq], o_t[:, tq:]], axis=0)
            o_ref[rows, p * LANES:(p + 1) * LANES] = both.T.astype(BF16)
        return carry

    lax.fori_loop(0, trips, body, 0)


def _attention(qa, kd, vt, fac, batch0, b, s):
    n = qa.shape[0]
    tq = Q_TILE
    tk = min(KV_TILE, s // 2)
    tiles = Q_TILES_PER_TRIP
    assert tq == LANES and s % (tiles * tq) == 0 and s % (2 * tk) == 0 and tk % LANES == 0, (s, tq, tk)
    gw = ATTN_W // KV_HEADS
    chains = tiles * ((Q_HEADS // KV_HEADS) // 2)
    return pl.pallas_call(
        functools.partial(_attn_kernel, tq=tq, tk=tk, tiles=tiles),
        grid=(b, KV_HEADS),
        in_specs=[
            pl.BlockSpec((s, gw), lambda bi, h: (bi, h)),
            pl.BlockSpec((1, 1, s, LANES), lambda bi, h: (bi, h, 0, 0)),
            pl.BlockSpec((1, 1, LANES, s), lambda bi, h: (bi, h, 0, 0)),
            pl.BlockSpec((1, 1, LANES), lambda bi, h: (batch0 + bi, 0, 0)),
        ],
        out_specs=pl.BlockSpec((s, gw), lambda bi, h: (bi, h)),
        out_shape=jax.ShapeDtypeStruct((n, ATTN_W), BF16),
        scratch_shapes=[
            pltpu.VMEM((chains, LANES, 2 * tq), BF16),
            pltpu.VMEM((chains, LANES, 2 * tq), BF16),
            pltpu.VMEM((chains, 1, 2 * tq), F32),
            pltpu.VMEM((chains, LANES, 2 * tq), F32),
            pltpu.VMEM((chains, tk, 2 * tq), F32),
            pltpu.VMEM((chains, tk, 2 * tq), F32),
        ],
        compiler_params=_params(("parallel", "parallel")),
        name="attention",
    )(qa, kd, vt, fac)


def _retention_kernel(dec_ref, q_ref, k_ref, v_ref, g_ref, gng_ref, gnb_ref, o_ref,
                      fbuf, bbuf, sf_ref, sb_ref):
    h = pl.program_id(1)
    c = RET_CHUNK
    s = q_ref.shape[1]
    nc = s // c
    lgf = jnp.full((1, 1), dec_ref[0, h], F32)
    lgb = jnp.full((1, 1), dec_ref[1, h], F32)
    ii = lax.broadcasted_iota(I32, (c, c), 0)
    jj = lax.broadcasted_iota(I32, (c, c), 1)
    diff = (ii - jj).astype(F32)
    d_f = jnp.where(diff >= 0, jnp.exp(lgf * jnp.maximum(diff, 0.0)), 0.0)
    d_b = jnp.where(diff <= 0, jnp.exp(lgb * jnp.maximum(-diff, 0.0)), 0.0)
    idx = lax.broadcasted_iota(I32, (c, 1), 0).astype(F32)
    xi_f = jnp.exp(lgf * (idx + 1.0))
    zeta_f = jnp.exp(lgf * (c - 1.0 - idx))
    xi_b = jnp.exp(lgb * (c - idx))
    zeta_b = jnp.exp(lgb * idx)
    cd_f = jnp.exp(lgf * c)
    cd_b = jnp.exp(lgb * c)
    sf_ref[...] = jnp.zeros(sf_ref.shape, F32)
    sb_ref[...] = jnp.zeros(sb_ref.shape, F32)
    nt = (((1,), (1,)), ((), ()))

    fwd = (d_f, xi_f, zeta_f, cd_f, sf_ref, fbuf)
    bwd = (d_b, xi_b, zeta_b, cd_b, sb_ref, bbuf)

    per = min(RET_CHUNKS_PER_TRIP, nc)

    def body(i, carry):
        jobs = []
        for n in range(per):
            jobs += [(fwd, per * i + n), (bwd, nc - 1 - per * i - n)]
        offs = [pl.multiple_of(ci * c, c) for _, ci in jobs]
        qs = [q_ref[0, pl.ds(off, c), :] for off in offs]
        ks = [k_ref[0, pl.ds(off, c), :] for off in offs]
        vs = [v_ref[0, pl.ds(off, c), :] for off in offs]
        scores = [lax.dot_general(q, k, nt, preferred_element_type=F32) for q, k in zip(qs, ks)]
        incs = []
        for (d, _), k, v in zip(jobs, ks, vs):
            kzt = (k.astype(F32) * d[2]).T.astype(BF16)
            incs.append(jnp.dot(kzt, v, preferred_element_type=F32))
        states = [fwd[4][...], bwd[4][...]]
        cross = []
        for n, (d, _) in enumerate(jobs):
            cross.append(jnp.dot(qs[n], states[n].astype(BF16), preferred_element_type=F32))
            states.append(states[n] * d[3] + incs[n])
        for n, ((d, _), off) in enumerate(zip(jobs, offs)):
            a = (scores[n] * d[0]).astype(BF16)
            d[5][pl.ds(off, c), :] = jnp.dot(a, vs[n], preferred_element_type=F32) + cross[n] * d[1]
        fwd[4][...] = states[2 * per]
        bwd[4][...] = states[2 * per + 1]
        return carry

    lax.fori_loop(0, nc // per, body, 0)

    rows = min(1024, s)

    def fin(t, carry):
        off = pl.multiple_of(t * rows, rows)
        r = fbuf[pl.ds(off, rows), :] + bbuf[pl.ds(off, rows), :]
        mu = jnp.mean(r, axis=-1, keepdims=True)
        d = r - mu
        var = jnp.mean(d * d, axis=-1, keepdims=True)
        y = d * lax.rsqrt(var + NORM_EPS) * gng_ref[...] + gnb_ref[...]
        o_ref[0, pl.ds(off, rows), :] = (g_ref[0, pl.ds(off, rows), :].astype(F32) * y).astype(BF16)
        return carry

    lax.fori_loop(0, s // rows, fin, 0)


def _retention(qr, kr, vr, gr, dec, gn_g, gn_b, b, s):
    shp = (b, s, RET_W)
    assert (s // RET_CHUNK) % min(RET_CHUNKS_PER_TRIP, s // RET_CHUNK) == 0, s
    head = pl.BlockSpec((1, s, RET_DIM), lambda bi, h: (bi, 0, h))
    vec = pl.BlockSpec((1, RET_DIM), lambda bi, h: (0, h))
    out = pl.pallas_call(
        _retention_kernel,
        grid=(b, RET_HEADS),
        in_specs=[pl.BlockSpec(memory_space=pltpu.SMEM), head, head, head, head, vec, vec],
        out_specs=head,
        out_shape=jax.ShapeDtypeStruct(shp, BF16),
        scratch_shapes=[
            pltpu.VMEM((s, RET_DIM), F32),
            pltpu.VMEM((s, RET_DIM), F32),
            pltpu.VMEM((RET_DIM, RET_DIM), F32),
            pltpu.VMEM((RET_DIM, RET_DIM), F32),
        ],
        compiler_params=_params(("parallel", "parallel")),
        name="retention",
    )(dec, qr.reshape(shp), kr.reshape(shp), vr.reshape(shp), gr.reshape(shp), gn_g, gn_b)
    return out.reshape(b * s, RET_W)


def _outproj_kernel(oa_ref, or_ref, w_ref, x_ref, mod_ref, pg_ref, fg_ref, rw_ref, rb_ref,
                    xn_ref, hp_ref, ti_ref, tw_ref, rk_ref, cnt_ref, base_ref):
    @pl.when(pl.program_id(0) == 0)
    def _():
        base_ref[...] = jnp.zeros(base_ref.shape, F32)

    m = mod_ref[0]
    tm = x_ref.shape[0]
    nt = (((1,), (1,)), ((), ()))
    halves = [slice(0, tm // 2), slice(tm // 2, tm)]
    ys = [jnp.dot(oa_ref[r, :], w_ref[:ATTN_W, :], preferred_element_type=F32)
          + jnp.dot(or_ref[r, :], w_ref[ATTN_W:, :], preferred_element_type=F32) for r in halves]
    parts = []
    for r, y in zip(halves, ys):
        xn = x_ref[r, :] + m[2:3, :] * _rms(y, pg_ref[...])
        xn_ref[r, :] = xn
        h = _rms(xn, fg_ref[...]) * (1.0 + m[4:5, :]) + m[3:4, :]
        hb = h.astype(BF16)
        u = lax.bitcast_convert_type(hb.astype(F32), U32)
        hp_ref[r, :] = (u[:, :HALF] >> 16) | u[:, HALF:]
        parts.append(lax.dot_general(rw_ref[...], hb, nt, preferred_element_type=F32))
    logits = jnp.concatenate(parts, axis=1) + rb_ref[...]
    e_iota = lax.broadcasted_iota(I32, (N_EXPERTS, tm), 0).astype(F32)
    vals, idxs = [], []
    cur = logits
    for _ in range(TOP_K):
        mx = jnp.max(cur, axis=0, keepdims=True)
        ik = jnp.min(jnp.where(cur == mx, e_iota, float(N_EXPERTS)), axis=0, keepdims=True)
        vals.append(mx)
        idxs.append(ik)
        cur = jnp.where(e_iota == ik, -jnp.inf, cur)
    v = jnp.concatenate(vals, axis=0)
    w = jnp.exp(v - v[0:1, :])
    w = w / jnp.sum(w, axis=0, keepdims=True)
    tw_ref[...] = jnp.concatenate([w, jnp.zeros_like(w)], axis=0).T
    ti_ref[...] = jnp.concatenate(idxs, axis=0).astype(I32)

    onehot = jnp.zeros((N_EXPERTS, tm), F32)
    for ik in idxs:
        onehot = onehot + (e_iota == ik).astype(F32)
    before = (lax.broadcasted_iota(I32, (tm, tm), 0) < lax.broadcasted_iota(I32, (tm, tm), 1)).astype(BF16)
    seen = jnp.dot(onehot.astype(BF16), before, preferred_element_type=F32) + base_ref[...]
    ranks = [jnp.sum(jnp.where(e_iota == ik, seen, 0.0), axis=0, keepdims=True) for ik in idxs]
    rk_ref[...] = jnp.concatenate(ranks, axis=0).astype(I32)
    total = base_ref[...] + jnp.sum(onehot, axis=1, keepdims=True)
    base_ref[...] = total
    cnt_ref[...] = jnp.broadcast_to(total, cnt_ref.shape).astype(I32)


def _outproj(oa, orr, w_out_b, x2, batch0, x_batch0, mod_l, post_g, ffn_g, rw_t, rb, b, s):
    n = b * s
    tm = min(ROW_TILE, s)
    nb = s // tm
    row = lambda i: (i, 0)
    full = lambda i: (0, 0)
    col = lambda i: (0, i)
    return pl.pallas_call(
        _outproj_kernel,
        grid=(n // tm,),
        in_specs=[
            pl.BlockSpec((tm, ATTN_W), row),
            pl.BlockSpec((tm, RET_W), row),
            pl.BlockSpec((D_MODEL, D_MODEL), full),
            pl.BlockSpec((tm, D_MODEL), lambda i: (x_batch0 * nb + i, 0)),
            pl.BlockSpec((1, 6, D_MODEL), lambda i: (batch0 + i // nb, 0, 0)),
            pl.BlockSpec((1, D_MODEL), full),
            pl.BlockSpec((1, D_MODEL), full),
            pl.BlockSpec((N_EXPERTS, D_MODEL), full),
            pl.BlockSpec((N_EXPERTS, 1), full),
        ],
        out_specs=[
            pl.BlockSpec((tm, D_MODEL), row),
            pl.BlockSpec((tm, HALF), row),
            pl.BlockSpec((TOP_K, tm), col),
            pl.BlockSpec((tm, 2 * TOP_K), row),
            pl.BlockSpec((TOP_K, tm), col),
            pl.BlockSpec((N_EXPERTS, LANES), full),
        ],
        out_shape=[
            jax.ShapeDtypeStruct((n, D_MODEL), F32),
            jax.ShapeDtypeStruct((n, HALF), U32),
            jax.ShapeDtypeStruct((TOP_K, n), I32),
            jax.ShapeDtypeStruct((n, 2 * TOP_K), F32),
            jax.ShapeDtypeStruct((TOP_K, n), I32),
            jax.ShapeDtypeStruct((N_EXPERTS, LANES), I32),
        ],
        scratch_shapes=[pltpu.VMEM((N_EXPERTS, 1), F32)],
        compiler_params=_params(("arbitrary",)),
        name="outproj_router",
    )(oa, orr, w_out_b, x2, mod_l, post_g, ffn_g, rw_t, rb)


def _sc_scatter_rows(rows, pos3, n_out):
    n, w = rows.shape
    nchunk, kk, c = pos3.shape
    per_w = nchunk // SC_WORKERS
    mesh = plsc.VectorSubcoreMesh(core_axis_name="c", subcore_axis_name="s")

    @functools.partial(
        pl.kernel, mesh=mesh,
        out_type=jax.ShapeDtypeStruct((n_out, w), rows.dtype),
        scratch_types=[pltpu.VMEM((kk, c), I32), pltpu.VMEM((c, w), rows.dtype)],
    )
    def k(rows_hbm, pos_hbm, out_hbm, idx_v, rows_v):
        wid = lax.axis_index("s") * SC_CORES + lax.axis_index("c")

        @pl.loop(0, per_w)
        def _(i):
            ch = wid * per_w + i
            pltpu.sync_copy(pos_hbm.at[ch], idx_v)
            pltpu.sync_copy(rows_hbm.at[pl.ds(ch * c, c)], rows_v)
            for j in range(kk):
                pltpu.sync_copy(rows_v, out_hbm.at[idx_v.at[j]])

    return k(rows, pos3)


def _sc_gather_rows(table, idx):
    w = table.shape[1]
    b = idx.shape[0]
    c = SC_CHUNK
    per_w = b // (SC_WORKERS * c)
    mesh = plsc.VectorSubcoreMesh(core_axis_name="c", subcore_axis_name="s")

    @functools.partial(
        pl.kernel, mesh=mesh,
        out_type=jax.ShapeDtypeStruct((b, w), table.dtype),
        scratch_types=[pltpu.VMEM((c,), I32), pltpu.VMEM((c, w), table.dtype), pltpu.SemaphoreType.DMA],
    )
    def k(table_hbm, idx_hbm, out_hbm, idx_v, rows_v, sem):
        wid = lax.axis_index("s") * SC_CORES + lax.axis_index("c")

        @pl.loop(0, per_w)
        def _(i):
            base = (wid * per_w + i) * c
            pltpu.sync_copy(idx_hbm.at[pl.ds(base, c)], idx_v)
            pltpu.async_copy(table_hbm.at[idx_v], rows_v, sem).wait()
            pltpu.sync_copy(rows_v, out_hbm.at[pl.ds(base, c)])

    return k(table, idx)


def _experts_kernel(te_ref, nv_ref, xs_ref, wu_ref, bu_ref, wd_ref, bd_ref, ys_ref, wu_b, wd_b):
    i = pl.program_id(0)
    live = i < nv_ref[0]
    new_expert = jnp.logical_or(i == 0, te_ref[i] != te_ref[jnp.maximum(i - 1, 0)])

    @pl.when(jnp.logical_and(live, new_expert))
    def _():
        wu_b[...] = wu_ref[0, 0].astype(BF16)
        wd_b[...] = wd_ref[0, 0].astype(BF16)

    @pl.when(live)
    def _():
        t = xs_ref.shape[0]
        halves = [slice(0, t // 2), slice(t // 2, t)]
        ups = []
        for r in halves:
            w = xs_ref[r, :]
            lo = _unpack_lo(w).astype(BF16)
            hi = _unpack_hi(w).astype(BF16)
            ups.append(jnp.dot(lo, wu_b[:HALF, :], preferred_element_type=F32)
                       + jnp.dot(hi, wu_b[HALF:, :], preferred_element_type=F32))
        for r, up in zip(halves, ups):
            up = up + bu_ref[0, 0]
            glu = jnp.minimum(up[:, :D_FF], SWIGLU_LIMIT)
            lin = jnp.clip(up[:, D_FF:], -SWIGLU_LIMIT, SWIGLU_LIMIT)
            act = glu * jax.nn.sigmoid(SWIGLU_ALPHA * glu) * (lin + 1.0)
            y = jnp.dot(act.astype(BF16), wd_b[...], preferred_element_type=F32) + bd_ref[0, 0]
            ys_ref[r, :] = _pack_bf16_pairs(y)


def _experts(xs, tile_expert, n_valid, layer, wu, bu, wd, bd):
    n_pad = xs.shape[0]
    t = MOE_TILE
    grid_spec = pltpu.PrefetchScalarGridSpec(
        num_scalar_prefetch=2,
        grid=(n_pad // t,),
        in_specs=[
            pl.BlockSpec((t, HALF), lambda i, te, nv: (i, 0)),
            pl.BlockSpec((1, 1, D_MODEL, 2 * D_FF), lambda i, te, nv: (layer, te[i], 0, 0)),
            pl.BlockSpec((1, 1, 1, 2 * D_FF), lambda i, te, nv: (layer, te[i], 0, 0)),
            pl.BlockSpec((1, 1, D_FF, D_MODEL), lambda i, te, nv: (layer, te[i], 0, 0)),
            pl.BlockSpec((1, 1, 1, D_MODEL), lambda i, te, nv: (layer, te[i], 0, 0)),
        ],
        out_specs=pl.BlockSpec((t, HALF), lambda i, te, nv: (i, 0)),
        scratch_shapes=[pltpu.VMEM((D_MODEL, 2 * D_FF), BF16), pltpu.VMEM((D_FF, D_MODEL), BF16)],
    )
    return pl.pallas_call(
        _experts_kernel,
        grid_spec=grid_spec,
        out_shape=jax.ShapeDtypeStruct((n_pad, HALF), U32),
        compiler_params=pltpu.CompilerParams(dimension_semantics=("arbitrary",), vmem_limit_bytes=EXPERTS_VMEM_LIMIT),
        name="experts",
    )(tile_expert, n_valid, xs, wu, bu, wd, bd)


def _combine_kernel(y0_ref, y1_ref, y2_ref, y3_ref, w_ref, x_ref, mod_ref, g_ref, *rest):
    o_ref = rest[-1]
    w = w_ref[...]
    lo = None
    hi = None
    for k, y_ref in enumerate((y0_ref, y1_ref, y2_ref, y3_ref)):
        word = y_ref[...]
        wk = w[:, k:k + 1]
        lo_k = wk * _unpack_lo(word)
        hi_k = wk * _unpack_hi(word)
        lo = lo_k if lo is None else lo + lo_k
        hi = hi_k if hi is None else hi + hi_k
    yf = jnp.concatenate([lo, hi], axis=1)
    m = mod_ref[0]
    o_ref[...] = x_ref[...] + m[5:6, :] * _rms(yf, g_ref[...])


def _combine(yg, tw_t, xn, batch0, mod_l, post_g, b, s, out_batches=None, shared_out=None):
    n = b * s
    tm = min(ROW_TILE, s)
    nb = s // tm
    nblk = n // tm
    row = lambda i: (i, 0)
    full = lambda i: (0, 0)
    y_specs = [pl.BlockSpec((tm, HALF), functools.partial(lambda i, k: (k * nblk + i, 0), k=k)) for k in range(TOP_K)]
    out_rows, out_blk0 = (n, 0) if out_batches is None else (out_batches * s, batch0 * nb)
    extra_specs, extra_args, aliases = [], [], {}
    if shared_out is not None:
        extra_specs, extra_args, aliases = [pl.BlockSpec(memory_space=pl.ANY)], [shared_out], {8: 0}
    return pl.pallas_call(
        _combine_kernel,
        grid=(nblk,),
        in_specs=y_specs + [
            pl.BlockSpec((tm, 2 * TOP_K), row),
            pl.BlockSpec((tm, D_MODEL), row),
            pl.BlockSpec((1, 6, D_MODEL), lambda i: (batch0 + i // nb, 0, 0)),
            pl.BlockSpec((1, D_MODEL), full),
        ] + extra_specs,
        out_specs=pl.BlockSpec((tm, D_MODEL), lambda i: (out_blk0 + i, 0)),
        out_shape=jax.ShapeDtypeStruct((out_rows, D_MODEL), F32),
        input_output_aliases=aliases,
        compiler_params=_params(("parallel",)),
        name="combine",
    )(yg, yg, yg, yg, tw_t, xn, mod_l, post_g, *extra_args)


def _route(counts, tidx, rank, n_tiles):
    t = MOE_TILE
    cnt = counts[:, 0]
    padded = ((cnt + t - 1) // t) * t
    ends = jnp.cumsum(padded)
    starts = ends - padded
    experts = jnp.arange(N_EXPERTS, dtype=I32)[:, None, None]
    start_of = jnp.sum(jnp.where(tidx[None] == experts, starts[:, None, None], 0), axis=0)
    pos = start_of + rank
    tile_ends = ends // t
    n_valid = tile_ends[-1]
    tile = jnp.minimum(jnp.arange(n_tiles, dtype=I32), n_valid - 1)
    tile_expert = jnp.minimum(jnp.sum(tile_ends[None, :] <= tile[:, None], axis=1), N_EXPERTS - 1).astype(I32)
    return pos.astype(I32), tile_expert, n_valid.reshape(1).astype(I32)


def _value_range_scale(mod_l, pre_g, w_in_l):
    d = pre_g.shape[0]
    shift, scale = mod_l[:, 0, :], mod_l[:, 1, :]
    h_norm = d ** 0.5 * jnp.max(jnp.abs(pre_g[None, :] * (1.0 + scale)), axis=1) + jnp.linalg.norm(shift, axis=1)
    w_v = w_in_l[:, ATTN_W + KV_W:ATTN_W + 2 * KV_W]
    bound = 1.02 * h_norm * jnp.max(jnp.linalg.norm(w_v, axis=0))
    v_scale = V_RANGE / jnp.maximum(bound, 1e-30)
    lanes = lambda t: jnp.broadcast_to(t[:, None, None], (mod_l.shape[0], 1, LANES))
    return lanes(v_scale), lanes(1.0 / v_scale)


def kernel(x, c, ada_w, ada_b, pre_mix_g, post_mix_g, w_in, q_norm_g, k_norm_g, ret_decay_fwd, ret_decay_bwd, ret_gn_g, ret_gn_b, w_out, pre_ffn_g, post_ffn_g, router_w, router_b, exp_w_up, exp_b_up, exp_w_down, exp_b_down):
    b, s, d = x.shape
    depth = ada_w.shape[0]
    groups = 2 if b % 2 == 0 else 1
    bg = b // groups
    n = bg * s
    n_asg = n * TOP_K
    n_pad = n_asg + N_EXPERTS * MOE_TILE
    n_tiles = n_pad // MOE_TILE

    mod = _modulation(c, ada_w, ada_b)
    ropes = _rope_tables(s, HEAD_DIM) + _rope_tables(s, RET_DIM)
    bu = exp_b_up.reshape(depth, N_EXPERTS, 1, 2 * D_FF)
    bd = exp_b_down.reshape(depth, N_EXPERTS, 1, d)
    xs_in = [x.reshape(b * s, d)] * groups
    x_b0 = [g * bg for g in range(groups)]
    out = None
    for l in range(depth):
        mod_l = mod[l].reshape(b, 6, d)
        qg = jnp.tile(q_norm_g[l], LANES // HEAD_DIM).reshape(1, LANES)
        kg = jnp.tile(k_norm_g[l], LANES // HEAD_DIM).reshape(1, LANES)
        v_scale, fac = _value_range_scale(mod_l, pre_mix_g[l], w_in[l])
        w_in_b = w_in[l].astype(BF16)
        w_out_b = w_out[l].astype(BF16)
        rw_t = router_w[l].T.astype(BF16)
        dec = jnp.stack([ret_decay_fwd[l], ret_decay_bwd[l]]).astype(F32)
        last = l + 1 == depth
        proj = [_inproj(xs_in[g], g * bg, x_b0[g], mod_l, pre_mix_g[l].reshape(1, d), w_in_b, ropes, qg, kg, v_scale,
                        bg, s) for g in range(groups)]
        oa = [_attention(p[0], p[1], p[2], fac, g * bg, bg, s) for g, p in enumerate(proj)]
        orr = [_retention(p[3], p[4], p[5], p[6], dec, ret_gn_g[l].reshape(1, RET_W), ret_gn_b[l].reshape(1, RET_W),
                          bg, s) for p in proj]
        routed = []
        for g in range(groups):
            xn, hp, tidx, tw, rank, counts = _outproj(
                oa[g], orr[g], w_out_b, xs_in[g], g * bg, x_b0[g], mod_l, post_mix_g[l].reshape(1, d),
                pre_ffn_g[l].reshape(1, d), rw_t, router_b[l].reshape(N_EXPERTS, 1), bg, s)
            pos, tile_expert, n_valid = _route(counts, tidx, rank, n_tiles)
            pos3 = pos.reshape(TOP_K, n // SC_CHUNK, SC_CHUNK).transpose(1, 0, 2)
            routed.append((xn, tw, pos, tile_expert, n_valid, _sc_scatter_rows(hp, pos3, n_pad)))
        yg = []
        for xn, tw, pos, tile_expert, n_valid, xs in routed:
            ys = _experts(xs, tile_expert, n_valid, l, exp_w_up, bu, exp_w_down, bd)
            yg.append(_sc_gather_rows(ys, pos.reshape(n_asg)))
        nxt = []
        for g, (xn, tw, *_) in enumerate(routed):
            if last:
                out = _combine(yg[g], tw, xn, g * bg, mod_l, post_ffn_g[l].reshape(1, d), bg, s,
                               out_batches=b, shared_out=out)
            else:
                nxt.append(_combine(yg[g], tw, xn, g * bg, mod_l, post_ffn_g[l].reshape(1, d), bg, s))
        xs_in, x_b0 = nxt, [0] * groups
    return out.reshape(b, s, d)
```

```python
import functools

import numpy as np
import jax
import jax.numpy as jnp
from jax import lax
from jax.experimental import pallas as pl
from jax.experimental.pallas import tpu as pltpu
from jax.experimental.pallas import tpu_sc as plsc

F32 = jnp.float32
BF16 = jnp.bfloat16
U32 = jnp.uint32
I32 = jnp.int32
FP8 = jnp.float8_e4m3fn

D_MODEL = 1024
GRID_W = 64
ROPE_THETA = 10000.0
HEAD_DIM = 64
Q_HEADS = 8
KV_HEADS = 2
RET_HEADS = 4
RET_DIM = 128
RET_CHUNK = 128
ATTN_W = Q_HEADS * HEAD_DIM
KV_W = KV_HEADS * HEAD_DIM
RET_W = RET_HEADS * RET_DIM
IN_W = ATTN_W + 2 * KV_W + 4 * RET_W
N_EXPERTS = 32
TOP_K = 4
D_FF = D_MODEL
SWIGLU_LIMIT = 7.0
SWIGLU_ALPHA = 1.702
NORM_EPS = 1e-6
Q_SCALE = HEAD_DIM ** -0.5 * float(np.log2(np.e))
FP8_RANGE = 224.0
ACT_SCALE = FP8_RANGE / (SWIGLU_LIMIT * (SWIGLU_LIMIT + 1.0))
V_RANGE = 224.0
P_SHIFT = 8.0
HALF = D_MODEL // 2
LANES = 128

ROW_TILE = 512
Q_TILE = 128
Q_TILES_PER_TRIP = 2
KV_TILE = 1024
MOE_TILE = 512
RET_CHUNKS_PER_TRIP = 8
VMEM_LIMIT = 48 * 1024 * 1024
EXPERTS_VMEM_LIMIT = 56 * 1024 * 1024

SC_CORES = 2
SC_SUBCORES = 16
SC_WORKERS = SC_CORES * SC_SUBCORES
SC_CHUNK = 64


def _params(sem):
    return pltpu.CompilerParams(dimension_semantics=sem, vmem_limit_bytes=VMEM_LIMIT)


def _rms(x, g):
    return x * lax.rsqrt(jnp.mean(x * x, axis=-1, keepdims=True) + NORM_EPS) * g


def _pack_bf16_pairs(y):
    u = lax.bitcast_convert_type(y.astype(BF16).astype(F32), U32)
    return (u[:, :HALF] >> 16) | u[:, HALF:]


def _unpack_lo(w):
    return lax.bitcast_convert_type(w << 16, F32)


def _unpack_hi(w):
    return lax.bitcast_convert_type(w & jnp.uint32(0xFFFF0000), F32)


def _mod_kernel(c_ref, w_ref, b_ref, o_ref):
    c = c_ref[...]
    cond = c * jax.nn.sigmoid(c)
    o_ref[0] = jnp.dot(cond, w_ref[0], preferred_element_type=F32, precision=lax.Precision.HIGHEST) + b_ref[0]


def _modulation(c, ada_w, ada_b):
    depth, d, w6 = ada_w.shape
    b = c.shape[0]
    tn = 1536
    return pl.pallas_call(
        _mod_kernel,
        grid=(depth, w6 // tn),
        in_specs=[
            pl.BlockSpec((b, d), lambda l, j: (0, 0)),
            pl.BlockSpec((1, d, tn), lambda l, j: (l, 0, j)),
            pl.BlockSpec((1, 1, tn), lambda l, j: (l, 0, j)),
        ],
        out_specs=pl.BlockSpec((1, b, tn), lambda l, j: (l, 0, j)),
        out_shape=jax.ShapeDtypeStruct((depth, b, w6), F32),
        compiler_params=_params(("arbitrary", "arbitrary")),
        name="modulation",
    )(c, ada_w, ada_b.reshape(depth, 1, w6))


def _rope_tables(s, head_dim):
    quarter = head_dim // 4
    t = jnp.arange(s, dtype=F32)
    row = jnp.floor(t / GRID_W)
    col = t - row * GRID_W
    inv_freq = ROPE_THETA ** (-jnp.arange(quarter, dtype=F32) / quarter)
    lane = np.arange(LANES) % head_dim
    use_col = lane >= head_dim // 2
    second = (lane % (head_dim // 2)) >= quarter
    freq = inv_freq[lane % quarter]
    pos = jnp.where(use_col[None, :], col[:, None], row[:, None])
    ang = pos * freq[None, :]
    sign = jnp.where(second, 1.0, -1.0).astype(F32)
    return jnp.cos(ang), jnp.sin(ang) * sign[None, :]


def _rope(z, cos, sin_signed, quarter):
    lane = lax.broadcasted_iota(I32, z.shape, 1)
    first = (lane % (2 * quarter)) < quarter
    partner = jnp.where(first, pltpu.roll(z, LANES - quarter, 1), pltpu.roll(z, quarter, 1))
    return z * cos + partner * sin_signed


def _inproj_kernel(x_ref, mod_ref, g_ref, w_ref, ca_ref, sa_ref, cr_ref, sr_ref, qg_ref, kg_ref, vs_ref,
                   qa_ref, kd_ref, vt_ref, qr_ref, kr_ref, vr_ref, gr_ref):
    x = x_ref[...]
    m = mod_ref[0]
    h = _rms(x, g_ref[...]) * (1.0 + m[1:2, :]) + m[0:1, :]
    hb = h.astype(BF16)
    tm = x.shape[0]

    def proj(c0):
        z = jnp.dot(hb, w_ref[:, c0:c0 + 2 * LANES], preferred_element_type=F32)
        return z[:, :LANES], z[:, LANES:]

    ri = lax.broadcasted_iota(I32, (LANES, LANES), 0) // HEAD_DIM
    ci = lax.broadcasted_iota(I32, (LANES, LANES), 1) // HEAD_DIM
    head_ones = (ri == ci).astype(BF16)
    ca, sa = ca_ref[...], sa_ref[...]
    cr, sr = cr_ref[...], sr_ref[...]
    lane = lax.broadcasted_iota(I32, (tm, LANES), 1)
    low = lane < HEAD_DIM

    def head_norm_rope(z, g):
        ss = jnp.dot((z * z).astype(BF16), head_ones, preferred_element_type=F32)
        zn = z * lax.rsqrt(ss * (1.0 / HEAD_DIM) + NORM_EPS) * g
        return _rope(zn, ca, sa, HEAD_DIM // 4)

    def store_pair(ref, j, fn):
        def finish(zs):
            for i, z in enumerate(zs):
                ref[:, (2 * j + i) * LANES:(2 * j + i + 1) * LANES] = fn(z).astype(BF16)
        return finish

    def store_kv(zs):
        k, v = zs
        k = head_norm_rope(k, kg_ref[...])
        k_sw = pltpu.roll(k, HEAD_DIM, 1)
        v = v * vs_ref[0]
        kd_ref[0, 0] = jnp.where(low, k, k_sw).astype(BF16)
        kd_ref[0, 1] = jnp.where(low, k_sw, k).astype(BF16)
        vt_ref[0, 0] = jnp.where(low, v, 1.0).T.astype(FP8)
        vt_ref[0, 1] = jnp.where(low, pltpu.roll(v, HEAD_DIM, 1), 1.0).T.astype(FP8)

    jobs = [(2 * j * LANES, store_pair(qa_ref, j, lambda z: head_norm_rope(z, qg_ref[...]) * Q_SCALE))
            for j in range(ATTN_W // (2 * LANES))]
    jobs.append((ATTN_W, store_kv))
    base = ATTN_W + 2 * KV_W
    for j in range(RET_W // (2 * LANES)):
        off = 2 * j * LANES
        jobs += [
            (base + off, store_pair(qr_ref, j, lambda z: _rope(z, cr, sr, RET_DIM // 4))),
            (base + RET_W + off, store_pair(kr_ref, j, lambda z: _rope(z, cr, sr, RET_DIM // 4) * (RET_DIM ** -0.5))),
            (base + 2 * RET_W + off, store_pair(vr_ref, j, lambda z: z)),
            (base + 3 * RET_W + off, store_pair(gr_ref, j, lambda z: z * jax.nn.sigmoid(z))),
        ]
    zs = proj(jobs[0][0])
    for i, (_, finish) in enumerate(jobs):
        nxt = proj(jobs[i + 1][0]) if i + 1 < len(jobs) else None
        finish(zs)
        zs = nxt


def _inproj(x2, batch0, x_batch0, mod_l, pre_g, w_in_b, ropes, qg, kg, v_scale, b, s):
    n = b * s
    tm = min(ROW_TILE, s)
    nb = s // tm
    ca, sa, cr, sr = ropes
    row = lambda i: (i, 0)
    full = lambda i: (0, 0)
    rope_spec = pl.BlockSpec((tm, LANES), lambda i: (i % nb, 0))
    wide = jax.ShapeDtypeStruct((n, RET_W), BF16)
    return pl.pallas_call(
        _inproj_kernel,
        grid=(n // tm,),
        in_specs=[
            pl.BlockSpec((tm, D_MODEL), lambda i: (x_batch0 * nb + i, 0)),
            pl.BlockSpec((1, 6, D_MODEL), lambda i: (batch0 + i // nb, 0, 0)),
            pl.BlockSpec((1, D_MODEL), full),
            pl.BlockSpec((D_MODEL, IN_W), full),
            rope_spec, rope_spec, rope_spec, rope_spec,
            pl.BlockSpec((1, LANES), full),
            pl.BlockSpec((1, LANES), full),
            pl.BlockSpec((1, 1, LANES), lambda i: (batch0 + i // nb, 0, 0)),
        ],
        out_specs=[
            pl.BlockSpec((tm, ATTN_W), row),
            pl.BlockSpec((1, KV_HEADS, tm, LANES), lambda i: (i // nb, 0, i % nb, 0)),
            pl.BlockSpec((1, KV_HEADS, LANES, tm), lambda i: (i // nb, 0, 0, i % nb)),
            pl.BlockSpec((tm, RET_W), row),
            pl.BlockSpec((tm, RET_W), row),
            pl.BlockSpec((tm, RET_W), row),
            pl.BlockSpec((tm, RET_W), row),
        ],
        out_shape=[
            jax.ShapeDtypeStruct((n, ATTN_W), BF16),
            jax.ShapeDtypeStruct((b, KV_HEADS, s, LANES), BF16),
            jax.ShapeDtypeStruct((b, KV_HEADS, LANES, s), FP8),
            wide, wide, wide, wide,
        ],
        compiler_params=_params(("parallel",)),
        name="inproj",
    )(x2, mod_l, pre_g, w_in_b, ca, sa, cr, sr, qg, kg, v_scale)


def _attn_kernel(q_ref, kd_ref, vt_ref, fac_ref, o_ref, q_s, qn_s, m_ref, acc_ref, s0_ref, s1_ref,
                 *, tq, tk, tiles):
    s = kd_ref.shape[2]
    nk = s // tk
    nq = s // tq
    pairs = (Q_HEADS // KV_HEADS) // 2
    top = lax.broadcasted_iota(I32, (LANES, tq), 0) < HEAD_DIM
    unscale = jnp.concatenate([fac_ref[0]] * (2 * tq // LANES), axis=1)

    chains = tiles * pairs

    def load_q(trip, dst):
        for u in range(tiles):
            rows = pl.ds(pl.multiple_of((trip * tiles + u) * tq, tq), tq)
            for p in range(pairs):
                qt = q_ref[rows, p * LANES:(p + 1) * LANES].astype(F32).T
                dst[u * pairs + p, :, :tq] = jnp.where(top, qt, 0.0).astype(BF16)
                dst[u * pairs + p, :, tq:] = jnp.where(top, 0.0, qt).astype(BF16)

    def keys(j):
        return kd_ref[0, 0, j * tk:(j + 1) * tk, :]

    def step(j, cur_ref, nxt_ref, q_next, j_next):
        vt = vt_ref[0, 0, :, j * tk:(j + 1) * tk]
        kd = keys(j_next)
        nxt_ref[0] = jnp.dot(kd, q_next[0], preferred_element_type=F32)
        for c in range(chains):
            sc = cur_ref[c]
            m_old = m_ref[c]
            m_new = jnp.maximum(m_old, jnp.max(sc, axis=0, keepdims=True))
            pt = jnp.exp2(sc - (m_new - P_SHIFT))
            alpha = jnp.exp2(m_old - m_new)
            if c + 1 < chains:
                nxt_ref[c + 1] = jnp.dot(kd, q_next[c + 1], preferred_element_type=F32)
            acc_ref[c] = alpha * acc_ref[c] + jnp.dot(vt, pt.astype(FP8), preferred_element_type=F32)
            m_ref[c] = m_new

    bufs = (s0_ref, s1_ref)
    trips = nq // tiles
    load_q(0, q_s)
    for c in range(chains):
        s0_ref[c] = jnp.dot(keys(0), q_s[c], preferred_element_type=F32)

    def body(trip, carry):
        load_q(trip, q_s)
        load_q(jnp.minimum(trip + 1, trips - 1), qn_s)
        m_ref[...] = jnp.full(m_ref.shape, -jnp.inf, F32)
        acc_ref[...] = jnp.zeros(acc_ref.shape, F32)
        for j in range(nk):
            last = j + 1 == nk
            step(j, bufs[j % len(bufs)], bufs[(j + 1) % len(bufs)], qn_s if last else q_s, 0 if last else j + 1)
        for c in range(chains):
            u, p = divmod(c, pairs)
            rows = pl.ds(pl.multiple_of((trip * tiles + u) * tq, tq), tq)
            acc = acc_ref[c]
            o_t = acc[:HEAD_DIM] / acc[HEAD_DIM:] * unscale
            both = jnp.concatenate([o_t[:, :tq], o_t[:, tq:]], axis=0)
            o_ref[rows, p * LANES:(p + 1) * LANES] = both.T.astype(BF16)
        return carry

    lax.fori_loop(0, trips, body, 0)


def _attention(qa, kd, vt, fac, batch0, b, s):
    n = qa.shape[0]
    tq = Q_TILE
    tk = min(KV_TILE, s // 2)
    tiles = Q_TILES_PER_TRIP
    assert tq == LANES and s % (tiles * tq) == 0 and s % (2 * tk) == 0 and tk % LANES == 0, (s, tq, tk)
    gw = ATTN_W // KV_HEADS
    chains = tiles * ((Q_HEADS // KV_HEADS) // 2)
    return pl.pallas_call(
        functools.partial(_attn_kernel, tq=tq, tk=tk, tiles=tiles),
        grid=(b, KV_HEADS),
        in_specs=[
            pl.BlockSpec((s, gw), lambda bi, h: (bi, h)),
            pl.BlockSpec((1, 1, s, LANES), lambda bi, h: (bi, h, 0, 0)),
            pl.BlockSpec((1, 1, LANES, s), lambda bi, h: (bi, h, 0, 0)),
            pl.BlockSpec((1, 1, LANES), lambda bi, h: (batch0 + bi, 0, 0)),
        ],
        out_specs=pl.BlockSpec((s, gw), lambda bi, h: (bi, h)),
        out_shape=jax.ShapeDtypeStruct((n, ATTN_W), BF16),
        scratch_shapes=[
            pltpu.VMEM((chains, LANES, 2 * tq), BF16),
            pltpu.VMEM((chains, LANES, 2 * tq), BF16),
            pltpu.VMEM((chains, 1, 2 * tq), F32),
            pltpu.VMEM((chains, LANES, 2 * tq), F32),
            pltpu.VMEM((chains, tk, 2 * tq), F32),
            pltpu.VMEM((chains, tk, 2 * tq), F32),
        ],
        compiler_params=_params(("parallel", "parallel")),
        name="attention",
    )(qa, kd, vt, fac)


def _retention_kernel(dec_ref, q_ref, k_ref, v_ref, g_ref, gng_ref, gnb_ref, o_ref,
                      fbuf, bbuf, sf_ref, sb_ref):
    h = pl.program_id(1)
    c = RET_CHUNK
    s = q_ref.shape[1]
    nc = s // c
    lgf = jnp.full((1, 1), dec_ref[0, h], F32)
    lgb = jnp.full((1, 1), dec_ref[1, h], F32)
    ii = lax.broadcasted_iota(I32, (c, c), 0)
    jj = lax.broadcasted_iota(I32, (c, c), 1)
    diff = (ii - jj).astype(F32)
    d_f = jnp.where(diff >= 0, jnp.exp(lgf * jnp.maximum(diff, 0.0)), 0.0)
    d_b = jnp.where(diff <= 0, jnp.exp(lgb * jnp.maximum(-diff, 0.0)), 0.0)
    idx = lax.broadcasted_iota(I32, (c, 1), 0).astype(F32)
    xi_f = jnp.exp(lgf * (idx + 1.0))
    zeta_f = jnp.exp(lgf * (c - 1.0 - idx))
    xi_b = jnp.exp(lgb * (c - idx))
    zeta_b = jnp.exp(lgb * idx)
    cd_f = jnp.exp(lgf * c)
    cd_b = jnp.exp(lgb * c)
    sf_ref[...] = jnp.zeros(sf_ref.shape, F32)
    sb_ref[...] = jnp.zeros(sb_ref.shape, F32)
    nt = (((1,), (1,)), ((), ()))

    fwd = (d_f, xi_f, zeta_f, cd_f, sf_ref, fbuf)
    bwd = (d_b, xi_b, zeta_b, cd_b, sb_ref, bbuf)

    per = min(RET_CHUNKS_PER_TRIP, nc)

    def body(i, carry):
        jobs = []
        for n in range(per):
            jobs += [(fwd, per * i + n), (bwd, nc - 1 - per * i - n)]
        offs = [pl.multiple_of(ci * c, c) for _, ci in jobs]
        qs = [q_ref[0, pl.ds(off, c), :] for off in offs]
        ks = [k_ref[0, pl.ds(off, c), :] for off in offs]
        vs = [v_ref[0, pl.ds(off, c), :] for off in offs]
        scores = [lax.dot_general(q, k, nt, preferred_element_type=F32) for q, k in zip(qs, ks)]
        incs = []
        for (d, _), k, v in zip(jobs, ks, vs):
            kzt = (k.astype(F32) * d[2]).T.astype(BF16)
            incs.append(jnp.dot(kzt, v, preferred_element_type=F32))
        states = [fwd[4][...], bwd[4][...]]
        cross = []
        for n, (d, _) in enumerate(jobs):
            cross.append(jnp.dot(qs[n], states[n].astype(BF16), preferred_element_type=F32))
            states.append(states[n] * d[3] + incs[n])
        for n, ((d, _), off) in enumerate(zip(jobs, offs)):
            a = (scores[n] * d[0]).astype(BF16)
            d[5][pl.ds(off, c), :] = jnp.dot(a, vs[n], preferred_element_type=F32) + cross[n] * d[1]
        fwd[4][...] = states[2 * per]
        bwd[4][...] = states[2 * per + 1]
        return carry

    lax.fori_loop(0, nc // per, body, 0)

    rows = min(1024, s)

    def fin(t, carry):
        off = pl.multiple_of(t * rows, rows)
        r = fbuf[pl.ds(off, rows), :] + bbuf[pl.ds(off, rows), :]
        mu = jnp.mean(r, axis=-1, keepdims=True)
        d = r - mu
        var = jnp.mean(d * d, axis=-1, keepdims=True)
        y = d * lax.rsqrt(var + NORM_EPS) * gng_ref[...] + gnb_ref[...]
        o_ref[0, pl.ds(off, rows), :] = (g_ref[0, pl.ds(off, rows), :].astype(F32) * y).astype(BF16)
        return carry

    lax.fori_loop(0, s // rows, fin, 0)


def _retention(qr, kr, vr, gr, dec, gn_g, gn_b, b, s):
    shp = (b, s, RET_W)
    assert (s // RET_CHUNK) % min(RET_CHUNKS_PER_TRIP, s // RET_CHUNK) == 0, s
    head = pl.BlockSpec((1, s, RET_DIM), lambda bi, h: (bi, 0, h))
    vec = pl.BlockSpec((1, RET_DIM), lambda bi, h: (0, h))
    out = pl.pallas_call(
        _retention_kernel,
        grid=(b, RET_HEADS),
        in_specs=[pl.BlockSpec(memory_space=pltpu.SMEM), head, head, head, head, vec, vec],
        out_specs=head,
        out_shape=jax.ShapeDtypeStruct(shp, BF16),
        scratch_shapes=[
            pltpu.VMEM((s, RET_DIM), F32),
            pltpu.VMEM((s, RET_DIM), F32),
            pltpu.VMEM((RET_DIM, RET_DIM), F32),
            pltpu.VMEM((RET_DIM, RET_DIM), F32),
        ],
        compiler_params=_params(("parallel", "parallel")),
        name="retention",
    )(dec, qr.reshape(shp), kr.reshape(shp), vr.reshape(shp), gr.reshape(shp), gn_g, gn_b)
    return out.reshape(b * s, RET_W)


def _outproj_kernel(oa_ref, or_ref, w_ref, x_ref, mod_ref, pg_ref, fg_ref, rw_ref, rb_ref,
                    xn_ref, hp_ref, ti_ref, tw_ref, rk_ref, cnt_ref, base_ref):
    @pl.when(pl.program_id(0) == 0)
    def _():
        base_ref[...] = jnp.zeros(base_ref.shape, F32)

    m = mod_ref[0]
    tm = x_ref.shape[0]
    nt = (((1,), (1,)), ((), ()))
    halves = [slice(0, tm // 2), slice(tm // 2, tm)]
    ys = [jnp.dot(oa_ref[r, :], w_ref[:ATTN_W, :], preferred_element_type=F32)
          + jnp.dot(or_ref[r, :], w_ref[ATTN_W:, :], preferred_element_type=F32) for r in halves]
    parts = []
    for r, y in zip(halves, ys):
        xn = x_ref[r, :] + m[2:3, :] * _rms(y, pg_ref[...])
        xn_ref[r, :] = xn
        h = _rms(xn, fg_ref[...]) * (1.0 + m[4:5, :]) + m[3:4, :]
        hb = h.astype(BF16)
        u = lax.bitcast_convert_type(hb.astype(F32), U32)
        hp_ref[r, :] = (u[:, :HALF] >> 16) | u[:, HALF:]
        parts.append(lax.dot_general(rw_ref[...], hb, nt, preferred_element_type=F32))
    logits = jnp.concatenate(parts, axis=1) + rb_ref[...]
    e_iota = lax.broadcasted_iota(I32, (N_EXPERTS, tm), 0).astype(F32)
    vals, idxs = [], []
    cur = logits
    for _ in range(TOP_K):
        mx = jnp.max(cur, axis=0, keepdims=True)
        ik = jnp.min(jnp.where(cur == mx, e_iota, float(N_EXPERTS)), axis=0, keepdims=True)
        vals.append(mx)
        idxs.append(ik)
        cur = jnp.where(e_iota == ik, -jnp.inf, cur)
    v = jnp.concatenate(vals, axis=0)
    w = jnp.exp(v - v[0:1, :])
    tw_ref[...] = w / jnp.sum(w, axis=0, keepdims=True)
    ti_ref[...] = jnp.concatenate(idxs, axis=0).astype(I32)

    onehot = jnp.zeros((N_EXPERTS, tm), F32)
    for ik in idxs:
        onehot = onehot + (e_iota == ik).astype(F32)
    before = (lax.broadcasted_iota(I32, (tm, tm), 0) < lax.broadcasted_iota(I32, (tm, tm), 1)).astype(BF16)
    seen = jnp.dot(onehot.astype(BF16), before, preferred_element_type=F32) + base_ref[...]
    ranks = [jnp.sum(jnp.where(e_iota == ik, seen, 0.0), axis=0, keepdims=True) for ik in idxs]
    rk_ref[...] = jnp.concatenate(ranks, axis=0).astype(I32)
    total = base_ref[...] + jnp.sum(onehot, axis=1, keepdims=True)
    base_ref[...] = total
    cnt_ref[...] = jnp.broadcast_to(total, cnt_ref.shape).astype(I32)


def _outproj(oa, orr, w_out_b, x2, batch0, x_batch0, mod_l, post_g, ffn_g, rw_t, rb, b, s):
    n = b * s
    tm = min(ROW_TILE, s)
    nb = s // tm
    row = lambda i: (i, 0)
    full = lambda i: (0, 0)
    col = lambda i: (0, i)
    return pl.pallas_call(
        _outproj_kernel,
        grid=(n // tm,),
        in_specs=[
            pl.BlockSpec((tm, ATTN_W), row),
            pl.BlockSpec((tm, RET_W), row),
            pl.BlockSpec((D_MODEL, D_MODEL), full),
            pl.BlockSpec((tm, D_MODEL), lambda i: (x_batch0 * nb + i, 0)),
            pl.BlockSpec((1, 6, D_MODEL), lambda i: (batch0 + i // nb, 0, 0)),
            pl.BlockSpec((1, D_MODEL), full),
            pl.BlockSpec((1, D_MODEL), full),
            pl.BlockSpec((N_EXPERTS, D_MODEL), full),
            pl.BlockSpec((N_EXPERTS, 1), full),
        ],
        out_specs=[
            pl.BlockSpec((tm, D_MODEL), row),
            pl.BlockSpec((tm, HALF), row),
            pl.BlockSpec((TOP_K, tm), col),
            pl.BlockSpec((TOP_K, tm), col),
            pl.BlockSpec((TOP_K, tm), col),
            pl.BlockSpec((N_EXPERTS, LANES), full),
        ],
        out_shape=[
            jax.ShapeDtypeStruct((n, D_MODEL), F32),
            jax.ShapeDtypeStruct((n, HALF), U32),
            jax.ShapeDtypeStruct((TOP_K, n), I32),
            jax.ShapeDtypeStruct((TOP_K, n), F32),
            jax.ShapeDtypeStruct((TOP_K, n), I32),
            jax.ShapeDtypeStruct((N_EXPERTS, LANES), I32),
        ],
        scratch_shapes=[pltpu.VMEM((N_EXPERTS, 1), F32)],
        compiler_params=_params(("arbitrary",)),
        name="outproj_router",
    )(oa, orr, w_out_b, x2, mod_l, post_g, ffn_g, rw_t, rb)


def _sc_scatter_rows(rows, pos3, n_out):
    n, w = rows.shape
    nchunk, kk, c = pos3.shape
    per_w = nchunk // SC_WORKERS
    mesh = plsc.VectorSubcoreMesh(core_axis_name="c", subcore_axis_name="s")

    @functools.partial(
        pl.kernel, mesh=mesh,
        out_type=jax.ShapeDtypeStruct((n_out, w), rows.dtype),
        scratch_types=[pltpu.VMEM((kk, c), I32), pltpu.VMEM((c, w), rows.dtype)],
    )
    def k(rows_hbm, pos_hbm, out_hbm, idx_v, rows_v):
        wid = lax.axis_index("s") * SC_CORES + lax.axis_index("c")

        @pl.loop(0, per_w)
        def _(i):
            ch = wid * per_w + i
            pltpu.sync_copy(pos_hbm.at[ch], idx_v)
            pltpu.sync_copy(rows_hbm.at[pl.ds(ch * c, c)], rows_v)
            for j in range(kk):
                pltpu.sync_copy(rows_v, out_hbm.at[idx_v.at[j]])

    return k(rows, pos3)


def _sc_gather_rows(table, idx):
    w = table.shape[1]
    b = idx.shape[0]
    c = SC_CHUNK
    per_w = b // (SC_WORKERS * c)
    mesh = plsc.VectorSubcoreMesh(core_axis_name="c", subcore_axis_name="s")

    @functools.partial(
        pl.kernel, mesh=mesh,
        out_type=jax.ShapeDtypeStruct((b, w), table.dtype),
        scratch_types=[pltpu.VMEM((c,), I32), pltpu.VMEM((c, w), table.dtype), pltpu.SemaphoreType.DMA],
    )
    def k(table_hbm, idx_hbm, out_hbm, idx_v, rows_v, sem):
        wid = lax.axis_index("s") * SC_CORES + lax.axis_index("c")

        @pl.loop(0, per_w)
        def _(i):
            base = (wid * per_w + i) * c
            pltpu.sync_copy(idx_hbm.at[pl.ds(base, c)], idx_v)
            pltpu.async_copy(table_hbm.at[idx_v], rows_v, sem).wait()
            pltpu.sync_copy(rows_v, out_hbm.at[pl.ds(base, c)])

    return k(table, idx)


def _experts_kernel(te_ref, nv_ref, xs_ref, wu_ref, bu_ref, wd_ref, bd_ref, ys_ref, wu_b, wd_b, inv_ref):
    i = pl.program_id(0)
    live = i < nv_ref[0]
    new_expert = jnp.logical_or(i == 0, te_ref[i] != te_ref[jnp.maximum(i - 1, 0)])

    def full_max(a):
        return jnp.max(jnp.max(jnp.abs(a), axis=0, keepdims=True), axis=1, keepdims=True)

    @pl.when(jnp.logical_and(live, new_expert))
    def _():
        wu = wu_ref[0, 0]
        wd = wd_ref[0, 0]
        mu = jnp.maximum(full_max(wu), 1e-30)
        md = jnp.maximum(full_max(wd), 1e-30)
        wu_b[...] = (wu * (FP8_RANGE / mu)).astype(FP8)
        wd_b[...] = (wd * (FP8_RANGE / md)).astype(FP8)
        inv_ref[0:1, :] = jnp.broadcast_to(mu / FP8_RANGE, (1, LANES))
        inv_ref[1:2, :] = jnp.broadcast_to(md / (FP8_RANGE * ACT_SCALE), (1, LANES))

    @pl.when(live)
    def _():
        t = xs_ref.shape[0]
        halves = [slice(0, t // 2), slice(t // 2, t)]
        inv_up = inv_ref[0:1, 0:1]
        inv_dn = inv_ref[1:2, 0:1]
        ups = []
        for r in halves:
            w = xs_ref[r, :]
            lo = _unpack_lo(w)
            hi = _unpack_hi(w)
            amax = jnp.max(jnp.maximum(jnp.abs(lo), jnp.abs(hi)), axis=1, keepdims=True)
            amax = jnp.maximum(amax, 1e-30)
            c = FP8_RANGE / amax
            up = (jnp.dot((lo * c).astype(FP8), wu_b[:HALF, :], preferred_element_type=F32)
                  + jnp.dot((hi * c).astype(FP8), wu_b[HALF:, :], preferred_element_type=F32))
            ups.append(up * (amax * (inv_up / FP8_RANGE)))
        for r, up in zip(halves, ups):
            up = up + bu_ref[0, 0]
            glu = jnp.minimum(up[:, :D_FF], SWIGLU_LIMIT)
            lin = jnp.clip(up[:, D_FF:], -SWIGLU_LIMIT, SWIGLU_LIMIT)
            act = glu * jax.nn.sigmoid(SWIGLU_ALPHA * glu) * (lin + 1.0)
            y = jnp.dot((act * ACT_SCALE).astype(FP8), wd_b[...], preferred_element_type=F32) * inv_dn + bd_ref[0, 0]
            ys_ref[r, :] = _pack_bf16_pairs(y)


def _experts(xs, tile_expert, n_valid, layer, wu, bu, wd, bd):
    n_pad = xs.shape[0]
    t = MOE_TILE
    grid_spec = pltpu.PrefetchScalarGridSpec(
        num_scalar_prefetch=2,
        grid=(n_pad // t,),
        in_specs=[
            pl.BlockSpec((t, HALF), lambda i, te, nv: (i, 0)),
            pl.BlockSpec((1, 1, D_MODEL, 2 * D_FF), lambda i, te, nv: (layer, te[i], 0, 0)),
            pl.BlockSpec((1, 1, 1, 2 * D_FF), lambda i, te, nv: (layer, te[i], 0, 0)),
            pl.BlockSpec((1, 1, D_FF, D_MODEL), lambda i, te, nv: (layer, te[i], 0, 0)),
            pl.BlockSpec((1, 1, 1, D_MODEL), lambda i, te, nv: (layer, te[i], 0, 0)),
        ],
        out_specs=pl.BlockSpec((t, HALF), lambda i, te, nv: (i, 0)),
        scratch_shapes=[pltpu.VMEM((D_MODEL, 2 * D_FF), FP8), pltpu.VMEM((D_FF, D_MODEL), FP8),
                        pltpu.VMEM((2, LANES), F32)],
    )
    return pl.pallas_call(
        _experts_kernel,
        grid_spec=grid_spec,
        out_shape=jax.ShapeDtypeStruct((n_pad, HALF), U32),
        compiler_params=pltpu.CompilerParams(dimension_semantics=("arbitrary",), vmem_limit_bytes=EXPERTS_VMEM_LIMIT),
        name="experts",
    )(tile_expert, n_valid, xs, wu, bu, wd, bd)


def _combine_kernel(y0_ref, y1_ref, y2_ref, y3_ref, w_ref, x_ref, mod_ref, g_ref, *rest):
    o_ref = rest[-1]
    w = w_ref[...]
    lo = None
    hi = None
    for k, y_ref in enumerate((y0_ref, y1_ref, y2_ref, y3_ref)):
        word = y_ref[...]
        wk = w[:, k:k + 1]
        lo_k = wk * _unpack_lo(word)
        hi_k = wk * _unpack_hi(word)
        lo = lo_k if lo is None else lo + lo_k
        hi = hi_k if hi is None else hi + hi_k
    yf = jnp.concatenate([lo, hi], axis=1)
    m = mod_ref[0]
    o_ref[...] = x_ref[...] + m[5:6, :] * _rms(yf, g_ref[...])


def _combine(yg, tw_t, xn, batch0, mod_l, post_g, b, s, out_batches=None, shared_out=None):
    n = b * s
    tm = min(ROW_TILE, s)
    nb = s // tm
    nblk = n // tm
    row = lambda i: (i, 0)
    full = lambda i: (0, 0)
    y_specs = [pl.BlockSpec((tm, HALF), functools.partial(lambda i, k: (k * nblk + i, 0), k=k)) for k in range(TOP_K)]
    out_rows, out_blk0 = (n, 0) if out_batches is None else (out_batches * s, batch0 * nb)
    extra_specs, extra_args, aliases = [], [], {}
    if shared_out is not None:
        extra_specs, extra_args, aliases = [pl.BlockSpec(memory_space=pl.ANY)], [shared_out], {8: 0}
    return pl.pallas_call(
        _combine_kernel,
        grid=(nblk,),
        in_specs=y_specs + [
            pl.BlockSpec((tm, TOP_K), row),
            pl.BlockSpec((tm, D_MODEL), row),
            pl.BlockSpec((1, 6, D_MODEL), lambda i: (batch0 + i // nb, 0, 0)),
            pl.BlockSpec((1, D_MODEL), full),
        ] + extra_specs,
        out_specs=pl.BlockSpec((tm, D_MODEL), lambda i: (out_blk0 + i, 0)),
        out_shape=jax.ShapeDtypeStruct((out_rows, D_MODEL), F32),
        input_output_aliases=aliases,
        compiler_params=_params(("parallel",)),
        name="combine",
    )(yg, yg, yg, yg, tw_t, xn, mod_l, post_g, *extra_args)


def _route(counts, tidx, rank, n_tiles):
    t = MOE_TILE
    cnt = counts[:, 0]
    padded = ((cnt + t - 1) // t) * t
    ends = jnp.cumsum(padded)
    starts = ends - padded
    experts = jnp.arange(N_EXPERTS, dtype=I32)[:, None, None]
    start_of = jnp.sum(jnp.where(tidx[None] == experts, starts[:, None, None], 0), axis=0)
    pos = start_of + rank
    tile_ends = ends // t
    n_valid = tile_ends[-1]
    tile = jnp.minimum(jnp.arange(n_tiles, dtype=I32), n_valid - 1)
    tile_expert = jnp.minimum(jnp.sum(tile_ends[None, :] <= tile[:, None], axis=1), N_EXPERTS - 1).astype(I32)
    return pos.astype(I32), tile_expert, n_valid.reshape(1).astype(I32)


def _value_range_scale(mod_l, pre_g, w_in_l):
    d = pre_g.shape[0]
    shift, scale = mod_l[:, 0, :], mod_l[:, 1, :]
    h_norm = d ** 0.5 * jnp.max(jnp.abs(pre_g[None, :] * (1.0 + scale)), axis=1) + jnp.linalg.norm(shift, axis=1)
    w_v = w_in_l[:, ATTN_W + KV_W:ATTN_W + 2 * KV_W]
    bound = 1.02 * h_norm * jnp.max(jnp.linalg.norm(w_v, axis=0))
    v_scale = V_RANGE / jnp.maximum(bound, 1e-30)
    lanes = lambda t: jnp.broadcast_to(t[:, None, None], (mod_l.shape[0], 1, LANES))
    return lanes(v_scale), lanes(1.0 / v_scale)


def kernel(x, c, ada_w, ada_b, pre_mix_g, post_mix_g, w_in, q_norm_g, k_norm_g, ret_decay_fwd, ret_decay_bwd, ret_gn_g, ret_gn_b, w_out, pre_ffn_g, post_ffn_g, router_w, router_b, exp_w_up, exp_b_up, exp_w_down, exp_b_down):
    b, s, d = x.shape
    depth = ada_w.shape[0]
    groups = 2 if b % 2 == 0 else 1
    bg = b // groups
    n = bg * s
    n_asg = n * TOP_K
    n_pad = n_asg + N_EXPERTS * MOE_TILE
    n_tiles = n_pad // MOE_TILE

    mod = _modulation(c, ada_w, ada_b)
    ropes = _rope_tables(s, HEAD_DIM) + _rope_tables(s, RET_DIM)
    bu = exp_b_up.reshape(depth, N_EXPERTS, 1, 2 * D_FF)
    bd = exp_b_down.reshape(depth, N_EXPERTS, 1, d)
    xs_in = [x.reshape(b * s, d)] * groups
    x_b0 = [g * bg for g in range(groups)]
    out = None
    for l in range(depth):
        mod_l = mod[l].reshape(b, 6, d)
        qg = jnp.tile(q_norm_g[l], LANES // HEAD_DIM).reshape(1, LANES)
        kg = jnp.tile(k_norm_g[l], LANES // HEAD_DIM).reshape(1, LANES)
        v_scale, fac = _value_range_scale(mod_l, pre_mix_g[l], w_in[l])
        w_in_b = w_in[l].astype(BF16)
        w_out_b = w_out[l].astype(BF16)
        rw_t = router_w[l].T.astype(BF16)
        dec = jnp.stack([ret_decay_fwd[l], ret_decay_bwd[l]]).astype(F32)
        last = l + 1 == depth
        proj = [_inproj(xs_in[g], g * bg, x_b0[g], mod_l, pre_mix_g[l].reshape(1, d), w_in_b, ropes, qg, kg, v_scale,
                        bg, s) for g in range(groups)]
        oa = [_attention(p[0], p[1], p[2], fac, g * bg, bg, s) for g, p in enumerate(proj)]
        orr = [_retention(p[3], p[4], p[5], p[6], dec, ret_gn_g[l].reshape(1, RET_W), ret_gn_b[l].reshape(1, RET_W),
                          bg, s) for p in proj]
        routed = []
        for g in range(groups):
            xn, hp, tidx, tw, rank, counts = _outproj(
                oa[g], orr[g], w_out_b, xs_in[g], g * bg, x_b0[g], mod_l, post_mix_g[l].reshape(1, d),
                pre_ffn_g[l].reshape(1, d), rw_t, router_b[l].reshape(N_EXPERTS, 1), bg, s)
            pos, tile_expert, n_valid = _route(counts, tidx, rank, n_tiles)
            pos3 = pos.reshape(TOP_K, n // SC_CHUNK, SC_CHUNK).transpose(1, 0, 2)
            routed.append((xn, tw, pos, tile_expert, n_valid, _sc_scatter_rows(hp, pos3, n_pad)))
        yg = []
        for xn, tw, pos, tile_expert, n_valid, xs in routed:
            ys = _experts(xs, tile_expert, n_valid, l, exp_w_up, bu, exp_w_down, bd)
            yg.append(_sc_gather_rows(ys, pos.reshape(n_asg)))
        nxt = []
        for g, (xn, tw, *_) in enumerate(routed):
            if last:
                out = _combine(yg[g], tw.T, xn, g * bg, mod_l, post_ffn_g[l].reshape(1, d), bg, s,
                               out_batches=b, shared_out=out)
            else:
                nxt.append(_combine(yg[g], tw.T, xn, g * bg, mod_l, post_ffn_g[l].reshape(1, d), bg, s))
        xs_in, x_b0 = nxt, [0] * groups
    return out.reshape(b, s, d)
```

```python
import functools

import numpy as np
import jax
import jax.numpy as jnp
from jax import lax
from jax.experimental import pallas as pl
from jax.experimental.pallas import tpu as pltpu
from jax.experimental.pallas import tpu_sc as plsc

F32 = jnp.float32
BF16 = jnp.bfloat16
U32 = jnp.uint32
I32 = jnp.int32
FP8 = jnp.float8_e4m3fn

D_MODEL = 1024
GRID_W = 64
ROPE_THETA = 10000.0
HEAD_DIM = 64
Q_HEADS = 8
KV_HEADS = 2
RET_HEADS = 4
RET_DIM = 128
RET_CHUNK = 128
ATTN_W = Q_HEADS * HEAD_DIM
KV_W = KV_HEADS * HEAD_DIM
RET_W = RET_HEADS * RET_DIM
IN_W = ATTN_W + 2 * KV_W + 4 * RET_W
N_EXPERTS = 32
TOP_K = 4
D_FF = D_MODEL
SWIGLU_LIMIT = 7.0
SWIGLU_ALPHA = 1.702
NORM_EPS = 1e-6
Q_SCALE = HEAD_DIM ** -0.5 * float(np.log2(np.e))
FP8_RANGE = 224.0
ACT_SCALE = FP8_RANGE / (SWIGLU_LIMIT * (SWIGLU_LIMIT + 1.0))
V_RANGE = FP8_RANGE
P_SHIFT = 8.0
HALF = D_MODEL // 2
LANES = 128

ROW_TILE = 512
Q_TILE = 128
Q_TILES_PER_TRIP = 2
KV_TILE = 1024
MOE_TILE = 512
RET_CHUNKS_PER_TRIP = 8
VMEM_LIMIT = 48 * 1024 * 1024
EXPERTS_VMEM_LIMIT = 56 * 1024 * 1024

SC_CORES = 2
SC_SUBCORES = 16
SC_WORKERS = SC_CORES * SC_SUBCORES
SC_CHUNK = 64


def _params(sem):
    return pltpu.CompilerParams(dimension_semantics=sem, vmem_limit_bytes=VMEM_LIMIT)


def _rms(x, g):
    return x * lax.rsqrt(jnp.mean(x * x, axis=-1, keepdims=True) + NORM_EPS) * g


def _pack_bf16_pairs(y):
    u = lax.bitcast_convert_type(y.astype(BF16).astype(F32), U32)
    return (u[:, :HALF] >> 16) | u[:, HALF:]


def _unpack_lo(w):
    return lax.bitcast_convert_type(w << 16, F32)


def _unpack_hi(w):
    return lax.bitcast_convert_type(w & jnp.uint32(0xFFFF0000), F32)


def _mod_kernel(c_ref, w_ref, b_ref, o_ref):
    c = c_ref[...]
    cond = c * jax.nn.sigmoid(c)
    o_ref[0] = jnp.dot(cond, w_ref[0], preferred_element_type=F32, precision=lax.Precision.HIGHEST) + b_ref[0]


def _modulation(c, ada_w, ada_b):
    depth, d, w6 = ada_w.shape
    b = c.shape[0]
    tn = 1536
    return pl.pallas_call(
        _mod_kernel,
        grid=(depth, w6 // tn),
        in_specs=[
            pl.BlockSpec((b, d), lambda l, j: (0, 0)),
            pl.BlockSpec((1, d, tn), lambda l, j: (l, 0, j)),
            pl.BlockSpec((1, 1, tn), lambda l, j: (l, 0, j)),
        ],
        out_specs=pl.BlockSpec((1, b, tn), lambda l, j: (l, 0, j)),
        out_shape=jax.ShapeDtypeStruct((depth, b, w6), F32),
        compiler_params=_params(("arbitrary", "arbitrary")),
        name="modulation",
    )(c, ada_w, ada_b.reshape(depth, 1, w6))


def _rope_tables(s, head_dim):
    quarter = head_dim // 4
    t = jnp.arange(s, dtype=F32)
    row = jnp.floor(t / GRID_W)
    col = t - row * GRID_W
    inv_freq = ROPE_THETA ** (-jnp.arange(quarter, dtype=F32) / quarter)
    lane = np.arange(LANES) % head_dim
    use_col = lane >= head_dim // 2
    second = (lane % (head_dim // 2)) >= quarter
    freq = inv_freq[lane % quarter]
    pos = jnp.where(use_col[None, :], col[:, None], row[:, None])
    ang = pos * freq[None, :]
    sign = jnp.where(second, 1.0, -1.0).astype(F32)
    return jnp.cos(ang), jnp.sin(ang) * sign[None, :]


def _rope(z, cos, sin_signed, quarter):
    lane = lax.broadcasted_iota(I32, z.shape, 1)
    first = (lane % (2 * quarter)) < quarter
    partner = jnp.where(first, pltpu.roll(z, LANES - quarter, 1), pltpu.roll(z, quarter, 1))
    return z * cos + partner * sin_signed


def _inproj_kernel(x_ref, mod_ref, g_ref, w_ref, ca_ref, sa_ref, cr_ref, sr_ref, qg_ref, kg_ref, vs_ref,
                   qa_ref, kd_ref, vt_ref, qr_ref, kr_ref, vr_ref, gr_ref):
    x = x_ref[...]
    m = mod_ref[0]
    h = _rms(x, g_ref[...]) * (1.0 + m[1:2, :]) + m[0:1, :]
    hb = h.astype(BF16)
    tm = x.shape[0]

    def proj(c0):
        z = jnp.dot(hb, w_ref[:, c0:c0 + 2 * LANES], preferred_element_type=F32)
        return z[:, :LANES], z[:, LANES:]

    ri = lax.broadcasted_iota(I32, (LANES, LANES), 0) // HEAD_DIM
    ci = lax.broadcasted_iota(I32, (LANES, LANES), 1) // HEAD_DIM
    head_ones = (ri == ci).astype(BF16)
    ca, sa = ca_ref[...], sa_ref[...]
    cr, sr = cr_ref[...], sr_ref[...]
    lane = lax.broadcasted_iota(I32, (tm, LANES), 1)
    low = lane < HEAD_DIM

    def head_norm_rope(z, g):
        ss = jnp.dot((z * z).astype(BF16), head_ones, preferred_element_type=F32)
        zn = z * lax.rsqrt(ss * (1.0 / HEAD_DIM) + NORM_EPS) * g
        return _rope(zn, ca, sa, HEAD_DIM // 4)

    def store_pair(ref, j, fn):
        def finish(zs):
            for i, z in enumerate(zs):
                ref[:, (2 * j + i) * LANES:(2 * j + i + 1) * LANES] = fn(z).astype(BF16)
        return finish

    def store_kv(zs):
        k, v = zs
        k = head_norm_rope(k, kg_ref[...])
        k_sw = pltpu.roll(k, HEAD_DIM, 1)
        v = v * vs_ref[0]
        kd_ref[0, 0] = jnp.where(low, k, k_sw).astype(BF16)
        kd_ref[0, 1] = jnp.where(low, k_sw, k).astype(BF16)
        vt_ref[0, 0] = jnp.where(low, v, 1.0).T.astype(FP8)
        vt_ref[0, 1] = jnp.where(low, pltpu.roll(v, HEAD_DIM, 1), 1.0).T.astype(FP8)

    jobs = [(2 * j * LANES, store_pair(qa_ref, j, lambda z: head_norm_rope(z, qg_ref[...]) * Q_SCALE))
            for j in range(ATTN_W // (2 * LANES))]
    jobs.append((ATTN_W, store_kv))
    base = ATTN_W + 2 * KV_W
    for j in range(RET_W // (2 * LANES)):
        off = 2 * j * LANES
        jobs += [
            (base + off, store_pair(qr_ref, j, lambda z: _rope(z, cr, sr, RET_DIM // 4))),
            (base + RET_W + off, store_pair(kr_ref, j, lambda z: _rope(z, cr, sr, RET_DIM // 4) * (RET_DIM ** -0.5))),
            (base + 2 * RET_W + off, store_pair(vr_ref, j, lambda z: z)),
            (base + 3 * RET_W + off, store_pair(gr_ref, j, lambda z: z * jax.nn.sigmoid(z))),
        ]
    zs = proj(jobs[0][0])
    for i, (_, finish) in enumerate(jobs):
        nxt = proj(jobs[i + 1][0]) if i + 1 < len(jobs) else None
        finish(zs)
        zs = nxt


def _inproj(x2, batch0, x_batch0, mod_l, pre_g, w_in_b, ropes, qg, kg, v_scale, b, s):
    n = b * s
    tm = min(ROW_TILE, s)
    nb = s // tm
    ca, sa, cr, sr = ropes
    row = lambda i: (i, 0)
    full = lambda i: (0, 0)
    rope_spec = pl.BlockSpec((tm, LANES), lambda i: (i % nb, 0))
    wide = jax.ShapeDtypeStruct((n, RET_W), BF16)
    return pl.pallas_call(
        _inproj_kernel,
        grid=(n // tm,),
        in_specs=[
            pl.BlockSpec((tm, D_MODEL), lambda i: (x_batch0 * nb + i, 0)),
            pl.BlockSpec((1, 6, D_MODEL), lambda i: (batch0 + i // nb, 0, 0)),
            pl.BlockSpec((1, D_MODEL), full),
            pl.BlockSpec((D_MODEL, IN_W), full),
            rope_spec, rope_spec, rope_spec, rope_spec,
            pl.BlockSpec((1, LANES), full),
            pl.BlockSpec((1, LANES), full),
            pl.BlockSpec((1, 1, LANES), lambda i: (batch0 + i // nb, 0, 0)),
        ],
        out_specs=[
            pl.BlockSpec((tm, ATTN_W), row),
            pl.BlockSpec((1, KV_HEADS, tm, LANES), lambda i: (i // nb, 0, i % nb, 0)),
            pl.BlockSpec((1, KV_HEADS, LANES, tm), lambda i: (i // nb, 0, 0, i % nb)),
            pl.BlockSpec((tm, RET_W), row),
            pl.BlockSpec((tm, RET_W), row),
            pl.BlockSpec((tm, RET_W), row),
            pl.BlockSpec((tm, RET_W), row),
        ],
        out_shape=[
            jax.ShapeDtypeStruct((n, ATTN_W), BF16),
            jax.ShapeDtypeStruct((b, KV_HEADS, s, LANES), BF16),
            jax.ShapeDtypeStruct((b, KV_HEADS, LANES, s), FP8),
            wide, wide, wide, wide,
        ],
        compiler_params=_params(("parallel",)),
        name="inproj",
    )(x2, mod_l, pre_g, w_in_b, ca, sa, cr, sr, qg, kg, v_scale)


def _attn_kernel(q_ref, kd_ref, vt_ref, fac_ref, o_ref, q_s, qn_s, m_ref, acc_ref, s0_ref, s1_ref,
                 *, tq, tk, tiles):
    s = kd_ref.shape[2]
    nk = s // tk
    nq = s // tq
    pairs = (Q_HEADS // KV_HEADS) // 2
    top = lax.broadcasted_iota(I32, (LANES, tq), 0) < HEAD_DIM
    unscale = jnp.concatenate([fac_ref[0]] * (2 * tq // LANES), axis=1)

    chains = tiles * pairs

    def load_q(trip, dst):
        for u in range(tiles):
            rows = pl.ds(pl.multiple_of((trip * tiles + u) * tq, tq), tq)
            for p in range(pairs):
                qt = q_ref[rows, p * LANES:(p + 1) * LANES].astype(F32).T
                dst[u * pairs + p, :, :tq] = jnp.where(top, qt, 0.0).astype(BF16)
                dst[u * pairs + p, :, tq:] = jnp.where(top, 0.0, qt).astype(BF16)

    def keys(j):
        return kd_ref[0, 0, j * tk:(j + 1) * tk, :]

    def step(j, cur_ref, nxt_ref, q_next, j_next):
        vt = vt_ref[0, 0, :, j * tk:(j + 1) * tk]
        kd = keys(j_next)
        nxt_ref[0] = jnp.dot(kd, q_next[0], preferred_element_type=F32)
        for c in range(chains):
            sc = cur_ref[c]
            m_old = m_ref[c]
            m_new = jnp.maximum(m_old, jnp.max(sc, axis=0, keepdims=True))
            pt = jnp.exp2(sc - (m_new - P_SHIFT))
            alpha = jnp.exp2(m_old - m_new)
            if c + 1 < chains:
                nxt_ref[c + 1] = jnp.dot(kd, q_next[c + 1], preferred_element_type=F32)
            acc_ref[c] = alpha * acc_ref[c] + jnp.dot(vt, pt.astype(FP8), preferred_element_type=F32)
            m_ref[c] = m_new

    bufs = (s0_ref, s1_ref)
    trips = nq // tiles
    load_q(0, q_s)
    for c in range(chains):
        s0_ref[c] = jnp.dot(keys(0), q_s[c], preferred_element_type=F32)

    def body(trip, carry):
        load_q(trip, q_s)
        load_q(jnp.minimum(trip + 1, trips - 1), qn_s)
        m_ref[...] = jnp.full(m_ref.shape, -jnp.inf, F32)
        acc_ref[...] = jnp.zeros(acc_ref.shape, F32)
        for j in range(nk):
            last = j + 1 == nk
            step(j, bufs[j % len(bufs)], bufs[(j + 1) % len(bufs)], qn_s if last else q_s, 0 if last else j + 1)
        for c in range(chains):
            u, p = divmod(c, pairs)
            rows = pl.ds(pl.multiple_of((trip * tiles + u) * tq, tq), tq)
            acc = acc_ref[c]
            o_t = acc[:HEAD_DIM] / acc[HEAD_DIM:] * unscale
            both = jnp.concatenate([o_t[:, :tq], o_t[:, tq:]], axis=0)
            o_ref[rows, p * LANES:(p + 1) * LANES] = both.T.astype(BF16)
        return carry

    lax.fori_loop(0, trips, body, 0)


def _attention(qa, kd, vt, fac, batch0, b, s):
    n = qa.shape[0]
    tq = Q_TILE
    tk = min(KV_TILE, s // 2)
    tiles = Q_TILES_PER_TRIP
    assert tq == LANES and s % (tiles * tq) == 0 and s % (2 * tk) == 0 and tk % LANES == 0, (s, tq, tk)
    gw = ATTN_W // KV_HEADS
    chains = tiles * ((Q_HEADS // KV_HEADS) // 2)
    return pl.pallas_call(
        functools.partial(_attn_kernel, tq=tq, tk=tk, tiles=tiles),
        grid=(b, KV_HEADS),
        in_specs=[
            pl.BlockSpec((s, gw), lambda bi, h: (bi, h)),
            pl.BlockSpec((1, 1, s, LANES), lambda bi, h: (bi, h, 0, 0)),
            pl.BlockSpec((1, 1, LANES, s), lambda bi, h: (bi, h, 0, 0)),
            pl.BlockSpec((1, 1, LANES), lambda bi, h: (batch0 + bi, 0, 0)),
        ],
        out_specs=pl.BlockSpec((s, gw), lambda bi, h: (bi, h)),
        out_shape=jax.ShapeDtypeStruct((n, ATTN_W), BF16),
        scratch_shapes=[
            pltpu.VMEM((chains, LANES, 2 * tq), BF16),
            pltpu.VMEM((chains, LANES, 2 * tq), BF16),
            pltpu.VMEM((chains, 1, 2 * tq), F32),
            pltpu.VMEM((chains, LANES, 2 * tq), F32),
            pltpu.VMEM((chains, tk, 2 * tq), F32),
            pltpu.VMEM((chains, tk, 2 * tq), F32),
        ],
        compiler_params=_params(("parallel", "parallel")),
        name="attention",
    )(qa, kd, vt, fac)


def _retention_kernel(dec_ref, q_ref, k_ref, v_ref, g_ref, gng_ref, gnb_ref, o_ref,
                      fbuf, bbuf, sf_ref, sb_ref):
    h = pl.program_id(1)
    c = RET_CHUNK
    s = q_ref.shape[1]
    nc = s // c
    lgf = jnp.full((1, 1), dec_ref[0, h], F32)
    lgb = jnp.full((1, 1), dec_ref[1, h], F32)
    ii = lax.broadcasted_iota(I32, (c, c), 0)
    jj = lax.broadcasted_iota(I32, (c, c), 1)
    diff = (ii - jj).astype(F32)
    d_f = jnp.where(diff >= 0, jnp.exp(lgf * jnp.maximum(diff, 0.0)), 0.0)
    d_b = jnp.where(diff <= 0, jnp.exp(lgb * jnp.maximum(-diff, 0.0)), 0.0)
    idx = lax.broadcasted_iota(I32, (c, 1), 0).astype(F32)
    xi_f = jnp.exp(lgf * (idx + 1.0))
    zeta_f = jnp.exp(lgf * (c - 1.0 - idx))
    xi_b = jnp.exp(lgb * (c - idx))
    zeta_b = jnp.exp(lgb * idx)
    cd_f = jnp.exp(lgf * c)
    cd_b = jnp.exp(lgb * c)
    sf_ref[...] = jnp.zeros(sf_ref.shape, F32)
    sb_ref[...] = jnp.zeros(sb_ref.shape, F32)
    nt = (((1,), (1,)), ((), ()))

    fwd = (d_f, xi_f, zeta_f, cd_f, sf_ref, fbuf)
    bwd = (d_b, xi_b, zeta_b, cd_b, sb_ref, bbuf)

    per = min(RET_CHUNKS_PER_TRIP, nc)

    def body(i, carry):
        jobs = []
        for n in range(per):
            jobs += [(fwd, per * i + n), (bwd, nc - 1 - per * i - n)]
        offs = [pl.multiple_of(ci * c, c) for _, ci in jobs]
        qs = [q_ref[0, pl.ds(off, c), :] for off in offs]
        ks = [k_ref[0, pl.ds(off, c), :] for off in offs]
        vs = [v_ref[0, pl.ds(off, c), :] for off in offs]
        scores = [lax.dot_general(q, k, nt, preferred_element_type=F32) for q, k in zip(qs, ks)]
        incs = []
        for (d, _), k, v in zip(jobs, ks, vs):
            kzt = (k.astype(F32) * d[2]).T.astype(BF16)
            incs.append(jnp.dot(kzt, v, preferred_element_type=F32))
        states = [fwd[4][...], bwd[4][...]]
        cross = []
        for n, (d, _) in enumerate(jobs):
            cross.append(jnp.dot(qs[n], states[n].astype(BF16), preferred_element_type=F32))
            states.append(states[n] * d[3] + incs[n])
        for n, ((d, _), off) in enumerate(zip(jobs, offs)):
            a = (scores[n] * d[0]).astype(BF16)
            d[5][pl.ds(off, c), :] = jnp.dot(a, vs[n], preferred_element_type=F32) + cross[n] * d[1]
        fwd[4][...] = states[2 * per]
        bwd[4][...] = states[2 * per + 1]
        return carry

    lax.fori_loop(0, nc // per, body, 0)

    rows = min(1024, s)

    def fin(t, carry):
        off = pl.multiple_of(t * rows, rows)
        r = fbuf[pl.ds(off, rows), :] + bbuf[pl.ds(off, rows), :]
        mu = jnp.mean(r, axis=-1, keepdims=True)
        d = r - mu
        var = jnp.mean(d * d, axis=-1, keepdims=True)
        y = d * lax.rsqrt(var + NORM_EPS) * gng_ref[...] + gnb_ref[...]
        o_ref[0, pl.ds(off, rows), :] = (g_ref[0, pl.ds(off, rows), :].astype(F32) * y).astype(BF16)
        return carry

    lax.fori_loop(0, s // rows, fin, 0)


def _retention(qr, kr, vr, gr, dec, gn_g, gn_b, b, s):
    shp = (b, s, RET_W)
    assert (s // RET_CHUNK) % min(RET_CHUNKS_PER_TRIP, s // RET_CHUNK) == 0, s
    head = pl.BlockSpec((1, s, RET_DIM), lambda bi, h: (bi, 0, h))
    vec = pl.BlockSpec((1, RET_DIM), lambda bi, h: (0, h))
    out = pl.pallas_call(
        _retention_kernel,
        grid=(b, RET_HEADS),
        in_specs=[pl.BlockSpec(memory_space=pltpu.SMEM), head, head, head, head, vec, vec],
        out_specs=head,
        out_shape=jax.ShapeDtypeStruct(shp, BF16),
        scratch_shapes=[
            pltpu.VMEM((s, RET_DIM), F32),
            pltpu.VMEM((s, RET_DIM), F32),
            pltpu.VMEM((RET_DIM, RET_DIM), F32),
            pltpu.VMEM((RET_DIM, RET_DIM), F32),
        ],
        compiler_params=_params(("parallel", "parallel")),
        name="retention",
    )(dec, qr.reshape(shp), kr.reshape(shp), vr.reshape(shp), gr.reshape(shp), gn_g, gn_b)
    return out.reshape(b * s, RET_W)


def _outproj_kernel(oa_ref, or_ref, w_ref, x_ref, mod_ref, pg_ref, fg_ref, rw_ref, rb_ref,
                    xn_ref, hp_ref, ti_ref, tw_ref, rk_ref, cnt_ref, base_ref):
    @pl.when(pl.program_id(0) == 0)
    def _():
        base_ref[...] = jnp.zeros(base_ref.shape, F32)

    m = mod_ref[0]
    tm = x_ref.shape[0]
    nt = (((1,), (1,)), ((), ()))
    halves = [slice(0, tm // 2), slice(tm // 2, tm)]
    ys = [jnp.dot(oa_ref[r, :], w_ref[:ATTN_W, :], preferred_element_type=F32)
          + jnp.dot(or_ref[r, :], w_ref[ATTN_W:, :], preferred_element_type=F32) for r in halves]
    parts = []
    for r, y in zip(halves, ys):
        xn = x_ref[r, :] + m[2:3, :] * _rms(y, pg_ref[...])
        xn_ref[r, :] = xn
        h = _rms(xn, fg_ref[...]) * (1.0 + m[4:5, :]) + m[3:4, :]
        hb = h.astype(BF16)
        u = lax.bitcast_convert_type(hb.astype(F32), U32)
        hp_ref[r, :] = (u[:, :HALF] >> 16) | u[:, HALF:]
        parts.append(lax.dot_general(rw_ref[...], hb, nt, preferred_element_type=F32))
    logits = jnp.concatenate(parts, axis=1) + rb_ref[...]
    e_iota = lax.broadcasted_iota(I32, (N_EXPERTS, tm), 0).astype(F32)
    vals, idxs = [], []
    cur = logits
    for _ in range(TOP_K):
        mx = jnp.max(cur, axis=0, keepdims=True)
        ik = jnp.min(jnp.where(cur == mx, e_iota, float(N_EXPERTS)), axis=0, keepdims=True)
        vals.append(mx)
        idxs.append(ik)
        cur = jnp.where(e_iota == ik, -jnp.inf, cur)
    v = jnp.concatenate(vals, axis=0)
    w = jnp.exp(v - v[0:1, :])
    tw_ref[...] = w / jnp.sum(w, axis=0, keepdims=True)
    ti_ref[...] = jnp.concatenate(idxs, axis=0).astype(I32)

    onehot = jnp.zeros((N_EXPERTS, tm), F32)
    for ik in idxs:
        onehot = onehot + (e_iota == ik).astype(F32)
    before = (lax.broadcasted_iota(I32, (tm, tm), 0) < lax.broadcasted_iota(I32, (tm, tm), 1)).astype(BF16)
    seen = jnp.dot(onehot.astype(BF16), before, preferred_element_type=F32) + base_ref[...]
    ranks = [jnp.sum(jnp.where(e_iota == ik, seen, 0.0), axis=0, keepdims=True) for ik in idxs]
    rk_ref[...] = jnp.concatenate(ranks, axis=0).astype(I32)
    total = base_ref[...] + jnp.sum(onehot, axis=1, keepdims=True)
    base_ref[...] = total
    cnt_ref[...] = jnp.broadcast_to(total, cnt_ref.shape).astype(I32)


def _outproj(oa, orr, w_out_b, x2, batch0, x_batch0, mod_l, post_g, ffn_g, rw_t, rb, b, s):
    n = b * s
    tm = min(ROW_TILE, s)
    nb = s // tm
    row = lambda i: (i, 0)
    full = lambda i: (0, 0)
    col = lambda i: (0, i)
    return pl.pallas_call(
        _outproj_kernel,
        grid=(n // tm,),
        in_specs=[
            pl.BlockSpec((tm, ATTN_W), row),
            pl.BlockSpec((tm, RET_W), row),
            pl.BlockSpec((D_MODEL, D_MODEL), full),
            pl.BlockSpec((tm, D_MODEL), lambda i: (x_batch0 * nb + i, 0)),
            pl.BlockSpec((1, 6, D_MODEL), lambda i: (batch0 + i // nb, 0, 0)),
            pl.BlockSpec((1, D_MODEL), full),
            pl.BlockSpec((1, D_MODEL), full),
            pl.BlockSpec((N_EXPERTS, D_MODEL), full),
            pl.BlockSpec((N_EXPERTS, 1), full),
        ],
        out_specs=[
            pl.BlockSpec((tm, D_MODEL), row),
            pl.BlockSpec((tm, HALF), row),
            pl.BlockSpec((TOP_K, tm), col),
            pl.BlockSpec((TOP_K, tm), col),
            pl.BlockSpec((TOP_K, tm), col),
            pl.BlockSpec((N_EXPERTS, LANES), full),
        ],
        out_shape=[
            jax.ShapeDtypeStruct((n, D_MODEL), F32),
            jax.ShapeDtypeStruct((n, HALF), U32),
            jax.ShapeDtypeStruct((TOP_K, n), I32),
            jax.ShapeDtypeStruct((TOP_K, n), F32),
            jax.ShapeDtypeStruct((TOP_K, n), I32),
            jax.ShapeDtypeStruct((N_EXPERTS, LANES), I32),
        ],
        scratch_shapes=[pltpu.VMEM((N_EXPERTS, 1), F32)],
        compiler_params=_params(("arbitrary",)),
        name="outproj_router",
    )(oa, orr, w_out_b, x2, mod_l, post_g, ffn_g, rw_t, rb)


def _sc_scatter_rows(rows, pos3, n_out):
    n, w = rows.shape
    nchunk, kk, c = pos3.shape
    per_w = nchunk // SC_WORKERS
    mesh = plsc.VectorSubcoreMesh(core_axis_name="c", subcore_axis_name="s")

    @functools.partial(
        pl.kernel, mesh=mesh,
        out_type=jax.ShapeDtypeStruct((n_out, w), rows.dtype),
        scratch_types=[pltpu.VMEM((kk, c), I32), pltpu.VMEM((c, w), rows.dtype)],
    )
    def k(rows_hbm, pos_hbm, out_hbm, idx_v, rows_v):
        wid = lax.axis_index("s") * SC_CORES + lax.axis_index("c")

        @pl.loop(0, per_w)
        def _(i):
            ch = wid * per_w + i
            pltpu.sync_copy(pos_hbm.at[ch], idx_v)
            pltpu.sync_copy(rows_hbm.at[pl.ds(ch * c, c)], rows_v)
            for j in range(kk):
                pltpu.sync_copy(rows_v, out_hbm.at[idx_v.at[j]])

    return k(rows, pos3)


def _sc_gather_rows(table, idx):
    w = table.shape[1]
    b = idx.shape[0]
    c = SC_CHUNK
    per_w = b // (SC_WORKERS * c)
    mesh = plsc.VectorSubcoreMesh(core_axis_name="c", subcore_axis_name="s")

    @functools.partial(
        pl.kernel, mesh=mesh,
        out_type=jax.ShapeDtypeStruct((b, w), table.dtype),
        scratch_types=[pltpu.VMEM((c,), I32), pltpu.VMEM((c, w), table.dtype), pltpu.SemaphoreType.DMA],
    )
    def k(table_hbm, idx_hbm, out_hbm, idx_v, rows_v, sem):
        wid = lax.axis_index("s") * SC_CORES + lax.axis_index("c")

        @pl.loop(0, per_w)
        def _(i):
            base = (wid * per_w + i) * c
            pltpu.sync_copy(idx_hbm.at[pl.ds(base, c)], idx_v)
            pltpu.async_copy(table_hbm.at[idx_v], rows_v, sem).wait()
            pltpu.sync_copy(rows_v, out_hbm.at[pl.ds(base, c)])

    return k(table, idx)


def _experts_kernel(te_ref, nv_ref, xs_ref, wu_ref, bu_ref, wd_ref, bd_ref, ys_ref, wu_b, wd_b, inv_ref):
    i = pl.program_id(0)
    live = i < nv_ref[0]
    new_expert = jnp.logical_or(i == 0, te_ref[i] != te_ref[jnp.maximum(i - 1, 0)])

    def full_max(a):
        return jnp.max(jnp.max(jnp.abs(a), axis=0, keepdims=True), axis=1, keepdims=True)

    @pl.when(jnp.logical_and(live, new_expert))
    def _():
        wu = wu_ref[0, 0].astype(BF16)
        wd = wd_ref[0, 0].astype(BF16)
        mu = jnp.maximum(full_max(wu).astype(F32), 1e-30)
        md = jnp.maximum(full_max(wd).astype(F32), 1e-30)
        wu_b[...] = (wu * (FP8_RANGE / mu).astype(BF16)).astype(FP8)
        wd_b[...] = (wd * (FP8_RANGE / md).astype(BF16)).astype(FP8)
        inv_ref[0:1, :] = jnp.broadcast_to(mu / FP8_RANGE, (1, LANES))
        inv_ref[1:2, :] = jnp.broadcast_to(md / (FP8_RANGE * ACT_SCALE), (1, LANES))

    @pl.when(live)
    def _():
        t = xs_ref.shape[0]
        halves = [slice(0, t // 2), slice(t // 2, t)]
        inv_up = inv_ref[0:1, 0:1]
        inv_dn = inv_ref[1:2, 0:1]
        ups = []
        for r in halves:
            w = xs_ref[r, :]
            lo = _unpack_lo(w)
            hi = _unpack_hi(w)
            amax = jnp.max(jnp.maximum(jnp.abs(lo), jnp.abs(hi)), axis=1, keepdims=True)
            amax = jnp.maximum(amax, 1e-30)
            c = FP8_RANGE / amax
            up = (jnp.dot((lo * c).astype(FP8), wu_b[:HALF, :], preferred_element_type=F32)
                  + jnp.dot((hi * c).astype(FP8), wu_b[HALF:, :], preferred_element_type=F32))
            ups.append(up * (amax * (inv_up / FP8_RANGE)))
        for r, up in zip(halves, ups):
            up = up + bu_ref[0, 0]
            glu = jnp.minimum(up[:, :D_FF], SWIGLU_LIMIT)
            lin = jnp.clip(up[:, D_FF:], -SWIGLU_LIMIT, SWIGLU_LIMIT)
            act = glu * jax.nn.sigmoid(SWIGLU_ALPHA * glu) * (lin + 1.0)
            y = jnp.dot((act * ACT_SCALE).astype(FP8), wd_b[...], preferred_element_type=F32) * inv_dn + bd_ref[0, 0]
            ys_ref[r, :] = _pack_bf16_pairs(y)


def _experts(xs, tile_expert, n_valid, layer, wu, bu, wd, bd):
    n_pad = xs.shape[0]
    t = MOE_TILE
    grid_spec = pltpu.PrefetchScalarGridSpec(
        num_scalar_prefetch=2,
        grid=(n_pad // t,),
        in_specs=[
            pl.BlockSpec((t, HALF), lambda i, te, nv: (i, 0)),
            pl.BlockSpec((1, 1, D_MODEL, 2 * D_FF), lambda i, te, nv: (layer, te[i], 0, 0)),
            pl.BlockSpec((1, 1, 1, 2 * D_FF), lambda i, te, nv: (layer, te[i], 0, 0)),
            pl.BlockSpec((1, 1, D_FF, D_MODEL), lambda i, te, nv: (layer, te[i], 0, 0)),
            pl.BlockSpec((1, 1, 1, D_MODEL), lambda i, te, nv: (layer, te[i], 0, 0)),
        ],
        out_specs=pl.BlockSpec((t, HALF), lambda i, te, nv: (i, 0)),
        scratch_shapes=[pltpu.VMEM((D_MODEL, 2 * D_FF), FP8), pltpu.VMEM((D_FF, D_MODEL), FP8),
                        pltpu.VMEM((2, LANES), F32)],
    )
    return pl.pallas_call(
        _experts_kernel,
        grid_spec=grid_spec,
        out_shape=jax.ShapeDtypeStruct((n_pad, HALF), U32),
        compiler_params=pltpu.CompilerParams(dimension_semantics=("arbitrary",), vmem_limit_bytes=EXPERTS_VMEM_LIMIT),
        name="experts",
    )(tile_expert, n_valid, xs, wu, bu, wd, bd)


def _combine_kernel(y0_ref, y1_ref, y2_ref, y3_ref, w_ref, x_ref, mod_ref, g_ref, *rest):
    o_ref = rest[-1]
    w = w_ref[...]
    lo = None
    hi = None
    for k, y_ref in enumerate((y0_ref, y1_ref, y2_ref, y3_ref)):
        word = y_ref[...]
        wk = w[:, k:k + 1]
        lo_k = wk * _unpack_lo(word)
        hi_k = wk * _unpack_hi(word)
        lo = lo_k if lo is None else lo + lo_k
        hi = hi_k if hi is None else hi + hi_k
    yf = jnp.concatenate([lo, hi], axis=1)
    m = mod_ref[0]
    o_ref[...] = x_ref[...] + m[5:6, :] * _rms(yf, g_ref[...])


def _combine(yg, tw_t, xn, batch0, mod_l, post_g, b, s, out_batches=None, shared_out=None):
    n = b * s
    tm = min(ROW_TILE, s)
    nb = s // tm
    nblk = n // tm
    row = lambda i: (i, 0)
    full = lambda i: (0, 0)
    y_specs = [pl.BlockSpec((tm, HALF), functools.partial(lambda i, k: (k * nblk + i, 0), k=k)) for k in range(TOP_K)]
    out_rows, out_blk0 = (n, 0) if out_batches is None else (out_batches * s, batch0 * nb)
    extra_specs, extra_args, aliases = [], [], {}
    if shared_out is not None:
        extra_specs, extra_args, aliases = [pl.BlockSpec(memory_space=pl.ANY)], [shared_out], {8: 0}
    return pl.pallas_call(
        _combine_kernel,
        grid=(nblk,),
        in_specs=y_specs + [
            pl.BlockSpec((tm, TOP_K), row),
            pl.BlockSpec((tm, D_MODEL), row),
            pl.BlockSpec((1, 6, D_MODEL), lambda i: (batch0 + i // nb, 0, 0)),
            pl.BlockSpec((1, D_MODEL), full),
        ] + extra_specs,
        out_specs=pl.BlockSpec((tm, D_MODEL), lambda i: (out_blk0 + i, 0)),
        out_shape=jax.ShapeDtypeStruct((out_rows, D_MODEL), F32),
        input_output_aliases=aliases,
        compiler_params=_params(("parallel",)),
        name="combine",
    )(yg, yg, yg, yg, tw_t, xn, mod_l, post_g, *extra_args)


def _route(counts, tidx, rank, n_tiles):
    t = MOE_TILE
    cnt = counts[:, 0]
    padded = ((cnt + t - 1) // t) * t
    ends = jnp.cumsum(padded)
    starts = ends - padded
    experts = jnp.arange(N_EXPERTS, dtype=I32)[:, None, None]
    start_of = jnp.sum(jnp.where(tidx[None] == experts, starts[:, None, None], 0), axis=0)
    pos = start_of + rank
    tile_ends = ends // t
    n_valid = tile_ends[-1]
    tile = jnp.minimum(jnp.arange(n_tiles, dtype=I32), n_valid - 1)
    tile_expert = jnp.minimum(jnp.sum(tile_ends[None, :] <= tile[:, None], axis=1), N_EXPERTS - 1).astype(I32)
    return pos.astype(I32), tile_expert, n_valid.reshape(1).astype(I32)


def _value_range_scale(mod_l, pre_g, w_in_l):
    d = pre_g.shape[0]
    shift, scale = mod_l[:, 0, :], mod_l[:, 1, :]
    h_norm = d ** 0.5 * jnp.max(jnp.abs(pre_g[None, :] * (1.0 + scale)), axis=1) + jnp.linalg.norm(shift, axis=1)
    w_v = w_in_l[:, ATTN_W + KV_W:ATTN_W + 2 * KV_W]
    bound = 1.02 * h_norm * jnp.max(jnp.linalg.norm(w_v, axis=0))
    v_scale = V_RANGE / jnp.maximum(bound, 1e-30)
    lanes = lambda t: jnp.broadcast_to(t[:, None, None], (mod_l.shape[0], 1, LANES))
    return lanes(v_scale), lanes(1.0 / v_scale)


def kernel(x, c, ada_w, ada_b, pre_mix_g, post_mix_g, w_in, q_norm_g, k_norm_g, ret_decay_fwd, ret_decay_bwd, ret_gn_g, ret_gn_b, w_out, pre_ffn_g, post_ffn_g, router_w, router_b, exp_w_up, exp_b_up, exp_w_down, exp_b_down):
    b, s, d = x.shape
    depth = ada_w.shape[0]
    groups = 2 if b % 2 == 0 else 1
    bg = b // groups
    n = bg * s
    n_asg = n * TOP_K
    n_pad = n_asg + N_EXPERTS * MOE_TILE
    n_tiles = n_pad // MOE_TILE

    mod = _modulation(c, ada_w, ada_b)
    ropes = _rope_tables(s, HEAD_DIM) + _rope_tables(s, RET_DIM)
    bu = exp_b_up.reshape(depth, N_EXPERTS, 1, 2 * D_FF)
    bd = exp_b_down.reshape(depth, N_EXPERTS, 1, d)
    xs_in = [x.reshape(b * s, d)] * groups
    x_b0 = [g * bg for g in range(groups)]
    out = None
    for l in range(depth):
        mod_l = mod[l].reshape(b, 6, d)
        qg = jnp.tile(q_norm_g[l], LANES // HEAD_DIM).reshape(1, LANES)
        kg = jnp.tile(k_norm_g[l], LANES // HEAD_DIM).reshape(1, LANES)
        v_scale, fac = _value_range_scale(mod_l, pre_mix_g[l], w_in[l])
        w_in_b = w_in[l].astype(BF16)
        w_out_b = w_out[l].astype(BF16)
        rw_t = router_w[l].T.astype(BF16)
        dec = jnp.stack([ret_decay_fwd[l], ret_decay_bwd[l]]).astype(F32)
        last = l + 1 == depth
        proj = [_inproj(xs_in[g], g * bg, x_b0[g], mod_l, pre_mix_g[l].reshape(1, d), w_in_b, ropes, qg, kg, v_scale,
                        bg, s) for g in range(groups)]
        oa = [_attention(p[0], p[1], p[2], fac, g * bg, bg, s) for g, p in enumerate(proj)]
        orr = [_retention(p[3], p[4], p[5], p[6], dec, ret_gn_g[l].reshape(1, RET_W), ret_gn_b[l].reshape(1, RET_W),
                          bg, s) for p in proj]
        routed = []
        for g in range(groups):
            xn, hp, tidx, tw, rank, counts = _outproj(
                oa[g], orr[g], w_out_b, xs_in[g], g * bg, x_b0[g], mod_l, post_mix_g[l].reshape(1, d),
                pre_ffn_g[l].reshape(1, d), rw_t, router_b[l].reshape(N_EXPERTS, 1), bg, s)
            pos, tile_expert, n_valid = _route(counts, tidx, rank, n_tiles)
            pos3 = pos.reshape(TOP_K, n // SC_CHUNK, SC_CHUNK).transpose(1, 0, 2)
            routed.append((xn, tw, pos, tile_expert, n_valid, _sc_scatter_rows(hp, pos3, n_pad)))
        yg = []
        for xn, tw, pos, tile_expert, n_valid, xs in routed:
            ys = _experts(xs, tile_expert, n_valid, l, exp_w_up, bu, exp_w_down, bd)
            yg.append(_sc_gather_rows(ys, pos.reshape(n_asg)))
        nxt = []
        for g, (xn, tw, *_) in enumerate(routed):
            if last:
                out = _combine(yg[g], tw.T, xn, g * bg, mod_l, post_ffn_g[l].reshape(1, d), bg, s,
                               out_batches=b, shared_out=out)
            else:
                nxt.append(_combine(yg[g], tw.T, xn, g * bg, mod_l, post_ffn_g[l].reshape(1, d), bg, s))
        xs_in, x_b0 = nxt, [0] * groups
    return out.reshape(b, s, d)
```

```python
import functools

import numpy as np
import jax
import jax.numpy as jnp
from jax import lax
from jax.experimental import pallas as pl
from jax.experimental.pallas import tpu as pltpu
from jax.experimental.pallas import tpu_sc as plsc

F32 = jnp.float32
BF16 = jnp.bfloat16
U32 = jnp.uint32
I32 = jnp.int32
FP8 = jnp.float8_e4m3fn

D_MODEL = 1024
GRID_W = 64
ROPE_THETA = 10000.0
HEAD_DIM = 64
Q_HEADS = 8
KV_HEADS = 2
RET_HEADS = 4
RET_DIM = 128
RET_CHUNK = 128
ATTN_W = Q_HEADS * HEAD_DIM
KV_W = KV_HEADS * HEAD_DIM
RET_W = RET_HEADS * RET_DIM
IN_W = ATTN_W + 2 * KV_W + 4 * RET_W
N_EXPERTS = 32
TOP_K = 4
D_FF = D_MODEL
SWIGLU_LIMIT = 7.0
SWIGLU_ALPHA = 1.702
NORM_EPS = 1e-6
Q_SCALE = HEAD_DIM ** -0.5 * float(np.log2(np.e))
FP8_RANGE = 224.0
ACT_SCALE = FP8_RANGE / (SWIGLU_LIMIT * (SWIGLU_LIMIT + 1.0))
V_RANGE = FP8_RANGE
P_SHIFT = 8.0
HALF = D_MODEL // 2
LANES = 128

ROW_TILE = 512
Q_TILE = 128
Q_TILES_PER_TRIP = 2
KV_TILE = 1024
MOE_TILE = 512
RET_CHUNKS_PER_TRIP = 8
VMEM_LIMIT = 48 * 1024 * 1024
EXPERTS_VMEM_LIMIT = 56 * 1024 * 1024

SC_CORES = 2
SC_SUBCORES = 16
SC_WORKERS = SC_CORES * SC_SUBCORES
SC_CHUNK = 64


def _params(sem):
    return pltpu.CompilerParams(dimension_semantics=sem, vmem_limit_bytes=VMEM_LIMIT)


def _rms(x, g):
    return x * lax.rsqrt(jnp.mean(x * x, axis=-1, keepdims=True) + NORM_EPS) * g


def _pack_bf16_pairs(y):
    u = lax.bitcast_convert_type(y.astype(BF16).astype(F32), U32)
    return (u[:, :HALF] >> 16) | u[:, HALF:]


def _unpack_lo(w):
    return lax.bitcast_convert_type(w << 16, F32)


def _unpack_hi(w):
    return lax.bitcast_convert_type(w & jnp.uint32(0xFFFF0000), F32)


def _mod_kernel(c_ref, w_ref, b_ref, o_ref):
    c = c_ref[...]
    cond = c * jax.nn.sigmoid(c)
    o_ref[0] = jnp.dot(cond, w_ref[0], preferred_element_type=F32, precision=lax.Precision.HIGHEST) + b_ref[0]


def _modulation(c, ada_w, ada_b):
    depth, d, w6 = ada_w.shape
    b = c.shape[0]
    tn = 1536
    return pl.pallas_call(
        _mod_kernel,
        grid=(depth, w6 // tn),
        in_specs=[
            pl.BlockSpec((b, d), lambda l, j: (0, 0)),
            pl.BlockSpec((1, d, tn), lambda l, j: (l, 0, j)),
            pl.BlockSpec((1, 1, tn), lambda l, j: (l, 0, j)),
        ],
        out_specs=pl.BlockSpec((1, b, tn), lambda l, j: (l, 0, j)),
        out_shape=jax.ShapeDtypeStruct((depth, b, w6), F32),
        compiler_params=_params(("arbitrary", "arbitrary")),
        name="modulation",
    )(c, ada_w, ada_b.reshape(depth, 1, w6))


def _rope_tables(s, head_dim):
    quarter = head_dim // 4
    t = jnp.arange(s, dtype=F32)
    row = jnp.floor(t / GRID_W)
    col = t - row * GRID_W
    inv_freq = ROPE_THETA ** (-jnp.arange(quarter, dtype=F32) / quarter)
    lane = np.arange(LANES) % head_dim
    use_col = lane >= head_dim // 2
    second = (lane % (head_dim // 2)) >= quarter
    freq = inv_freq[lane % quarter]
    pos = jnp.where(use_col[None, :], col[:, None], row[:, None])
    ang = pos * freq[None, :]
    sign = jnp.where(second, 1.0, -1.0).astype(F32)
    return jnp.cos(ang), jnp.sin(ang) * sign[None, :]


def _rope(z, cos, sin_signed, quarter):
    lane = lax.broadcasted_iota(I32, z.shape, 1)
    first = (lane % (2 * quarter)) < quarter
    partner = jnp.where(first, pltpu.roll(z, LANES - quarter, 1), pltpu.roll(z, quarter, 1))
    return z * cos + partner * sin_signed


def _inproj_kernel(x_ref, mod_ref, g_ref, w_ref, ca_ref, sa_ref, cr_ref, sr_ref, qg_ref, kg_ref, vs_ref,
                   qa_ref, kd_ref, vt_ref, qr_ref, kr_ref, vr_ref, gr_ref):
    x = x_ref[...]
    m = mod_ref[0]
    h = _rms(x, g_ref[...]) * (1.0 + m[1:2, :]) + m[0:1, :]
    hb = h.astype(BF16)
    tm = x.shape[0]

    def proj(c0):
        z = jnp.dot(hb, w_ref[:, c0:c0 + 2 * LANES], preferred_element_type=F32)
        return z[:, :LANES], z[:, LANES:]

    ri = lax.broadcasted_iota(I32, (LANES, LANES), 0) // HEAD_DIM
    ci = lax.broadcasted_iota(I32, (LANES, LANES), 1) // HEAD_DIM
    head_ones = (ri == ci).astype(BF16)
    ca, sa = ca_ref[...], sa_ref[...]
    cr, sr = cr_ref[...], sr_ref[...]
    lane = lax.broadcasted_iota(I32, (tm, LANES), 1)
    low = lane < HEAD_DIM

    def head_norm_rope(z, g):
        ss = jnp.dot((z * z).astype(BF16), head_ones, preferred_element_type=F32)
        zn = z * lax.rsqrt(ss * (1.0 / HEAD_DIM) + NORM_EPS) * g
        return _rope(zn, ca, sa, HEAD_DIM // 4)

    def store_pair(ref, j, fn):
        def finish(zs):
            for i, z in enumerate(zs):
                ref[:, (2 * j + i) * LANES:(2 * j + i + 1) * LANES] = fn(z).astype(BF16)
        return finish

    def store_kv(zs):
        k, v = zs
        k = head_norm_rope(k, kg_ref[...])
        k_sw = pltpu.roll(k, HEAD_DIM, 1)
        v = v * vs_ref[0]
        kd_ref[0, 0] = jnp.where(low, k, k_sw).astype(BF16)
        kd_ref[0, 1] = jnp.where(low, k_sw, k).astype(BF16)
        vt_ref[0, 0] = jnp.where(low, v, 1.0).T.astype(FP8)
        vt_ref[0, 1] = jnp.where(low, pltpu.roll(v, HEAD_DIM, 1), 1.0).T.astype(FP8)

    jobs = [(2 * j * LANES, store_pair(qa_ref, j, lambda z: head_norm_rope(z, qg_ref[...]) * Q_SCALE))
            for j in range(ATTN_W // (2 * LANES))]
    jobs.append((ATTN_W, store_kv))
    base = ATTN_W + 2 * KV_W
    for j in range(RET_W // (2 * LANES)):
        off = 2 * j * LANES
        jobs += [
            (base + off, store_pair(qr_ref, j, lambda z: _rope(z, cr, sr, RET_DIM // 4))),
            (base + RET_W + off, store_pair(kr_ref, j, lambda z: _rope(z, cr, sr, RET_DIM // 4) * (RET_DIM ** -0.5))),
            (base + 2 * RET_W + off, store_pair(vr_ref, j, lambda z: z)),
            (base + 3 * RET_W + off, store_pair(gr_ref, j, lambda z: z * jax.nn.sigmoid(z))),
        ]
    zs = proj(jobs[0][0])
    for i, (_, finish) in enumerate(jobs):
        nxt = proj(jobs[i + 1][0]) if i + 1 < len(jobs) else None
        finish(zs)
        zs = nxt


def _inproj(x2, batch0, x_batch0, mod_l, pre_g, w_in_b, ropes, qg, kg, v_scale, b, s):
    n = b * s
    tm = min(ROW_TILE, s)
    nb = s // tm
    ca, sa, cr, sr = ropes
    row = lambda i: (i, 0)
    full = lambda i: (0, 0)
    rope_spec = pl.BlockSpec((tm, LANES), lambda i: (i % nb, 0))
    wide = jax.ShapeDtypeStruct((n, RET_W), BF16)
    return pl.pallas_call(
        _inproj_kernel,
        grid=(n // tm,),
        in_specs=[
            pl.BlockSpec((tm, D_MODEL), lambda i: (x_batch0 * nb + i, 0)),
            pl.BlockSpec((1, 6, D_MODEL), lambda i: (batch0 + i // nb, 0, 0)),
            pl.BlockSpec((1, D_MODEL), full),
            pl.BlockSpec((D_MODEL, IN_W), full),
            rope_spec, rope_spec, rope_spec, rope_spec,
            pl.BlockSpec((1, LANES), full),
            pl.BlockSpec((1, LANES), full),
            pl.BlockSpec((1, 1, LANES), lambda i: (batch0 + i // nb, 0, 0)),
        ],
        out_specs=[
            pl.BlockSpec((tm, ATTN_W), row),
            pl.BlockSpec((1, KV_HEADS, tm, LANES), lambda i: (i // nb, 0, i % nb, 0)),
            pl.BlockSpec((1, KV_HEADS, LANES, tm), lambda i: (i // nb, 0, 0, i % nb)),
            pl.BlockSpec((tm, RET_W), row),
            pl.BlockSpec((tm, RET_W), row),
            pl.BlockSpec((tm, RET_W), row),
            pl.BlockSpec((tm, RET_W), row),
        ],
        out_shape=[
            jax.ShapeDtypeStruct((n, ATTN_W), BF16),
            jax.ShapeDtypeStruct((b, KV_HEADS, s, LANES), BF16),
            jax.ShapeDtypeStruct((b, KV_HEADS, LANES, s), FP8),
            wide, wide, wide, wide,
        ],
        compiler_params=_params(("parallel",)),
        name="inproj",
    )(x2, mod_l, pre_g, w_in_b, ca, sa, cr, sr, qg, kg, v_scale)


def _attn_kernel(q_ref, kd_ref, vt_ref, fac_ref, o_ref, q_s, qn_s, m_ref, acc_ref, s0_ref, s1_ref,
                 *, tq, tk, tiles):
    s = kd_ref.shape[2]
    nk = s // tk
    nq = s // tq
    pairs = (Q_HEADS // KV_HEADS) // 2
    top = lax.broadcasted_iota(I32, (LANES, tq), 0) < HEAD_DIM
    unscale = jnp.concatenate([fac_ref[0]] * (2 * tq // LANES), axis=1)

    chains = tiles * pairs

    def load_q(trip, dst):
        for u in range(tiles):
            rows = pl.ds(pl.multiple_of((trip * tiles + u) * tq, tq), tq)
            for p in range(pairs):
                qt = q_ref[rows, p * LANES:(p + 1) * LANES].astype(F32).T
                dst[u * pairs + p, :, :tq] = jnp.where(top, qt, 0.0).astype(BF16)
                dst[u * pairs + p, :, tq:] = jnp.where(top, 0.0, qt).astype(BF16)

    def keys(j):
        return kd_ref[0, 0, j * tk:(j + 1) * tk, :]

    def step(j, cur_ref, nxt_ref, q_next, j_next):
        vt = vt_ref[0, 0, :, j * tk:(j + 1) * tk]
        kd = keys(j_next)
        nxt_ref[0] = jnp.dot(kd, q_next[0], preferred_element_type=F32)
        for c in range(chains):
            sc = cur_ref[c]
            m_old = m_ref[c]
            m_new = jnp.maximum(m_old, jnp.max(sc, axis=0, keepdims=True))
            pt = jnp.exp2(sc - (m_new - P_SHIFT))
            alpha = jnp.exp2(m_old - m_new)
            if c + 1 < chains:
                nxt_ref[c + 1] = jnp.dot(kd, q_next[c + 1], preferred_element_type=F32)
            acc_ref[c] = alpha * acc_ref[c] + jnp.dot(vt, pt.astype(FP8), preferred_element_type=F32)
            m_ref[c] = m_new

    bufs = (s0_ref, s1_ref)
    trips = nq // tiles
    load_q(0, q_s)
    for c in range(chains):
        s0_ref[c] = jnp.dot(keys(0), q_s[c], preferred_element_type=F32)

    def body(trip, carry):
        load_q(trip, q_s)
        load_q(jnp.minimum(trip + 1, trips - 1), qn_s)
        m_ref[...] = jnp.full(m_ref.shape, -jnp.inf, F32)
        acc_ref[...] = jnp.zeros(acc_ref.shape, F32)
        for j in range(nk):
            last = j + 1 == nk
            step(j, bufs[j % len(bufs)], bufs[(j + 1) % len(bufs)], qn_s if last else q_s, 0 if last else j + 1)
        for c in range(chains):
            u, p = divmod(c, pairs)
            rows = pl.ds(pl.multiple_of((trip * tiles + u) * tq, tq), tq)
            acc = acc_ref[c]
            o_t = acc[:HEAD_DIM] / acc[HEAD_DIM:] * unscale
            both = jnp.concatenate([o_t[:, :tq], o_t[:, tq:]], axis=0)
            o_ref[rows, p * LANES:(p + 1) * LANES] = both.T.astype(BF16)
        return carry

    lax.fori_loop(0, trips, body, 0)


def _attention(qa, kd, vt, fac, batch0, b, s):
    n = qa.shape[0]
    tq = Q_TILE
    tk = min(KV_TILE, s // 2)
    tiles = Q_TILES_PER_TRIP
    assert tq == LANES and s % (tiles * tq) == 0 and s % (2 * tk) == 0 and tk % LANES == 0, (s, tq, tk)
    gw = ATTN_W // KV_HEADS
    chains = tiles * ((Q_HEADS // KV_HEADS) // 2)
    return pl.pallas_call(
        functools.partial(_attn_kernel, tq=tq, tk=tk, tiles=tiles),
        grid=(b, KV_HEADS),
        in_specs=[
            pl.BlockSpec((s, gw), lambda bi, h: (bi, h)),
            pl.BlockSpec((1, 1, s, LANES), lambda bi, h: (bi, h, 0, 0)),
            pl.BlockSpec((1, 1, LANES, s), lambda bi, h: (bi, h, 0, 0)),
            pl.BlockSpec((1, 1, LANES), lambda bi, h: (batch0 + bi, 0, 0)),
        ],
        out_specs=pl.BlockSpec((s, gw), lambda bi, h: (bi, h)),
        out_shape=jax.ShapeDtypeStruct((n, ATTN_W), BF16),
        scratch_shapes=[
            pltpu.VMEM((chains, LANES, 2 * tq), BF16),
            pltpu.VMEM((chains, LANES, 2 * tq), BF16),
            pltpu.VMEM((chains, 1, 2 * tq), F32),
            pltpu.VMEM((chains, LANES, 2 * tq), F32),
            pltpu.VMEM((chains, tk, 2 * tq), F32),
            pltpu.VMEM((chains, tk, 2 * tq), F32),
        ],
        compiler_params=_params(("parallel", "parallel")),
        name="attention",
    )(qa, kd, vt, fac)


def _retention_kernel(dec_ref, q_ref, k_ref, v_ref, g_ref, gng_ref, gnb_ref, o_ref,
                      fbuf, bbuf, sf_ref, sb_ref):
    h = pl.program_id(1)
    c = RET_CHUNK
    s = q_ref.shape[1]
    nc = s // c
    lgf = jnp.full((1, 1), dec_ref[0, h], F32)
    lgb = jnp.full((1, 1), dec_ref[1, h], F32)
    ii = lax.broadcasted_iota(I32, (c, c), 0)
    jj = lax.broadcasted_iota(I32, (c, c), 1)
    diff = (ii - jj).astype(F32)
    d_f = jnp.where(diff >= 0, jnp.exp(lgf * jnp.maximum(diff, 0.0)), 0.0)
    d_b = jnp.where(diff <= 0, jnp.exp(lgb * jnp.maximum(-diff, 0.0)), 0.0)
    idx = lax.broadcasted_iota(I32, (c, 1), 0).astype(F32)
    xi_f = jnp.exp(lgf * (idx + 1.0))
    zeta_f = jnp.exp(lgf * (c - 1.0 - idx))
    xi_b = jnp.exp(lgb * (c - idx))
    zeta_b = jnp.exp(lgb * idx)
    cd_f = jnp.exp(lgf * c)
    cd_b = jnp.exp(lgb * c)
    sf_ref[...] = jnp.zeros(sf_ref.shape, F32)
    sb_ref[...] = jnp.zeros(sb_ref.shape, F32)
    nt = (((1,), (1,)), ((), ()))

    fwd = (d_f, xi_f, zeta_f, cd_f, sf_ref, fbuf)
    bwd = (d_b, xi_b, zeta_b, cd_b, sb_ref, bbuf)

    per = min(RET_CHUNKS_PER_TRIP, nc)

    def body(i, carry):
        jobs = []
        for n in range(per):
            jobs += [(fwd, per * i + n), (bwd, nc - 1 - per * i - n)]
        offs = [pl.multiple_of(ci * c, c) for _, ci in jobs]
        qs = [q_ref[0, pl.ds(off, c), :] for off in offs]
        ks = [k_ref[0, pl.ds(off, c), :] for off in offs]
        vs = [v_ref[0, pl.ds(off, c), :] for off in offs]
        scores = [lax.dot_general(q, k, nt, preferred_element_type=F32) for q, k in zip(qs, ks)]
        incs = []
        for (d, _), k, v in zip(jobs, ks, vs):
            kzt = (k.astype(F32) * d[2]).T.astype(BF16)
            incs.append(jnp.dot(kzt, v, preferred_element_type=F32))
        states = [fwd[4][...], bwd[4][...]]
        cross = []
        for n, (d, _) in enumerate(jobs):
            cross.append(jnp.dot(qs[n], states[n].astype(BF16), preferred_element_type=F32))
            states.append(states[n] * d[3] + incs[n])
        for n, ((d, _), off) in enumerate(zip(jobs, offs)):
            a = (scores[n] * d[0]).astype(BF16)
            d[5][pl.ds(off, c), :] = jnp.dot(a, vs[n], preferred_element_type=F32) + cross[n] * d[1]
        fwd[4][...] = states[2 * per]
        bwd[4][...] = states[2 * per + 1]
        return carry

    lax.fori_loop(0, nc // per, body, 0)

    rows = min(1024, s)

    def fin(t, carry):
        off = pl.multiple_of(t * rows, rows)
        r = fbuf[pl.ds(off, rows), :] + bbuf[pl.ds(off, rows), :]
        mu = jnp.mean(r, axis=-1, keepdims=True)
        d = r - mu
        var = jnp.mean(d * d, axis=-1, keepdims=True)
        y = d * lax.rsqrt(var + NORM_EPS) * gng_ref[...] + gnb_ref[...]
        o_ref[0, pl.ds(off, rows), :] = (g_ref[0, pl.ds(off, rows), :].astype(F32) * y).astype(BF16)
        return carry

    lax.fori_loop(0, s // rows, fin, 0)


def _retention(qr, kr, vr, gr, dec, gn_g, gn_b, b, s):
    shp = (b, s, RET_W)
    assert (s // RET_CHUNK) % min(RET_CHUNKS_PER_TRIP, s // RET_CHUNK) == 0, s
    head = pl.BlockSpec((1, s, RET_DIM), lambda bi, h: (bi, 0, h))
    vec = pl.BlockSpec((1, RET_DIM), lambda bi, h: (0, h))
    out = pl.pallas_call(
        _retention_kernel,
        grid=(b, RET_HEADS),
        in_specs=[pl.BlockSpec(memory_space=pltpu.SMEM), head, head, head, head, vec, vec],
        out_specs=head,
        out_shape=jax.ShapeDtypeStruct(shp, BF16),
        scratch_shapes=[
            pltpu.VMEM((s, RET_DIM), F32),
            pltpu.VMEM((s, RET_DIM), F32),
            pltpu.VMEM((RET_DIM, RET_DIM), F32),
            pltpu.VMEM((RET_DIM, RET_DIM), F32),
        ],
        compiler_params=_params(("parallel", "parallel")),
        name="retention",
    )(dec, qr.reshape(shp), kr.reshape(shp), vr.reshape(shp), gr.reshape(shp), gn_g, gn_b)
    return out.reshape(b * s, RET_W)


def _outproj_kernel(oa_ref, or_ref, w_ref, x_ref, mod_ref, pg_ref, fg_ref, rw_ref, rb_ref,
                    xn_ref, hp_ref, ti_ref, tw_ref, rk_ref, cnt_ref, base_ref):
    @pl.when(pl.program_id(0) == 0)
    def _():
        base_ref[...] = jnp.zeros(base_ref.shape, F32)

    m = mod_ref[0]
    tm = x_ref.shape[0]
    nt = (((1,), (1,)), ((), ()))
    halves = [slice(0, tm // 2), slice(tm // 2, tm)]
    ys = [jnp.dot(oa_ref[r, :], w_ref[:ATTN_W, :], preferred_element_type=F32)
          + jnp.dot(or_ref[r, :], w_ref[ATTN_W:, :], preferred_element_type=F32) for r in halves]
    parts = []
    for r, y in zip(halves, ys):
        xn = x_ref[r, :] + m[2:3, :] * _rms(y, pg_ref[...])
        xn_ref[r, :] = xn
        h = _rms(xn, fg_ref[...]) * (1.0 + m[4:5, :]) + m[3:4, :]
        hb = h.astype(BF16)
        u = lax.bitcast_convert_type(hb.astype(F32), U32)
        hp_ref[r, :] = (u[:, :HALF] >> 16) | u[:, HALF:]
        parts.append(lax.dot_general(rw_ref[...], hb, nt, preferred_element_type=F32))
    logits = jnp.concatenate(parts, axis=1) + rb_ref[...]
    e_iota = lax.broadcasted_iota(I32, (N_EXPERTS, tm), 0).astype(F32)
    vals, idxs = [], []
    cur = logits
    for _ in range(TOP_K):
        mx = jnp.max(cur, axis=0, keepdims=True)
        ik = jnp.min(jnp.where(cur == mx, e_iota, float(N_EXPERTS)), axis=0, keepdims=True)
        vals.append(mx)
        idxs.append(ik)
        cur = jnp.where(e_iota == ik, -jnp.inf, cur)
    v = jnp.concatenate(vals, axis=0)
    w = jnp.exp(v - v[0:1, :])
    tw_ref[...] = w / jnp.sum(w, axis=0, keepdims=True)
    ti_ref[...] = jnp.concatenate(idxs, axis=0).astype(I32)

    onehot = jnp.zeros((N_EXPERTS, tm), F32)
    for ik in idxs:
        onehot = onehot + (e_iota == ik).astype(F32)
    before = (lax.broadcasted_iota(I32, (tm, tm), 0) < lax.broadcasted_iota(I32, (tm, tm), 1)).astype(BF16)
    seen = jnp.dot(onehot.astype(BF16), before, preferred_element_type=F32) + base_ref[...]
    ranks = [jnp.sum(jnp.where(e_iota == ik, seen, 0.0), axis=0, keepdims=True) for ik in idxs]
    rk_ref[...] = jnp.concatenate(ranks, axis=0).astype(I32)
    total = base_ref[...] + jnp.sum(onehot, axis=1, keepdims=True)
    base_ref[...] = total
    cnt_ref[...] = jnp.broadcast_to(total, cnt_ref.shape).astype(I32)


def _outproj(oa, orr, w_out_b, x2, batch0, x_batch0, mod_l, post_g, ffn_g, rw_t, rb, b, s):
    n = b * s
    tm = min(ROW_TILE, s)
    nb = s // tm
    row = lambda i: (i, 0)
    full = lambda i: (0, 0)
    col = lambda i: (0, i)
    return pl.pallas_call(
        _outproj_kernel,
        grid=(n // tm,),
        in_specs=[
            pl.BlockSpec((tm, ATTN_W), row),
            pl.BlockSpec((tm, RET_W), row),
            pl.BlockSpec((D_MODEL, D_MODEL), full),
            pl.BlockSpec((tm, D_MODEL), lambda i: (x_batch0 * nb + i, 0)),
            pl.BlockSpec((1, 6, D_MODEL), lambda i: (batch0 + i // nb, 0, 0)),
            pl.BlockSpec((1, D_MODEL), full),
            pl.BlockSpec((1, D_MODEL), full),
            pl.BlockSpec((N_EXPERTS, D_MODEL), full),
            pl.BlockSpec((N_EXPERTS, 1), full),
        ],
        out_specs=[
            pl.BlockSpec((tm, D_MODEL), row),
            pl.BlockSpec((tm, HALF), row),
            pl.BlockSpec((TOP_K, tm), col),
            pl.BlockSpec((TOP_K, tm), col),
            pl.BlockSpec((TOP_K, tm), col),
            pl.BlockSpec((N_EXPERTS, LANES), full),
        ],
        out_shape=[
            jax.ShapeDtypeStruct((n, D_MODEL), F32),
            jax.ShapeDtypeStruct((n, HALF), U32),
            jax.ShapeDtypeStruct((TOP_K, n), I32),
            jax.ShapeDtypeStruct((TOP_K, n), F32),
            jax.ShapeDtypeStruct((TOP_K, n), I32),
            jax.ShapeDtypeStruct((N_EXPERTS, LANES), I32),
        ],
        scratch_shapes=[pltpu.VMEM((N_EXPERTS, 1), F32)],
        compiler_params=_params(("arbitrary",)),
        name="outproj_router",
    )(oa, orr, w_out_b, x2, mod_l, post_g, ffn_g, rw_t, rb)


def _sc_scatter_rows(rows, pos3, n_out):
    n, w = rows.shape
    nchunk, kk, c = pos3.shape
    per_w = nchunk // SC_WORKERS
    mesh = plsc.VectorSubcoreMesh(core_axis_name="c", subcore_axis_name="s")

    @functools.partial(
        pl.kernel, mesh=mesh,
        out_type=jax.ShapeDtypeStruct((n_out, w), rows.dtype),
        scratch_types=[pltpu.VMEM((kk, c), I32), pltpu.VMEM((c, w), rows.dtype)],
    )
    def k(rows_hbm, pos_hbm, out_hbm, idx_v, rows_v):
        wid = lax.axis_index("s") * SC_CORES + lax.axis_index("c")

        @pl.loop(0, per_w)
        def _(i):
            ch = wid * per_w + i
            pltpu.sync_copy(pos_hbm.at[ch], idx_v)
            pltpu.sync_copy(rows_hbm.at[pl.ds(ch * c, c)], rows_v)
            for j in range(kk):
                pltpu.sync_copy(rows_v, out_hbm.at[idx_v.at[j]])

    return k(rows, pos3)


def _sc_gather_rows(table, idx):
    w = table.shape[1]
    b = idx.shape[0]
    c = SC_CHUNK
    per_w = b // (SC_WORKERS * c)
    mesh = plsc.VectorSubcoreMesh(core_axis_name="c", subcore_axis_name="s")

    @functools.partial(
        pl.kernel, mesh=mesh,
        out_type=jax.ShapeDtypeStruct((b, w), table.dtype),
        scratch_types=[pltpu.VMEM((c,), I32), pltpu.VMEM((c, w), table.dtype), pltpu.SemaphoreType.DMA],
    )
    def k(table_hbm, idx_hbm, out_hbm, idx_v, rows_v, sem):
        wid = lax.axis_index("s") * SC_CORES + lax.axis_index("c")

        @pl.loop(0, per_w)
        def _(i):
            base = (wid * per_w + i) * c
            pltpu.sync_copy(idx_hbm.at[pl.ds(base, c)], idx_v)
            pltpu.async_copy(table_hbm.at[idx_v], rows_v, sem).wait()
            pltpu.sync_copy(rows_v, out_hbm.at[pl.ds(base, c)])

    return k(table, idx)


def _experts_kernel(te_ref, nv_ref, xs_ref, wu_ref, bu_ref, wd_ref, bd_ref, ys_ref, wu_b, wd_b, inv_ref):
    i = pl.program_id(0)
    live = i < nv_ref[0]
    new_expert = jnp.logical_or(i == 0, te_ref[i] != te_ref[jnp.maximum(i - 1, 0)])

    def full_max(a):
        return jnp.max(jnp.max(jnp.abs(a), axis=0, keepdims=True), axis=1, keepdims=True)

    @pl.when(jnp.logical_and(live, new_expert))
    def _():
        wu = wu_ref[0, 0].astype(BF16)
        wd = wd_ref[0, 0].astype(BF16)
        mu = jnp.maximum(full_max(wu).astype(F32), 1e-30)
        md = jnp.maximum(full_max(wd).astype(F32), 1e-30)
        wu_b[...] = (wu * (FP8_RANGE / mu).astype(BF16)).astype(FP8)
        wd_b[...] = (wd * (FP8_RANGE / md).astype(BF16)).astype(FP8)
        inv_ref[0:1, :] = jnp.broadcast_to(mu / FP8_RANGE, (1, LANES))
        inv_ref[1:2, :] = jnp.broadcast_to(md / (FP8_RANGE * ACT_SCALE), (1, LANES))

    @pl.when(live)
    def _():
        t = xs_ref.shape[0]
        halves = [slice(0, t // 2), slice(t // 2, t)]
        inv_up = inv_ref[0:1, 0:1]
        inv_dn = inv_ref[1:2, 0:1]
        ups = []
        for r in halves:
            w = xs_ref[r, :]
            lo = _unpack_lo(w).astype(BF16)
            hi = _unpack_hi(w).astype(BF16)
            amax = jnp.max(jnp.maximum(jnp.abs(lo), jnp.abs(hi)), axis=1, keepdims=True).astype(F32)
            c = (FP8_RANGE / jnp.maximum(amax, 1e-30)).astype(BF16)
            up = (jnp.dot((lo * c).astype(FP8), wu_b[:HALF, :], preferred_element_type=F32)
                  + jnp.dot((hi * c).astype(FP8), wu_b[HALF:, :], preferred_element_type=F32))
            ups.append(up * (inv_up / c.astype(F32)))
        for r, up in zip(halves, ups):
            up = up + bu_ref[0, 0]
            glu = jnp.minimum(up[:, :D_FF], SWIGLU_LIMIT)
            lin = jnp.clip(up[:, D_FF:], -SWIGLU_LIMIT, SWIGLU_LIMIT)
            act = (glu * (0.5 * ACT_SCALE)) * (1.0 + jnp.tanh((0.5 * SWIGLU_ALPHA) * glu)) * (lin + 1.0)
            y = jnp.dot(act.astype(FP8), wd_b[...], preferred_element_type=F32) * inv_dn + bd_ref[0, 0]
            ys_ref[r, :] = _pack_bf16_pairs(y)


def _experts(xs, tile_expert, n_valid, layer, wu, bu, wd, bd):
    n_pad = xs.shape[0]
    t = MOE_TILE
    grid_spec = pltpu.PrefetchScalarGridSpec(
        num_scalar_prefetch=2,
        grid=(n_pad // t,),
        in_specs=[
            pl.BlockSpec((t, HALF), lambda i, te, nv: (i, 0)),
            pl.BlockSpec((1, 1, D_MODEL, 2 * D_FF), lambda i, te, nv: (layer, te[i], 0, 0)),
            pl.BlockSpec((1, 1, 1, 2 * D_FF), lambda i, te, nv: (layer, te[i], 0, 0)),
            pl.BlockSpec((1, 1, D_FF, D_MODEL), lambda i, te, nv: (layer, te[i], 0, 0)),
            pl.BlockSpec((1, 1, 1, D_MODEL), lambda i, te, nv: (layer, te[i], 0, 0)),
        ],
        out_specs=pl.BlockSpec((t, HALF), lambda i, te, nv: (i, 0)),
        scratch_shapes=[pltpu.VMEM((D_MODEL, 2 * D_FF), FP8), pltpu.VMEM((D_FF, D_MODEL), FP8),
                        pltpu.VMEM((2, LANES), F32)],
    )
    return pl.pallas_call(
        _experts_kernel,
        grid_spec=grid_spec,
        out_shape=jax.ShapeDtypeStruct((n_pad, HALF), U32),
        compiler_params=pltpu.CompilerParams(dimension_semantics=("arbitrary",), vmem_limit_bytes=EXPERTS_VMEM_LIMIT),
        name="experts",
    )(tile_expert, n_valid, xs, wu, bu, wd, bd)


def _combine_kernel(y0_ref, y1_ref, y2_ref, y3_ref, w_ref, x_ref, mod_ref, g_ref, *rest):
    o_ref = rest[-1]
    w = w_ref[...]
    lo = None
    hi = None
    for k, y_ref in enumerate((y0_ref, y1_ref, y2_ref, y3_ref)):
        word = y_ref[...]
        wk = w[:, k:k + 1]
        lo_k = wk * _unpack_lo(word)
        hi_k = wk * _unpack_hi(word)
        lo = lo_k if lo is None else lo + lo_k
        hi = hi_k if hi is None else hi + hi_k
    yf = jnp.concatenate([lo, hi], axis=1)
    m = mod_ref[0]
    o_ref[...] = x_ref[...] + m[5:6, :] * _rms(yf, g_ref[...])


def _combine(yg, tw_t, xn, batch0, mod_l, post_g, b, s, out_batches=None, shared_out=None):
    n = b * s
    tm = min(ROW_TILE, s)
    nb = s // tm
    nblk = n // tm
    row = lambda i: (i, 0)
    full = lambda i: (0, 0)
    y_specs = [pl.BlockSpec((tm, HALF), functools.partial(lambda i, k: (k * nblk + i, 0), k=k)) for k in range(TOP_K)]
    out_rows, out_blk0 = (n, 0) if out_batches is None else (out_batches * s, batch0 * nb)
    extra_specs, extra_args, aliases = [], [], {}
    if shared_out is not None:
        extra_specs, extra_args, aliases = [pl.BlockSpec(memory_space=pl.ANY)], [shared_out], {8: 0}
    return pl.pallas_call(
        _combine_kernel,
        grid=(nblk,),
        in_specs=y_specs + [
            pl.BlockSpec((tm, TOP_K), row),
            pl.BlockSpec((tm, D_MODEL), row),
            pl.BlockSpec((1, 6, D_MODEL), lambda i: (batch0 + i // nb, 0, 0)),
            pl.BlockSpec((1, D_MODEL), full),
        ] + extra_specs,
        out_specs=pl.BlockSpec((tm, D_MODEL), lambda i: (out_blk0 + i, 0)),
        out_shape=jax.ShapeDtypeStruct((out_rows, D_MODEL), F32),
        input_output_aliases=aliases,
        compiler_params=_params(("parallel",)),
        name="combine",
    )(yg, yg, yg, yg, tw_t, xn, mod_l, post_g, *extra_args)


def _route(counts, tidx, rank, n_tiles):
    t = MOE_TILE
    cnt = counts[:, 0]
    padded = ((cnt + t - 1) // t) * t
    ends = jnp.cumsum(padded)
    starts = ends - padded
    experts = jnp.arange(N_EXPERTS, dtype=I32)[:, None, None]
    start_of = jnp.sum(jnp.where(tidx[None] == experts, starts[:, None, None], 0), axis=0)
    pos = start_of + rank
    tile_ends = ends // t
    n_valid = tile_ends[-1]
    tile = jnp.minimum(jnp.arange(n_tiles, dtype=I32), n_valid - 1)
    tile_expert = jnp.minimum(jnp.sum(tile_ends[None, :] <= tile[:, None], axis=1), N_EXPERTS - 1).astype(I32)
    return pos.astype(I32), tile_expert, n_valid.reshape(1).astype(I32)


def _value_range_scale(mod_l, pre_g, w_in_l):
    d = pre_g.shape[0]
    shift, scale = mod_l[:, 0, :], mod_l[:, 1, :]
    h_norm = d ** 0.5 * jnp.max(jnp.abs(pre_g[None, :] * (1.0 + scale)), axis=1) + jnp.linalg.norm(shift, axis=1)
    w_v = w_in_l[:, ATTN_W + KV_W:ATTN_W + 2 * KV_W]
    bound = 1.02 * h_norm * jnp.max(jnp.linalg.norm(w_v, axis=0))
    v_scale = V_RANGE / jnp.maximum(bound, 1e-30)
    lanes = lambda t: jnp.broadcast_to(t[:, None, None], (mod_l.shape[0], 1, LANES))
    return lanes(v_scale), lanes(1.0 / v_scale)


def kernel(x, c, ada_w, ada_b, pre_mix_g, post_mix_g, w_in, q_norm_g, k_norm_g, ret_decay_fwd, ret_decay_bwd, ret_gn_g, ret_gn_b, w_out, pre_ffn_g, post_ffn_g, router_w, router_b, exp_w_up, exp_b_up, exp_w_down, exp_b_down):
    b, s, d = x.shape
    depth = ada_w.shape[0]
    groups = 2 if b % 2 == 0 else 1
    bg = b // groups
    n = bg * s
    n_asg = n * TOP_K
    n_pad = n_asg + N_EXPERTS * MOE_TILE
    n_tiles = n_pad // MOE_TILE

    mod = _modulation(c, ada_w, ada_b)
    ropes = _rope_tables(s, HEAD_DIM) + _rope_tables(s, RET_DIM)
    bu = exp_b_up.reshape(depth, N_EXPERTS, 1, 2 * D_FF)
    bd = exp_b_down.reshape(depth, N_EXPERTS, 1, d)
    xs_in = [x.reshape(b * s, d)] * groups
    x_b0 = [g * bg for g in range(groups)]
    out = None
    for l in range(depth):
        mod_l = mod[l].reshape(b, 6, d)
        qg = jnp.tile(q_norm_g[l], LANES // HEAD_DIM).reshape(1, LANES)
        kg = jnp.tile(k_norm_g[l], LANES // HEAD_DIM).reshape(1, LANES)
        v_scale, fac = _value_range_scale(mod_l, pre_mix_g[l], w_in[l])
        w_in_b = w_in[l].astype(BF16)
        w_out_b = w_out[l].astype(BF16)
        rw_t = router_w[l].T.astype(BF16)
        dec = jnp.stack([ret_decay_fwd[l], ret_decay_bwd[l]]).astype(F32)
        last = l + 1 == depth
        proj = [_inproj(xs_in[g], g * bg, x_b0[g], mod_l, pre_mix_g[l].reshape(1, d), w_in_b, ropes, qg, kg, v_scale,
                        bg, s) for g in range(groups)]
        oa = [_attention(p[0], p[1], p[2], fac, g * bg, bg, s) for g, p in enumerate(proj)]
        orr = [_retention(p[3], p[4], p[5], p[6], dec, ret_gn_g[l].reshape(1, RET_W), ret_gn_b[l].reshape(1, RET_W),
                          bg, s) for p in proj]
        routed = []
        for g in range(groups):
            xn, hp, tidx, tw, rank, counts = _outproj(
                oa[g], orr[g], w_out_b, xs_in[g], g * bg, x_b0[g], mod_l, post_mix_g[l].reshape(1, d),
                pre_ffn_g[l].reshape(1, d), rw_t, router_b[l].reshape(N_EXPERTS, 1), bg, s)
            pos, tile_expert, n_valid = _route(counts, tidx, rank, n_tiles)
            pos3 = pos.reshape(TOP_K, n // SC_CHUNK, SC_CHUNK).transpose(1, 0, 2)
            routed.append((xn, tw, pos, tile_expert, n_valid, _sc_scatter_rows(hp, pos3, n_pad)))
        yg = []
        for xn, tw, pos, tile_expert, n_valid, xs in routed:
            ys = _experts(xs, tile_expert, n_valid, l, exp_w_up, bu, exp_w_down, bd)
            yg.append(_sc_gather_rows(ys, pos.reshape(n_asg)))
        nxt = []
        for g, (xn, tw, *_) in enumerate(routed):
            if last:
                out = _combine(yg[g], tw.T, xn, g * bg, mod_l, post_ffn_g[l].reshape(1, d), bg, s,
                               out_batches=b, shared_out=out)
            else:
                nxt.append(_combine(yg[g], tw.T, xn, g * bg, mod_l, post_ffn_g[l].reshape(1, d), bg, s))
        xs_in, x_b0 = nxt, [0] * groups
    return out.reshape(b, s, d)
```

```python
import functools

import numpy as np
import jax
import jax.numpy as jnp
from jax import lax
from jax.experimental import pallas as pl
from jax.experimental.pallas import tpu as pltpu
from jax.experimental.pallas import tpu_sc as plsc

F32 = jnp.float32
BF16 = jnp.bfloat16
U32 = jnp.uint32
I32 = jnp.int32
FP8 = jnp.float8_e4m3fn

D_MODEL = 1024
GRID_W = 64
ROPE_THETA = 10000.0
HEAD_DIM = 64
Q_HEADS = 8
KV_HEADS = 2
RET_HEADS = 4
RET_DIM = 128
RET_CHUNK = 128
ATTN_W = Q_HEADS * HEAD_DIM
KV_W = KV_HEADS * HEAD_DIM
RET_W = RET_HEADS * RET_DIM
IN_W = ATTN_W + 2 * KV_W + 4 * RET_W
N_EXPERTS = 32
TOP_K = 4
D_FF = D_MODEL
SWIGLU_LIMIT = 7.0
SWIGLU_ALPHA = 1.702
NORM_EPS = 1e-6
Q_SCALE = HEAD_DIM ** -0.5 * float(np.log2(np.e))
FP8_RANGE = 224.0
ACT_SCALE = FP8_RANGE / (SWIGLU_LIMIT * (SWIGLU_LIMIT + 1.0))
V_RANGE = FP8_RANGE
P_SHIFT = 8.0
HALF = D_MODEL // 2
LANES = 128

ROW_TILE = 512
Q_TILE = 128
Q_TILES_PER_TRIP = 2
KV_TILE = 1024
MOE_TILE = 512
RET_CHUNKS_PER_TRIP = 8
VMEM_LIMIT = 48 * 1024 * 1024
EXPERTS_VMEM_LIMIT = 56 * 1024 * 1024

SC_CORES = 2
SC_SUBCORES = 16
SC_WORKERS = SC_CORES * SC_SUBCORES
SC_CHUNK = 64


def _params(sem):
    return pltpu.CompilerParams(dimension_semantics=sem, vmem_limit_bytes=VMEM_LIMIT)


def _rms(x, g):
    return x * lax.rsqrt(jnp.mean(x * x, axis=-1, keepdims=True) + NORM_EPS) * g


def _pack_bf16_pairs(y):
    u = lax.bitcast_convert_type(y.astype(BF16).astype(F32), U32)
    return (u[:, :HALF] >> 16) | u[:, HALF:]


def _unpack_lo(w):
    return lax.bitcast_convert_type(w << 16, F32)


def _unpack_hi(w):
    return lax.bitcast_convert_type(w & jnp.uint32(0xFFFF0000), F32)


def _mod_kernel(c_ref, w_ref, b_ref, o_ref):
    c = c_ref[...]
    cond = c * jax.nn.sigmoid(c)
    o_ref[0] = jnp.dot(cond, w_ref[0], preferred_element_type=F32, precision=lax.Precision.HIGHEST) + b_ref[0]


def _modulation(c, ada_w, ada_b):
    depth, d, w6 = ada_w.shape
    b = c.shape[0]
    tn = 1536
    return pl.pallas_call(
        _mod_kernel,
        grid=(depth, w6 // tn),
        in_specs=[
            pl.BlockSpec((b, d), lambda l, j: (0, 0)),
            pl.BlockSpec((1, d, tn), lambda l, j: (l, 0, j)),
            pl.BlockSpec((1, 1, tn), lambda l, j: (l, 0, j)),
        ],
        out_specs=pl.BlockSpec((1, b, tn), lambda l, j: (l, 0, j)),
        out_shape=jax.ShapeDtypeStruct((depth, b, w6), F32),
        compiler_params=_params(("arbitrary", "arbitrary")),
        name="modulation",
    )(c, ada_w, ada_b.reshape(depth, 1, w6))


def _rope_tables(s, head_dim):
    quarter = head_dim // 4
    t = jnp.arange(s, dtype=F32)
    row = jnp.floor(t / GRID_W)
    col = t - row * GRID_W
    inv_freq = ROPE_THETA ** (-jnp.arange(quarter, dtype=F32) / quarter)
    lane = np.arange(LANES) % head_dim
    use_col = lane >= head_dim // 2
    second = (lane % (head_dim // 2)) >= quarter
    freq = inv_freq[lane % quarter]
    pos = jnp.where(use_col[None, :], col[:, None], row[:, None])
    ang = pos * freq[None, :]
    sign = jnp.where(second, 1.0, -1.0).astype(F32)
    return jnp.cos(ang), jnp.sin(ang) * sign[None, :]


def _rope(z, cos, sin_signed, quarter):
    lane = lax.broadcasted_iota(I32, z.shape, 1)
    first = (lane % (2 * quarter)) < quarter
    partner = jnp.where(first, pltpu.roll(z, LANES - quarter, 1), pltpu.roll(z, quarter, 1))
    return z * cos + partner * sin_signed


def _inproj_kernel(x_ref, mod_ref, g_ref, w_ref, ca_ref, sa_ref, cr_ref, sr_ref, qg_ref, kg_ref, vs_ref,
                   qa_ref, kd_ref, vt_ref, qr_ref, kr_ref, vr_ref, gr_ref):
    x = x_ref[...]
    m = mod_ref[0]
    h = _rms(x, g_ref[...]) * (1.0 + m[1:2, :]) + m[0:1, :]
    hb = h.astype(BF16)
    tm = x.shape[0]

    def proj(c0):
        z = jnp.dot(hb, w_ref[:, c0:c0 + 2 * LANES], preferred_element_type=F32)
        return z[:, :LANES], z[:, LANES:]

    ri = lax.broadcasted_iota(I32, (LANES, LANES), 0) // HEAD_DIM
    ci = lax.broadcasted_iota(I32, (LANES, LANES), 1) // HEAD_DIM
    head_ones = (ri == ci).astype(BF16)
    ca, sa = ca_ref[...], sa_ref[...]
    cr, sr = cr_ref[...], sr_ref[...]
    lane = lax.broadcasted_iota(I32, (tm, LANES), 1)
    low = lane < HEAD_DIM

    def head_norm_rope(z, g):
        ss = jnp.dot((z * z).astype(BF16), head_ones, preferred_element_type=F32)
        zn = z * lax.rsqrt(ss * (1.0 / HEAD_DIM) + NORM_EPS) * g
        return _rope(zn, ca, sa, HEAD_DIM // 4)

    def store_pair(ref, j, fn):
        def finish(zs):
            for i, z in enumerate(zs):
                ref[:, (2 * j + i) * LANES:(2 * j + i + 1) * LANES] = fn(z).astype(BF16)
        return finish

    def store_kv(zs):
        k, v = zs
        k = head_norm_rope(k, kg_ref[...])
        k_sw = pltpu.roll(k, HEAD_DIM, 1)
        v = v * vs_ref[0]
        kd_ref[0, 0] = jnp.where(low, k, k_sw).astype(BF16)
        kd_ref[0, 1] = jnp.where(low, k_sw, k).astype(BF16)
        vt_ref[0, 0] = jnp.where(low, v, 1.0).T.astype(FP8)
        vt_ref[0, 1] = jnp.where(low, pltpu.roll(v, HEAD_DIM, 1), 1.0).T.astype(FP8)

    jobs = [(2 * j * LANES, store_pair(qa_ref, j, lambda z: head_norm_rope(z, qg_ref[...]) * Q_SCALE))
            for j in range(ATTN_W // (2 * LANES))]
    jobs.append((ATTN_W, store_kv))
    base = ATTN_W + 2 * KV_W
    for j in range(RET_W // (2 * LANES)):
        off = 2 * j * LANES
        jobs += [
            (base + off, store_pair(qr_ref, j, lambda z: _rope(z, cr, sr, RET_DIM // 4))),
            (base + RET_W + off, store_pair(kr_ref, j, lambda z: _rope(z, cr, sr, RET_DIM // 4) * (RET_DIM ** -0.5))),
            (base + 2 * RET_W + off, store_pair(vr_ref, j, lambda z: z)),
            (base + 3 * RET_W + off, store_pair(gr_ref, j, lambda z: z * jax.nn.sigmoid(z))),
        ]
    zs = proj(jobs[0][0])
    for i, (_, finish) in enumerate(jobs):
        nxt = proj(jobs[i + 1][0]) if i + 1 < len(jobs) else None
        finish(zs)
        zs = nxt


def _inproj(x2, batch0, x_batch0, mod_l, pre_g, w_in_b, ropes, qg, kg, v_scale, b, s):
    n = b * s
    tm = min(ROW_TILE, s)
    nb = s // tm
    ca, sa, cr, sr = ropes
    row = lambda i: (i, 0)
    full = lambda i: (0, 0)
    rope_spec = pl.BlockSpec((tm, LANES), lambda i: (i % nb, 0))
    wide = jax.ShapeDtypeStruct((n, RET_W), BF16)
    return pl.pallas_call(
        _inproj_kernel,
        grid=(n // tm,),
        in_specs=[
            pl.BlockSpec((tm, D_MODEL), lambda i: (x_batch0 * nb + i, 0)),
            pl.BlockSpec((1, 6, D_MODEL), lambda i: (batch0 + i // nb, 0, 0)),
            pl.BlockSpec((1, D_MODEL), full),
            pl.BlockSpec((D_MODEL, IN_W), full),
            rope_spec, rope_spec, rope_spec, rope_spec,
            pl.BlockSpec((1, LANES), full),
            pl.BlockSpec((1, LANES), full),
            pl.BlockSpec((1, 1, LANES), lambda i: (batch0 + i // nb, 0, 0)),
        ],
        out_specs=[
            pl.BlockSpec((tm, ATTN_W), row),
            pl.BlockSpec((1, KV_HEADS, tm, LANES), lambda i: (i // nb, 0, i % nb, 0)),
            pl.BlockSpec((1, KV_HEADS, LANES, tm), lambda i: (i // nb, 0, 0, i % nb)),
            pl.BlockSpec((tm, RET_W), row),
            pl.BlockSpec((tm, RET_W), row),
            pl.BlockSpec((tm, RET_W), row),
            pl.BlockSpec((tm, RET_W), row),
        ],
        out_shape=[
            jax.ShapeDtypeStruct((n, ATTN_W), BF16),
            jax.ShapeDtypeStruct((b, KV_HEADS, s, LANES), BF16),
            jax.ShapeDtypeStruct((b, KV_HEADS, LANES, s), FP8),
            wide, wide, wide, wide,
        ],
        compiler_params=_params(("parallel",)),
        name="inproj",
    )(x2, mod_l, pre_g, w_in_b, ca, sa, cr, sr, qg, kg, v_scale)


def _attn_kernel(q_ref, kd_ref, vt_ref, fac_ref, o_ref, q_s, qn_s, m_ref, acc_ref, s0_ref, s1_ref,
                 *, tq, tk, tiles):
    s = kd_ref.shape[2]
    nk = s // tk
    nq = s // tq
    pairs = (Q_HEADS // KV_HEADS) // 2
    top = lax.broadcasted_iota(I32, (LANES, tq), 0) < HEAD_DIM
    unscale = jnp.concatenate([fac_ref[0]] * (2 * tq // LANES), axis=1)

    chains = tiles * pairs

    def load_q(trip, dst):
        for u in range(tiles):
            rows = pl.ds(pl.multiple_of((trip * tiles + u) * tq, tq), tq)
            for p in range(pairs):
                qt = q_ref[rows, p * LANES:(p + 1) * LANES].astype(F32).T
                dst[u * pairs + p, :, :tq] = jnp.where(top, qt, 0.0).astype(BF16)
                dst[u * pairs + p, :, tq:] = jnp.where(top, 0.0, qt).astype(BF16)

    def keys(j):
        return kd_ref[0, 0, j * tk:(j + 1) * tk, :]

    def step(j, cur_ref, nxt_ref, q_next, j_next):
        vt = vt_ref[0, 0, :, j * tk:(j + 1) * tk]
        kd = keys(j_next)
        nxt_ref[0] = jnp.dot(kd, q_next[0], preferred_element_type=F32)
        for c in range(chains):
            sc = cur_ref[c]
            m_old = m_ref[c]
            m_new = jnp.maximum(m_old, jnp.max(sc, axis=0, keepdims=True))
            pt = jnp.exp2(sc - (m_new - P_SHIFT))
            alpha = jnp.exp2(m_old - m_new)
            if c + 1 < chains:
                nxt_ref[c + 1] = jnp.dot(kd, q_next[c + 1], preferred_element_type=F32)
            acc_ref[c] = alpha * acc_ref[c] + jnp.dot(vt, pt.astype(FP8), preferred_element_type=F32)
            m_ref[c] = m_new

    bufs = (s0_ref, s1_ref)
    trips = nq // tiles
    load_q(0, q_s)
    for c in range(chains):
        s0_ref[c] = jnp.dot(keys(0), q_s[c], preferred_element_type=F32)

    def body(trip, carry):
        load_q(trip, q_s)
        load_q(jnp.minimum(trip + 1, trips - 1), qn_s)
        m_ref[...] = jnp.full(m_ref.shape, -jnp.inf, F32)
        acc_ref[...] = jnp.zeros(acc_ref.shape, F32)
        for j in range(nk):
            last = j + 1 == nk
            step(j, bufs[j % len(bufs)], bufs[(j + 1) % len(bufs)], qn_s if last else q_s, 0 if last else j + 1)
        for c in range(chains):
            u, p = divmod(c, pairs)
            rows = pl.ds(pl.multiple_of((trip * tiles + u) * tq, tq), tq)
            acc = acc_ref[c]
            o_t = acc[:HEAD_DIM] / acc[HEAD_DIM:] * unscale
            both = jnp.concatenate([o_t[:, :tq], o_t[:, tq:]], axis=0)
            o_ref[rows, p * LANES:(p + 1) * LANES] = both.T.astype(BF16)
        return carry

    lax.fori_loop(0, trips, body, 0)


def _attention(qa, kd, vt, fac, batch0, b, s):
    n = qa.shape[0]
    tq = Q_TILE
    tk = min(KV_TILE, s // 2)
    tiles = Q_TILES_PER_TRIP
    assert tq == LANES and s % (tiles * tq) == 0 and s % (2 * tk) == 0 and tk % LANES == 0, (s, tq, tk)
    gw = ATTN_W // KV_HEADS
    chains = tiles * ((Q_HEADS // KV_HEADS) // 2)
    return pl.pallas_call(
        functools.partial(_attn_kernel, tq=tq, tk=tk, tiles=tiles),
        grid=(b, KV_HEADS),
        in_specs=[
            pl.BlockSpec((s, gw), lambda bi, h: (bi, h)),
            pl.BlockSpec((1, 1, s, LANES), lambda bi, h: (bi, h, 0, 0)),
            pl.BlockSpec((1, 1, LANES, s), lambda bi, h: (bi, h, 0, 0)),
            pl.BlockSpec((1, 1, LANES), lambda bi, h: (batch0 + bi, 0, 0)),
        ],
        out_specs=pl.BlockSpec((s, gw), lambda bi, h: (bi, h)),
        out_shape=jax.ShapeDtypeStruct((n, ATTN_W), BF16),
        scratch_shapes=[
            pltpu.VMEM((chains, LANES, 2 * tq), BF16),
            pltpu.VMEM((chains, LANES, 2 * tq), BF16),
            pltpu.VMEM((chains, 1, 2 * tq), F32),
            pltpu.VMEM((chains, LANES, 2 * tq), F32),
            pltpu.VMEM((chains, tk, 2 * tq), F32),
            pltpu.VMEM((chains, tk, 2 * tq), F32),
        ],
        compiler_params=_params(("parallel", "parallel")),
        name="attention",
    )(qa, kd, vt, fac)


def _retention_kernel(dec_ref, q_ref, k_ref, v_ref, g_ref, gng_ref, gnb_ref, o_ref,
                      fbuf, bbuf, sf_ref, sb_ref):
    h = pl.program_id(1)
    c = RET_CHUNK
    s = q_ref.shape[1]
    nc = s // c
    lgf = jnp.full((1, 1), dec_ref[0, h], F32)
    lgb = jnp.full((1, 1), dec_ref[1, h], F32)
    ii = lax.broadcasted_iota(I32, (c, c), 0)
    jj = lax.broadcasted_iota(I32, (c, c), 1)
    diff = (ii - jj).astype(F32)
    d_f = jnp.where(diff >= 0, jnp.exp(lgf * jnp.maximum(diff, 0.0)), 0.0)
    d_b = jnp.where(diff <= 0, jnp.exp(lgb * jnp.maximum(-diff, 0.0)), 0.0)
    idx = lax.broadcasted_iota(I32, (c, 1), 0).astype(F32)
    xi_f = jnp.exp(lgf * (idx + 1.0))
    zeta_f = jnp.exp(lgf * (c - 1.0 - idx))
    xi_b = jnp.exp(lgb * (c - idx))
    zeta_b = jnp.exp(lgb * idx)
    cd_f = jnp.exp(lgf * c)
    cd_b = jnp.exp(lgb * c)
    sf_ref[...] = jnp.zeros(sf_ref.shape, F32)
    sb_ref[...] = jnp.zeros(sb_ref.shape, F32)
    nt = (((1,), (1,)), ((), ()))

    fwd = (d_f, xi_f, zeta_f, cd_f, sf_ref, fbuf)
    bwd = (d_b, xi_b, zeta_b, cd_b, sb_ref, bbuf)

    per = min(RET_CHUNKS_PER_TRIP, nc)

    def body(i, carry):
        jobs = []
        for n in range(per):
            jobs += [(fwd, per * i + n), (bwd, nc - 1 - per * i - n)]
        offs = [pl.multiple_of(ci * c, c) for _, ci in jobs]
        qs = [q_ref[0, pl.ds(off, c), :] for off in offs]
        ks = [k_ref[0, pl.ds(off, c), :] for off in offs]
        vs = [v_ref[0, pl.ds(off, c), :] for off in offs]
        scores = [lax.dot_general(q, k, nt, preferred_element_type=F32) for q, k in zip(qs, ks)]
        incs = []
        for (d, _), k, v in zip(jobs, ks, vs):
            kzt = (k.astype(F32) * d[2]).T.astype(BF16)
            incs.append(jnp.dot(kzt, v, preferred_element_type=F32))
        states = [fwd[4][...], bwd[4][...]]
        cross = []
        for n, (d, _) in enumerate(jobs):
            cross.append(jnp.dot(qs[n], states[n].astype(BF16), preferred_element_type=F32))
            states.append(states[n] * d[3] + incs[n])
        for n, ((d, _), off) in enumerate(zip(jobs, offs)):
            a = (scores[n] * d[0]).astype(BF16)
            d[5][pl.ds(off, c), :] = jnp.dot(a, vs[n], preferred_element_type=F32) + cross[n] * d[1]
        fwd[4][...] = states[2 * per]
        bwd[4][...] = states[2 * per + 1]
        return carry

    lax.fori_loop(0, nc // per, body, 0)

    rows = min(1024, s)

    def fin(t, carry):
        off = pl.multiple_of(t * rows, rows)
        r = fbuf[pl.ds(off, rows), :] + bbuf[pl.ds(off, rows), :]
        mu = jnp.mean(r, axis=-1, keepdims=True)
        d = r - mu
        var = jnp.mean(d * d, axis=-1, keepdims=True)
        y = d * lax.rsqrt(var + NORM_EPS) * gng_ref[...] + gnb_ref[...]
        o_ref[0, pl.ds(off, rows), :] = (g_ref[0, pl.ds(off, rows), :].astype(F32) * y).astype(BF16)
        return carry

    lax.fori_loop(0, s // rows, fin, 0)


def _retention(qr, kr, vr, gr, dec, gn_g, gn_b, b, s):
    shp = (b, s, RET_W)
    assert (s // RET_CHUNK) % min(RET_CHUNKS_PER_TRIP, s // RET_CHUNK) == 0, s
    head = pl.BlockSpec((1, s, RET_DIM), lambda bi, h: (bi, 0, h))
    vec = pl.BlockSpec((1, RET_DIM), lambda bi, h: (0, h))
    out = pl.pallas_call(
        _retention_kernel,
        grid=(b, RET_HEADS),
        in_specs=[pl.BlockSpec(memory_space=pltpu.SMEM), head, head, head, head, vec, vec],
        out_specs=head,
        out_shape=jax.ShapeDtypeStruct(shp, BF16),
        scratch_shapes=[
            pltpu.VMEM((s, RET_DIM), F32),
            pltpu.VMEM((s, RET_DIM), F32),
            pltpu.VMEM((RET_DIM, RET_DIM), F32),
            pltpu.VMEM((RET_DIM, RET_DIM), F32),
        ],
        compiler_params=_params(("parallel", "parallel")),
        name="retention",
    )(dec, qr.reshape(shp), kr.reshape(shp), vr.reshape(shp), gr.reshape(shp), gn_g, gn_b)
    return out.reshape(b * s, RET_W)


def _outproj_kernel(oa_ref, or_ref, w_ref, x_ref, mod_ref, pg_ref, fg_ref, rw_ref, rb_ref,
                    xn_ref, hp_ref, ti_ref, tw_ref, rk_ref, cnt_ref, base_ref):
    @pl.when(pl.program_id(0) == 0)
    def _():
        base_ref[...] = jnp.zeros(base_ref.shape, F32)

    m = mod_ref[0]
    tm = x_ref.shape[0]
    nt = (((1,), (1,)), ((), ()))
    halves = [slice(0, tm // 2), slice(tm // 2, tm)]
    ys = [jnp.dot(oa_ref[r, :], w_ref[:ATTN_W, :], preferred_element_type=F32)
          + jnp.dot(or_ref[r, :], w_ref[ATTN_W:, :], preferred_element_type=F32) for r in halves]
    parts = []
    for r, y in zip(halves, ys):
        xn = x_ref[r, :] + m[2:3, :] * _rms(y, pg_ref[...])
        xn_ref[r, :] = xn
        h = _rms(xn, fg_ref[...]) * (1.0 + m[4:5, :]) + m[3:4, :]
        hb = h.astype(BF16)
        u = lax.bitcast_convert_type(hb.astype(F32), U32)
        hp_ref[r, :] = (u[:, :HALF] >> 16) | u[:, HALF:]
        parts.append(lax.dot_general(rw_ref[...], hb, nt, preferred_element_type=F32))
    logits = jnp.concatenate(parts, axis=1) + rb_ref[...]
    e_iota = lax.broadcasted_iota(I32, (N_EXPERTS, tm), 0).astype(F32)
    vals, idxs = [], []
    cur = logits
    for _ in range(TOP_K):
        mx = jnp.max(cur, axis=0, keepdims=True)
        ik = jnp.min(jnp.where(cur == mx, e_iota, float(N_EXPERTS)), axis=0, keepdims=True)
        vals.append(mx)
        idxs.append(ik)
        cur = jnp.where(e_iota == ik, -jnp.inf, cur)
    v = jnp.concatenate(vals, axis=0)
    w = jnp.exp(v - v[0:1, :])
    tw_ref[...] = w / jnp.sum(w, axis=0, keepdims=True)
    ti_ref[...] = jnp.concatenate(idxs, axis=0).astype(I32)

    onehot = jnp.zeros((N_EXPERTS, tm), F32)
    for ik in idxs:
        onehot = onehot + (e_iota == ik).astype(F32)
    before = (lax.broadcasted_iota(I32, (tm, tm), 0) < lax.broadcasted_iota(I32, (tm, tm), 1)).astype(BF16)
    seen = jnp.dot(onehot.astype(BF16), before, preferred_element_type=F32) + base_ref[...]
    ranks = [jnp.sum(jnp.where(e_iota == ik, seen, 0.0), axis=0, keepdims=True) for ik in idxs]
    rk_ref[...] = jnp.concatenate(ranks, axis=0).astype(I32)
    total = base_ref[...] + jnp.sum(onehot, axis=1, keepdims=True)
    base_ref[...] = total
    cnt_ref[...] = jnp.broadcast_to(total, cnt_ref.shape).astype(I32)


def _outproj(oa, orr, w_out_b, x2, batch0, x_batch0, mod_l, post_g, ffn_g, rw_t, rb, b, s):
    n = b * s
    tm = min(ROW_TILE, s)
    nb = s // tm
    row = lambda i: (i, 0)
    full = lambda i: (0, 0)
    col = lambda i: (0, i)
    return pl.pallas_call(
        _outproj_kernel,
        grid=(n // tm,),
        in_specs=[
            pl.BlockSpec((tm, ATTN_W), row),
            pl.BlockSpec((tm, RET_W), row),
            pl.BlockSpec((D_MODEL, D_MODEL), full),
            pl.BlockSpec((tm, D_MODEL), lambda i: (x_batch0 * nb + i, 0)),
            pl.BlockSpec((1, 6, D_MODEL), lambda i: (batch0 + i // nb, 0, 0)),
            pl.BlockSpec((1, D_MODEL), full),
            pl.BlockSpec((1, D_MODEL), full),
            pl.BlockSpec((N_EXPERTS, D_MODEL), full),
            pl.BlockSpec((N_EXPERTS, 1), full),
        ],
        out_specs=[
            pl.BlockSpec((tm, D_MODEL), row),
            pl.BlockSpec((tm, HALF), row),
            pl.BlockSpec((TOP_K, tm), col),
            pl.BlockSpec((TOP_K, tm), col),
            pl.BlockSpec((TOP_K, tm), col),
            pl.BlockSpec((N_EXPERTS, LANES), full),
        ],
        out_shape=[
            jax.ShapeDtypeStruct((n, D_MODEL), F32),
            jax.ShapeDtypeStruct((n, HALF), U32),
            jax.ShapeDtypeStruct((TOP_K, n), I32),
            jax.ShapeDtypeStruct((TOP_K, n), F32),
            jax.ShapeDtypeStruct((TOP_K, n), I32),
            jax.ShapeDtypeStruct((N_EXPERTS, LANES), I32),
        ],
        scratch_shapes=[pltpu.VMEM((N_EXPERTS, 1), F32)],
        compiler_params=_params(("arbitrary",)),
        name="outproj_router",
    )(oa, orr, w_out_b, x2, mod_l, post_g, ffn_g, rw_t, rb)


def _sc_scatter_rows(rows, pos3, n_out):
    n, w = rows.shape
    nchunk, kk, c = pos3.shape
    per_w = nchunk // SC_WORKERS
    mesh = plsc.VectorSubcoreMesh(core_axis_name="c", subcore_axis_name="s")

    @functools.partial(
        pl.kernel, mesh=mesh,
        out_type=jax.ShapeDtypeStruct((n_out, w), rows.dtype),
        scratch_types=[pltpu.VMEM((kk, c), I32), pltpu.VMEM((c, w), rows.dtype)],
    )
    def k(rows_hbm, pos_hbm, out_hbm, idx_v, rows_v):
        wid = lax.axis_index("s") * SC_CORES + lax.axis_index("c")

        @pl.loop(0, per_w)
        def _(i):
            ch = wid * per_w + i
            pltpu.sync_copy(pos_hbm.at[ch], idx_v)
            pltpu.sync_copy(rows_hbm.at[pl.ds(ch * c, c)], rows_v)
            for j in range(kk):
                pltpu.sync_copy(rows_v, out_hbm.at[idx_v.at[j]])

    return k(rows, pos3)


def _sc_gather_rows(table, idx):
    w = table.shape[1]
    b = idx.shape[0]
    c = SC_CHUNK
    per_w = b // (SC_WORKERS * c)
    mesh = plsc.VectorSubcoreMesh(core_axis_name="c", subcore_axis_name="s")

    @functools.partial(
        pl.kernel, mesh=mesh,
        out_type=jax.ShapeDtypeStruct((b, w), table.dtype),
        scratch_types=[pltpu.VMEM((c,), I32), pltpu.VMEM((c, w), table.dtype), pltpu.SemaphoreType.DMA],
    )
    def k(table_hbm, idx_hbm, out_hbm, idx_v, rows_v, sem):
        wid = lax.axis_index("s") * SC_CORES + lax.axis_index("c")

        @pl.loop(0, per_w)
        def _(i):
            base = (wid * per_w + i) * c
            pltpu.sync_copy(idx_hbm.at[pl.ds(base, c)], idx_v)
            pltpu.async_copy(table_hbm.at[idx_v], rows_v, sem).wait()
            pltpu.sync_copy(rows_v, out_hbm.at[pl.ds(base, c)])

    return k(table, idx)


def _experts_kernel(te_ref, nv_ref, xs_ref, wu_ref, bu_ref, wd_ref, bd_ref, ys_ref, wu_b, wd_b, inv_ref):
    i = pl.program_id(0)
    live = i < nv_ref[0]
    new_expert = jnp.logical_or(i == 0, te_ref[i] != te_ref[jnp.maximum(i - 1, 0)])

    def full_max(a):
        return jnp.max(jnp.max(jnp.abs(a), axis=0, keepdims=True), axis=1, keepdims=True)

    @pl.when(jnp.logical_and(live, new_expert))
    def _():
        wu = wu_ref[0, 0].astype(BF16)
        wd = wd_ref[0, 0].astype(BF16)
        mu = jnp.maximum(full_max(wu).astype(F32), 1e-30)
        md = jnp.maximum(full_max(wd).astype(F32), 1e-30)
        wu_b[...] = (wu * (FP8_RANGE / mu).astype(BF16)).astype(FP8)
        wd_b[...] = (wd * (FP8_RANGE / md).astype(BF16)).astype(FP8)
        inv_ref[0:1, :] = jnp.broadcast_to(mu / FP8_RANGE, (1, LANES))
        inv_ref[1:2, :] = jnp.broadcast_to(md / (FP8_RANGE * ACT_SCALE), (1, LANES))

    @pl.when(live)
    def _():
        t = xs_ref.shape[0]
        halves = [slice(0, t // 2), slice(t // 2, t)]
        inv_up = inv_ref[0:1, 0:1]
        inv_dn = inv_ref[1:2, 0:1]
        ups = []
        for r in halves:
            w = xs_ref[r, :]
            lo = _unpack_lo(w).astype(BF16)
            hi = _unpack_hi(w).astype(BF16)
            amax = jnp.max(jnp.maximum(jnp.abs(lo), jnp.abs(hi)), axis=1, keepdims=True).astype(F32)
            c = (FP8_RANGE / jnp.maximum(amax, 1e-30)).astype(BF16)
            up = (jnp.dot((lo * c).astype(FP8), wu_b[:HALF, :], preferred_element_type=F32)
                  + jnp.dot((hi * c).astype(FP8), wu_b[HALF:, :], preferred_element_type=F32))
            ups.append(up * (inv_up / c.astype(F32)))
        for r, up in zip(halves, ups):
            up = (up + bu_ref[0, 0]).astype(BF16)
            glu = jnp.minimum(up[:, :D_FF], SWIGLU_LIMIT)
            lin = jnp.clip(up[:, D_FF:], -SWIGLU_LIMIT, SWIGLU_LIMIT)
            act = (glu * (0.5 * ACT_SCALE)) * (1.0 + jnp.tanh((0.5 * SWIGLU_ALPHA) * glu)) * (lin + 1.0)
            y = jnp.dot(act.astype(FP8), wd_b[...], preferred_element_type=F32) * inv_dn + bd_ref[0, 0]
            ys_ref[r, :] = _pack_bf16_pairs(y)


def _experts(xs, tile_expert, n_valid, layer, wu, bu, wd, bd):
    n_pad = xs.shape[0]
    t = MOE_TILE
    grid_spec = pltpu.PrefetchScalarGridSpec(
        num_scalar_prefetch=2,
        grid=(n_pad // t,),
        in_specs=[
            pl.BlockSpec((t, HALF), lambda i, te, nv: (i, 0)),
            pl.BlockSpec((1, 1, D_MODEL, 2 * D_FF), lambda i, te, nv: (layer, te[i], 0, 0)),
            pl.BlockSpec((1, 1, 1, 2 * D_FF), lambda i, te, nv: (layer, te[i], 0, 0)),
            pl.BlockSpec((1, 1, D_FF, D_MODEL), lambda i, te, nv: (layer, te[i], 0, 0)),
            pl.BlockSpec((1, 1, 1, D_MODEL), lambda i, te, nv: (layer, te[i], 0, 0)),
        ],
        out_specs=pl.BlockSpec((t, HALF), lambda i, te, nv: (i, 0)),
        scratch_shapes=[pltpu.VMEM((D_MODEL, 2 * D_FF), FP8), pltpu.VMEM((D_FF, D_MODEL), FP8),
                        pltpu.VMEM((2, LANES), F32)],
    )
    return pl.pallas_call(
        _experts_kernel,
        grid_spec=grid_spec,
        out_shape=jax.ShapeDtypeStruct((n_pad, HALF), U32),
        compiler_params=pltpu.CompilerParams(dimension_semantics=("arbitrary",), vmem_limit_bytes=EXPERTS_VMEM_LIMIT),
        name="experts",
    )(tile_expert, n_valid, xs, wu, bu, wd, bd)


def _combine_kernel(y0_ref, y1_ref, y2_ref, y3_ref, w_ref, x_ref, mod_ref, g_ref, *rest):
    o_ref = rest[-1]
    w = w_ref[...]
    lo = None
    hi = None
    for k, y_ref in enumerate((y0_ref, y1_ref, y2_ref, y3_ref)):
        word = y_ref[...]
        wk = w[:, k:k + 1]
        lo_k = wk * _unpack_lo(word)
        hi_k = wk * _unpack_hi(word)
        lo = lo_k if lo is None else lo + lo_k
        hi = hi_k if hi is None else hi + hi_k
    yf = jnp.concatenate([lo, hi], axis=1)
    m = mod_ref[0]
    o_ref[...] = x_ref[...] + m[5:6, :] * _rms(yf, g_ref[...])


def _combine(yg, tw_t, xn, batch0, mod_l, post_g, b, s, out_batches=None, shared_out=None):
    n = b * s
    tm = min(ROW_TILE, s)
    nb = s // tm
    nblk = n // tm
    row = lambda i: (i, 0)
    full = lambda i: (0, 0)
    y_specs = [pl.BlockSpec((tm, HALF), functools.partial(lambda i, k: (k * nblk + i, 0), k=k)) for k in range(TOP_K)]
    out_rows, out_blk0 = (n, 0) if out_batches is None else (out_batches * s, batch0 * nb)
    extra_specs, extra_args, aliases = [], [], {}
    if shared_out is not None:
        extra_specs, extra_args, aliases = [pl.BlockSpec(memory_space=pl.ANY)], [shared_out], {8: 0}
    return pl.pallas_call(
        _combine_kernel,
        grid=(nblk,),
        in_specs=y_specs + [
            pl.BlockSpec((tm, TOP_K), row),
            pl.BlockSpec((tm, D_MODEL), row),
            pl.BlockSpec((1, 6, D_MODEL), lambda i: (batch0 + i // nb, 0, 0)),
            pl.BlockSpec((1, D_MODEL), full),
        ] + extra_specs,
        out_specs=pl.BlockSpec((tm, D_MODEL), lambda i: (out_blk0 + i, 0)),
        out_shape=jax.ShapeDtypeStruct((out_rows, D_MODEL), F32),
        input_output_aliases=aliases,
        compiler_params=_params(("parallel",)),
        name="combine",
    )(yg, yg, yg, yg, tw_t, xn, mod_l, post_g, *extra_args)


def _route(counts, tidx, rank, n_tiles):
    t = MOE_TILE
    cnt = counts[:, 0]
    padded = ((cnt + t - 1) // t) * t
    ends = jnp.cumsum(padded)
    starts = ends - padded
    experts = jnp.arange(N_EXPERTS, dtype=I32)[:, None, None]
    start_of = jnp.sum(jnp.where(tidx[None] == experts, starts[:, None, None], 0), axis=0)
    pos = start_of + rank
    tile_ends = ends // t
    n_valid = tile_ends[-1]
    tile = jnp.minimum(jnp.arange(n_tiles, dtype=I32), n_valid - 1)
    tile_expert = jnp.minimum(jnp.sum(tile_ends[None, :] <= tile[:, None], axis=1), N_EXPERTS - 1).astype(I32)
    return pos.astype(I32), tile_expert, n_valid.reshape(1).astype(I32)


def _value_range_scale(mod_l, pre_g, w_in_l):
    d = pre_g.shape[0]
    shift, scale = mod_l[:, 0, :], mod_l[:, 1, :]
    h_norm = d ** 0.5 * jnp.max(jnp.abs(pre_g[None, :] * (1.0 + scale)), axis=1) + jnp.linalg.norm(shift, axis=1)
    w_v = w_in_l[:, ATTN_W + KV_W:ATTN_W + 2 * KV_W]
    bound = 1.02 * h_norm * jnp.max(jnp.linalg.norm(w_v, axis=0))
    v_scale = V_RANGE / jnp.maximum(bound, 1e-30)
    lanes = lambda t: jnp.broadcast_to(t[:, None, None], (mod_l.shape[0], 1, LANES))
    return lanes(v_scale), lanes(1.0 / v_scale)


def kernel(x, c, ada_w, ada_b, pre_mix_g, post_mix_g, w_in, q_norm_g, k_norm_g, ret_decay_fwd, ret_decay_bwd, ret_gn_g, ret_gn_b, w_out, pre_ffn_g, post_ffn_g, router_w, router_b, exp_w_up, exp_b_up, exp_w_down, exp_b_down):
    b, s, d = x.shape
    depth = ada_w.shape[0]
    groups = 2 if b % 2 == 0 else 1
    bg = b // groups
    n = bg * s
    n_asg = n * TOP_K
    n_pad = n_asg + N_EXPERTS * MOE_TILE
    n_tiles = n_pad // MOE_TILE

    mod = _modulation(c, ada_w, ada_b)
    ropes = _rope_tables(s, HEAD_DIM) + _rope_tables(s, RET_DIM)
    bu = exp_b_up.reshape(depth, N_EXPERTS, 1, 2 * D_FF)
    bd = exp_b_down.reshape(depth, N_EXPERTS, 1, d)
    xs_in = [x.reshape(b * s, d)] * groups
    x_b0 = [g * bg for g in range(groups)]
    out = None
    for l in range(depth):
        mod_l = mod[l].reshape(b, 6, d)
        qg = jnp.tile(q_norm_g[l], LANES // HEAD_DIM).reshape(1, LANES)
        kg = jnp.tile(k_norm_g[l], LANES // HEAD_DIM).reshape(1, LANES)
        v_scale, fac = _value_range_scale(mod_l, pre_mix_g[l], w_in[l])
        w_in_b = w_in[l].astype(BF16)
        w_out_b = w_out[l].astype(BF16)
        rw_t = router_w[l].T.astype(BF16)
        dec = jnp.stack([ret_decay_fwd[l], ret_decay_bwd[l]]).astype(F32)
        last = l + 1 == depth
        proj = [_inproj(xs_in[g], g * bg, x_b0[g], mod_l, pre_mix_g[l].reshape(1, d), w_in_b, ropes, qg, kg, v_scale,
                        bg, s) for g in range(groups)]
        oa = [_attention(p[0], p[1], p[2], fac, g * bg, bg, s) for g, p in enumerate(proj)]
        orr = [_retention(p[3], p[4], p[5], p[6], dec, ret_gn_g[l].reshape(1, RET_W), ret_gn_b[l].reshape(1, RET_W),
                          bg, s) for p in proj]
        routed = []
        for g in range(groups):
            xn, hp, tidx, tw, rank, counts = _outproj(
                oa[g], orr[g], w_out_b, xs_in[g], g * bg, x_b0[g], mod_l, post_mix_g[l].reshape(1, d),
                pre_ffn_g[l].reshape(1, d), rw_t, router_b[l].reshape(N_EXPERTS, 1), bg, s)
            pos, tile_expert, n_valid = _route(counts, tidx, rank, n_tiles)
            pos3 = pos.reshape(TOP_K, n // SC_CHUNK, SC_CHUNK).transpose(1, 0, 2)
            routed.append((xn, tw, pos, tile_expert, n_valid, _sc_scatter_rows(hp, pos3, n_pad)))
        yg = []
        for xn, tw, pos, tile_expert, n_valid, xs in routed:
            ys = _experts(xs, tile_expert, n_valid, l, exp_w_up, bu, exp_w_down, bd)
            yg.append(_sc_gather_rows(ys, pos.reshape(n_asg)))
        nxt = []
        for g, (xn, tw, *_) in enumerate(routed):
            if last:
                out = _combine(yg[g], tw.T, xn, g * bg, mod_l, post_ffn_g[l].reshape(1, d), bg, s,
                               out_batches=b, shared_out=out)
            else:
                nxt.append(_combine(yg[g], tw.T, xn, g * bg, mod_l, post_ffn_g[l].reshape(1, d), bg, s))
        xs_in, x_b0 = nxt, [0] * groups
    return out.reshape(b, s, d)
```

```python
import functools

import numpy as np
import jax
import jax.numpy as jnp
from jax import lax
from jax.experimental import pallas as pl
from jax.experimental.pallas import tpu as pltpu
from jax.experimental.pallas import tpu_sc as plsc

F32 = jnp.float32
BF16 = jnp.bfloat16
U32 = jnp.uint32
I32 = jnp.int32
FP8 = jnp.float8_e4m3fn

D_MODEL = 1024
GRID_W = 64
ROPE_THETA = 10000.0
HEAD_DIM = 64
Q_HEADS = 8
KV_HEADS = 2
RET_HEADS = 4
RET_DIM = 128
RET_CHUNK = 128
ATTN_W = Q_HEADS * HEAD_DIM
KV_W = KV_HEADS * HEAD_DIM
RET_W = RET_HEADS * RET_DIM
IN_W = ATTN_W + 2 * KV_W + 4 * RET_W
N_EXPERTS = 32
TOP_K = 4
D_FF = D_MODEL
SWIGLU_LIMIT = 7.0
SWIGLU_ALPHA = 1.702
NORM_EPS = 1e-6
Q_SCALE = HEAD_DIM ** -0.5 * float(np.log2(np.e))
FP8_RANGE = 224.0
ACT_SCALE = FP8_RANGE / (SWIGLU_LIMIT * (SWIGLU_LIMIT + 1.0))
V_RANGE = FP8_RANGE
P_SHIFT = 8.0
HALF = D_MODEL // 2
LANES = 128

ROW_TILE = 512
Q_TILE = 128
Q_TILES_PER_TRIP = 2
KV_TILE = 1024
MOE_TILE = 512
RET_CHUNKS_PER_TRIP = 8
VMEM_LIMIT = 48 * 1024 * 1024
EXPERTS_VMEM_LIMIT = 56 * 1024 * 1024

SC_CORES = 2
SC_SUBCORES = 16
SC_WORKERS = SC_CORES * SC_SUBCORES
SC_CHUNK = 64


def _params(sem):
    return pltpu.CompilerParams(dimension_semantics=sem, vmem_limit_bytes=VMEM_LIMIT)


def _rms(x, g):
    return x * lax.rsqrt(jnp.mean(x * x, axis=-1, keepdims=True) + NORM_EPS) * g


def _pack_bf16_pairs(y):
    u = lax.bitcast_convert_type(y.astype(BF16).astype(F32), U32)
    return (u[:, :HALF] >> 16) | u[:, HALF:]


def _unpack_lo(w):
    return lax.bitcast_convert_type(w << 16, F32)


def _unpack_hi(w):
    return lax.bitcast_convert_type(w & jnp.uint32(0xFFFF0000), F32)


def _mod_kernel(c_ref, w_ref, b_ref, o_ref):
    c = c_ref[...]
    cond = c * jax.nn.sigmoid(c)
    o_ref[0] = jnp.dot(cond, w_ref[0], preferred_element_type=F32, precision=lax.Precision.HIGHEST) + b_ref[0]


def _modulation(c, ada_w, ada_b):
    depth, d, w6 = ada_w.shape
    b = c.shape[0]
    tn = 1536
    return pl.pallas_call(
        _mod_kernel,
        grid=(depth, w6 // tn),
        in_specs=[
            pl.BlockSpec((b, d), lambda l, j: (0, 0)),
            pl.BlockSpec((1, d, tn), lambda l, j: (l, 0, j)),
            pl.BlockSpec((1, 1, tn), lambda l, j: (l, 0, j)),
        ],
        out_specs=pl.BlockSpec((1, b, tn), lambda l, j: (l, 0, j)),
        out_shape=jax.ShapeDtypeStruct((depth, b, w6), F32),
        compiler_params=_params(("arbitrary", "arbitrary")),
        name="modulation",
    )(c, ada_w, ada_b.reshape(depth, 1, w6))


def _rope_tables(s, head_dim):
    quarter = head_dim // 4
    t = jnp.arange(s, dtype=F32)
    row = jnp.floor(t / GRID_W)
    col = t - row * GRID_W
    inv_freq = ROPE_THETA ** (-jnp.arange(quarter, dtype=F32) / quarter)
    lane = np.arange(LANES) % head_dim
    use_col = lane >= head_dim // 2
    second = (lane % (head_dim // 2)) >= quarter
    freq = inv_freq[lane % quarter]
    pos = jnp.where(use_col[None, :], col[:, None], row[:, None])
    ang = pos * freq[None, :]
    sign = jnp.where(second, 1.0, -1.0).astype(F32)
    return jnp.cos(ang), jnp.sin(ang) * sign[None, :]


def _rope(z, cos, sin_signed, quarter):
    lane = lax.broadcasted_iota(I32, z.shape, 1)
    first = (lane % (2 * quarter)) < quarter
    partner = jnp.where(first, pltpu.roll(z, LANES - quarter, 1), pltpu.roll(z, quarter, 1))
    return z * cos + partner * sin_signed


def _inproj_kernel(x_ref, mod_ref, g_ref, w_ref, ca_ref, sa_ref, cr_ref, sr_ref, qg_ref, kg_ref, vs_ref,
                   qa_ref, kd_ref, vt_ref, qr_ref, kr_ref, vr_ref, gr_ref):
    x = x_ref[...]
    m = mod_ref[0]
    h = _rms(x, g_ref[...]) * (1.0 + m[1:2, :]) + m[0:1, :]
    hb = h.astype(BF16)
    tm = x.shape[0]

    def proj(c0):
        z = jnp.dot(hb, w_ref[:, c0:c0 + 2 * LANES], preferred_element_type=F32)
        return z[:, :LANES], z[:, LANES:]

    ri = lax.broadcasted_iota(I32, (LANES, LANES), 0) // HEAD_DIM
    ci = lax.broadcasted_iota(I32, (LANES, LANES), 1) // HEAD_DIM
    head_ones = (ri == ci).astype(BF16)
    ca, sa = ca_ref[...], sa_ref[...]
    cr, sr = cr_ref[...], sr_ref[...]
    lane = lax.broadcasted_iota(I32, (tm, LANES), 1)
    low = lane < HEAD_DIM

    def head_norm_rope(z, g):
        ss = jnp.dot((z * z).astype(BF16), head_ones, preferred_element_type=F32)
        zn = z * lax.rsqrt(ss * (1.0 / HEAD_DIM) + NORM_EPS) * g
        return _rope(zn, ca, sa, HEAD_DIM // 4)

    def store_pair(ref, j, fn):
        def finish(zs):
            for i, z in enumerate(zs):
                ref[:, (2 * j + i) * LANES:(2 * j + i + 1) * LANES] = fn(z).astype(BF16)
        return finish

    def store_kv(zs):
        k, v = zs
        k = head_norm_rope(k, kg_ref[...])
        k_sw = pltpu.roll(k, HEAD_DIM, 1)
        v = v * vs_ref[0]
        kd_ref[0, 0] = jnp.where(low, k, k_sw).astype(BF16)
        kd_ref[0, 1] = jnp.where(low, k_sw, k).astype(BF16)
        vt_ref[0, 0] = jnp.where(low, v, 1.0).T.astype(FP8)
        vt_ref[0, 1] = jnp.where(low, pltpu.roll(v, HEAD_DIM, 1), 1.0).T.astype(FP8)

    jobs = [(2 * j * LANES, store_pair(qa_ref, j, lambda z: head_norm_rope(z, qg_ref[...]) * Q_SCALE))
            for j in range(ATTN_W // (2 * LANES))]
    jobs.append((ATTN_W, store_kv))
    base = ATTN_W + 2 * KV_W
    for j in range(RET_W // (2 * LANES)):
        off = 2 * j * LANES
        jobs += [
            (base + off, store_pair(qr_ref, j, lambda z: _rope(z, cr, sr, RET_DIM // 4))),
            (base + RET_W + off, store_pair(kr_ref, j, lambda z: _rope(z, cr, sr, RET_DIM // 4) * (RET_DIM ** -0.5))),
            (base + 2 * RET_W + off, store_pair(vr_ref, j, lambda z: z)),
            (base + 3 * RET_W + off, store_pair(gr_ref, j, lambda z: z * jax.nn.sigmoid(z))),
        ]
    zs = proj(jobs[0][0])
    for i, (_, finish) in enumerate(jobs):
        nxt = proj(jobs[i + 1][0]) if i + 1 < len(jobs) else None
        finish(zs)
        zs = nxt


def _inproj(x2, batch0, x_batch0, mod_l, pre_g, w_in_b, ropes, qg, kg, v_scale, b, s):
    n = b * s
    tm = min(ROW_TILE, s)
    nb = s // tm
    ca, sa, cr, sr = ropes
    row = lambda i: (i, 0)
    full = lambda i: (0, 0)
    rope_spec = pl.BlockSpec((tm, LANES), lambda i: (i % nb, 0))
    wide = jax.ShapeDtypeStruct((n, RET_W), BF16)
    return pl.pallas_call(
        _inproj_kernel,
        grid=(n // tm,),
        in_specs=[
            pl.BlockSpec((tm, D_MODEL), lambda i: (x_batch0 * nb + i, 0)),
            pl.BlockSpec((1, 6, D_MODEL), lambda i: (batch0 + i // nb, 0, 0)),
            pl.BlockSpec((1, D_MODEL), full),
            pl.BlockSpec((D_MODEL, IN_W), full),
            rope_spec, rope_spec, rope_spec, rope_spec,
            pl.BlockSpec((1, LANES), full),
            pl.BlockSpec((1, LANES), full),
            pl.BlockSpec((1, 1, LANES), lambda i: (batch0 + i // nb, 0, 0)),
        ],
        out_specs=[
            pl.BlockSpec((tm, ATTN_W), row),
            pl.BlockSpec((1, KV_HEADS, tm, LANES), lambda i: (i // nb, 0, i % nb, 0)),
            pl.BlockSpec((1, KV_HEADS, LANES, tm), lambda i: (i // nb, 0, 0, i % nb)),
            pl.BlockSpec((tm, RET_W), row),
            pl.BlockSpec((tm, RET_W), row),
            pl.BlockSpec((tm, RET_W), row),
            pl.BlockSpec((tm, RET_W), row),
        ],
        out_shape=[
            jax.ShapeDtypeStruct((n, ATTN_W), BF16),
            jax.ShapeDtypeStruct((b, KV_HEADS, s, LANES), BF16),
            jax.ShapeDtypeStruct((b, KV_HEADS, LANES, s), FP8),
            wide, wide, wide, wide,
        ],
        compiler_params=_params(("parallel",)),
        name="inproj",
    )(x2, mod_l, pre_g, w_in_b, ca, sa, cr, sr, qg, kg, v_scale)


def _attn_kernel(q_ref, kd_ref, vt_ref, fac_ref, o_ref, q_s, qn_s, m_ref, acc_ref, s0_ref, s1_ref,
                 *, tq, tk, tiles):
    s = kd_ref.shape[2]
    nk = s // tk
    nq = s // tq
    pairs = (Q_HEADS // KV_HEADS) // 2
    top = lax.broadcasted_iota(I32, (LANES, tq), 0) < HEAD_DIM
    unscale = jnp.concatenate([fac_ref[0]] * (2 * tq // LANES), axis=1)

    chains = tiles * pairs

    def load_q(trip, dst):
        for u in range(tiles):
            rows = pl.ds(pl.multiple_of((trip * tiles + u) * tq, tq), tq)
            for p in range(pairs):
                qt = q_ref[rows, p * LANES:(p + 1) * LANES].astype(F32).T
                dst[u * pairs + p, :, :tq] = jnp.where(top, qt, 0.0).astype(BF16)
                dst[u * pairs + p, :, tq:] = jnp.where(top, 0.0, qt).astype(BF16)

    def keys(j):
        return kd_ref[0, 0, j * tk:(j + 1) * tk, :]

    def step(j, cur_ref, nxt_ref, q_next, j_next):
        vt = vt_ref[0, 0, :, j * tk:(j + 1) * tk]
        kd = keys(j_next)
        nxt_ref[0] = jnp.dot(kd, q_next[0], preferred_element_type=F32)
        for c in range(chains):
            sc = cur_ref[c]
            m_old = m_ref[c]
            m_new = jnp.maximum(m_old, jnp.max(sc, axis=0, keepdims=True))
            pt = jnp.exp2(sc - (m_new - P_SHIFT))
            alpha = jnp.exp2(m_old - m_new)
            if c + 1 < chains:
                nxt_ref[c + 1] = jnp.dot(kd, q_next[c + 1], preferred_element_type=F32)
            acc_ref[c] = alpha * acc_ref[c] + jnp.dot(vt, pt.astype(FP8), preferred_element_type=F32)
            m_ref[c] = m_new

    bufs = (s0_ref, s1_ref)
    trips = nq // tiles
    load_q(0, q_s)
    for c in range(chains):
        s0_ref[c] = jnp.dot(keys(0), q_s[c], preferred_element_type=F32)

    def body(trip, carry):
        load_q(trip, q_s)
        load_q(jnp.minimum(trip + 1, trips - 1), qn_s)
        m_ref[...] = jnp.full(m_ref.shape, -jnp.inf, F32)
        acc_ref[...] = jnp.zeros(acc_ref.shape, F32)
        for j in range(nk):
            last = j + 1 == nk
            step(j, bufs[j % len(bufs)], bufs[(j + 1) % len(bufs)], qn_s if last else q_s, 0 if last else j + 1)
        for c in range(chains):
            u, p = divmod(c, pairs)
            rows = pl.ds(pl.multiple_of((trip * tiles + u) * tq, tq), tq)
            acc = acc_ref[c]
            o_t = acc[:HEAD_DIM] / acc[HEAD_DIM:] * unscale
            both = jnp.concatenate([o_t[:, :tq], o_t[:, tq:]], axis=0)
            o_ref[rows, p * LANES:(p + 1) * LANES] = both.T.astype(BF16)
        return carry

    lax.fori_loop(0, trips, body, 0)


def _attention(qa, kd, vt, fac, batch0, b, s):
    n = qa.shape[0]
    tq = Q_TILE
    tk = min(KV_TILE, s // 2)
    tiles = Q_TILES_PER_TRIP
    assert tq == LANES and s % (tiles * tq) == 0 and s % (2 * tk) == 0 and tk % LANES == 0, (s, tq, tk)
    gw = ATTN_W // KV_HEADS
    chains = tiles * ((Q_HEADS // KV_HEADS) // 2)
    return pl.pallas_call(
        functools.partial(_attn_kernel, tq=tq, tk=tk, tiles=tiles),
        grid=(b, KV_HEADS),
        in_specs=[
            pl.BlockSpec((s, gw), lambda bi, h: (bi, h)),
            pl.BlockSpec((1, 1, s, LANES), lambda bi, h: (bi, h, 0, 0)),
            pl.BlockSpec((1, 1, LANES, s), lambda bi, h: (bi, h, 0, 0)),
            pl.BlockSpec((1, 1, LANES), lambda bi, h: (batch0 + bi, 0, 0)),
        ],
        out_specs=pl.BlockSpec((s, gw), lambda bi, h: (bi, h)),
        out_shape=jax.ShapeDtypeStruct((n, ATTN_W), BF16),
        scratch_shapes=[
            pltpu.VMEM((chains, LANES, 2 * tq), BF16),
            pltpu.VMEM((chains, LANES, 2 * tq), BF16),
            pltpu.VMEM((chains, 1, 2 * tq), F32),
            pltpu.VMEM((chains, LANES, 2 * tq), F32),
            pltpu.VMEM((chains, tk, 2 * tq), F32),
            pltpu.VMEM((chains, tk, 2 * tq), F32),
        ],
        compiler_params=_params(("parallel", "parallel")),
        name="attention",
    )(qa, kd, vt, fac)


def _retention_kernel(dec_ref, q_ref, k_ref, v_ref, g_ref, gng_ref, gnb_ref, o_ref,
                      fbuf, bbuf, sf_ref, sb_ref):
    h = pl.program_id(1)
    c = RET_CHUNK
    s = q_ref.shape[1]
    nc = s // c
    lgf = jnp.full((1, 1), dec_ref[0, h], F32)
    lgb = jnp.full((1, 1), dec_ref[1, h], F32)
    ii = lax.broadcasted_iota(I32, (c, c), 0)
    jj = lax.broadcasted_iota(I32, (c, c), 1)
    diff = (ii - jj).astype(F32)
    d_f = jnp.where(diff >= 0, jnp.exp(lgf * jnp.maximum(diff, 0.0)), 0.0)
    d_b = jnp.where(diff <= 0, jnp.exp(lgb * jnp.maximum(-diff, 0.0)), 0.0)
    idx = lax.broadcasted_iota(I32, (c, 1), 0).astype(F32)
    xi_f = jnp.exp(lgf * (idx + 1.0))
    zeta_f = jnp.exp(lgf * (c - 1.0 - idx))
    xi_b = jnp.exp(lgb * (c - idx))
    zeta_b = jnp.exp(lgb * idx)
    cd_f = jnp.exp(lgf * c)
    cd_b = jnp.exp(lgb * c)
    sf_ref[...] = jnp.zeros(sf_ref.shape, F32)
    sb_ref[...] = jnp.zeros(sb_ref.shape, F32)
    nt = (((1,), (1,)), ((), ()))

    fwd = (d_f, xi_f, zeta_f, cd_f, sf_ref, fbuf)
    bwd = (d_b, xi_b, zeta_b, cd_b, sb_ref, bbuf)

    per = min(RET_CHUNKS_PER_TRIP, nc)

    def body(i, carry):
        jobs = []
        for n in range(per):
            jobs += [(fwd, per * i + n), (bwd, nc - 1 - per * i - n)]
        offs = [pl.multiple_of(ci * c, c) for _, ci in jobs]
        qs = [q_ref[0, pl.ds(off, c), :] for off in offs]
        ks = [k_ref[0, pl.ds(off, c), :] for off in offs]
        vs = [v_ref[0, pl.ds(off, c), :] for off in offs]
        scores = [lax.dot_general(q, k, nt, preferred_element_type=F32) for q, k in zip(qs, ks)]
        incs = []
        for (d, _), k, v in zip(jobs, ks, vs):
            kzt = (k.astype(F32) * d[2]).T.astype(BF16)
            incs.append(jnp.dot(kzt, v, preferred_element_type=F32))
        states = [fwd[4][...], bwd[4][...]]
        cross = []
        for n, (d, _) in enumerate(jobs):
            cross.append(jnp.dot(qs[n], states[n].astype(BF16), preferred_element_type=F32))
            states.append(states[n] * d[3] + incs[n])
        for n, ((d, _), off) in enumerate(zip(jobs, offs)):
            a = (scores[n] * d[0]).astype(BF16)
            d[5][pl.ds(off, c), :] = jnp.dot(a, vs[n], preferred_element_type=F32) + cross[n] * d[1]
        fwd[4][...] = states[2 * per]
        bwd[4][...] = states[2 * per + 1]
        return carry

    lax.fori_loop(0, nc // per, body, 0)

    rows = min(1024, s)

    def fin(t, carry):
        off = pl.multiple_of(t * rows, rows)
        r = fbuf[pl.ds(off, rows), :] + bbuf[pl.ds(off, rows), :]
        mu = jnp.mean(r, axis=-1, keepdims=True)
        d = r - mu
        var = jnp.mean(d * d, axis=-1, keepdims=True)
        y = d * lax.rsqrt(var + NORM_EPS) * gng_ref[...] + gnb_ref[...]
        o_ref[0, pl.ds(off, rows), :] = (g_ref[0, pl.ds(off, rows), :].astype(F32) * y).astype(BF16)
        return carry

    lax.fori_loop(0, s // rows, fin, 0)


def _retention(qr, kr, vr, gr, dec, gn_g, gn_b, b, s):
    shp = (b, s, RET_W)
    assert (s // RET_CHUNK) % min(RET_CHUNKS_PER_TRIP, s // RET_CHUNK) == 0, s
    head = pl.BlockSpec((1, s, RET_DIM), lambda bi, h: (bi, 0, h))
    vec = pl.BlockSpec((1, RET_DIM), lambda bi, h: (0, h))
    out = pl.pallas_call(
        _retention_kernel,
        grid=(b, RET_HEADS),
        in_specs=[pl.BlockSpec(memory_space=pltpu.SMEM), head, head, head, head, vec, vec],
        out_specs=head,
        out_shape=jax.ShapeDtypeStruct(shp, BF16),
        scratch_shapes=[
            pltpu.VMEM((s, RET_DIM), F32),
            pltpu.VMEM((s, RET_DIM), F32),
            pltpu.VMEM((RET_DIM, RET_DIM), F32),
            pltpu.VMEM((RET_DIM, RET_DIM), F32),
        ],
        compiler_params=_params(("parallel", "parallel")),
        name="retention",
    )(dec, qr.reshape(shp), kr.reshape(shp), vr.reshape(shp), gr.reshape(shp), gn_g, gn_b)
    return out.reshape(b * s, RET_W)


def _outproj_kernel(oa_ref, or_ref, w_ref, x_ref, mod_ref, pg_ref, fg_ref, rw_ref, rb_ref,
                    xn_ref, hp_ref, ti_ref, tw_ref, rk_ref, cnt_ref, base_ref):
    @pl.when(pl.program_id(0) == 0)
    def _():
        base_ref[...] = jnp.zeros(base_ref.shape, F32)

    m = mod_ref[0]
    tm = x_ref.shape[0]
    nt = (((1,), (1,)), ((), ()))
    halves = [slice(0, tm // 2), slice(tm // 2, tm)]
    ys = [jnp.dot(oa_ref[r, :], w_ref[:ATTN_W, :], preferred_element_type=F32)
          + jnp.dot(or_ref[r, :], w_ref[ATTN_W:, :], preferred_element_type=F32) for r in halves]
    parts = []
    for r, y in zip(halves, ys):
        xn = x_ref[r, :] + m[2:3, :] * _rms(y, pg_ref[...])
        xn_ref[r, :] = xn
        h = _rms(xn, fg_ref[...]) * (1.0 + m[4:5, :]) + m[3:4, :]
        hb = h.astype(BF16)
        u = lax.bitcast_convert_type(hb.astype(F32), U32)
        hp_ref[r, :] = (u[:, :HALF] >> 16) | u[:, HALF:]
        parts.append(lax.dot_general(rw_ref[...], hb, nt, preferred_element_type=F32))
    logits = jnp.concatenate(parts, axis=1) + rb_ref[...]
    e_iota = lax.broadcasted_iota(I32, (N_EXPERTS, tm), 0).astype(F32)
    vals, idxs = [], []
    cur = logits
    for _ in range(TOP_K):
        mx = jnp.max(cur, axis=0, keepdims=True)
        ik = jnp.min(jnp.where(cur == mx, e_iota, float(N_EXPERTS)), axis=0, keepdims=True)
        vals.append(mx)
        idxs.append(ik)
        cur = jnp.where(e_iota == ik, -jnp.inf, cur)
    v = jnp.concatenate(vals, axis=0)
    w = jnp.exp(v - v[0:1, :])
    tw_ref[...] = w / jnp.sum(w, axis=0, keepdims=True)
    ti_ref[...] = jnp.concatenate(idxs, axis=0).astype(I32)

    onehot = jnp.zeros((N_EXPERTS, tm), F32)
    for ik in idxs:
        onehot = onehot + (e_iota == ik).astype(F32)
    before = (lax.broadcasted_iota(I32, (tm, tm), 0) < lax.broadcasted_iota(I32, (tm, tm), 1)).astype(BF16)
    seen = jnp.dot(onehot.astype(BF16), before, preferred_element_type=F32) + base_ref[...]
    ranks = [jnp.sum(jnp.where(e_iota == ik, seen, 0.0), axis=0, keepdims=True) for ik in idxs]
    rk_ref[...] = jnp.concatenate(ranks, axis=0).astype(I32)
    total = base_ref[...] + jnp.sum(onehot, axis=1, keepdims=True)
    base_ref[...] = total
    cnt_ref[...] = jnp.broadcast_to(total, cnt_ref.shape).astype(I32)


def _outproj(oa, orr, w_out_b, x2, batch0, x_batch0, mod_l, post_g, ffn_g, rw_t, rb, b, s):
    n = b * s
    tm = min(ROW_TILE, s)
    nb = s // tm
    row = lambda i: (i, 0)
    full = lambda i: (0, 0)
    col = lambda i: (0, i)
    return pl.pallas_call(
        _outproj_kernel,
        grid=(n // tm,),
        in_specs=[
            pl.BlockSpec((tm, ATTN_W), row),
            pl.BlockSpec((tm, RET_W), row),
            pl.BlockSpec((D_MODEL, D_MODEL), full),
            pl.BlockSpec((tm, D_MODEL), lambda i: (x_batch0 * nb + i, 0)),
            pl.BlockSpec((1, 6, D_MODEL), lambda i: (batch0 + i // nb, 0, 0)),
            pl.BlockSpec((1, D_MODEL), full),
            pl.BlockSpec((1, D_MODEL), full),
            pl.BlockSpec((N_EXPERTS, D_MODEL), full),
            pl.BlockSpec((N_EXPERTS, 1), full),
        ],
        out_specs=[
            pl.BlockSpec((tm, D_MODEL), row),
            pl.BlockSpec((tm, HALF), row),
            pl.BlockSpec((TOP_K, tm), col),
            pl.BlockSpec((TOP_K, tm), col),
            pl.BlockSpec((TOP_K, tm), col),
            pl.BlockSpec((N_EXPERTS, LANES), full),
        ],
        out_shape=[
            jax.ShapeDtypeStruct((n, D_MODEL), F32),
            jax.ShapeDtypeStruct((n, HALF), U32),
            jax.ShapeDtypeStruct((TOP_K, n), I32),
            jax.ShapeDtypeStruct((TOP_K, n), F32),
            jax.ShapeDtypeStruct((TOP_K, n), I32),
            jax.ShapeDtypeStruct((N_EXPERTS, LANES), I32),
        ],
        scratch_shapes=[pltpu.VMEM((N_EXPERTS, 1), F32)],
        compiler_params=_params(("arbitrary",)),
        name="outproj_router",
    )(oa, orr, w_out_b, x2, mod_l, post_g, ffn_g, rw_t, rb)


def _sc_scatter_rows(rows, pos3, n_out):
    n, w = rows.shape
    nchunk, kk, c = pos3.shape
    per_w = nchunk // SC_WORKERS
    mesh = plsc.VectorSubcoreMesh(core_axis_name="c", subcore_axis_name="s")

    @functools.partial(
        pl.kernel, mesh=mesh,
        out_type=jax.ShapeDtypeStruct((n_out, w), rows.dtype),
        scratch_types=[pltpu.VMEM((kk, c), I32), pltpu.VMEM((c, w), rows.dtype)],
    )
    def k(rows_hbm, pos_hbm, out_hbm, idx_v, rows_v):
        wid = lax.axis_index("s") * SC_CORES + lax.axis_index("c")

        @pl.loop(0, per_w)
        def _(i):
            ch = wid * per_w + i
            pltpu.sync_copy(pos_hbm.at[ch], idx_v)
            pltpu.sync_copy(rows_hbm.at[pl.ds(ch * c, c)], rows_v)
            for j in range(kk):
                pltpu.sync_copy(rows_v, out_hbm.at[idx_v.at[j]])

    return k(rows, pos3)


def _sc_gather_rows(table, idx):
    w = table.shape[1]
    b = idx.shape[0]
    c = SC_CHUNK
    per_w = b // (SC_WORKERS * c)
    mesh = plsc.VectorSubcoreMesh(core_axis_name="c", subcore_axis_name="s")

    @functools.partial(
        pl.kernel, mesh=mesh,
        out_type=jax.ShapeDtypeStruct((b, w), table.dtype),
        scratch_types=[pltpu.VMEM((c,), I32), pltpu.VMEM((c, w), table.dtype), pltpu.SemaphoreType.DMA],
    )
    def k(table_hbm, idx_hbm, out_hbm, idx_v, rows_v, sem):
        wid = lax.axis_index("s") * SC_CORES + lax.axis_index("c")

        @pl.loop(0, per_w)
        def _(i):
            base = (wid * per_w + i) * c
            pltpu.sync_copy(idx_hbm.at[pl.ds(base, c)], idx_v)
            pltpu.async_copy(table_hbm.at[idx_v], rows_v, sem).wait()
            pltpu.sync_copy(rows_v, out_hbm.at[pl.ds(base, c)])

    return k(table, idx)


def _experts_kernel(te_ref, nv_ref, xs_ref, wu_ref, bu_ref, wd_ref, bd_ref, ys_ref, wu_b, wd_b, inv_ref):
    i = pl.program_id(0)
    live = i < nv_ref[0]
    new_expert = jnp.logical_or(i == 0, te_ref[i] != te_ref[jnp.maximum(i - 1, 0)])

    def full_max(a):
        return jnp.max(jnp.max(jnp.abs(a), axis=0, keepdims=True), axis=1, keepdims=True)

    @pl.when(jnp.logical_and(live, new_expert))
    def _():
        wu = wu_ref[0, 0].astype(BF16)
        wd = wd_ref[0, 0].astype(BF16)
        mu = jnp.maximum(full_max(wu).astype(F32), 1e-30)
        md = jnp.maximum(full_max(wd).astype(F32), 1e-30)
        wu_b[...] = (wu * (FP8_RANGE / mu).astype(BF16)).astype(FP8)
        wd_b[...] = (wd * (FP8_RANGE / md).astype(BF16)).astype(FP8)
        inv_ref[0:1, :] = jnp.broadcast_to(mu / FP8_RANGE, (1, LANES))
        inv_ref[1:2, :] = jnp.broadcast_to(md / (FP8_RANGE * ACT_SCALE), (1, LANES))

    @pl.when(live)
    def _():
        t = xs_ref.shape[0]
        halves = [slice(0, t // 2), slice(t // 2, t)]
        inv_up = inv_ref[0:1, 0:1]
        inv_dn = inv_ref[1:2, 0:1]
        ups = []
        for r in halves:
            w = xs_ref[r, :]
            lo = _unpack_lo(w).astype(BF16)
            hi = _unpack_hi(w).astype(BF16)
            amax = jnp.max(jnp.maximum(jnp.abs(lo), jnp.abs(hi)), axis=1, keepdims=True).astype(F32)
            c = (FP8_RANGE / jnp.maximum(amax, 1e-30)).astype(BF16)
            up = (jnp.dot((lo * c).astype(FP8), wu_b[:HALF, :], preferred_element_type=F32)
                  + jnp.dot((hi * c).astype(FP8), wu_b[HALF:, :], preferred_element_type=F32))
            ups.append(up * (inv_up / c.astype(F32)))
        for r, up in zip(halves, ups):
            up = (up + bu_ref[0, 0]).astype(BF16)
            glu = jnp.minimum(up[:, :D_FF], SWIGLU_LIMIT)
            lin = jnp.clip(up[:, D_FF:], -SWIGLU_LIMIT, SWIGLU_LIMIT)
            act = (glu * (0.5 * ACT_SCALE)) * (1.0 + jnp.tanh((0.5 * SWIGLU_ALPHA) * glu)) * (lin + 1.0)
            y = jnp.dot(act.astype(FP8), wd_b[...], preferred_element_type=F32) * inv_dn + bd_ref[0, 0]
            ys_ref[r, :] = _pack_bf16_pairs(y)


def _experts(xs, tile_expert, n_valid, layer, wu, bu, wd, bd):
    n_pad = xs.shape[0]
    t = MOE_TILE
    grid_spec = pltpu.PrefetchScalarGridSpec(
        num_scalar_prefetch=2,
        grid=(n_pad // t,),
        in_specs=[
            pl.BlockSpec((t, HALF), lambda i, te, nv: (i, 0)),
            pl.BlockSpec((1, 1, D_MODEL, 2 * D_FF), lambda i, te, nv: (layer, te[i], 0, 0)),
            pl.BlockSpec((1, 1, 1, 2 * D_FF), lambda i, te, nv: (layer, te[i], 0, 0)),
            pl.BlockSpec((1, 1, D_FF, D_MODEL), lambda i, te, nv: (layer, te[i], 0, 0)),
            pl.BlockSpec((1, 1, 1, D_MODEL), lambda i, te, nv: (layer, te[i], 0, 0)),
        ],
        out_specs=pl.BlockSpec((t, HALF), lambda i, te, nv: (i, 0)),
        scratch_shapes=[pltpu.VMEM((D_MODEL, 2 * D_FF), FP8), pltpu.VMEM((D_FF, D_MODEL), FP8),
                        pltpu.VMEM((2, LANES), F32)],
    )
    return pl.pallas_call(
        _experts_kernel,
        grid_spec=grid_spec,
        out_shape=jax.ShapeDtypeStruct((n_pad, HALF), U32),
        compiler_params=pltpu.CompilerParams(dimension_semantics=("arbitrary",), vmem_limit_bytes=EXPERTS_VMEM_LIMIT),
        name="experts",
    )(tile_expert, n_valid, xs, wu, bu, wd, bd)


def _combine_kernel(y0_ref, y1_ref, y2_ref, y3_ref, w_ref, x_ref, mod_ref, g_ref, *rest):
    o_ref = rest[-1]
    wt = w_ref[...]
    w = jnp.concatenate([wt, jnp.zeros_like(wt)], axis=0).T
    lo = None
    hi = None
    for k, y_ref in enumerate((y0_ref, y1_ref, y2_ref, y3_ref)):
        word = y_ref[...]
        wk = w[:, k:k + 1]
        lo_k = wk * _unpack_lo(word)
        hi_k = wk * _unpack_hi(word)
        lo = lo_k if lo is None else lo + lo_k
        hi = hi_k if hi is None else hi + hi_k
    yf = jnp.concatenate([lo, hi], axis=1)
    m = mod_ref[0]
    o_ref[...] = x_ref[...] + m[5:6, :] * _rms(yf, g_ref[...])


def _combine(yg, tw, xn, batch0, mod_l, post_g, b, s, out_batches=None, shared_out=None):
    n = b * s
    tm = min(ROW_TILE, s)
    nb = s // tm
    nblk = n // tm
    row = lambda i: (i, 0)
    full = lambda i: (0, 0)
    y_specs = [pl.BlockSpec((tm, HALF), functools.partial(lambda i, k: (k * nblk + i, 0), k=k)) for k in range(TOP_K)]
    out_rows, out_blk0 = (n, 0) if out_batches is None else (out_batches * s, batch0 * nb)
    extra_specs, extra_args, aliases = [], [], {}
    if shared_out is not None:
        extra_specs, extra_args, aliases = [pl.BlockSpec(memory_space=pl.ANY)], [shared_out], {8: 0}
    return pl.pallas_call(
        _combine_kernel,
        grid=(nblk,),
        in_specs=y_specs + [
            pl.BlockSpec((TOP_K, tm), lambda i: (0, i)),
            pl.BlockSpec((tm, D_MODEL), row),
            pl.BlockSpec((1, 6, D_MODEL), lambda i: (batch0 + i // nb, 0, 0)),
            pl.BlockSpec((1, D_MODEL), full),
        ] + extra_specs,
        out_specs=pl.BlockSpec((tm, D_MODEL), lambda i: (out_blk0 + i, 0)),
        out_shape=jax.ShapeDtypeStruct((out_rows, D_MODEL), F32),
        input_output_aliases=aliases,
        compiler_params=_params(("parallel",)),
        name="combine",
    )(yg, yg, yg, yg, tw, xn, mod_l, post_g, *extra_args)


def _route(counts, tidx, rank, n_tiles):
    t = MOE_TILE
    cnt = counts[:, 0]
    padded = ((cnt + t - 1) // t) * t
    ends = jnp.cumsum(padded)
    starts = ends - padded
    experts = jnp.arange(N_EXPERTS, dtype=I32)[:, None, None]
    start_of = jnp.sum(jnp.where(tidx[None] == experts, starts[:, None, None], 0), axis=0)
    pos = start_of + rank
    tile_ends = ends // t
    n_valid = tile_ends[-1]
    tile = jnp.minimum(jnp.arange(n_tiles, dtype=I32), n_valid - 1)
    tile_expert = jnp.minimum(jnp.sum(tile_ends[None, :] <= tile[:, None], axis=1), N_EXPERTS - 1).astype(I32)
    return pos.astype(I32), tile_expert, n_valid.reshape(1).astype(I32)


def _value_range_scale(mod_l, pre_g, w_in_l):
    d = pre_g.shape[0]
    shift, scale = mod_l[:, 0, :], mod_l[:, 1, :]
    h_norm = d ** 0.5 * jnp.max(jnp.abs(pre_g[None, :] * (1.0 + scale)), axis=1) + jnp.linalg.norm(shift, axis=1)
    w_v = w_in_l[:, ATTN_W + KV_W:ATTN_W + 2 * KV_W]
    bound = 1.02 * h_norm * jnp.max(jnp.linalg.norm(w_v, axis=0))
    v_scale = V_RANGE / jnp.maximum(bound, 1e-30)
    lanes = lambda t: jnp.broadcast_to(t[:, None, None], (mod_l.shape[0], 1, LANES))
    return lanes(v_scale), lanes(1.0 / v_scale)


def kernel(x, c, ada_w, ada_b, pre_mix_g, post_mix_g, w_in, q_norm_g, k_norm_g, ret_decay_fwd, ret_decay_bwd, ret_gn_g, ret_gn_b, w_out, pre_ffn_g, post_ffn_g, router_w, router_b, exp_w_up, exp_b_up, exp_w_down, exp_b_down):
    b, s, d = x.shape
    depth = ada_w.shape[0]
    groups = 2 if b % 2 == 0 else 1
    bg = b // groups
    n = bg * s
    n_asg = n * TOP_K
    n_pad = n_asg + N_EXPERTS * MOE_TILE
    n_tiles = n_pad // MOE_TILE

    mod = _modulation(c, ada_w, ada_b)
    ropes = _rope_tables(s, HEAD_DIM) + _rope_tables(s, RET_DIM)
    bu = exp_b_up.reshape(depth, N_EXPERTS, 1, 2 * D_FF)
    bd = exp_b_down.reshape(depth, N_EXPERTS, 1, d)
    xs_in = [x.reshape(b * s, d)] * groups
    x_b0 = [g * bg for g in range(groups)]
    out = None
    for l in range(depth):
        mod_l = mod[l].reshape(b, 6, d)
        qg = jnp.tile(q_norm_g[l], LANES // HEAD_DIM).reshape(1, LANES)
        kg = jnp.tile(k_norm_g[l], LANES // HEAD_DIM).reshape(1, LANES)
        v_scale, fac = _value_range_scale(mod_l, pre_mix_g[l], w_in[l])
        w_in_b = w_in[l].astype(BF16)
        w_out_b = w_out[l].astype(BF16)
        rw_t = router_w[l].T.astype(BF16)
        dec = jnp.stack([ret_decay_fwd[l], ret_decay_bwd[l]]).astype(F32)
        last = l + 1 == depth
        proj = [_inproj(xs_in[g], g * bg, x_b0[g], mod_l, pre_mix_g[l].reshape(1, d), w_in_b, ropes, qg, kg, v_scale,
                        bg, s) for g in range(groups)]
        oa = [_attention(p[0], p[1], p[2], fac, g * bg, bg, s) for g, p in enumerate(proj)]
        orr = [_retention(p[3], p[4], p[5], p[6], dec, ret_gn_g[l].reshape(1, RET_W), ret_gn_b[l].reshape(1, RET_W),
                          bg, s) for p in proj]
        routed = []
        for g in range(groups):
            xn, hp, tidx, tw, rank, counts = _outproj(
                oa[g], orr[g], w_out_b, xs_in[g], g * bg, x_b0[g], mod_l, post_mix_g[l].reshape(1, d),
                pre_ffn_g[l].reshape(1, d), rw_t, router_b[l].reshape(N_EXPERTS, 1), bg, s)
            pos, tile_expert, n_valid = _route(counts, tidx, rank, n_tiles)
            pos3 = pos.reshape(TOP_K, n // SC_CHUNK, SC_CHUNK).transpose(1, 0, 2)
            routed.append((xn, tw, pos, tile_expert, n_valid, _sc_scatter_rows(hp, pos3, n_pad)))
        yg = []
        for xn, tw, pos, tile_expert, n_valid, xs in routed:
            ys = _experts(xs, tile_expert, n_valid, l, exp_w_up, bu, exp_w_down, bd)
            yg.append(_sc_gather_rows(ys, pos.reshape(n_asg)))
        nxt = []
        for g, (xn, tw, *_) in enumerate(routed):
            if last:
                out = _combine(yg[g], tw, xn, g * bg, mod_l, post_ffn_g[l].reshape(1, d), bg, s,
                               out_batches=b, shared_out=out)
            else:
                nxt.append(_combine(yg[g], tw, xn, g * bg, mod_l, post_ffn_g[l].reshape(1, d), bg, s))
        xs_in, x_b0 = nxt, [0] * groups
    return out.reshape(b, s, d)
```

```python
import functools

import numpy as np
import jax
import jax.numpy as jnp
from jax import lax
from jax.experimental import pallas as pl
from jax.experimental.pallas import tpu as pltpu
from jax.experimental.pallas import tpu_sc as plsc

F32 = jnp.float32
BF16 = jnp.bfloat16
U32 = jnp.uint32
I32 = jnp.int32
FP8 = jnp.float8_e4m3fn

D_MODEL = 1024
GRID_W = 64
ROPE_THETA = 10000.0
HEAD_DIM = 64
Q_HEADS = 8
KV_HEADS = 2
RET_HEADS = 4
RET_DIM = 128
RET_CHUNK = 128
ATTN_W = Q_HEADS * HEAD_DIM
KV_W = KV_HEADS * HEAD_DIM
RET_W = RET_HEADS * RET_DIM
IN_W = ATTN_W + 2 * KV_W + 4 * RET_W
N_EXPERTS = 32
TOP_K = 4
D_FF = D_MODEL
SWIGLU_LIMIT = 7.0
SWIGLU_ALPHA = 1.702
NORM_EPS = 1e-6
Q_SCALE = HEAD_DIM ** -0.5 * float(np.log2(np.e))
FP8_RANGE = 224.0
ACT_SCALE = FP8_RANGE / (SWIGLU_LIMIT * (SWIGLU_LIMIT + 1.0))
V_RANGE = FP8_RANGE
P_SHIFT = 8.0
HALF = D_MODEL // 2
LANES = 128

ROW_TILE = 512
Q_TILE = 128
Q_TILES_PER_TRIP = 2
KV_TILE = 1024
MOE_TILE = 512
RET_CHUNKS_PER_TRIP = 8
VMEM_LIMIT = 48 * 1024 * 1024
EXPERTS_VMEM_LIMIT = 56 * 1024 * 1024

SC_CORES = 2
SC_SUBCORES = 16
SC_WORKERS = SC_CORES * SC_SUBCORES
SC_CHUNK = 64


def _params(sem):
    return pltpu.CompilerParams(dimension_semantics=sem, vmem_limit_bytes=VMEM_LIMIT)


def _rms(x, g):
    return x * lax.rsqrt(jnp.mean(x * x, axis=-1, keepdims=True) + NORM_EPS) * g


def _pack_bf16_pairs(y):
    u = lax.bitcast_convert_type(y.astype(BF16).astype(F32), U32)
    return (u[:, :HALF] >> 16) | u[:, HALF:]


def _unpack_lo(w):
    return lax.bitcast_convert_type(w << 16, F32)


def _unpack_hi(w):
    return lax.bitcast_convert_type(w & jnp.uint32(0xFFFF0000), F32)


def _mod_kernel(c_ref, w_ref, b_ref, o_ref):
    c = c_ref[...]
    cond = c * jax.nn.sigmoid(c)
    o_ref[0] = jnp.dot(cond, w_ref[0], preferred_element_type=F32, precision=lax.Precision.HIGHEST) + b_ref[0]


def _modulation(c, ada_w, ada_b):
    depth, d, w6 = ada_w.shape
    b = c.shape[0]
    tn = 1536
    return pl.pallas_call(
        _mod_kernel,
        grid=(depth, w6 // tn),
        in_specs=[
            pl.BlockSpec((b, d), lambda l, j: (0, 0)),
            pl.BlockSpec((1, d, tn), lambda l, j: (l, 0, j)),
            pl.BlockSpec((1, 1, tn), lambda l, j: (l, 0, j)),
        ],
        out_specs=pl.BlockSpec((1, b, tn), lambda l, j: (l, 0, j)),
        out_shape=jax.ShapeDtypeStruct((depth, b, w6), F32),
        compiler_params=_params(("arbitrary", "arbitrary")),
        name="modulation",
    )(c, ada_w, ada_b.reshape(depth, 1, w6))


def _rope_tables(s, head_dim):
    quarter = head_dim // 4
    t = jnp.arange(s, dtype=F32)
    row = jnp.floor(t / GRID_W)
    col = t - row * GRID_W
    inv_freq = ROPE_THETA ** (-jnp.arange(quarter, dtype=F32) / quarter)
    lane = np.arange(LANES) % head_dim
    use_col = lane >= head_dim // 2
    second = (lane % (head_dim // 2)) >= quarter
    freq = inv_freq[lane % quarter]
    pos = jnp.where(use_col[None, :], col[:, None], row[:, None])
    ang = pos * freq[None, :]
    sign = jnp.where(second, 1.0, -1.0).astype(F32)
    return jnp.cos(ang), jnp.sin(ang) * sign[None, :]


def _rope(z, cos, sin_signed, quarter):
    lane = lax.broadcasted_iota(I32, z.shape, 1)
    first = (lane % (2 * quarter)) < quarter
    partner = jnp.where(first, pltpu.roll(z, LANES - quarter, 1), pltpu.roll(z, quarter, 1))
    return z * cos + partner * sin_signed


def _inproj_kernel(x_ref, mod_ref, g_ref, w_ref, ca_ref, sa_ref, cr_ref, sr_ref, qg_ref, kg_ref, vs_ref,
                   qa_ref, kd_ref, vt_ref, qr_ref, kr_ref, vr_ref, gr_ref):
    x = x_ref[...]
    m = mod_ref[0]
    h = _rms(x, g_ref[...]) * (1.0 + m[1:2, :]) + m[0:1, :]
    hb = h.astype(BF16)
    tm = x.shape[0]

    def proj(c0):
        z = jnp.dot(hb, w_ref[:, c0:c0 + 2 * LANES], preferred_element_type=F32)
        return z[:, :LANES], z[:, LANES:]

    ri = lax.broadcasted_iota(I32, (LANES, LANES), 0) // HEAD_DIM
    ci = lax.broadcasted_iota(I32, (LANES, LANES), 1) // HEAD_DIM
    head_ones = (ri == ci).astype(BF16)
    ca, sa = ca_ref[...], sa_ref[...]
    cr, sr = cr_ref[...], sr_ref[...]
    lane = lax.broadcasted_iota(I32, (tm, LANES), 1)
    low = lane < HEAD_DIM

    def head_norm_rope(z, g):
        ss = jnp.dot((z * z).astype(BF16), head_ones, preferred_element_type=F32)
        zn = z * lax.rsqrt(ss * (1.0 / HEAD_DIM) + NORM_EPS) * g
        return _rope(zn, ca, sa, HEAD_DIM // 4)

    def store_pair(ref, j, fn):
        def finish(zs):
            for i, z in enumerate(zs):
                ref[:, (2 * j + i) * LANES:(2 * j + i + 1) * LANES] = fn(z).astype(BF16)
        return finish

    def store_kv(zs):
        k, v = zs
        k = head_norm_rope(k, kg_ref[...])
        k_sw = pltpu.roll(k, HEAD_DIM, 1)
        v = v * vs_ref[0]
        kd_ref[0, 0] = jnp.where(low, k, k_sw).astype(BF16)
        kd_ref[0, 1] = jnp.where(low, k_sw, k).astype(BF16)
        vt_ref[0, 0] = jnp.where(low, v, 1.0).T.astype(FP8)
        vt_ref[0, 1] = jnp.where(low, pltpu.roll(v, HEAD_DIM, 1), 1.0).T.astype(FP8)

    jobs = [(2 * j * LANES, store_pair(qa_ref, j, lambda z: head_norm_rope(z, qg_ref[...]) * Q_SCALE))
            for j in range(ATTN_W // (2 * LANES))]
    jobs.append((ATTN_W, store_kv))
    base = ATTN_W + 2 * KV_W
    for j in range(RET_W // (2 * LANES)):
        off = 2 * j * LANES
        jobs += [
            (base + off, store_pair(qr_ref, j, lambda z: _rope(z, cr, sr, RET_DIM // 4))),
            (base + RET_W + off, store_pair(kr_ref, j, lambda z: _rope(z, cr, sr, RET_DIM // 4) * (RET_DIM ** -0.5))),
            (base + 2 * RET_W + off, store_pair(vr_ref, j, lambda z: z)),
            (base + 3 * RET_W + off, store_pair(gr_ref, j, lambda z: z * jax.nn.sigmoid(z))),
        ]
    zs = proj(jobs[0][0])
    for i, (_, finish) in enumerate(jobs):
        nxt = proj(jobs[i + 1][0]) if i + 1 < len(jobs) else None
        finish(zs)
        zs = nxt


def _inproj(x2, batch0, x_batch0, mod_l, pre_g, w_in_b, ropes, qg, kg, v_scale, b, s):
    n = b * s
    tm = min(ROW_TILE, s)
    nb = s // tm
    ca, sa, cr, sr = ropes
    row = lambda i: (i, 0)
    full = lambda i: (0, 0)
    rope_spec = pl.BlockSpec((tm, LANES), lambda i: (i % nb, 0))
    wide = jax.ShapeDtypeStruct((n, RET_W), BF16)
    return pl.pallas_call(
        _inproj_kernel,
        grid=(n // tm,),
        in_specs=[
            pl.BlockSpec((tm, D_MODEL), lambda i: (x_batch0 * nb + i, 0)),
            pl.BlockSpec((1, 6, D_MODEL), lambda i: (batch0 + i // nb, 0, 0)),
            pl.BlockSpec((1, D_MODEL), full),
            pl.BlockSpec((D_MODEL, IN_W), full),
            rope_spec, rope_spec, rope_spec, rope_spec,
            pl.BlockSpec((1, LANES), full),
            pl.BlockSpec((1, LANES), full),
            pl.BlockSpec((1, 1, LANES), lambda i: (batch0 + i // nb, 0, 0)),
        ],
        out_specs=[
            pl.BlockSpec((tm, ATTN_W), row),
            pl.BlockSpec((1, KV_HEADS, tm, LANES), lambda i: (i // nb, 0, i % nb, 0)),
            pl.BlockSpec((1, KV_HEADS, LANES, tm), lambda i: (i // nb, 0, 0, i % nb)),
            pl.BlockSpec((tm, RET_W), row),
            pl.BlockSpec((tm, RET_W), row),
            pl.BlockSpec((tm, RET_W), row),
            pl.BlockSpec((tm, RET_W), row),
        ],
        out_shape=[
            jax.ShapeDtypeStruct((n, ATTN_W), BF16),
            jax.ShapeDtypeStruct((b, KV_HEADS, s, LANES), BF16),
            jax.ShapeDtypeStruct((b, KV_HEADS, LANES, s), FP8),
            wide, wide, wide, wide,
        ],
        compiler_params=_params(("parallel",)),
        name="inproj",
    )(x2, mod_l, pre_g, w_in_b, ca, sa, cr, sr, qg, kg, v_scale)


def _attn_kernel(q_ref, kd_ref, vt_ref, fac_ref, o_ref, q_s, qn_s, m_ref, acc_ref, s0_ref, s1_ref,
                 *, tq, tk, tiles):
    s = kd_ref.shape[2]
    nk = s // tk
    nq = s // tq
    pairs = (Q_HEADS // KV_HEADS) // 2
    top = lax.broadcasted_iota(I32, (LANES, tq), 0) < HEAD_DIM
    unscale = jnp.concatenate([fac_ref[0]] * (2 * tq // LANES), axis=1)

    chains = tiles * pairs

    def load_q(trip, dst):
        for u in range(tiles):
            rows = pl.ds(pl.multiple_of((trip * tiles + u) * tq, tq), tq)
            for p in range(pairs):
                qt = q_ref[rows, p * LANES:(p + 1) * LANES].astype(F32).T
                dst[u * pairs + p, :, :tq] = jnp.where(top, qt, 0.0).astype(BF16)
                dst[u * pairs + p, :, tq:] = jnp.where(top, 0.0, qt).astype(BF16)

    def keys(j):
        return kd_ref[0, 0, j * tk:(j + 1) * tk, :]

    def step(j, cur_ref, nxt_ref, q_next, j_next):
        vt = vt_ref[0, 0, :, j * tk:(j + 1) * tk]
        kd = keys(j_next)
        nxt_ref[0] = jnp.dot(kd, q_next[0], preferred_element_type=F32)
        for c in range(chains):
            sc = cur_ref[c]
            m_old = m_ref[c]
            m_new = jnp.maximum(m_old, jnp.max(sc, axis=0, keepdims=True))
            pt = jnp.exp2(sc - (m_new - P_SHIFT))
            alpha = jnp.exp2(m_old - m_new)
            if c + 1 < chains:
                nxt_ref[c + 1] = jnp.dot(kd, q_next[c + 1], preferred_element_type=F32)
            acc_ref[c] = alpha * acc_ref[c] + jnp.dot(vt, pt.astype(FP8), preferred_element_type=F32)
            m_ref[c] = m_new

    bufs = (s0_ref, s1_ref)
    trips = nq // tiles
    load_q(0, q_s)
    for c in range(chains):
        s0_ref[c] = jnp.dot(keys(0), q_s[c], preferred_element_type=F32)

    def body(trip, carry):
        load_q(trip, q_s)
        load_q(jnp.minimum(trip + 1, trips - 1), qn_s)
        m_ref[...] = jnp.full(m_ref.shape, -jnp.inf, F32)
        acc_ref[...] = jnp.zeros(acc_ref.shape, F32)
        for j in range(nk):
            last = j + 1 == nk
            step(j, bufs[j % len(bufs)], bufs[(j + 1) % len(bufs)], qn_s if last else q_s, 0 if last else j + 1)
        for c in range(chains):
            u, p = divmod(c, pairs)
            rows = pl.ds(pl.multiple_of((trip * tiles + u) * tq, tq), tq)
            acc = acc_ref[c]
            o_t = acc[:HEAD_DIM] / acc[HEAD_DIM:] * unscale
            both = jnp.concatenate([o_t[:, :tq], o_t[:, tq:]], axis=0)
            o_ref[rows, p * LANES:(p + 1) * LANES] = both.T.astype(BF16)
        return carry

    lax.fori_loop(0, trips, body, 0)


def _attention(qa, kd, vt, fac, batch0, b, s):
    n = qa.shape[0]
    tq = Q_TILE
    tk = min(KV_TILE, s // 2)
    tiles = Q_TILES_PER_TRIP
    assert tq == LANES and s % (tiles * tq) == 0 and s % (2 * tk) == 0 and tk % LANES == 0, (s, tq, tk)
    gw = ATTN_W // KV_HEADS
    chains = tiles * ((Q_HEADS // KV_HEADS) // 2)
    return pl.pallas_call(
        functools.partial(_attn_kernel, tq=tq, tk=tk, tiles=tiles),
        grid=(b, KV_HEADS),
        in_specs=[
            pl.BlockSpec((s, gw), lambda bi, h: (bi, h)),
            pl.BlockSpec((1, 1, s, LANES), lambda bi, h: (bi, h, 0, 0)),
            pl.BlockSpec((1, 1, LANES, s), lambda bi, h: (bi, h, 0, 0)),
            pl.BlockSpec((1, 1, LANES), lambda bi, h: (batch0 + bi, 0, 0)),
        ],
        out_specs=pl.BlockSpec((s, gw), lambda bi, h: (bi, h)),
        out_shape=jax.ShapeDtypeStruct((n, ATTN_W), BF16),
        scratch_shapes=[
            pltpu.VMEM((chains, LANES, 2 * tq), BF16),
            pltpu.VMEM((chains, LANES, 2 * tq), BF16),
            pltpu.VMEM((chains, 1, 2 * tq), F32),
            pltpu.VMEM((chains, LANES, 2 * tq), F32),
            pltpu.VMEM((chains, tk, 2 * tq), F32),
            pltpu.VMEM((chains, tk, 2 * tq), F32),
        ],
        compiler_params=_params(("parallel", "parallel")),
        name="attention",
    )(qa, kd, vt, fac)


def _retention_kernel(dec_ref, q_ref, k_ref, v_ref, g_ref, gng_ref, gnb_ref, o_ref,
                      fbuf, bbuf, sf_ref, sb_ref):
    h = pl.program_id(1)
    c = RET_CHUNK
    s = q_ref.shape[1]
    nc = s // c
    lgf = jnp.full((1, 1), dec_ref[0, h], F32)
    lgb = jnp.full((1, 1), dec_ref[1, h], F32)
    ii = lax.broadcasted_iota(I32, (c, c), 0)
    jj = lax.broadcasted_iota(I32, (c, c), 1)
    diff = (ii - jj).astype(F32)
    d_f = jnp.where(diff >= 0, jnp.exp(lgf * jnp.maximum(diff, 0.0)), 0.0)
    d_b = jnp.where(diff <= 0, jnp.exp(lgb * jnp.maximum(-diff, 0.0)), 0.0)
    idx = lax.broadcasted_iota(I32, (c, 1), 0).astype(F32)
    xi_f = jnp.exp(lgf * (idx + 1.0))
    zeta_f = jnp.exp(lgf * (c - 1.0 - idx))
    xi_b = jnp.exp(lgb * (c - idx))
    zeta_b = jnp.exp(lgb * idx)
    cd_f = jnp.exp(lgf * c)
    cd_b = jnp.exp(lgb * c)
    sf_ref[...] = jnp.zeros(sf_ref.shape, F32)
    sb_ref[...] = jnp.zeros(sb_ref.shape, F32)
    nt = (((1,), (1,)), ((), ()))

    fwd = (d_f, xi_f, zeta_f, cd_f, sf_ref, fbuf)
    bwd = (d_b, xi_b, zeta_b, cd_b, sb_ref, bbuf)

    per = min(RET_CHUNKS_PER_TRIP, nc)

    def body(i, carry):
        jobs = []
        for n in range(per):
            jobs += [(fwd, per * i + n), (bwd, nc - 1 - per * i - n)]
        offs = [pl.multiple_of(ci * c, c) for _, ci in jobs]
        qs = [q_ref[0, pl.ds(off, c), :] for off in offs]
        ks = [k_ref[0, pl.ds(off, c), :] for off in offs]
        vs = [v_ref[0, pl.ds(off, c), :] for off in offs]
        scores = [lax.dot_general(q, k, nt, preferred_element_type=F32) for q, k in zip(qs, ks)]
        incs = []
        for (d, _), k, v in zip(jobs, ks, vs):
            kzt = (k.astype(F32) * d[2]).T.astype(BF16)
            incs.append(jnp.dot(kzt, v, preferred_element_type=F32))
        states = [fwd[4][...], bwd[4][...]]
        cross = []
        for n, (d, _) in enumerate(jobs):
            cross.append(jnp.dot(qs[n], states[n].astype(BF16), preferred_element_type=F32))
            states.append(states[n] * d[3] + incs[n])
        for n, ((d, _), off) in enumerate(zip(jobs, offs)):
            a = (scores[n] * d[0]).astype(BF16)
            d[5][pl.ds(off, c), :] = jnp.dot(a, vs[n], preferred_element_type=F32) + cross[n] * d[1]
        fwd[4][...] = states[2 * per]
        bwd[4][...] = states[2 * per + 1]
        return carry

    lax.fori_loop(0, nc // per, body, 0)

    rows = min(1024, s)

    def fin(t, carry):
        off = pl.multiple_of(t * rows, rows)
        r = fbuf[pl.ds(off, rows), :] + bbuf[pl.ds(off, rows), :]
        mu = jnp.mean(r, axis=-1, keepdims=True)
        d = r - mu
        var = jnp.mean(d * d, axis=-1, keepdims=True)
        y = d * lax.rsqrt(var + NORM_EPS) * gng_ref[...] + gnb_ref[...]
        o_ref[0, pl.ds(off, rows), :] = (g_ref[0, pl.ds(off, rows), :].astype(F32) * y).astype(BF16)
        return carry

    lax.fori_loop(0, s // rows, fin, 0)


def _retention(qr, kr, vr, gr, dec, gn_g, gn_b, b, s):
    shp = (b, s, RET_W)
    assert (s // RET_CHUNK) % min(RET_CHUNKS_PER_TRIP, s // RET_CHUNK) == 0, s
    head = pl.BlockSpec((1, s, RET_DIM), lambda bi, h: (bi, 0, h))
    vec = pl.BlockSpec((1, RET_DIM), lambda bi, h: (0, h))
    out = pl.pallas_call(
        _retention_kernel,
        grid=(b, RET_HEADS),
        in_specs=[pl.BlockSpec(memory_space=pltpu.SMEM), head, head, head, head, vec, vec],
        out_specs=head,
        out_shape=jax.ShapeDtypeStruct(shp, BF16),
        scratch_shapes=[
            pltpu.VMEM((s, RET_DIM), F32),
            pltpu.VMEM((s, RET_DIM), F32),
            pltpu.VMEM((RET_DIM, RET_DIM), F32),
            pltpu.VMEM((RET_DIM, RET_DIM), F32),
        ],
        compiler_params=_params(("parallel", "parallel")),
        name="retention",
    )(dec, qr.reshape(shp), kr.reshape(shp), vr.reshape(shp), gr.reshape(shp), gn_g, gn_b)
    return out.reshape(b * s, RET_W)


def _outproj_kernel(oa_ref, or_ref, w_ref, x_ref, mod_ref, pg_ref, fg_ref, rw_ref, rb_ref,
                    xn_ref, hp_ref, ti_ref, tw_ref, rk_ref, cnt_ref, base_ref):
    @pl.when(pl.program_id(0) == 0)
    def _():
        base_ref[...] = jnp.zeros(base_ref.shape, F32)

    m = mod_ref[0]
    tm = x_ref.shape[0]
    nt = (((1,), (1,)), ((), ()))
    halves = [slice(0, tm // 2), slice(tm // 2, tm)]
    ys = [jnp.dot(oa_ref[r, :], w_ref[:ATTN_W, :], preferred_element_type=F32)
          + jnp.dot(or_ref[r, :], w_ref[ATTN_W:, :], preferred_element_type=F32) for r in halves]
    parts = []
    for r, y in zip(halves, ys):
        xn = x_ref[r, :] + m[2:3, :] * _rms(y, pg_ref[...])
        xn_ref[r, :] = xn
        h = _rms(xn, fg_ref[...]) * (1.0 + m[4:5, :]) + m[3:4, :]
        hb = h.astype(BF16)
        u = lax.bitcast_convert_type(hb.astype(F32), U32)
        hp_ref[r, :] = (u[:, :HALF] >> 16) | u[:, HALF:]
        parts.append(lax.dot_general(rw_ref[...], hb, nt, preferred_element_type=F32))
    logits = jnp.concatenate(parts, axis=1) + rb_ref[...]
    e_iota = lax.broadcasted_iota(I32, (N_EXPERTS, tm), 0).astype(F32)
    vals, idxs = [], []
    cur = logits
    for _ in range(TOP_K):
        mx = jnp.max(cur, axis=0, keepdims=True)
        ik = jnp.min(jnp.where(cur == mx, e_iota, float(N_EXPERTS)), axis=0, keepdims=True)
        vals.append(mx)
        idxs.append(ik)
        cur = jnp.where(e_iota == ik, -jnp.inf, cur)
    v = jnp.concatenate(vals, axis=0)
    w = jnp.exp(v - v[0:1, :])
    tw_ref[...] = w / jnp.sum(w, axis=0, keepdims=True)
    ti_ref[...] = jnp.concatenate(idxs, axis=0).astype(I32)

    onehot = jnp.zeros((N_EXPERTS, tm), F32)
    for ik in idxs:
        onehot = onehot + (e_iota == ik).astype(F32)
    before = (lax.broadcasted_iota(I32, (tm, tm), 0) < lax.broadcasted_iota(I32, (tm, tm), 1)).astype(BF16)
    seen = jnp.dot(onehot.astype(BF16), before, preferred_element_type=F32) + base_ref[...]
    ranks = [jnp.sum(jnp.where(e_iota == ik, seen, 0.0), axis=0, keepdims=True) for ik in idxs]
    rk_ref[...] = jnp.concatenate(ranks, axis=0).astype(I32)
    total = base_ref[...] + jnp.sum(onehot, axis=1, keepdims=True)
    base_ref[...] = total
    cnt_ref[...] = jnp.broadcast_to(total, cnt_ref.shape).astype(I32)


def _outproj(oa, orr, w_out_b, x2, batch0, x_batch0, mod_l, post_g, ffn_g, rw_t, rb, b, s):
    n = b * s
    tm = min(ROW_TILE, s)
    nb = s // tm
    row = lambda i: (i, 0)
    full = lambda i: (0, 0)
    col = lambda i: (0, i)
    return pl.pallas_call(
        _outproj_kernel,
        grid=(n // tm,),
        in_specs=[
            pl.BlockSpec((tm, ATTN_W), row),
            pl.BlockSpec((tm, RET_W), row),
            pl.BlockSpec((D_MODEL, D_MODEL), full),
            pl.BlockSpec((tm, D_MODEL), lambda i: (x_batch0 * nb + i, 0)),
            pl.BlockSpec((1, 6, D_MODEL), lambda i: (batch0 + i // nb, 0, 0)),
            pl.BlockSpec((1, D_MODEL), full),
            pl.BlockSpec((1, D_MODEL), full),
            pl.BlockSpec((N_EXPERTS, D_MODEL), full),
            pl.BlockSpec((N_EXPERTS, 1), full),
        ],
        out_specs=[
            pl.BlockSpec((tm, D_MODEL), row),
            pl.BlockSpec((tm, HALF), row),
            pl.BlockSpec((TOP_K, tm), col),
            pl.BlockSpec((TOP_K, tm), col),
            pl.BlockSpec((TOP_K, tm), col),
            pl.BlockSpec((N_EXPERTS, LANES), full),
        ],
        out_shape=[
            jax.ShapeDtypeStruct((n, D_MODEL), F32),
            jax.ShapeDtypeStruct((n, HALF), U32),
            jax.ShapeDtypeStruct((TOP_K, n), I32),
            jax.ShapeDtypeStruct((TOP_K, n), F32),
            jax.ShapeDtypeStruct((TOP_K, n), I32),
            jax.ShapeDtypeStruct((N_EXPERTS, LANES), I32),
        ],
        scratch_shapes=[pltpu.VMEM((N_EXPERTS, 1), F32)],
        compiler_params=_params(("arbitrary",)),
        name="outproj_router",
    )(oa, orr, w_out_b, x2, mod_l, post_g, ffn_g, rw_t, rb)


def _sc_scatter_rows(rows, pos3, n_out):
    n, w = rows.shape
    nchunk, kk, c = pos3.shape
    per_w = nchunk // SC_WORKERS
    mesh = plsc.VectorSubcoreMesh(core_axis_name="c", subcore_axis_name="s")

    @functools.partial(
        pl.kernel, mesh=mesh,
        out_type=jax.ShapeDtypeStruct((n_out, w), rows.dtype),
        scratch_types=[pltpu.VMEM((kk, c), I32), pltpu.VMEM((c, w), rows.dtype)],
    )
    def k(rows_hbm, pos_hbm, out_hbm, idx_v, rows_v):
        wid = lax.axis_index("s") * SC_CORES + lax.axis_index("c")

        @pl.loop(0, per_w)
        def _(i):
            ch = wid * per_w + i
            pltpu.sync_copy(pos_hbm.at[ch], idx_v)
            pltpu.sync_copy(rows_hbm.at[pl.ds(ch * c, c)], rows_v)
            for j in range(kk):
                pltpu.sync_copy(rows_v, out_hbm.at[idx_v.at[j]])

    return k(rows, pos3)


def _sc_gather_rows(table, idx):
    w = table.shape[1]
    b = idx.shape[0]
    c = SC_CHUNK
    per_w = b // (SC_WORKERS * c)
    mesh = plsc.VectorSubcoreMesh(core_axis_name="c", subcore_axis_name="s")

    @functools.partial(
        pl.kernel, mesh=mesh,
        out_type=jax.ShapeDtypeStruct((b, w), table.dtype),
        scratch_types=[pltpu.VMEM((c,), I32), pltpu.VMEM((c, w), table.dtype), pltpu.SemaphoreType.DMA],
    )
    def k(table_hbm, idx_hbm, out_hbm, idx_v, rows_v, sem):
        wid = lax.axis_index("s") * SC_CORES + lax.axis_index("c")

        @pl.loop(0, per_w)
        def _(i):
            base = (wid * per_w + i) * c
            pltpu.sync_copy(idx_hbm.at[pl.ds(base, c)], idx_v)
            pltpu.async_copy(table_hbm.at[idx_v], rows_v, sem).wait()
            pltpu.sync_copy(rows_v, out_hbm.at[pl.ds(base, c)])

    return k(table, idx)


def _experts_kernel(te_ref, nh_ref, xs_ref, wu_ref, bu_ref, wd_ref, bd_ref, ys_ref, wu_b, wd_b, inv_ref):
    i = pl.program_id(0)
    live_halves = nh_ref[i]
    live = live_halves > 0
    new_expert = jnp.logical_or(i == 0, te_ref[i] != te_ref[jnp.maximum(i - 1, 0)])

    def full_max(a):
        return jnp.max(jnp.max(jnp.abs(a), axis=0, keepdims=True), axis=1, keepdims=True)

    @pl.when(jnp.logical_and(live, new_expert))
    def _():
        wu = wu_ref[0, 0].astype(BF16)
        wd = wd_ref[0, 0].astype(BF16)
        mu = jnp.maximum(full_max(wu).astype(F32), 1e-30)
        md = jnp.maximum(full_max(wd).astype(F32), 1e-30)
        wu_b[...] = (wu * (FP8_RANGE / mu).astype(BF16)).astype(FP8)
        wd_b[...] = (wd * (FP8_RANGE / md).astype(BF16)).astype(FP8)
        inv_ref[0:1, :] = jnp.broadcast_to(mu / FP8_RANGE, (1, LANES))
        inv_ref[1:2, :] = jnp.broadcast_to(md / (FP8_RANGE * ACT_SCALE), (1, LANES))

    t = xs_ref.shape[0]

    def compute(halves):
        inv_up = inv_ref[0:1, 0:1]
        inv_dn = inv_ref[1:2, 0:1]
        ups = []
        for r in halves:
            w = xs_ref[r, :]
            lo = _unpack_lo(w).astype(BF16)
            hi = _unpack_hi(w).astype(BF16)
            amax = jnp.max(jnp.maximum(jnp.abs(lo), jnp.abs(hi)), axis=1, keepdims=True).astype(F32)
            c = (FP8_RANGE / jnp.maximum(amax, 1e-30)).astype(BF16)
            up = (jnp.dot((lo * c).astype(FP8), wu_b[:HALF, :], preferred_element_type=F32)
                  + jnp.dot((hi * c).astype(FP8), wu_b[HALF:, :], preferred_element_type=F32))
            ups.append(up * (inv_up / c.astype(F32)))
        for r, up in zip(halves, ups):
            up = (up + bu_ref[0, 0]).astype(BF16)
            glu = jnp.minimum(up[:, :D_FF], SWIGLU_LIMIT)
            lin = jnp.clip(up[:, D_FF:], -SWIGLU_LIMIT, SWIGLU_LIMIT)
            act = (glu * (0.5 * ACT_SCALE)) * (1.0 + jnp.tanh((0.5 * SWIGLU_ALPHA) * glu)) * (lin + 1.0)
            y = jnp.dot(act.astype(FP8), wd_b[...], preferred_element_type=F32) * inv_dn + bd_ref[0, 0]
            ys_ref[r, :] = _pack_bf16_pairs(y)

    pl.when(live_halves == 2)(lambda: compute([slice(0, t // 2), slice(t // 2, t)]))
    pl.when(live_halves == 1)(lambda: compute([slice(0, t // 2)]))


def _experts(xs, tile_expert, live_halves, layer, wu, bu, wd, bd):
    n_pad = xs.shape[0]
    t = MOE_TILE
    grid_spec = pltpu.PrefetchScalarGridSpec(
        num_scalar_prefetch=2,
        grid=(n_pad // t,),
        in_specs=[
            pl.BlockSpec((t, HALF), lambda i, te, nv: (i, 0)),
            pl.BlockSpec((1, 1, D_MODEL, 2 * D_FF), lambda i, te, nv: (layer, te[i], 0, 0)),
            pl.BlockSpec((1, 1, 1, 2 * D_FF), lambda i, te, nv: (layer, te[i], 0, 0)),
            pl.BlockSpec((1, 1, D_FF, D_MODEL), lambda i, te, nv: (layer, te[i], 0, 0)),
            pl.BlockSpec((1, 1, 1, D_MODEL), lambda i, te, nv: (layer, te[i], 0, 0)),
        ],
        out_specs=pl.BlockSpec((t, HALF), lambda i, te, nv: (i, 0)),
        scratch_shapes=[pltpu.VMEM((D_MODEL, 2 * D_FF), FP8), pltpu.VMEM((D_FF, D_MODEL), FP8),
                        pltpu.VMEM((2, LANES), F32)],
    )
    return pl.pallas_call(
        _experts_kernel,
        grid_spec=grid_spec,
        out_shape=jax.ShapeDtypeStruct((n_pad, HALF), U32),
        compiler_params=pltpu.CompilerParams(dimension_semantics=("arbitrary",), vmem_limit_bytes=EXPERTS_VMEM_LIMIT),
        name="experts",
    )(tile_expert, live_halves, xs, wu, bu, wd, bd)


def _combine_kernel(y0_ref, y1_ref, y2_ref, y3_ref, w_ref, x_ref, mod_ref, g_ref, *rest):
    o_ref = rest[-1]
    wt = w_ref[...]
    w = jnp.concatenate([wt, jnp.zeros_like(wt)], axis=0).T
    lo = None
    hi = None
    for k, y_ref in enumerate((y0_ref, y1_ref, y2_ref, y3_ref)):
        word = y_ref[...]
        wk = w[:, k:k + 1]
        lo_k = wk * _unpack_lo(word)
        hi_k = wk * _unpack_hi(word)
        lo = lo_k if lo is None else lo + lo_k
        hi = hi_k if hi is None else hi + hi_k
    yf = jnp.concatenate([lo, hi], axis=1)
    m = mod_ref[0]
    o_ref[...] = x_ref[...] + m[5:6, :] * _rms(yf, g_ref[...])


def _combine(yg, tw, xn, batch0, mod_l, post_g, b, s, out_batches=None, shared_out=None):
    n = b * s
    tm = min(ROW_TILE, s)
    nb = s // tm
    nblk = n // tm
    row = lambda i: (i, 0)
    full = lambda i: (0, 0)
    y_specs = [pl.BlockSpec((tm, HALF), functools.partial(lambda i, k: (k * nblk + i, 0), k=k)) for k in range(TOP_K)]
    out_rows, out_blk0 = (n, 0) if out_batches is None else (out_batches * s, batch0 * nb)
    extra_specs, extra_args, aliases = [], [], {}
    if shared_out is not None:
        extra_specs, extra_args, aliases = [pl.BlockSpec(memory_space=pl.ANY)], [shared_out], {8: 0}
    return pl.pallas_call(
        _combine_kernel,
        grid=(nblk,),
        in_specs=y_specs + [
            pl.BlockSpec((TOP_K, tm), lambda i: (0, i)),
            pl.BlockSpec((tm, D_MODEL), row),
            pl.BlockSpec((1, 6, D_MODEL), lambda i: (batch0 + i // nb, 0, 0)),
            pl.BlockSpec((1, D_MODEL), full),
        ] + extra_specs,
        out_specs=pl.BlockSpec((tm, D_MODEL), lambda i: (out_blk0 + i, 0)),
        out_shape=jax.ShapeDtypeStruct((out_rows, D_MODEL), F32),
        input_output_aliases=aliases,
        compiler_params=_params(("parallel",)),
        name="combine",
    )(yg, yg, yg, yg, tw, xn, mod_l, post_g, *extra_args)


def _route(counts, tidx, rank, n_tiles):
    t = MOE_TILE
    cnt = counts[:, 0]
    padded = ((cnt + t - 1) // t) * t
    ends = jnp.cumsum(padded)
    starts = ends - padded
    experts = jnp.arange(N_EXPERTS, dtype=I32)[:, None, None]
    start_of = jnp.sum(jnp.where(tidx[None] == experts, starts[:, None, None], 0), axis=0)
    pos = start_of + rank
    tile_ends = ends // t
    n_valid = tile_ends[-1]
    index = jnp.arange(n_tiles, dtype=I32)
    tile = jnp.minimum(index, n_valid - 1)
    tile_expert = jnp.minimum(jnp.sum(tile_ends[None, :] <= tile[:, None], axis=1), N_EXPERTS - 1).astype(I32)
    mine = tile_expert[:, None] == jnp.arange(N_EXPERTS, dtype=I32)[None, :]
    first_tile = jnp.sum(jnp.where(mine, (starts // t)[None, :], 0), axis=1)
    rows = jnp.sum(jnp.where(mine, cnt[None, :], 0), axis=1) - (tile - first_tile) * t
    halves = jnp.where(index < n_valid, jnp.where(rows > t // 2, 2, 1), 0)
    return pos.astype(I32), tile_expert, halves.astype(I32)


def _value_range_scale(mod_l, pre_g, w_in_l):
    d = pre_g.shape[0]
    shift, scale = mod_l[:, 0, :], mod_l[:, 1, :]
    h_norm = d ** 0.5 * jnp.max(jnp.abs(pre_g[None, :] * (1.0 + scale)), axis=1) + jnp.linalg.norm(shift, axis=1)
    w_v = w_in_l[:, ATTN_W + KV_W:ATTN_W + 2 * KV_W]
    bound = 1.02 * h_norm * jnp.max(jnp.linalg.norm(w_v, axis=0))
    v_scale = V_RANGE / jnp.maximum(bound, 1e-30)
    lanes = lambda t: jnp.broadcast_to(t[:, None, None], (mod_l.shape[0], 1, LANES))
    return lanes(v_scale), lanes(1.0 / v_scale)


def kernel(x, c, ada_w, ada_b, pre_mix_g, post_mix_g, w_in, q_norm_g, k_norm_g, ret_decay_fwd, ret_decay_bwd, ret_gn_g, ret_gn_b, w_out, pre_ffn_g, post_ffn_g, router_w, router_b, exp_w_up, exp_b_up, exp_w_down, exp_b_down):
    b, s, d = x.shape
    depth = ada_w.shape[0]
    groups = 2 if b % 2 == 0 else 1
    bg = b // groups
    n = bg * s
    n_asg = n * TOP_K
    n_pad = n_asg + N_EXPERTS * MOE_TILE
    n_tiles = n_pad // MOE_TILE

    mod = _modulation(c, ada_w, ada_b)
    ropes = _rope_tables(s, HEAD_DIM) + _rope_tables(s, RET_DIM)
    bu = exp_b_up.reshape(depth, N_EXPERTS, 1, 2 * D_FF)
    bd = exp_b_down.reshape(depth, N_EXPERTS, 1, d)
    xs_in = [x.reshape(b * s, d)] * groups
    x_b0 = [g * bg for g in range(groups)]
    out = None
    for l in range(depth):
        mod_l = mod[l].reshape(b, 6, d)
        qg = jnp.tile(q_norm_g[l], LANES // HEAD_DIM).reshape(1, LANES)
        kg = jnp.tile(k_norm_g[l], LANES // HEAD_DIM).reshape(1, LANES)
        v_scale, fac = _value_range_scale(mod_l, pre_mix_g[l], w_in[l])
        w_in_b = w_in[l].astype(BF16)
        w_out_b = w_out[l].astype(BF16)
        rw_t = router_w[l].T.astype(BF16)
        dec = jnp.stack([ret_decay_fwd[l], ret_decay_bwd[l]]).astype(F32)
        last = l + 1 == depth
        proj = [_inproj(xs_in[g], g * bg, x_b0[g], mod_l, pre_mix_g[l].reshape(1, d), w_in_b, ropes, qg, kg, v_scale,
                        bg, s) for g in range(groups)]
        oa = [_attention(p[0], p[1], p[2], fac, g * bg, bg, s) for g, p in enumerate(proj)]
        orr = [_retention(p[3], p[4], p[5], p[6], dec, ret_gn_g[l].reshape(1, RET_W), ret_gn_b[l].reshape(1, RET_W),
                          bg, s) for p in proj]
        routed = []
        for g in range(groups):
            xn, hp, tidx, tw, rank, counts = _outproj(
                oa[g], orr[g], w_out_b, xs_in[g], g * bg, x_b0[g], mod_l, post_mix_g[l].reshape(1, d),
                pre_ffn_g[l].reshape(1, d), rw_t, router_b[l].reshape(N_EXPERTS, 1), bg, s)
            pos, tile_expert, live_halves = _route(counts, tidx, rank, n_tiles)
            pos3 = pos.reshape(TOP_K, n // SC_CHUNK, SC_CHUNK).transpose(1, 0, 2)
            routed.append((xn, tw, pos, tile_expert, live_halves, _sc_scatter_rows(hp, pos3, n_pad)))
        yg = []
        for xn, tw, pos, tile_expert, live_halves, xs in routed:
            ys = _experts(xs, tile_expert, live_halves, l, exp_w_up, bu, exp_w_down, bd)
            yg.append(_sc_gather_rows(ys, pos.reshape(n_asg)))
        nxt = []
        for g, (xn, tw, *_) in enumerate(routed):
            if last:
                out = _combine(yg[g], tw, xn, g * bg, mod_l, post_ffn_g[l].reshape(1, d), bg, s,
                               out_batches=b, shared_out=out)
            else:
                nxt.append(_combine(yg[g], tw, xn, g * bg, mod_l, post_ffn_g[l].reshape(1, d), bg, s))
        xs_in, x_b0 = nxt, [0] * groups
    return out.reshape(b, s, d)
```

```python
import functools

import numpy as np
import jax
import jax.numpy as jnp
from jax import lax
from jax.experimental import pallas as pl
from jax.experimental.pallas import tpu as pltpu
from jax.experimental.pallas import tpu_sc as plsc

F32 = jnp.float32
BF16 = jnp.bfloat16
U32 = jnp.uint32
I32 = jnp.int32
FP8 = jnp.float8_e4m3fn

D_MODEL = 1024
GRID_W = 64
ROPE_THETA = 10000.0
HEAD_DIM = 64
Q_HEADS = 8
KV_HEADS = 2
RET_HEADS = 4
RET_DIM = 128
RET_CHUNK = 128
ATTN_W = Q_HEADS * HEAD_DIM
KV_W = KV_HEADS * HEAD_DIM
RET_W = RET_HEADS * RET_DIM
IN_W = ATTN_W + 2 * KV_W + 4 * RET_W
N_EXPERTS = 32
TOP_K = 4
D_FF = D_MODEL
SWIGLU_LIMIT = 7.0
SWIGLU_ALPHA = 1.702
NORM_EPS = 1e-6
Q_SCALE = HEAD_DIM ** -0.5 * float(np.log2(np.e))
FP8_RANGE = 224.0
ACT_SCALE = FP8_RANGE / (SWIGLU_LIMIT * (SWIGLU_LIMIT + 1.0))
V_RANGE = FP8_RANGE
P_SHIFT = 8.0
HALF = D_MODEL // 2
LANES = 128

ROW_TILE = 512
Q_TILE = 128
Q_TILES_PER_TRIP = 2
KV_TILE = 1024
MOE_TILE = 512
RET_CHUNKS_PER_TRIP = 8
VMEM_LIMIT = 48 * 1024 * 1024
EXPERTS_VMEM_LIMIT = 56 * 1024 * 1024

SC_CORES = 2
SC_SUBCORES = 16
SC_WORKERS = SC_CORES * SC_SUBCORES
SC_CHUNK = 64


def _params(sem):
    return pltpu.CompilerParams(dimension_semantics=sem, vmem_limit_bytes=VMEM_LIMIT)


def _rms(x, g):
    return x * lax.rsqrt(jnp.mean(x * x, axis=-1, keepdims=True) + NORM_EPS) * g


def _pack_bf16_pairs(y):
    u = lax.bitcast_convert_type(y.astype(BF16).astype(F32), U32)
    return (u[:, :HALF] >> 16) | u[:, HALF:]


def _unpack_lo(w):
    return lax.bitcast_convert_type(w << 16, F32)


def _unpack_hi(w):
    return lax.bitcast_convert_type(w & jnp.uint32(0xFFFF0000), F32)


def _mod_kernel(c_ref, w_ref, b_ref, o_ref):
    c = c_ref[...]
    cond = c * jax.nn.sigmoid(c)
    o_ref[0] = jnp.dot(cond, w_ref[0], preferred_element_type=F32, precision=lax.Precision.HIGHEST) + b_ref[0]


def _modulation(c, ada_w, ada_b):
    depth, d, w6 = ada_w.shape
    b = c.shape[0]
    tn = 1536
    return pl.pallas_call(
        _mod_kernel,
        grid=(depth, w6 // tn),
        in_specs=[
            pl.BlockSpec((b, d), lambda l, j: (0, 0)),
            pl.BlockSpec((1, d, tn), lambda l, j: (l, 0, j)),
            pl.BlockSpec((1, 1, tn), lambda l, j: (l, 0, j)),
        ],
        out_specs=pl.BlockSpec((1, b, tn), lambda l, j: (l, 0, j)),
        out_shape=jax.ShapeDtypeStruct((depth, b, w6), F32),
        compiler_params=_params(("arbitrary", "arbitrary")),
        name="modulation",
    )(c, ada_w, ada_b.reshape(depth, 1, w6))


def _rope_tables(s, head_dim):
    quarter = head_dim // 4
    t = jnp.arange(s, dtype=F32)
    row = jnp.floor(t / GRID_W)
    col = t - row * GRID_W
    inv_freq = ROPE_THETA ** (-jnp.arange(quarter, dtype=F32) / quarter)
    lane = np.arange(LANES) % head_dim
    use_col = lane >= head_dim // 2
    second = (lane % (head_dim // 2)) >= quarter
    freq = inv_freq[lane % quarter]
    pos = jnp.where(use_col[None, :], col[:, None], row[:, None])
    ang = pos * freq[None, :]
    sign = jnp.where(second, 1.0, -1.0).astype(F32)
    return jnp.cos(ang), jnp.sin(ang) * sign[None, :]


def _rope(z, cos, sin_signed, quarter):
    lane = lax.broadcasted_iota(I32, z.shape, 1)
    first = (lane % (2 * quarter)) < quarter
    partner = jnp.where(first, pltpu.roll(z, LANES - quarter, 1), pltpu.roll(z, quarter, 1))
    return z * cos + partner * sin_signed


def _inproj_kernel(x_ref, mod_ref, g_ref, w_ref, ca_ref, sa_ref, cr_ref, sr_ref, qg_ref, kg_ref, vs_ref,
                   qa_ref, kd_ref, vt_ref, qr_ref, kr_ref, vr_ref, gr_ref):
    x = x_ref[...]
    m = mod_ref[0]
    h = _rms(x, g_ref[...]) * (1.0 + m[1:2, :]) + m[0:1, :]
    hb = h.astype(BF16)
    tm = x.shape[0]

    def proj(c0):
        z = jnp.dot(hb, w_ref[:, c0:c0 + 2 * LANES], preferred_element_type=F32)
        return z[:, :LANES], z[:, LANES:]

    ri = lax.broadcasted_iota(I32, (LANES, LANES), 0) // HEAD_DIM
    ci = lax.broadcasted_iota(I32, (LANES, LANES), 1) // HEAD_DIM
    head_ones = (ri == ci).astype(BF16)
    ca, sa = ca_ref[...], sa_ref[...]
    cr, sr = cr_ref[...], sr_ref[...]
    lane = lax.broadcasted_iota(I32, (tm, LANES), 1)
    low = lane < HEAD_DIM

    def head_norm_rope(z, g):
        ss = jnp.dot((z * z).astype(BF16), head_ones, preferred_element_type=F32)
        zn = z * lax.rsqrt(ss * (1.0 / HEAD_DIM) + NORM_EPS) * g
        return _rope(zn, ca, sa, HEAD_DIM // 4)

    def store_pair(ref, j, fn):
        def finish(zs):
            for i, z in enumerate(zs):
                ref[:, (2 * j + i) * LANES:(2 * j + i + 1) * LANES] = fn(z).astype(BF16)
        return finish

    def store_kv(zs):
        k, v = zs
        k = head_norm_rope(k, kg_ref[...])
        k_sw = pltpu.roll(k, HEAD_DIM, 1)
        v = v * vs_ref[0]
        kd_ref[0, 0] = jnp.where(low, k, k_sw).astype(BF16)
        kd_ref[0, 1] = jnp.where(low, k_sw, k).astype(BF16)
        vt_ref[0, 0] = jnp.where(low, v, 1.0).T.astype(FP8)
        vt_ref[0, 1] = jnp.where(low, pltpu.roll(v, HEAD_DIM, 1), 1.0).T.astype(FP8)

    jobs = [(2 * j * LANES, store_pair(qa_ref, j, lambda z: head_norm_rope(z, qg_ref[...]) * Q_SCALE))
            for j in range(ATTN_W // (2 * LANES))]
    jobs.append((ATTN_W, store_kv))
    base = ATTN_W + 2 * KV_W
    for j in range(RET_W // (2 * LANES)):
        off = 2 * j * LANES
        jobs += [
            (base + off, store_pair(qr_ref, j, lambda z: _rope(z, cr, sr, RET_DIM // 4))),
            (base + RET_W + off, store_pair(kr_ref, j, lambda z: _rope(z, cr, sr, RET_DIM // 4) * (RET_DIM ** -0.5))),
            (base + 2 * RET_W + off, store_pair(vr_ref, j, lambda z: z)),
            (base + 3 * RET_W + off, store_pair(gr_ref, j, lambda z: z * jax.nn.sigmoid(z))),
        ]
    zs = proj(jobs[0][0])
    for i, (_, finish) in enumerate(jobs):
        nxt = proj(jobs[i + 1][0]) if i + 1 < len(jobs) else None
        finish(zs)
        zs = nxt


def _inproj(x2, batch0, x_batch0, mod_l, pre_g, w_in_b, ropes, qg, kg, v_scale, b, s):
    n = b * s
    tm = min(ROW_TILE, s)
    nb = s // tm
    ca, sa, cr, sr = ropes
    row = lambda i: (i, 0)
    full = lambda i: (0, 0)
    rope_spec = pl.BlockSpec((tm, LANES), lambda i: (i % nb, 0))
    wide = jax.ShapeDtypeStruct((n, RET_W), BF16)
    return pl.pallas_call(
        _inproj_kernel,
        grid=(n // tm,),
        in_specs=[
            pl.BlockSpec((tm, D_MODEL), lambda i: (x_batch0 * nb + i, 0)),
            pl.BlockSpec((1, 6, D_MODEL), lambda i: (batch0 + i // nb, 0, 0)),
            pl.BlockSpec((1, D_MODEL), full),
            pl.BlockSpec((D_MODEL, IN_W), full),
            rope_spec, rope_spec, rope_spec, rope_spec,
            pl.BlockSpec((1, LANES), full),
            pl.BlockSpec((1, LANES), full),
            pl.BlockSpec((1, 1, LANES), lambda i: (batch0 + i // nb, 0, 0)),
        ],
        out_specs=[
            pl.BlockSpec((tm, ATTN_W), row),
            pl.BlockSpec((1, KV_HEADS, tm, LANES), lambda i: (i // nb, 0, i % nb, 0)),
            pl.BlockSpec((1, KV_HEADS, LANES, tm), lambda i: (i // nb, 0, 0, i % nb)),
            pl.BlockSpec((tm, RET_W), row),
            pl.BlockSpec((tm, RET_W), row),
            pl.BlockSpec((tm, RET_W), row),
            pl.BlockSpec((tm, RET_W), row),
        ],
        out_shape=[
            jax.ShapeDtypeStruct((n, ATTN_W), BF16),
            jax.ShapeDtypeStruct((b, KV_HEADS, s, LANES), BF16),
            jax.ShapeDtypeStruct((b, KV_HEADS, LANES, s), FP8),
            wide, wide, wide, wide,
        ],
        compiler_params=_params(("parallel",)),
        name="inproj",
    )(x2, mod_l, pre_g, w_in_b, ca, sa, cr, sr, qg, kg, v_scale)


def _attn_kernel(q_ref, kd_ref, vt_ref, fac_ref, o_ref, q_s, qn_s, m_ref, acc_ref, s0_ref, s1_ref,
                 *, tq, tk, tiles):
    s = kd_ref.shape[2]
    nk = s // tk
    nq = s // tq
    pairs = (Q_HEADS // KV_HEADS) // 2
    top = lax.broadcasted_iota(I32, (LANES, tq), 0) < HEAD_DIM
    unscale = jnp.concatenate([fac_ref[0]] * (2 * tq // LANES), axis=1)

    chains = tiles * pairs

    def load_q(trip, dst):
        for u in range(tiles):
            rows = pl.ds(pl.multiple_of((trip * tiles + u) * tq, tq), tq)
            for p in range(pairs):
                qt = q_ref[rows, p * LANES:(p + 1) * LANES].astype(F32).T
                dst[u * pairs + p, :, :tq] = jnp.where(top, qt, 0.0).astype(BF16)
                dst[u * pairs + p, :, tq:] = jnp.where(top, 0.0, qt).astype(BF16)

    def keys(j):
        return kd_ref[0, 0, j * tk:(j + 1) * tk, :]

    def step(j, cur_ref, nxt_ref, q_next, j_next):
        vt = vt_ref[0, 0, :, j * tk:(j + 1) * tk]
        kd = keys(j_next)
        nxt_ref[0] = jnp.dot(kd, q_next[0], preferred_element_type=F32)
        for c in range(chains):
            sc = cur_ref[c]
            m_old = m_ref[c]
            m_new = jnp.maximum(m_old, jnp.max(sc, axis=0, keepdims=True))
            pt = jnp.exp2(sc - (m_new - P_SHIFT))
            alpha = jnp.exp2(m_old - m_new)
            if c + 1 < chains:
                nxt_ref[c + 1] = jnp.dot(kd, q_next[c + 1], preferred_element_type=F32)
            acc_ref[c] = alpha * acc_ref[c] + jnp.dot(vt, pt.astype(FP8), preferred_element_type=F32)
            m_ref[c] = m_new

    bufs = (s0_ref, s1_ref)
    trips = nq // tiles
    load_q(0, q_s)
    for c in range(chains):
        s0_ref[c] = jnp.dot(keys(0), q_s[c], preferred_element_type=F32)

    def body(trip, carry):
        load_q(trip, q_s)
        load_q(jnp.minimum(trip + 1, trips - 1), qn_s)
        m_ref[...] = jnp.full(m_ref.shape, -jnp.inf, F32)
        acc_ref[...] = jnp.zeros(acc_ref.shape, F32)
        for j in range(nk):
            last = j + 1 == nk
            step(j, bufs[j % len(bufs)], bufs[(j + 1) % len(bufs)], qn_s if last else q_s, 0 if last else j + 1)
        for c in range(chains):
            u, p = divmod(c, pairs)
            rows = pl.ds(pl.multiple_of((trip * tiles + u) * tq, tq), tq)
            acc = acc_ref[c]
            o_t = acc[:HEAD_DIM] / acc[HEAD_DIM:] * unscale
            both = jnp.concatenate([o_t[:, :tq], o_t[:, tq:]], axis=0)
            o_ref[rows, p * LANES:(p + 1) * LANES] = both.T.astype(BF16)
        return carry

    lax.fori_loop(0, trips, body, 0)


def _attention(qa, kd, vt, fac, batch0, b, s):
    n = qa.shape[0]
    tq = Q_TILE
    tk = min(KV_TILE, s // 2)
    tiles = Q_TILES_PER_TRIP
    assert tq == LANES and s % (tiles * tq) == 0 and s % (2 * tk) == 0 and tk % LANES == 0, (s, tq, tk)
    gw = ATTN_W // KV_HEADS
    chains = tiles * ((Q_HEADS // KV_HEADS) // 2)
    return pl.pallas_call(
        functools.partial(_attn_kernel, tq=tq, tk=tk, tiles=tiles),
        grid=(b, KV_HEADS),
        in_specs=[
            pl.BlockSpec((s, gw), lambda bi, h: (bi, h)),
            pl.BlockSpec((1, 1, s, LANES), lambda bi, h: (bi, h, 0, 0)),
            pl.BlockSpec((1, 1, LANES, s), lambda bi, h: (bi, h, 0, 0)),
            pl.BlockSpec((1, 1, LANES), lambda bi, h: (batch0 + bi, 0, 0)),
        ],
        out_specs=pl.BlockSpec((s, gw), lambda bi, h: (bi, h)),
        out_shape=jax.ShapeDtypeStruct((n, ATTN_W), BF16),
        scratch_shapes=[
            pltpu.VMEM((chains, LANES, 2 * tq), BF16),
            pltpu.VMEM((chains, LANES, 2 * tq), BF16),
            pltpu.VMEM((chains, 1, 2 * tq), F32),
            pltpu.VMEM((chains, LANES, 2 * tq), F32),
            pltpu.VMEM((chains, tk, 2 * tq), F32),
            pltpu.VMEM((chains, tk, 2 * tq), F32),
        ],
        compiler_params=_params(("parallel", "parallel")),
        name="attention",
    )(qa, kd, vt, fac)


def _retention_kernel(dec_ref, q_ref, k_ref, v_ref, g_ref, gng_ref, gnb_ref, o_ref,
                      fbuf, bbuf, sf_ref, sb_ref):
    h = pl.program_id(1)
    c = RET_CHUNK
    s = q_ref.shape[1]
    nc = s // c
    lgf = jnp.full((1, 1), dec_ref[0, h], F32)
    lgb = jnp.full((1, 1), dec_ref[1, h], F32)
    ii = lax.broadcasted_iota(I32, (c, c), 0)
    jj = lax.broadcasted_iota(I32, (c, c), 1)
    diff = (ii - jj).astype(F32)
    d_f = jnp.where(diff >= 0, jnp.exp(lgf * jnp.maximum(diff, 0.0)), 0.0)
    d_b = jnp.where(diff <= 0, jnp.exp(lgb * jnp.maximum(-diff, 0.0)), 0.0)
    idx = lax.broadcasted_iota(I32, (c, 1), 0).astype(F32)
    xi_f = jnp.exp(lgf * (idx + 1.0))
    zeta_f = jnp.exp(lgf * (c - 1.0 - idx))
    xi_b = jnp.exp(lgb * (c - idx))
    zeta_b = jnp.exp(lgb * idx)
    cd_f = jnp.exp(lgf * c)
    cd_b = jnp.exp(lgb * c)
    sf_ref[...] = jnp.zeros(sf_ref.shape, F32)
    sb_ref[...] = jnp.zeros(sb_ref.shape, F32)
    nt = (((1,), (1,)), ((), ()))

    fwd = (d_f, xi_f, zeta_f, cd_f, sf_ref, fbuf)
    bwd = (d_b, xi_b, zeta_b, cd_b, sb_ref, bbuf)

    per = min(RET_CHUNKS_PER_TRIP, nc)

    span = per * c

    def finish_rows(off):
        r = fbuf[pl.ds(off, span), :] + bbuf[pl.ds(off, span), :]
        mu = jnp.mean(r, axis=-1, keepdims=True)
        dlt = r - mu
        var = jnp.mean(dlt * dlt, axis=-1, keepdims=True)
        y = dlt * lax.rsqrt(var + NORM_EPS) * gng_ref[...] + gnb_ref[...]
        o_ref[0, pl.ds(off, span), :] = (g_ref[0, pl.ds(off, span), :].astype(F32) * y).astype(BF16)

    trips = nc // per

    def finish_trip(i, spans):
        if spans >= 1:
            finish_rows(pl.multiple_of(i * span, span))
        if spans == 2:
            finish_rows(pl.multiple_of((trips - 1 - i) * span, span))

    def body(i, carry, finish):
        finish_trip(i - 1, finish)
        jobs = []
        for n in range(per):
            jobs += [(fwd, per * i + n), (bwd, nc - 1 - per * i - n)]
        offs = [pl.multiple_of(ci * c, c) for _, ci in jobs]
        qs = [q_ref[0, pl.ds(off, c), :] for off in offs]
        ks = [k_ref[0, pl.ds(off, c), :] for off in offs]
        vs = [v_ref[0, pl.ds(off, c), :] for off in offs]
        scores = [lax.dot_general(q, k, nt, preferred_element_type=F32) for q, k in zip(qs, ks)]
        incs = []
        for (d, _), k, v in zip(jobs, ks, vs):
            kzt = (k.astype(F32) * d[2]).T.astype(BF16)
            incs.append(jnp.dot(kzt, v, preferred_element_type=F32))
        states = [fwd[4][...], bwd[4][...]]
        cross = []
        for n, (d, _) in enumerate(jobs):
            cross.append(jnp.dot(qs[n], states[n].astype(BF16), preferred_element_type=F32))
            states.append(states[n] * d[3] + incs[n])
        for n, ((d, _), off) in enumerate(zip(jobs, offs)):
            a = (scores[n] * d[0]).astype(BF16)
            d[5][pl.ds(off, c), :] = jnp.dot(a, vs[n], preferred_element_type=F32) + cross[n] * d[1]
        fwd[4][...] = states[2 * per]
        bwd[4][...] = states[2 * per + 1]
        return carry

    first_done = trips // 2
    lax.fori_loop(0, min(first_done + 1, trips), functools.partial(body, finish=0), 0)
    done_by_first = 1 if trips % 2 else 2
    if first_done + 1 < trips:
        body(jnp.int32(first_done + 1), 0, finish=done_by_first)
        lax.fori_loop(first_done + 2, trips, functools.partial(body, finish=2), 0)
        finish_trip(jnp.int32(trips - 1), 2)
    else:
        finish_trip(jnp.int32(trips - 1), done_by_first)


def _retention(qr, kr, vr, gr, dec, gn_g, gn_b, b, s):
    shp = (b, s, RET_W)
    assert (s // RET_CHUNK) % min(RET_CHUNKS_PER_TRIP, s // RET_CHUNK) == 0, s
    head = pl.BlockSpec((1, s, RET_DIM), lambda bi, h: (bi, 0, h))
    vec = pl.BlockSpec((1, RET_DIM), lambda bi, h: (0, h))
    out = pl.pallas_call(
        _retention_kernel,
        grid=(b, RET_HEADS),
        in_specs=[pl.BlockSpec(memory_space=pltpu.SMEM), head, head, head, head, vec, vec],
        out_specs=head,
        out_shape=jax.ShapeDtypeStruct(shp, BF16),
        scratch_shapes=[
            pltpu.VMEM((s, RET_DIM), F32),
            pltpu.VMEM((s, RET_DIM), F32),
            pltpu.VMEM((RET_DIM, RET_DIM), F32),
            pltpu.VMEM((RET_DIM, RET_DIM), F32),
        ],
        compiler_params=_params(("parallel", "parallel")),
        name="retention",
    )(dec, qr.reshape(shp), kr.reshape(shp), vr.reshape(shp), gr.reshape(shp), gn_g, gn_b)
    return out.reshape(b * s, RET_W)


def _outproj_kernel(oa_ref, or_ref, w_ref, x_ref, mod_ref, pg_ref, fg_ref, rw_ref, rb_ref,
                    xn_ref, hp_ref, ti_ref, tw_ref, rk_ref, cnt_ref, base_ref):
    @pl.when(pl.program_id(0) == 0)
    def _():
        base_ref[...] = jnp.zeros(base_ref.shape, F32)

    m = mod_ref[0]
    tm = x_ref.shape[0]
    nt = (((1,), (1,)), ((), ()))
    halves = [slice(0, tm // 2), slice(tm // 2, tm)]
    ys = [jnp.dot(oa_ref[r, :], w_ref[:ATTN_W, :], preferred_element_type=F32)
          + jnp.dot(or_ref[r, :], w_ref[ATTN_W:, :], preferred_element_type=F32) for r in halves]
    parts = []
    for r, y in zip(halves, ys):
        xn = x_ref[r, :] + m[2:3, :] * _rms(y, pg_ref[...])
        xn_ref[r, :] = xn
        h = _rms(xn, fg_ref[...]) * (1.0 + m[4:5, :]) + m[3:4, :]
        hb = h.astype(BF16)
        u = lax.bitcast_convert_type(hb.astype(F32), U32)
        hp_ref[r, :] = (u[:, :HALF] >> 16) | u[:, HALF:]
        parts.append(lax.dot_general(rw_ref[...], hb, nt, preferred_element_type=F32))
    logits = jnp.concatenate(parts, axis=1) + rb_ref[...]
    e_iota = lax.broadcasted_iota(I32, (N_EXPERTS, tm), 0).astype(F32)
    vals, idxs = [], []
    cur = logits
    for _ in range(TOP_K):
        mx = jnp.max(cur, axis=0, keepdims=True)
        ik = jnp.min(jnp.where(cur == mx, e_iota, float(N_EXPERTS)), axis=0, keepdims=True)
        vals.append(mx)
        idxs.append(ik)
        cur = jnp.where(e_iota == ik, -jnp.inf, cur)
    v = jnp.concatenate(vals, axis=0)
    w = jnp.exp(v - v[0:1, :])
    tw_ref[...] = w / jnp.sum(w, axis=0, keepdims=True)
    ti_ref[...] = jnp.concatenate(idxs, axis=0).astype(I32)

    onehot = jnp.zeros((N_EXPERTS, tm), F32)
    for ik in idxs:
        onehot = onehot + (e_iota == ik).astype(F32)
    before = (lax.broadcasted_iota(I32, (tm, tm), 0) < lax.broadcasted_iota(I32, (tm, tm), 1)).astype(BF16)
    seen = jnp.dot(onehot.astype(BF16), before, preferred_element_type=F32) + base_ref[...]
    ranks = [jnp.sum(jnp.where(e_iota == ik, seen, 0.0), axis=0, keepdims=True) for ik in idxs]
    rk_ref[...] = jnp.concatenate(ranks, axis=0).astype(I32)
    total = base_ref[...] + jnp.sum(onehot, axis=1, keepdims=True)
    base_ref[...] = total
    cnt_ref[...] = jnp.broadcast_to(total, cnt_ref.shape).astype(I32)


def _outproj(oa, orr, w_out_b, x2, batch0, x_batch0, mod_l, post_g, ffn_g, rw_t, rb, b, s):
    n = b * s
    tm = min(ROW_TILE, s)
    nb = s // tm
    row = lambda i: (i, 0)
    full = lambda i: (0, 0)
    col = lambda i: (0, i)
    return pl.pallas_call(
        _outproj_kernel,
        grid=(n // tm,),
        in_specs=[
            pl.BlockSpec((tm, ATTN_W), row),
            pl.BlockSpec((tm, RET_W), row),
            pl.BlockSpec((D_MODEL, D_MODEL), full),
            pl.BlockSpec((tm, D_MODEL), lambda i: (x_batch0 * nb + i, 0)),
            pl.BlockSpec((1, 6, D_MODEL), lambda i: (batch0 + i // nb, 0, 0)),
            pl.BlockSpec((1, D_MODEL), full),
            pl.BlockSpec((1, D_MODEL), full),
            pl.BlockSpec((N_EXPERTS, D_MODEL), full),
            pl.BlockSpec((N_EXPERTS, 1), full),
        ],
        out_specs=[
            pl.BlockSpec((tm, D_MODEL), row),
            pl.BlockSpec((tm, HALF), row),
            pl.BlockSpec((TOP_K, tm), col),
            pl.BlockSpec((TOP_K, tm), col),
            pl.BlockSpec((TOP_K, tm), col),
            pl.BlockSpec((N_EXPERTS, LANES), full),
        ],
        out_shape=[
            jax.ShapeDtypeStruct((n, D_MODEL), F32),
            jax.ShapeDtypeStruct((n, HALF), U32),
            jax.ShapeDtypeStruct((TOP_K, n), I32),
            jax.ShapeDtypeStruct((TOP_K, n), F32),
            jax.ShapeDtypeStruct((TOP_K, n), I32),
            jax.ShapeDtypeStruct((N_EXPERTS, LANES), I32),
        ],
        scratch_shapes=[pltpu.VMEM((N_EXPERTS, 1), F32)],
        compiler_params=_params(("arbitrary",)),
        name="outproj_router",
    )(oa, orr, w_out_b, x2, mod_l, post_g, ffn_g, rw_t, rb)


def _sc_scatter_rows(rows, pos3, n_out):
    n, w = rows.shape
    nchunk, kk, c = pos3.shape
    per_w = nchunk // SC_WORKERS
    mesh = plsc.VectorSubcoreMesh(core_axis_name="c", subcore_axis_name="s")

    @functools.partial(
        pl.kernel, mesh=mesh,
        out_type=jax.ShapeDtypeStruct((n_out, w), rows.dtype),
        scratch_types=[pltpu.VMEM((kk, c), I32), pltpu.VMEM((c, w), rows.dtype)],
    )
    def k(rows_hbm, pos_hbm, out_hbm, idx_v, rows_v):
        wid = lax.axis_index("s") * SC_CORES + lax.axis_index("c")

        @pl.loop(0, per_w)
        def _(i):
            ch = wid * per_w + i
            pltpu.sync_copy(pos_hbm.at[ch], idx_v)
            pltpu.sync_copy(rows_hbm.at[pl.ds(ch * c, c)], rows_v)
            for j in range(kk):
                pltpu.sync_copy(rows_v, out_hbm.at[idx_v.at[j]])

    return k(rows, pos3)


def _sc_gather_rows(table, idx):
    w = table.shape[1]
    b = idx.shape[0]
    c = SC_CHUNK
    per_w = b // (SC_WORKERS * c)
    mesh = plsc.VectorSubcoreMesh(core_axis_name="c", subcore_axis_name="s")

    @functools.partial(
        pl.kernel, mesh=mesh,
        out_type=jax.ShapeDtypeStruct((b, w), table.dtype),
        scratch_types=[pltpu.VMEM((c,), I32), pltpu.VMEM((c, w), table.dtype), pltpu.SemaphoreType.DMA],
    )
    def k(table_hbm, idx_hbm, out_hbm, idx_v, rows_v, sem):
        wid = lax.axis_index("s") * SC_CORES + lax.axis_index("c")

        @pl.loop(0, per_w)
        def _(i):
            base = (wid * per_w + i) * c
            pltpu.sync_copy(idx_hbm.at[pl.ds(base, c)], idx_v)
            pltpu.async_copy(table_hbm.at[idx_v], rows_v, sem).wait()
            pltpu.sync_copy(rows_v, out_hbm.at[pl.ds(base, c)])

    return k(table, idx)


def _experts_kernel(te_ref, nv_ref, xs_ref, wu_ref, bu_ref, wd_ref, bd_ref, ys_ref, wu_b, wd_b, inv_ref):
    i = pl.program_id(0)
    live = i < nv_ref[0]
    new_expert = jnp.logical_or(i == 0, te_ref[i] != te_ref[jnp.maximum(i - 1, 0)])

    def full_max(a):
        return jnp.max(jnp.max(jnp.abs(a), axis=0, keepdims=True), axis=1, keepdims=True)

    @pl.when(jnp.logical_and(live, new_expert))
    def _():
        wu = wu_ref[0, 0].astype(BF16)
        wd = wd_ref[0, 0].astype(BF16)
        mu = jnp.maximum(full_max(wu).astype(F32), 1e-30)
        md = jnp.maximum(full_max(wd).astype(F32), 1e-30)
        wu_b[...] = (wu * (FP8_RANGE / mu).astype(BF16)).astype(FP8)
        wd_b[...] = (wd * (FP8_RANGE / md).astype(BF16)).astype(FP8)
        inv_ref[0:1, :] = jnp.broadcast_to(mu / FP8_RANGE, (1, LANES))
        inv_ref[1:2, :] = jnp.broadcast_to(md / (FP8_RANGE * ACT_SCALE), (1, LANES))

    @pl.when(live)
    def _():
        t = xs_ref.shape[0]
        halves = [slice(0, t // 2), slice(t // 2, t)]
        inv_up = inv_ref[0:1, 0:1]
        inv_dn = inv_ref[1:2, 0:1]
        ups = []
        for r in halves:
            w = xs_ref[r, :]
            lo = _unpack_lo(w).astype(BF16)
            hi = _unpack_hi(w).astype(BF16)
            amax = jnp.max(jnp.maximum(jnp.abs(lo), jnp.abs(hi)), axis=1, keepdims=True).astype(F32)
            c = (FP8_RANGE / jnp.maximum(amax, 1e-30)).astype(BF16)
            up = (jnp.dot((lo * c).astype(FP8), wu_b[:HALF, :], preferred_element_type=F32)
                  + jnp.dot((hi * c).astype(FP8), wu_b[HALF:, :], preferred_element_type=F32))
            ups.append(up * (inv_up / c.astype(F32)))
        for r, up in zip(halves, ups):
            up = (up + bu_ref[0, 0]).astype(BF16)
            glu = jnp.minimum(up[:, :D_FF], SWIGLU_LIMIT)
            lin = jnp.clip(up[:, D_FF:], -SWIGLU_LIMIT, SWIGLU_LIMIT)
            act = (glu * (0.5 * ACT_SCALE)) * (1.0 + jnp.tanh((0.5 * SWIGLU_ALPHA) * glu)) * (lin + 1.0)
            y = jnp.dot(act.astype(FP8), wd_b[...], preferred_element_type=F32) * inv_dn + bd_ref[0, 0]
            ys_ref[r, :] = _pack_bf16_pairs(y)


def _experts(xs, tile_expert, n_valid, layer, wu, bu, wd, bd):
    n_pad = xs.shape[0]
    t = MOE_TILE
    grid_spec = pltpu.PrefetchScalarGridSpec(
        num_scalar_prefetch=2,
        grid=(n_pad // t,),
        in_specs=[
            pl.BlockSpec((t, HALF), lambda i, te, nv: (i, 0)),
            pl.BlockSpec((1, 1, D_MODEL, 2 * D_FF), lambda i, te, nv: (layer, te[i], 0, 0)),
            pl.BlockSpec((1, 1, 1, 2 * D_FF), lambda i, te, nv: (layer, te[i], 0, 0)),
            pl.BlockSpec((1, 1, D_FF, D_MODEL), lambda i, te, nv: (layer, te[i], 0, 0)),
            pl.BlockSpec((1, 1, 1, D_MODEL), lambda i, te, nv: (layer, te[i], 0, 0)),
        ],
        out_specs=pl.BlockSpec((t, HALF), lambda i, te, nv: (i, 0)),
        scratch_shapes=[pltpu.VMEM((D_MODEL, 2 * D_FF), FP8), pltpu.VMEM((D_FF, D_MODEL), FP8),
                        pltpu.VMEM((2, LANES), F32)],
    )
    return pl.pallas_call(
        _experts_kernel,
        grid_spec=grid_spec,
        out_shape=jax.ShapeDtypeStruct((n_pad, HALF), U32),
        compiler_params=pltpu.CompilerParams(dimension_semantics=("arbitrary",), vmem_limit_bytes=EXPERTS_VMEM_LIMIT),
        name="experts",
    )(tile_expert, n_valid, xs, wu, bu, wd, bd)


def _combine_kernel(y0_ref, y1_ref, y2_ref, y3_ref, w_ref, x_ref, mod_ref, g_ref, *rest):
    o_ref = rest[-1]
    wt = w_ref[...]
    w = jnp.concatenate([wt, jnp.zeros_like(wt)], axis=0).T
    lo = None
    hi = None
    for k, y_ref in enumerate((y0_ref, y1_ref, y2_ref, y3_ref)):
        word = y_ref[...]
        wk = w[:, k:k + 1]
        lo_k = wk * _unpack_lo(word)
        hi_k = wk * _unpack_hi(word)
        lo = lo_k if lo is None else lo + lo_k
        hi = hi_k if hi is None else hi + hi_k
    yf = jnp.concatenate([lo, hi], axis=1)
    m = mod_ref[0]
    o_ref[...] = x_ref[...] + m[5:6, :] * _rms(yf, g_ref[...])


def _combine(yg, tw, xn, batch0, mod_l, post_g, b, s, out_batches=None, shared_out=None):
    n = b * s
    tm = min(ROW_TILE, s)
    nb = s // tm
    nblk = n // tm
    row = lambda i: (i, 0)
    full = lambda i: (0, 0)
    y_specs = [pl.BlockSpec((tm, HALF), functools.partial(lambda i, k: (k * nblk + i, 0), k=k)) for k in range(TOP_K)]
    out_rows, out_blk0 = (n, 0) if out_batches is None else (out_batches * s, batch0 * nb)
    extra_specs, extra_args, aliases = [], [], {}
    if shared_out is not None:
        extra_specs, extra_args, aliases = [pl.BlockSpec(memory_space=pl.ANY)], [shared_out], {8: 0}
    return pl.pallas_call(
        _combine_kernel,
        grid=(nblk,),
        in_specs=y_specs + [
            pl.BlockSpec((TOP_K, tm), lambda i: (0, i)),
            pl.BlockSpec((tm, D_MODEL), row),
            pl.BlockSpec((1, 6, D_MODEL), lambda i: (batch0 + i // nb, 0, 0)),
            pl.BlockSpec((1, D_MODEL), full),
        ] + extra_specs,
        out_specs=pl.BlockSpec((tm, D_MODEL), lambda i: (out_blk0 + i, 0)),
        out_shape=jax.ShapeDtypeStruct((out_rows, D_MODEL), F32),
        input_output_aliases=aliases,
        compiler_params=_params(("parallel",)),
        name="combine",
    )(yg, yg, yg, yg, tw, xn, mod_l, post_g, *extra_args)


def _route(counts, tidx, rank, n_tiles):
    t = MOE_TILE
    cnt = counts[:, 0]
    padded = ((cnt + t - 1) // t) * t
    ends = jnp.cumsum(padded)
    starts = ends - padded
    experts = jnp.arange(N_EXPERTS, dtype=I32)[:, None, None]
    start_of = jnp.sum(jnp.where(tidx[None] == experts, starts[:, None, None], 0), axis=0)
    pos = start_of + rank
    tile_ends = ends // t
    n_valid = tile_ends[-1]
    tile = jnp.minimum(jnp.arange(n_tiles, dtype=I32), n_valid - 1)
    tile_expert = jnp.minimum(jnp.sum(tile_ends[None, :] <= tile[:, None], axis=1), N_EXPERTS - 1).astype(I32)
    return pos.astype(I32), tile_expert, n_valid.reshape(1).astype(I32)


def _value_range_scale(mod_l, pre_g, w_in_l):
    d = pre_g.shape[0]
    shift, scale = mod_l[:, 0, :], mod_l[:, 1, :]
    h_norm = d ** 0.5 * jnp.max(jnp.abs(pre_g[None, :] * (1.0 + scale)), axis=1) + jnp.linalg.norm(shift, axis=1)
    w_v = w_in_l[:, ATTN_W + KV_W:ATTN_W + 2 * KV_W]
    bound = 1.02 * h_norm * jnp.max(jnp.linalg.norm(w_v, axis=0))
    v_scale = V_RANGE / jnp.maximum(bound, 1e-30)
    lanes = lambda t: jnp.broadcast_to(t[:, None, None], (mod_l.shape[0], 1, LANES))
    return lanes(v_scale), lanes(1.0 / v_scale)


def kernel(x, c, ada_w, ada_b, pre_mix_g, post_mix_g, w_in, q_norm_g, k_norm_g, ret_decay_fwd, ret_decay_bwd, ret_gn_g, ret_gn_b, w_out, pre_ffn_g, post_ffn_g, router_w, router_b, exp_w_up, exp_b_up, exp_w_down, exp_b_down):
    b, s, d = x.shape
    depth = ada_w.shape[0]
    groups = 2 if b % 2 == 0 else 1
    bg = b // groups
    n = bg * s
    n_asg = n * TOP_K
    n_pad = n_asg + N_EXPERTS * MOE_TILE
    n_tiles = n_pad // MOE_TILE

    mod = _modulation(c, ada_w, ada_b)
    ropes = _rope_tables(s, HEAD_DIM) + _rope_tables(s, RET_DIM)
    bu = exp_b_up.reshape(depth, N_EXPERTS, 1, 2 * D_FF)
    bd = exp_b_down.reshape(depth, N_EXPERTS, 1, d)
    xs_in = [x.reshape(b * s, d)] * groups
    x_b0 = [g * bg for g in range(groups)]
    out = None
    for l in range(depth):
        mod_l = mod[l].reshape(b, 6, d)
        qg = jnp.tile(q_norm_g[l], LANES // HEAD_DIM).reshape(1, LANES)
        kg = jnp.tile(k_norm_g[l], LANES // HEAD_DIM).reshape(1, LANES)
        v_scale, fac = _value_range_scale(mod_l, pre_mix_g[l], w_in[l])
        w_in_b = w_in[l].astype(BF16)
        w_out_b = w_out[l].astype(BF16)
        rw_t = router_w[l].T.astype(BF16)
        dec = jnp.stack([ret_decay_fwd[l], ret_decay_bwd[l]]).astype(F32)
        last = l + 1 == depth
        proj = [_inproj(xs_in[g], g * bg, x_b0[g], mod_l, pre_mix_g[l].reshape(1, d), w_in_b, ropes, qg, kg, v_scale,
                        bg, s) for g in range(groups)]
        oa = [_attention(p[0], p[1], p[2], fac, g * bg, bg, s) for g, p in enumerate(proj)]
        orr = [_retention(p[3], p[4], p[5], p[6], dec, ret_gn_g[l].reshape(1, RET_W), ret_gn_b[l].reshape(1, RET_W),
                          bg, s) for p in proj]
        routed = []
        for g in range(groups):
            xn, hp, tidx, tw, rank, counts = _outproj(
                oa[g], orr[g], w_out_b, xs_in[g], g * bg, x_b0[g], mod_l, post_mix_g[l].reshape(1, d),
                pre_ffn_g[l].reshape(1, d), rw_t, router_b[l].reshape(N_EXPERTS, 1), bg, s)
            pos, tile_expert, n_valid = _route(counts, tidx, rank, n_tiles)
            pos3 = pos.reshape(TOP_K, n // SC_CHUNK, SC_CHUNK).transpose(1, 0, 2)
            routed.append((xn, tw, pos, tile_expert, n_valid, _sc_scatter_rows(hp, pos3, n_pad)))
        yg = []
        for xn, tw, pos, tile_expert, n_valid, xs in routed:
            ys = _experts(xs, tile_expert, n_valid, l, exp_w_up, bu, exp_w_down, bd)
            yg.append(_sc_gather_rows(ys, pos.reshape(n_asg)))
        nxt = []
        for g, (xn, tw, *_) in enumerate(routed):
            if last:
                out = _combine(yg[g], tw, xn, g * bg, mod_l, post_ffn_g[l].reshape(1, d), bg, s,
                               out_batches=b, shared_out=out)
            else:
                nxt.append(_combine(yg[g], tw, xn, g * bg, mod_l, post_ffn_g[l].reshape(1, d), bg, s))
        xs_in, x_b0 = nxt, [0] * groups
    return out.reshape(b, s, d)
```

```python
import functools

import numpy as np
import jax
import jax.numpy as jnp
from jax import lax
from jax.experimental import pallas as pl
from jax.experimental.pallas import tpu as pltpu
from jax.experimental.pallas import tpu_sc as plsc

F32 = jnp.float32
BF16 = jnp.bfloat16
U32 = jnp.uint32
I32 = jnp.int32
FP8 = jnp.float8_e4m3fn

D_MODEL = 1024
GRID_W = 64
ROPE_THETA = 10000.0
HEAD_DIM = 64
Q_HEADS = 8
KV_HEADS = 2
RET_HEADS = 4
RET_DIM = 128
RET_CHUNK = 128
ATTN_W = Q_HEADS * HEAD_DIM
KV_W = KV_HEADS * HEAD_DIM
RET_W = RET_HEADS * RET_DIM
IN_W = ATTN_W + 2 * KV_W + 4 * RET_W
N_EXPERTS = 32
TOP_K = 4
D_FF = D_MODEL
SWIGLU_LIMIT = 7.0
SWIGLU_ALPHA = 1.702
NORM_EPS = 1e-6
Q_SCALE = HEAD_DIM ** -0.5 * float(np.log2(np.e))
FP8_RANGE = 224.0
ACT_SCALE = FP8_RANGE / (SWIGLU_LIMIT * (SWIGLU_LIMIT + 1.0))
V_RANGE = FP8_RANGE
P_SHIFT = 8.0
HALF = D_MODEL // 2
LANES = 128

ROW_TILE = 512
Q_TILE = 128
Q_TILES_PER_TRIP = 2
KV_TILE = 1024
MOE_TILE = 512
RET_CHUNKS_PER_TRIP = 8
VMEM_LIMIT = 48 * 1024 * 1024
EXPERTS_VMEM_LIMIT = 56 * 1024 * 1024

SC_CORES = 2
SC_SUBCORES = 16
SC_WORKERS = SC_CORES * SC_SUBCORES
SC_CHUNK = 64


def _params(sem):
    return pltpu.CompilerParams(dimension_semantics=sem, vmem_limit_bytes=VMEM_LIMIT)


def _rms(x, g):
    return x * lax.rsqrt(jnp.mean(x * x, axis=-1, keepdims=True) + NORM_EPS) * g


def _pack_bf16_pairs(y):
    u = lax.bitcast_convert_type(y.astype(BF16).astype(F32), U32)
    return (u[:, :HALF] >> 16) | u[:, HALF:]


def _unpack_lo(w):
    return lax.bitcast_convert_type(w << 16, F32)


def _unpack_hi(w):
    return lax.bitcast_convert_type(w & jnp.uint32(0xFFFF0000), F32)


def _mod_kernel(c_ref, w_ref, b_ref, o_ref):
    c = c_ref[...]
    cond = c * jax.nn.sigmoid(c)
    o_ref[0] = jnp.dot(cond, w_ref[0], preferred_element_type=F32, precision=lax.Precision.HIGHEST) + b_ref[0]


def _modulation(c, ada_w, ada_b):
    depth, d, w6 = ada_w.shape
    b = c.shape[0]
    tn = 1536
    return pl.pallas_call(
        _mod_kernel,
        grid=(depth, w6 // tn),
        in_specs=[
            pl.BlockSpec((b, d), lambda l, j: (0, 0)),
            pl.BlockSpec((1, d, tn), lambda l, j: (l, 0, j)),
            pl.BlockSpec((1, 1, tn), lambda l, j: (l, 0, j)),
        ],
        out_specs=pl.BlockSpec((1, b, tn), lambda l, j: (l, 0, j)),
        out_shape=jax.ShapeDtypeStruct((depth, b, w6), F32),
        compiler_params=_params(("arbitrary", "arbitrary")),
        name="modulation",
    )(c, ada_w, ada_b.reshape(depth, 1, w6))


def _rope_tables(s, head_dim):
    quarter = head_dim // 4
    t = jnp.arange(s, dtype=F32)
    row = jnp.floor(t / GRID_W)
    col = t - row * GRID_W
    inv_freq = ROPE_THETA ** (-jnp.arange(quarter, dtype=F32) / quarter)
    lane = np.arange(LANES) % head_dim
    use_col = lane >= head_dim // 2
    second = (lane % (head_dim // 2)) >= quarter
    freq = inv_freq[lane % quarter]
    pos = jnp.where(use_col[None, :], col[:, None], row[:, None])
    ang = pos * freq[None, :]
    sign = jnp.where(second, 1.0, -1.0).astype(F32)
    return jnp.cos(ang), jnp.sin(ang) * sign[None, :]


def _rope(z, cos, sin_signed, quarter):
    lane = lax.broadcasted_iota(I32, z.shape, 1)
    first = (lane % (2 * quarter)) < quarter
    partner = jnp.where(first, pltpu.roll(z, LANES - quarter, 1), pltpu.roll(z, quarter, 1))
    return z * cos + partner * sin_signed


def _inproj_kernel(x_ref, mod_ref, g_ref, w_ref, ca_ref, sa_ref, cr_ref, sr_ref, qg_ref, kg_ref, vs_ref,
                   qa_ref, kd_ref, vt_ref, qr_ref, kr_ref, vr_ref, gr_ref):
    x = x_ref[...]
    m = mod_ref[0]
    h = _rms(x, g_ref[...]) * (1.0 + m[1:2, :]) + m[0:1, :]
    hb = h.astype(BF16)
    tm = x.shape[0]

    def proj(c0):
        z = jnp.dot(hb, w_ref[:, c0:c0 + 2 * LANES], preferred_element_type=F32)
        return z[:, :LANES], z[:, LANES:]

    ri = lax.broadcasted_iota(I32, (LANES, LANES), 0) // HEAD_DIM
    ci = lax.broadcasted_iota(I32, (LANES, LANES), 1) // HEAD_DIM
    head_ones = (ri == ci).astype(BF16)
    ca, sa = ca_ref[...], sa_ref[...]
    cr, sr = cr_ref[...], sr_ref[...]
    lane = lax.broadcasted_iota(I32, (tm, LANES), 1)
    low = lane < HEAD_DIM

    def head_norm_rope(z, g):
        ss = jnp.dot((z * z).astype(BF16), head_ones, preferred_element_type=F32)
        zn = z * lax.rsqrt(ss * (1.0 / HEAD_DIM) + NORM_EPS) * g
        return _rope(zn, ca, sa, HEAD_DIM // 4)

    def store_pair(ref, j, fn):
        def finish(zs):
            for i, z in enumerate(zs):
                ref[:, (2 * j + i) * LANES:(2 * j + i + 1) * LANES] = fn(z).astype(BF16)
        return finish

    def store_kv(zs):
        k, v = zs
        k = head_norm_rope(k, kg_ref[...])
        k_sw = pltpu.roll(k, HEAD_DIM, 1)
        v = v * vs_ref[0]
        kd_ref[0, 0] = jnp.where(low, k, k_sw).astype(BF16)
        kd_ref[0, 1] = jnp.where(low, k_sw, k).astype(BF16)
        vt_ref[0, 0] = jnp.where(low, v, 1.0).T.astype(FP8)
        vt_ref[0, 1] = jnp.where(low, pltpu.roll(v, HEAD_DIM, 1), 1.0).T.astype(FP8)

    jobs = [(2 * j * LANES, store_pair(qa_ref, j, lambda z: head_norm_rope(z, qg_ref[...]) * Q_SCALE))
            for j in range(ATTN_W // (2 * LANES))]
    jobs.append((ATTN_W, store_kv))
    base = ATTN_W + 2 * KV_W
    for j in range(RET_W // (2 * LANES)):
        off = 2 * j * LANES
        jobs += [
            (base + off, store_pair(qr_ref, j, lambda z: _rope(z, cr, sr, RET_DIM // 4))),
            (base + RET_W + off, store_pair(kr_ref, j, lambda z: _rope(z, cr, sr, RET_DIM // 4) * (RET_DIM ** -0.5))),
            (base + 2 * RET_W + off, store_pair(vr_ref, j, lambda z: z)),
            (base + 3 * RET_W + off, store_pair(gr_ref, j, lambda z: z * jax.nn.sigmoid(z))),
        ]
    zs = proj(jobs[0][0])
    for i, (_, finish) in enumerate(jobs):
        nxt = proj(jobs[i + 1][0]) if i + 1 < len(jobs) else None
        finish(zs)
        zs = nxt


def _inproj(x2, batch0, x_batch0, mod_l, pre_g, w_in_b, ropes, qg, kg, v_scale, b, s):
    n = b * s
    tm = min(ROW_TILE, s)
    nb = s // tm
    ca, sa, cr, sr = ropes
    row = lambda i: (i, 0)
    full = lambda i: (0, 0)
    rope_spec = pl.BlockSpec((tm, LANES), lambda i: (i % nb, 0))
    wide = jax.ShapeDtypeStruct((n, RET_W), BF16)
    return pl.pallas_call(
        _inproj_kernel,
        grid=(n // tm,),
        in_specs=[
            pl.BlockSpec((tm, D_MODEL), lambda i: (x_batch0 * nb + i, 0)),
            pl.BlockSpec((1, 6, D_MODEL), lambda i: (batch0 + i // nb, 0, 0)),
            pl.BlockSpec((1, D_MODEL), full),
            pl.BlockSpec((D_MODEL, IN_W), full),
            rope_spec, rope_spec, rope_spec, rope_spec,
            pl.BlockSpec((1, LANES), full),
            pl.BlockSpec((1, LANES), full),
            pl.BlockSpec((1, 1, LANES), lambda i: (batch0 + i // nb, 0, 0)),
        ],
        out_specs=[
            pl.BlockSpec((tm, ATTN_W), row),
            pl.BlockSpec((1, KV_HEADS, tm, LANES), lambda i: (i // nb, 0, i % nb, 0)),
            pl.BlockSpec((1, KV_HEADS, LANES, tm), lambda i: (i // nb, 0, 0, i % nb)),
            pl.BlockSpec((tm, RET_W), row),
            pl.BlockSpec((tm, RET_W), row),
            pl.BlockSpec((tm, RET_W), row),
            pl.BlockSpec((tm, RET_W), row),
        ],
        out_shape=[
            jax.ShapeDtypeStruct((n, ATTN_W), BF16),
            jax.ShapeDtypeStruct((b, KV_HEADS, s, LANES), BF16),
            jax.ShapeDtypeStruct((b, KV_HEADS, LANES, s), FP8),
            wide, wide, wide, wide,
        ],
        compiler_params=_params(("parallel",)),
        name="inproj",
    )(x2, mod_l, pre_g, w_in_b, ca, sa, cr, sr, qg, kg, v_scale)


def _attn_kernel(q_ref, kd_ref, vt_ref, fac_ref, o_ref, q_s, qn_s, m_ref, acc_ref, s0_ref, s1_ref,
                 *, tq, tk, tiles):
    s = kd_ref.shape[2]
    nk = s // tk
    nq = s // tq
    pairs = (Q_HEADS // KV_HEADS) // 2
    top = lax.broadcasted_iota(I32, (LANES, tq), 0) < HEAD_DIM
    unscale = jnp.concatenate([fac_ref[0]] * (2 * tq // LANES), axis=1)

    chains = tiles * pairs

    def load_q(trip, dst):
        for u in range(tiles):
            rows = pl.ds(pl.multiple_of((trip * tiles + u) * tq, tq), tq)
            for p in range(pairs):
                qt = q_ref[rows, p * LANES:(p + 1) * LANES].astype(F32).T
                dst[u * pairs + p, :, :tq] = jnp.where(top, qt, 0.0).astype(BF16)
                dst[u * pairs + p, :, tq:] = jnp.where(top, 0.0, qt).astype(BF16)

    def keys(j):
        return kd_ref[0, 0, j * tk:(j + 1) * tk, :]

    def step(j, cur_ref, nxt_ref, q_next, j_next):
        vt = vt_ref[0, 0, :, j * tk:(j + 1) * tk]
        kd = keys(j_next)
        nxt_ref[0] = jnp.dot(kd, q_next[0], preferred_element_type=F32)
        for c in range(chains):
            sc = cur_ref[c]
            m_old = m_ref[c]
            m_new = jnp.maximum(m_old, jnp.max(sc, axis=0, keepdims=True))
            pt = jnp.exp2(sc - (m_new - P_SHIFT))
            alpha = jnp.exp2(m_old - m_new)
            if c + 1 < chains:
                nxt_ref[c + 1] = jnp.dot(kd, q_next[c + 1], preferred_element_type=F32)
            acc_ref[c] = alpha * acc_ref[c] + jnp.dot(vt, pt.astype(FP8), preferred_element_type=F32)
            m_ref[c] = m_new

    bufs = (s0_ref, s1_ref)
    trips = nq // tiles
    load_q(0, q_s)
    for c in range(chains):
        s0_ref[c] = jnp.dot(keys(0), q_s[c], preferred_element_type=F32)

    def body(trip, carry):
        load_q(trip, q_s)
        load_q(jnp.minimum(trip + 1, trips - 1), qn_s)
        m_ref[...] = jnp.full(m_ref.shape, -jnp.inf, F32)
        acc_ref[...] = jnp.zeros(acc_ref.shape, F32)
        for j in range(nk):
            last = j + 1 == nk
            step(j, bufs[j % len(bufs)], bufs[(j + 1) % len(bufs)], qn_s if last else q_s, 0 if last else j + 1)
        for c in range(chains):
            u, p = divmod(c, pairs)
            rows = pl.ds(pl.multiple_of((trip * tiles + u) * tq, tq), tq)
            acc = acc_ref[c]
            o_t = acc[:HEAD_DIM] / acc[HEAD_DIM:] * unscale
            both = jnp.concatenate([o_t[:, :tq], o_t[:, tq:]], axis=0)
            o_ref[rows, p * LANES:(p + 1) * LANES] = both.T.astype(BF16)
        return carry

    lax.fori_loop(0, trips, body, 0)


def _attention(qa, kd, vt, fac, batch0, b, s):
    n = qa.shape[0]
    tq = Q_TILE
    tk = min(KV_TILE, s // 2)
    tiles = Q_TILES_PER_TRIP
    assert tq == LANES and s % (tiles * tq) == 0 and s % (2 * tk) == 0 and tk % LANES == 0, (s, tq, tk)
    gw = ATTN_W // KV_HEADS
    chains = tiles * ((Q_HEADS // KV_HEADS) // 2)
    return pl.pallas_call(
        functools.partial(_attn_kernel, tq=tq, tk=tk, tiles=tiles),
        grid=(b, KV_HEADS),
        in_specs=[
            pl.BlockSpec((s, gw), lambda bi, h: (bi, h)),
            pl.BlockSpec((1, 1, s, LANES), lambda bi, h: (bi, h, 0, 0)),
            pl.BlockSpec((1, 1, LANES, s), lambda bi, h: (bi, h, 0, 0)),
            pl.BlockSpec((1, 1, LANES), lambda bi, h: (batch0 + bi, 0, 0)),
        ],
        out_specs=pl.BlockSpec((s, gw), lambda bi, h: (bi, h)),
        out_shape=jax.ShapeDtypeStruct((n, ATTN_W), BF16),
        scratch_shapes=[
            pltpu.VMEM((chains, LANES, 2 * tq), BF16),
            pltpu.VMEM((chains, LANES, 2 * tq), BF16),
            pltpu.VMEM((chains, 1, 2 * tq), F32),
            pltpu.VMEM((chains, LANES, 2 * tq), F32),
            pltpu.VMEM((chains, tk, 2 * tq), F32),
            pltpu.VMEM((chains, tk, 2 * tq), F32),
        ],
        compiler_params=_params(("parallel", "parallel")),
        name="attention",
    )(qa, kd, vt, fac)


def _retention_kernel(dec_ref, q_ref, k_ref, v_ref, g_ref, gng_ref, gnb_ref, o_ref,
                      fbuf, bbuf, sf_ref, sb_ref):
    h = pl.program_id(1)
    c = RET_CHUNK
    s = q_ref.shape[1]
    nc = s // c
    lgf = jnp.full((1, 1), dec_ref[0, h], F32)
    lgb = jnp.full((1, 1), dec_ref[1, h], F32)
    ii = lax.broadcasted_iota(I32, (c, c), 0)
    jj = lax.broadcasted_iota(I32, (c, c), 1)
    diff = (ii - jj).astype(F32)
    d_f = jnp.where(diff >= 0, jnp.exp(lgf * jnp.maximum(diff, 0.0)), 0.0)
    d_b = jnp.where(diff <= 0, jnp.exp(lgb * jnp.maximum(-diff, 0.0)), 0.0)
    idx = lax.broadcasted_iota(I32, (c, 1), 0).astype(F32)
    xi_f = jnp.exp(lgf * (idx + 1.0))
    zeta_f = jnp.exp(lgf * (c - 1.0 - idx))
    xi_b = jnp.exp(lgb * (c - idx))
    zeta_b = jnp.exp(lgb * idx)
    cd_f = jnp.exp(lgf * c)
    cd_b = jnp.exp(lgb * c)
    sf_ref[...] = jnp.zeros(sf_ref.shape, F32)
    sb_ref[...] = jnp.zeros(sb_ref.shape, F32)
    nt = (((1,), (1,)), ((), ()))

    fwd = (d_f, xi_f, zeta_f, cd_f, sf_ref, fbuf)
    bwd = (d_b, xi_b, zeta_b, cd_b, sb_ref, bbuf)

    per = min(RET_CHUNKS_PER_TRIP, nc)

    span = per * c

    def finish_rows(off):
        r = fbuf[pl.ds(off, span), :] + bbuf[pl.ds(off, span), :]
        mu = jnp.mean(r, axis=-1, keepdims=True)
        dlt = r - mu
        var = jnp.mean(dlt * dlt, axis=-1, keepdims=True)
        y = dlt * lax.rsqrt(var + NORM_EPS) * gng_ref[...] + gnb_ref[...]
        o_ref[0, pl.ds(off, span), :] = (g_ref[0, pl.ds(off, span), :].astype(F32) * y).astype(BF16)

    trips = nc // per

    def finish_trip(i, spans):
        if spans >= 1:
            finish_rows(pl.multiple_of(i * span, span))
        if spans == 2:
            finish_rows(pl.multiple_of((trips - 1 - i) * span, span))

    def body(i, carry, finish):
        finish_trip(i - 1, finish)
        jobs = []
        for n in range(per):
            jobs += [(fwd, per * i + n), (bwd, nc - 1 - per * i - n)]
        offs = [pl.multiple_of(ci * c, c) for _, ci in jobs]
        qs = [q_ref[0, pl.ds(off, c), :] for off in offs]
        ks = [k_ref[0, pl.ds(off, c), :] for off in offs]
        vs = [v_ref[0, pl.ds(off, c), :] for off in offs]
        scores = [lax.dot_general(q, k, nt, preferred_element_type=F32) for q, k in zip(qs, ks)]
        incs = []
        for (d, _), k, v in zip(jobs, ks, vs):
            kzt = (k.astype(F32) * d[2]).T.astype(BF16)
            incs.append(jnp.dot(kzt, v, preferred_element_type=F32))
        states = [fwd[4][...], bwd[4][...]]
        cross = []
        for n, (d, _) in enumerate(jobs):
            cross.append(jnp.dot(qs[n], states[n].astype(BF16), preferred_element_type=F32))
            states.append(states[n] * d[3] + incs[n])
        for n, ((d, _), off) in enumerate(zip(jobs, offs)):
            a = (scores[n] * d[0]).astype(BF16)
            d[5][pl.ds(off, c), :] = jnp.dot(a, vs[n], preferred_element_type=F32) + cross[n] * d[1]
        fwd[4][...] = states[2 * per]
        bwd[4][...] = states[2 * per + 1]
        return carry

    first_done = trips // 2
    lax.fori_loop(0, min(first_done + 1, trips), functools.partial(body, finish=0), 0)
    done_by_first = 1 if trips % 2 else 2
    if first_done + 1 < trips:
        body(jnp.int32(first_done + 1), 0, finish=done_by_first)
        lax.fori_loop(first_done + 2, trips, functools.partial(body, finish=2), 0)
        finish_trip(jnp.int32(trips - 1), 2)
    else:
        finish_trip(jnp.int32(trips - 1), done_by_first)


def _retention(qr, kr, vr, gr, dec, gn_g, gn_b, b, s):
    shp = (b, s, RET_W)
    assert (s // RET_CHUNK) % min(RET_CHUNKS_PER_TRIP, s // RET_CHUNK) == 0, s
    head = pl.BlockSpec((1, s, RET_DIM), lambda bi, h: (bi, 0, h))
    vec = pl.BlockSpec((1, RET_DIM), lambda bi, h: (0, h))
    out = pl.pallas_call(
        _retention_kernel,
        grid=(b, RET_HEADS),
        in_specs=[pl.BlockSpec(memory_space=pltpu.SMEM), head, head, head, head, vec, vec],
        out_specs=head,
        out_shape=jax.ShapeDtypeStruct(shp, BF16),
        scratch_shapes=[
            pltpu.VMEM((s, RET_DIM), F32),
            pltpu.VMEM((s, RET_DIM), F32),
            pltpu.VMEM((RET_DIM, RET_DIM), F32),
            pltpu.VMEM((RET_DIM, RET_DIM), F32),
        ],
        compiler_params=_params(("parallel", "parallel")),
        name="retention",
    )(dec, qr.reshape(shp), kr.reshape(shp), vr.reshape(shp), gr.reshape(shp), gn_g, gn_b)
    return out.reshape(b * s, RET_W)


def _outproj_kernel(oa_ref, or_ref, w_ref, x_ref, mod_ref, pg_ref, fg_ref, rw_ref, rb_ref,
                    xn_ref, hp_ref, ti_ref, tw_ref, rk_ref, cnt_ref, base_ref, before_ref):
    tm = x_ref.shape[0]

    @pl.when(pl.program_id(0) == 0)
    def _():
        base_ref[...] = jnp.zeros(base_ref.shape, F32)
        earlier = lax.broadcasted_iota(I32, (tm, tm), 0) < lax.broadcasted_iota(I32, (tm, tm), 1)
        before_ref[...] = earlier.astype(BF16)

    m = mod_ref[0]
    nt = (((1,), (1,)), ((), ()))
    halves = [slice(0, tm // 2), slice(tm // 2, tm)]
    ys = [jnp.dot(oa_ref[r, :], w_ref[:ATTN_W, :], preferred_element_type=F32)
          + jnp.dot(or_ref[r, :], w_ref[ATTN_W:, :], preferred_element_type=F32) for r in halves]
    parts = []
    for r, y in zip(halves, ys):
        xn = x_ref[r, :] + m[2:3, :] * _rms(y, pg_ref[...])
        xn_ref[r, :] = xn
        h = _rms(xn, fg_ref[...]) * (1.0 + m[4:5, :]) + m[3:4, :]
        hb = h.astype(BF16)
        u = lax.bitcast_convert_type(hb.astype(F32), U32)
        hp_ref[r, :] = (u[:, :HALF] >> 16) | u[:, HALF:]
        parts.append(lax.dot_general(rw_ref[...], hb, nt, preferred_element_type=F32))
    logits = jnp.concatenate(parts, axis=1) + rb_ref[...]
    e_iota = lax.broadcasted_iota(I32, (N_EXPERTS, tm), 0).astype(F32)
    vals, idxs = [], []
    cur = logits
    for _ in range(TOP_K):
        mx = jnp.max(cur, axis=0, keepdims=True)
        ik = jnp.min(jnp.where(cur == mx, e_iota, float(N_EXPERTS)), axis=0, keepdims=True)
        vals.append(mx)
        idxs.append(ik)
        cur = jnp.where(e_iota == ik, -jnp.inf, cur)
    v = jnp.concatenate(vals, axis=0)
    w = jnp.exp(v - v[0:1, :])
    tw_ref[...] = w / jnp.sum(w, axis=0, keepdims=True)
    ti_ref[...] = jnp.concatenate(idxs, axis=0).astype(I32)

    onehot = jnp.zeros((N_EXPERTS, tm), F32)
    for ik in idxs:
        onehot = onehot + (e_iota == ik).astype(F32)
    seen = jnp.dot(onehot.astype(BF16), before_ref[...], preferred_element_type=F32) + base_ref[...]
    ranks = [jnp.sum(jnp.where(e_iota == ik, seen, 0.0), axis=0, keepdims=True) for ik in idxs]
    rk_ref[...] = jnp.concatenate(ranks, axis=0).astype(I32)
    total = base_ref[...] + jnp.sum(onehot, axis=1, keepdims=True)
    base_ref[...] = total
    cnt_ref[...] = jnp.broadcast_to(total, cnt_ref.shape).astype(I32)


def _outproj(oa, orr, w_out_b, x2, batch0, x_batch0, mod_l, post_g, ffn_g, rw_t, rb, b, s):
    n = b * s
    tm = min(ROW_TILE, s)
    nb = s // tm
    row = lambda i: (i, 0)
    full = lambda i: (0, 0)
    col = lambda i: (0, i)
    return pl.pallas_call(
        _outproj_kernel,
        grid=(n // tm,),
        in_specs=[
            pl.BlockSpec((tm, ATTN_W), row),
            pl.BlockSpec((tm, RET_W), row),
            pl.BlockSpec((D_MODEL, D_MODEL), full),
            pl.BlockSpec((tm, D_MODEL), lambda i: (x_batch0 * nb + i, 0)),
            pl.BlockSpec((1, 6, D_MODEL), lambda i: (batch0 + i // nb, 0, 0)),
            pl.BlockSpec((1, D_MODEL), full),
            pl.BlockSpec((1, D_MODEL), full),
            pl.BlockSpec((N_EXPERTS, D_MODEL), full),
            pl.BlockSpec((N_EXPERTS, 1), full),
        ],
        out_specs=[
            pl.BlockSpec((tm, D_MODEL), row),
            pl.BlockSpec((tm, HALF), row),
            pl.BlockSpec((TOP_K, tm), col),
            pl.BlockSpec((TOP_K, tm), col),
            pl.BlockSpec((TOP_K, tm), col),
            pl.BlockSpec((N_EXPERTS, LANES), full),
        ],
        out_shape=[
            jax.ShapeDtypeStruct((n, D_MODEL), F32),
            jax.ShapeDtypeStruct((n, HALF), U32),
            jax.ShapeDtypeStruct((TOP_K, n), I32),
            jax.ShapeDtypeStruct((TOP_K, n), F32),
            jax.ShapeDtypeStruct((TOP_K, n), I32),
            jax.ShapeDtypeStruct((N_EXPERTS, LANES), I32),
        ],
        scratch_shapes=[pltpu.VMEM((N_EXPERTS, 1), F32), pltpu.VMEM((tm, tm), BF16)],
        compiler_params=_params(("arbitrary",)),
        name="outproj_router",
    )(oa, orr, w_out_b, x2, mod_l, post_g, ffn_g, rw_t, rb)


def _sc_scatter_rows(rows, pos3, n_out):
    n, w = rows.shape
    nchunk, kk, c = pos3.shape
    per_w = nchunk // SC_WORKERS
    mesh = plsc.VectorSubcoreMesh(core_axis_name="c", subcore_axis_name="s")

    @functools.partial(
        pl.kernel, mesh=mesh,
        out_type=jax.ShapeDtypeStruct((n_out, w), rows.dtype),
        scratch_types=[pltpu.VMEM((kk, c), I32), pltpu.VMEM((c, w), rows.dtype)],
    )
    def k(rows_hbm, pos_hbm, out_hbm, idx_v, rows_v):
        wid = lax.axis_index("s") * SC_CORES + lax.axis_index("c")

        @pl.loop(0, per_w)
        def _(i):
            ch = wid * per_w + i
            pltpu.sync_copy(pos_hbm.at[ch], idx_v)
            pltpu.sync_copy(rows_hbm.at[pl.ds(ch * c, c)], rows_v)
            for j in range(kk):
                pltpu.sync_copy(rows_v, out_hbm.at[idx_v.at[j]])

    return k(rows, pos3)


def _sc_gather_rows(table, idx):
    w = table.shape[1]
    b = idx.shape[0]
    c = SC_CHUNK
    per_w = b // (SC_WORKERS * c)
    mesh = plsc.VectorSubcoreMesh(core_axis_name="c", subcore_axis_name="s")

    @functools.partial(
        pl.kernel, mesh=mesh,
        out_type=jax.ShapeDtypeStruct((b, w), table.dtype),
        scratch_types=[pltpu.VMEM((c,), I32), pltpu.VMEM((c, w), table.dtype), pltpu.SemaphoreType.DMA],
    )
    def k(table_hbm, idx_hbm, out_hbm, idx_v, rows_v, sem):
        wid = lax.axis_index("s") * SC_CORES + lax.axis_index("c")

        @pl.loop(0, per_w)
        def _(i):
            base = (wid * per_w + i) * c
            pltpu.sync_copy(idx_hbm.at[pl.ds(base, c)], idx_v)
            pltpu.async_copy(table_hbm.at[idx_v], rows_v, sem).wait()
            pltpu.sync_copy(rows_v, out_hbm.at[pl.ds(base, c)])

    return k(table, idx)


def _experts_kernel(te_ref, nv_ref, xs_ref, wu_ref, bu_ref, wd_ref, bd_ref, ys_ref, wu_b, wd_b, inv_ref):
    i = pl.program_id(0)
    live = i < nv_ref[0]
    new_expert = jnp.logical_or(i == 0, te_ref[i] != te_ref[jnp.maximum(i - 1, 0)])

    def full_max(a):
        return jnp.max(jnp.max(jnp.abs(a), axis=0, keepdims=True), axis=1, keepdims=True)

    @pl.when(jnp.logical_and(live, new_expert))
    def _():
        wu = wu_ref[0, 0].astype(BF16)
        wd = wd_ref[0, 0].astype(BF16)
        mu = jnp.maximum(full_max(wu).astype(F32), 1e-30)
        md = jnp.maximum(full_max(wd).astype(F32), 1e-30)
        wu_b[...] = (wu * (FP8_RANGE / mu).astype(BF16)).astype(FP8)
        wd_b[...] = (wd * (FP8_RANGE / md).astype(BF16)).astype(FP8)
        inv_ref[0:1, :] = jnp.broadcast_to(mu / FP8_RANGE, (1, LANES))
        inv_ref[1:2, :] = jnp.broadcast_to(md / (FP8_RANGE * ACT_SCALE), (1, LANES))

    @pl.when(live)
    def _():
        t = xs_ref.shape[0]
        halves = [slice(0, t // 2), slice(t // 2, t)]
        inv_up = inv_ref[0:1, 0:1]
        inv_dn = inv_ref[1:2, 0:1]
        ups = []
        for r in halves:
            w = xs_ref[r, :]
            lo = _unpack_lo(w).astype(BF16)
            hi = _unpack_hi(w).astype(BF16)
            amax = jnp.max(jnp.maximum(jnp.abs(lo), jnp.abs(hi)), axis=1, keepdims=True).astype(F32)
            c = (FP8_RANGE / jnp.maximum(amax, 1e-30)).astype(BF16)
            up = (jnp.dot((lo * c).astype(FP8), wu_b[:HALF, :], preferred_element_type=F32)
                  + jnp.dot((hi * c).astype(FP8), wu_b[HALF:, :], preferred_element_type=F32))
            ups.append(up * (inv_up / c.astype(F32)))
        for r, up in zip(halves, ups):
            up = (up + bu_ref[0, 0]).astype(BF16)
            glu = jnp.minimum(up[:, :D_FF], SWIGLU_LIMIT)
            lin = jnp.clip(up[:, D_FF:], -SWIGLU_LIMIT, SWIGLU_LIMIT)
            act = (glu * (0.5 * ACT_SCALE)) * (1.0 + jnp.tanh((0.5 * SWIGLU_ALPHA) * glu)) * (lin + 1.0)
            y = jnp.dot(act.astype(FP8), wd_b[...], preferred_element_type=F32) * inv_dn + bd_ref[0, 0]
            ys_ref[r, :] = _pack_bf16_pairs(y)


def _experts(xs, tile_expert, n_valid, layer, wu, bu, wd, bd):
    n_pad = xs.shape[0]
    t = MOE_TILE
    grid_spec = pltpu.PrefetchScalarGridSpec(
        num_scalar_prefetch=2,
        grid=(n_pad // t,),
        in_specs=[
            pl.BlockSpec((t, HALF), lambda i, te, nv: (i, 0)),
            pl.BlockSpec((1, 1, D_MODEL, 2 * D_FF), lambda i, te, nv: (layer, te[i], 0, 0)),
            pl.BlockSpec((1, 1, 1, 2 * D_FF), lambda i, te, nv: (layer, te[i], 0, 0)),
            pl.BlockSpec((1, 1, D_FF, D_MODEL), lambda i, te, nv: (layer, te[i], 0, 0)),
            pl.BlockSpec((1, 1, 1, D_MODEL), lambda i, te, nv: (layer, te[i], 0, 0)),
        ],
        out_specs=pl.BlockSpec((t, HALF), lambda i, te, nv: (i, 0)),
        scratch_shapes=[pltpu.VMEM((D_MODEL, 2 * D_FF), FP8), pltpu.VMEM((D_FF, D_MODEL), FP8),
                        pltpu.VMEM((2, LANES), F32)],
    )
    return pl.pallas_call(
        _experts_kernel,
        grid_spec=grid_spec,
        out_shape=jax.ShapeDtypeStruct((n_pad, HALF), U32),
        compiler_params=pltpu.CompilerParams(dimension_semantics=("arbitrary",), vmem_limit_bytes=EXPERTS_VMEM_LIMIT),
        name="experts",
    )(tile_expert, n_valid, xs, wu, bu, wd, bd)


def _combine_kernel(y0_ref, y1_ref, y2_ref, y3_ref, w_ref, x_ref, mod_ref, g_ref, *rest):
    o_ref = rest[-1]
    wt = w_ref[...]
    w = jnp.concatenate([wt, jnp.zeros_like(wt)], axis=0).T
    lo = None
    hi = None
    for k, y_ref in enumerate((y0_ref, y1_ref, y2_ref, y3_ref)):
        word = y_ref[...]
        wk = w[:, k:k + 1]
        lo_k = wk * _unpack_lo(word)
        hi_k = wk * _unpack_hi(word)
        lo = lo_k if lo is None else lo + lo_k
        hi = hi_k if hi is None else hi + hi_k
    yf = jnp.concatenate([lo, hi], axis=1)
    m = mod_ref[0]
    o_ref[...] = x_ref[...] + m[5:6, :] * _rms(yf, g_ref[...])


def _combine(yg, tw, xn, batch0, mod_l, post_g, b, s, out_batches=None, shared_out=None):
    n = b * s
    tm = min(ROW_TILE, s)
    nb = s // tm
    nblk = n // tm
    row = lambda i: (i, 0)
    full = lambda i: (0, 0)
    y_specs = [pl.BlockSpec((tm, HALF), functools.partial(lambda i, k: (k * nblk + i, 0), k=k)) for k in range(TOP_K)]
    out_rows, out_blk0 = (n, 0) if out_batches is None else (out_batches * s, batch0 * nb)
    extra_specs, extra_args, aliases = [], [], {}
    if shared_out is not None:
        extra_specs, extra_args, aliases = [pl.BlockSpec(memory_space=pl.ANY)], [shared_out], {8: 0}
    return pl.pallas_call(
        _combine_kernel,
        grid=(nblk,),
        in_specs=y_specs + [
            pl.BlockSpec((TOP_K, tm), lambda i: (0, i)),
            pl.BlockSpec((tm, D_MODEL), row),
            pl.BlockSpec((1, 6, D_MODEL), lambda i: (batch0 + i // nb, 0, 0)),
            pl.BlockSpec((1, D_MODEL), full),
        ] + extra_specs,
        out_specs=pl.BlockSpec((tm, D_MODEL), lambda i: (out_blk0 + i, 0)),
        out_shape=jax.ShapeDtypeStruct((out_rows, D_MODEL), F32),
        input_output_aliases=aliases,
        compiler_params=_params(("parallel",)),
        name="combine",
    )(yg, yg, yg, yg, tw, xn, mod_l, post_g, *extra_args)


def _route(counts, tidx, rank, n_tiles):
    t = MOE_TILE
    cnt = counts[:, 0]
    padded = ((cnt + t - 1) // t) * t
    ends = jnp.cumsum(padded)
    starts = ends - padded
    experts = jnp.arange(N_EXPERTS, dtype=I32)[:, None, None]
    start_of = jnp.sum(jnp.where(tidx[None] == experts, starts[:, None, None], 0), axis=0)
    pos = start_of + rank
    tile_ends = ends // t
    n_valid = tile_ends[-1]
    tile = jnp.minimum(jnp.arange(n_tiles, dtype=I32), n_valid - 1)
    tile_expert = jnp.minimum(jnp.sum(tile_ends[None, :] <= tile[:, None], axis=1), N_EXPERTS - 1).astype(I32)
    return pos.astype(I32), tile_expert, n_valid.reshape(1).astype(I32)


def _value_range_scale(mod_l, pre_g, w_in_l):
    d = pre_g.shape[0]
    shift, scale = mod_l[:, 0, :], mod_l[:, 1, :]
    h_norm = d ** 0.5 * jnp.max(jnp.abs(pre_g[None, :] * (1.0 + scale)), axis=1) + jnp.linalg.norm(shift, axis=1)
    w_v = w_in_l[:, ATTN_W + KV_W:ATTN_W + 2 * KV_W]
    bound = 1.02 * h_norm * jnp.max(jnp.linalg.norm(w_v, axis=0))
    v_scale = V_RANGE / jnp.maximum(bound, 1e-30)
    lanes = lambda t: jnp.broadcast_to(t[:, None, None], (mod_l.shape[0], 1, LANES))
    return lanes(v_scale), lanes(1.0 / v_scale)


def kernel(x, c, ada_w, ada_b, pre_mix_g, post_mix_g, w_in, q_norm_g, k_norm_g, ret_decay_fwd, ret_decay_bwd, ret_gn_g, ret_gn_b, w_out, pre_ffn_g, post_ffn_g, router_w, router_b, exp_w_up, exp_b_up, exp_w_down, exp_b_down):
    b, s, d = x.shape
    depth = ada_w.shape[0]
    groups = 2 if b % 2 == 0 else 1
    bg = b // groups
    n = bg * s
    n_asg = n * TOP_K
    n_pad = n_asg + N_EXPERTS * MOE_TILE
    n_tiles = n_pad // MOE_TILE

    mod = _modulation(c, ada_w, ada_b)
    ropes = _rope_tables(s, HEAD_DIM) + _rope_tables(s, RET_DIM)
    bu = exp_b_up.reshape(depth, N_EXPERTS, 1, 2 * D_FF)
    bd = exp_b_down.reshape(depth, N_EXPERTS, 1, d)
    xs_in = [x.reshape(b * s, d)] * groups
    x_b0 = [g * bg for g in range(groups)]
    out = None
    for l in range(depth):
        mod_l = mod[l].reshape(b, 6, d)
        qg = jnp.tile(q_norm_g[l], LANES // HEAD_DIM).reshape(1, LANES)
        kg = jnp.tile(k_norm_g[l], LANES // HEAD_DIM).reshape(1, LANES)
        v_scale, fac = _value_range_scale(mod_l, pre_mix_g[l], w_in[l])
        w_in_b = w_in[l].astype(BF16)
        w_out_b = w_out[l].astype(BF16)
        rw_t = router_w[l].T.astype(BF16)
        dec = jnp.stack([ret_decay_fwd[l], ret_decay_bwd[l]]).astype(F32)
        last = l + 1 == depth
        proj = [_inproj(xs_in[g], g * bg, x_b0[g], mod_l, pre_mix_g[l].reshape(1, d), w_in_b, ropes, qg, kg, v_scale,
                        bg, s) for g in range(groups)]
        oa = [_attention(p[0], p[1], p[2], fac, g * bg, bg, s) for g, p in enumerate(proj)]
        orr = [_retention(p[3], p[4], p[5], p[6], dec, ret_gn_g[l].reshape(1, RET_W), ret_gn_b[l].reshape(1, RET_W),
                          bg, s) for p in proj]
        routed = []
        for g in range(groups):
            xn, hp, tidx, tw, rank, counts = _outproj(
                oa[g], orr[g], w_out_b, xs_in[g], g * bg, x_b0[g], mod_l, post_mix_g[l].reshape(1, d),
                pre_ffn_g[l].reshape(1, d), rw_t, router_b[l].reshape(N_EXPERTS, 1), bg, s)
            pos, tile_expert, n_valid = _route(counts, tidx, rank, n_tiles)
            pos3 = pos.reshape(TOP_K, n // SC_CHUNK, SC_CHUNK).transpose(1, 0, 2)
            routed.append((xn, tw, pos, tile_expert, n_valid, _sc_scatter_rows(hp, pos3, n_pad)))
        yg = []
        for xn, tw, pos, tile_expert, n_valid, xs in routed:
            ys = _experts(xs, tile_expert, n_valid, l, exp_w_up, bu, exp_w_down, bd)
            yg.append(_sc_gather_rows(ys, pos.reshape(n_asg)))
        nxt = []
        for g, (xn, tw, *_) in enumerate(routed):
            if last:
                out = _combine(yg[g], tw, xn, g * bg, mod_l, post_ffn_g[l].reshape(1, d), bg, s,
                               out_batches=b, shared_out=out)
            else:
                nxt.append(_combine(yg[g], tw, xn, g * bg, mod_l, post_ffn_g[l].reshape(1, d), bg, s))
        xs_in, x_b0 = nxt, [0] * groups
    return out.reshape(b, s, d)
```
